```python
import math
import jax, jax.numpy as jnp
from jax import lax
import numpy as np

D_MODEL = 1024
BATCH = 32
SEQ = 256
DEPTH = 1
DEC_BATCH = 2
DEC_SEQ = 2048
PAST_LEN = 256

GRID_W = 64
D_RWKV = 512
D_FNET = D_MODEL - D_RWKV
HEAD_DIM = 64
N_HEADS = D_RWKV // HEAD_DIM
FNET_GROUP = 64
N_FGROUPS = D_FNET // FNET_GROUP
LORA_W = 32
LORA_A = 32
N_DIR = 2
D_SHIFT = 3 * D_RWKV + N_DIR * (LORA_W + LORA_A)
D_IN = D_SHIFT + D_RWKV + 2 * D_FNET
NORM_EPS = 1e-6
GN_EPS = 64e-5
POS_BASE = 10000.0

kernel_name = "hybrid_rwkv7_fnet_diffusion_step"


def _rmsnorm(x, g):
    xf = x.astype(jnp.float32)
    y = xf * lax.rsqrt(jnp.mean(xf * xf, axis=-1, keepdims=True) + NORM_EPS)
    return (y * g.astype(jnp.float32)).astype(x.dtype)


def _centred_shift(p):
    prev = jnp.pad(p[:, :-1], ((0, 0), (1, 0), (0, 0)))
    nxt = jnp.pad(p[:, 1:], ((0, 0), (0, 1), (0, 0)))
    return 0.5 * (prev + nxt)


def _sincos_2d(n_tokens, dtype):
    rows = n_tokens // GRID_W
    pos = jnp.arange(rows * GRID_W)
    row = (pos // GRID_W).astype(jnp.float32)
    col = (pos % GRID_W).astype(jnp.float32)
    quarter = D_MODEL // 4
    freq = jnp.exp(-math.log(POS_BASE) * jnp.arange(quarter, dtype=jnp.float32) / quarter)
    ang_r = row[:, None] * freq
    ang_c = col[:, None] * freq
    emb = jnp.concatenate([jnp.sin(ang_r), jnp.cos(ang_r), jnp.sin(ang_c), jnp.cos(ang_c)], axis=-1)
    return emb.astype(dtype)


def _rwkv7_scan(r, w, k, v, kk, a, s0, reverse):
    def step(S, inp):
        r_t, w_t, k_t, v_t, kk_t, a_t = inp
        S_kk = jnp.einsum('bhvk,bhk->bhv', S, kk_t)
        S = (S * w_t[:, :, None, :]
             - S_kk[..., :, None] * (kk_t * a_t)[:, :, None, :]
             + v_t[..., :, None] * k_t[..., None, :])
        y_t = jnp.einsum('bhvk,bhk->bhv', S, r_t)
        return S, y_t
    xs = tuple(jnp.moveaxis(t, 1, 0) for t in (r, w, k, v, kk, a))
    s_fin, ys = lax.scan(step, s0, xs, reverse=reverse)
    return jnp.moveaxis(ys, 0, 1), s_fin


def _mixer_layer(x, mod, s_fwd0, s_bwd0, norm_g, w_in, mu_shift, w0, w2, a0, a2,
                 k_k, k_a, r_k, gn_g, gn_b, w_fnet, b_fnet, w_out):
    B, T, _ = x.shape
    shift, scale, gate = jnp.split(mod, 3, axis=-1)
    h = _rmsnorm(x, norm_g) * (1 + scale[:, None]) + shift[:, None]
    proj = jnp.einsum('btd,de->bte', h, w_in)
    p_rec, g_rec, f_in, g_f = jnp.split(
        proj, [D_SHIFT, D_SHIFT + D_RWKV, D_SHIFT + D_RWKV + D_FNET], axis=-1)

    p_rec = (p_rec + mu_shift * (_centred_shift(p_rec) - p_rec)).astype(jnp.float32)
    r, k, v, lw, la = jnp.split(
        p_rec, [D_RWKV, 2 * D_RWKV, 3 * D_RWKV, 3 * D_RWKV + N_DIR * LORA_W], axis=-1)
    lw = lw.reshape(B, T, N_DIR, LORA_W)
    la = la.reshape(B, T, N_DIR, LORA_A)
    z_w = w0 + jnp.einsum('btdr,drc->btdc', jnp.tanh(lw), w2)
    decay = jnp.exp(-jnp.exp(-jax.nn.softplus(-z_w) - 0.5))
    a = jax.nn.sigmoid(a0 + jnp.einsum('btdr,drc->btdc', la, a2))
    k_dir = k[:, :, None] * (1 + (a - 1) * k_a)

    def heads(t):
        return t.reshape(t.shape[:-1] + (N_HEADS, HEAD_DIM))

    r_h, v_h = heads(r), heads(v)
    decay_h, a_h, k_h = heads(decay), heads(a), heads(k_dir)
    kk = heads(k * k_k)
    kk = kk / jnp.maximum(jnp.sqrt(jnp.sum(kk * kk, axis=-1, keepdims=True)), 1e-12)
    y_f, s_fwd = _rwkv7_scan(r_h, decay_h[:, :, 0], k_h[:, :, 0], v_h, kk, a_h[:, :, 0],
                             s_fwd0, False)
    y_b, s_bwd = _rwkv7_scan(r_h, decay_h[:, :, 1], k_h[:, :, 1], v_h, kk, a_h[:, :, 1],
                             s_bwd0, True)
    y = y_f + y_b
    mu = jnp.mean(y, axis=-1, keepdims=True)
    var = jnp.mean(jnp.square(y - mu), axis=-1, keepdims=True)
    y_n = (y - mu) * lax.rsqrt(var + GN_EPS) * gn_g + gn_b
    bonus = jnp.sum(r_h[:, :, None] * k_h * r_k, axis=(2, 4))[..., None] * v_h
    rec_out = (y_n + bonus).reshape(B, T, D_RWKV) * jax.nn.silu(g_rec.astype(jnp.float32))

    f = f_in.astype(jnp.float32).reshape(B, T, N_FGROUPS, FNET_GROUP)
    f_re = jnp.real(jnp.fft.fftn(f, axes=(1, 3), norm='ortho'))
    f_out = jnp.einsum('btgc,gce->btge', f_re, w_fnet.astype(jnp.float32)).reshape(B, T, D_FNET)
    f_out = (f_out + b_fnet) * jax.nn.silu(g_f.astype(jnp.float32))

    mixed = jnp.concatenate([rec_out, f_out], axis=-1).astype(x.dtype)
    out = jnp.einsum('btc,cd->btd', mixed, w_out)
    x = x + gate[:, None] * out
    return x, s_fwd, s_bwd


def setup_inputs(seed: int = 0) -> dict:
    key = jax.random.key(seed)
    ks = jax.random.split(key, 24)
    f32 = jnp.float32
    nrm = lambda k, s, sc: jax.random.normal(k, s, f32) * sc
    L = DEPTH
    return {
        "x_prompt": nrm(ks[0], (BATCH, SEQ, D_MODEL), 1.0),
        "x_sample": nrm(ks[1], (DEC_BATCH, DEC_SEQ, D_MODEL), 1.0),
        "state_rwkv_fwd": nrm(ks[2], (DEC_BATCH, L, N_HEADS, HEAD_DIM, HEAD_DIM), 0.3),
        "state_rwkv_bwd": nrm(ks[3], (DEC_BATCH, L, N_HEADS, HEAD_DIM, HEAD_DIM), 0.3),
        "c": nrm(ks[4], (DEC_BATCH, D_MODEL), 1.0),
        "c_ctx": nrm(ks[5], (D_MODEL,), 1.0),
        "w_ada": nrm(ks[6], (L, D_MODEL, 3 * D_MODEL), 0.5 * D_MODEL ** -0.5),
        "b_ada": nrm(ks[7], (L, 3 * D_MODEL), 0.01),
        "norm_g": 1.0 + nrm(ks[8], (L, D_MODEL), 0.01),
        "w_in": nrm(ks[9], (L, D_MODEL, D_IN), D_MODEL ** -0.5),
        "mu_shift": jax.random.uniform(ks[10], (L, D_SHIFT), f32),
        "w0": nrm(ks[11], (L, N_DIR, D_RWKV), 0.5),
        "w2": nrm(ks[12], (L, N_DIR, LORA_W, D_RWKV), 0.1),
        "a0": nrm(ks[13], (L, N_DIR, D_RWKV), 0.1),
        "a2": nrm(ks[14], (L, N_DIR, LORA_A, D_RWKV), 0.1),
        "k_k": 0.85 + nrm(ks[15], (L, D_RWKV), 0.02),
        "k_a": 1.0 + nrm(ks[16], (L, D_RWKV), 0.02),
        "r_k": nrm(ks[17], (L, N_HEADS, HEAD_DIM), 0.1),
        "gn_g": 1.0 + nrm(ks[18], (L, N_HEADS, HEAD_DIM), 0.01),
        "gn_b": nrm(ks[19], (L, N_HEADS, HEAD_DIM), 0.01),
        "w_fnet": nrm(ks[20], (L, N_FGROUPS, FNET_GROUP, FNET_GROUP), FNET_GROUP ** -0.5),
        "b_fnet": nrm(ks[21], (L, D_FNET), 0.01),
        "w_out": nrm(ks[22], (L, D_MODEL, D_MODEL), D_MODEL ** -0.5),
        "final_norm_g": 1.0 + nrm(ks[23], (D_MODEL,), 0.01),
    }


def reference(x_prompt, x_sample, state_rwkv_fwd, state_rwkv_bwd, c, c_ctx, w_ada, b_ada,
              norm_g, w_in, mu_shift, w0, w2, a0, a2, k_k, k_a, r_k, gn_g, gn_b,
              w_fnet, b_fnet, w_out, final_norm_g):
    xp = x_prompt
    bp = xp.shape[0]
    zero_state = jnp.zeros((bp, N_HEADS, HEAD_DIM, HEAD_DIM), jnp.float32)
    new_fwd, new_bwd = [], []
    for l in range(DEPTH):
        mod_ctx = (jax.nn.silu(c_ctx) @ w_ada[l] + b_ada[l])[None]
        xp, s_f, s_b = _mixer_layer(xp, mod_ctx, zero_state, zero_state, norm_g[l], w_in[l],
                                    mu_shift[l], w0[l], w2[l], a0[l], a2[l], k_k[l], k_a[l],
                                    r_k[l], gn_g[l], gn_b[l], w_fnet[l], b_fnet[l], w_out[l])
        new_fwd.append(s_f)
        new_bwd.append(s_b)
    y_prompt = _rmsnorm(xp, final_norm_g)
    new_state_rwkv_fwd = jnp.stack(new_fwd, axis=1)
    new_state_rwkv_bwd = jnp.stack(new_bwd, axis=1)

    xs = x_sample + _sincos_2d(x_sample.shape[1], x_sample.dtype)[None]
    for l in range(DEPTH):
        mod_lat = jax.nn.silu(c) @ w_ada[l] + b_ada[l]
        xs, _, _ = _mixer_layer(xs, mod_lat, state_rwkv_fwd[:, l].astype(jnp.float32),
                                state_rwkv_bwd[:, l].astype(jnp.float32), norm_g[l], w_in[l],
                                mu_shift[l], w0[l], w2[l], a0[l], a2[l], k_k[l], k_a[l],
                                r_k[l], gn_g[l], gn_b[l], w_fnet[l], b_fnet[l], w_out[l])
    y_sample = _rmsnorm(xs, final_norm_g)
    return (y_prompt, y_sample, new_state_rwkv_fwd, new_state_rwkv_bwd)
```

```python
import functools
import math

import numpy as np
import jax
import jax.numpy as jnp
from jax import lax
from jax.experimental import pallas as pl
from jax.experimental.pallas import tpu as pltpu

F32 = jnp.float32
BF16 = jnp.bfloat16

D_MODEL = 1024
GRID_W = 64
D_RWKV = 512
D_FNET = D_MODEL - D_RWKV
HEAD_DIM = 64
N_HEADS = D_RWKV // HEAD_DIM
FNET_GROUP = 64
LORA = 32
N_DIR = 2
D_SHIFT = 3 * D_RWKV + N_DIR * 2 * LORA
D_IN = D_SHIFT + D_RWKV + 2 * D_FNET
NORM_EPS = 1e-6
GN_EPS = 64e-5
POS_BASE = 10000.0

LANES = 128
PAIR = LANES // HEAD_DIM
N_PAIRS = N_HEADS // PAIR
CHUNK = 64
ROW_TILE = 256
HALO = 8
VMEM_LIMIT = 48 * 1024 * 1024


def _silu(x):
    return x * jax.nn.sigmoid(x)


def _bdot(a, b):
    return jnp.dot(a.astype(BF16), b.astype(BF16), preferred_element_type=F32)


def _bdot_nt(a, b):
    return lax.dot_general(a.astype(BF16), b.astype(BF16), (((1,), (1,)), ((), ())),
                           preferred_element_type=F32)


def _split2(x):
    hi = x.astype(BF16)
    lo = (x - hi.astype(F32)).astype(BF16)
    return hi, lo


def _split3(x):
    hi = x.astype(BF16)
    r1 = x - hi.astype(F32)
    mid = r1.astype(BF16)
    lo = (r1 - mid.astype(F32)).astype(BF16)
    return hi, mid, lo


def _dot_exact_rhs(x, m_bf16):
    hi, lo = _split2(x)
    return (jnp.dot(hi, m_bf16, preferred_element_type=F32)
            + jnp.dot(lo, m_bf16, preferred_element_type=F32))


def _dot3(a, b_hi, b_lo):
    a_hi, a_lo = _split2(a)
    return (jnp.dot(a_hi, b_hi, preferred_element_type=F32)
            + jnp.dot(a_hi, b_lo, preferred_element_type=F32)
            + jnp.dot(a_lo, b_hi, preferred_element_type=F32))


def _mod_kernel(c_ref, w_ref, b_ref, o_ref):
    s = _silu(c_ref[...])
    w = w_ref[...]
    w_hi, w_lo = _split2(w)
    o_ref[...] = _dot3(s, w_hi, w_lo) + b_ref[...]


def _mod_call(cvec, w_ada, b_ada):
    n_blk = 3
    return pl.pallas_call(
        _mod_kernel,
        grid=(n_blk,),
        in_specs=[pl.BlockSpec((8, D_MODEL), lambda i: (0, 0)),
                  pl.BlockSpec((D_MODEL, D_MODEL), lambda i: (0, i)),
                  pl.BlockSpec((1, D_MODEL), lambda i: (0, i))],
        out_specs=pl.BlockSpec((8, D_MODEL), lambda i: (0, i)),
        out_shape=jax.ShapeDtypeStruct((8, 3 * D_MODEL), F32),
        compiler_params=pltpu.CompilerParams(dimension_semantics=("arbitrary",),
                                             vmem_limit_bytes=VMEM_LIMIT),
        name="mod",
    )(cvec, w_ada, b_ada)


def _modulated_norm(x, g, scale, shift):
    ms = jnp.mean(x * x, axis=-1, keepdims=True)
    y = x * lax.rsqrt(ms + NORM_EPS) * g
    return y * (1.0 + scale) + shift


def _in_proj_kernel(*refs, has_emb, has_halo, n_tiles):
    it = iter(refs)
    x_ref = next(it)
    xp_ref = next(it) if has_halo else None
    xn_ref = next(it) if has_halo else None
    emb_ref = next(it) if has_emb else None
    embp_ref = next(it) if (has_emb and has_halo) else None
    embn_ref = next(it) if (has_emb and has_halo) else None
    mod_ref, g_ref, w_ref, mu_ref = next(it), next(it), next(it), next(it)
    rkv_ref, lwla_ref, grec_ref, fin_ref, gf_ref = next(it), next(it), next(it), next(it), next(it)

    i = pl.program_id(1)
    g = g_ref[...]
    shift = mod_ref[0, 0:1, :]
    scale = mod_ref[0, 1:2, :]
    x = x_ref[0]
    if has_emb:
        x = x + emb_ref[...]
    tm = x.shape[0]
    h = _modulated_norm(x, g, scale, shift)
    p = jnp.dot(h.astype(BF16), w_ref[...], preferred_element_type=F32)
    ps = p[:, :D_SHIFT]
    if has_halo:
        xh = jnp.concatenate([xp_ref[0], xn_ref[0]], axis=0)
        if has_emb:
            xh = xh + jnp.concatenate([embp_ref[...], embn_ref[...]], axis=0)
        hh = _modulated_norm(xh, g, scale, shift)
        ph = jnp.dot(hh.astype(BF16), w_ref[:, :D_SHIFT], preferred_element_type=F32)
        prev_row = jnp.where(i > 0, ph[HALO - 1:HALO, :], 0.0)
        next_row = jnp.where(i < n_tiles - 1, ph[HALO:HALO + 1, :], 0.0)
    else:
        prev_row = jnp.zeros((1, D_SHIFT), F32)
        next_row = jnp.zeros((1, D_SHIFT), F32)
    row = lax.broadcasted_iota(jnp.int32, (tm, D_SHIFT), 0)
    prev = jnp.where(row == 0, prev_row, pltpu.roll(ps, 1, 0))
    nxt = jnp.where(row == tm - 1, next_row, pltpu.roll(ps, tm - 1, 0))
    p_rec = ps + mu_ref[...] * (0.5 * (prev + nxt) - ps)
    rkv_ref[0] = p_rec[:, :3 * D_RWKV]
    lwla_ref[0] = p_rec[:, 3 * D_RWKV:]
    grec_ref[0] = p[:, D_SHIFT:D_SHIFT + D_RWKV]
    fin_ref[0] = p[:, D_SHIFT + D_RWKV:D_SHIFT + D_RWKV + D_FNET]
    gf_ref[0] = p[:, D_SHIFT + D_RWKV + D_FNET:]


def _in_proj_call(x, emb, mod, norm_g, w_in_bf16, mu):
    B, T, _ = x.shape
    tm = ROW_TILE
    n_tiles = T // tm
    has_halo = n_tiles > 1
    has_emb = emb is not None
    per_batch_mod = mod.shape[0] > 1
    blocks_per_tile = tm // HALO
    last_halo_block = T // HALO - 1

    in_specs = [pl.BlockSpec((1, tm, D_MODEL), lambda b, i: (b, i, 0))]
    args = [x]
    if has_halo:
        in_specs += [
            pl.BlockSpec((1, HALO, D_MODEL), lambda b, i: (b, jnp.maximum(i * blocks_per_tile - 1, 0), 0)),
            pl.BlockSpec((1, HALO, D_MODEL),
                         lambda b, i: (b, jnp.minimum((i + 1) * blocks_per_tile, last_halo_block), 0))]
        args += [x, x]
    if has_emb:
        in_specs.append(pl.BlockSpec((tm, D_MODEL), lambda b, i: (i, 0)))
        args.append(emb)
        if has_halo:
            in_specs += [
                pl.BlockSpec((HALO, D_MODEL), lambda b, i: (jnp.maximum(i * blocks_per_tile - 1, 0), 0)),
                pl.BlockSpec((HALO, D_MODEL),
                             lambda b, i: (jnp.minimum((i + 1) * blocks_per_tile, last_halo_block), 0))]
            args += [emb, emb]
    mod_map = (lambda b, i: (b, 0, 0)) if per_batch_mod else (lambda b, i: (0, 0, 0))
    in_specs += [pl.BlockSpec((1, 3, D_MODEL), mod_map),
                 pl.BlockSpec((1, D_MODEL), lambda b, i: (0, 0)),
                 pl.BlockSpec((D_MODEL, D_IN), lambda b, i: (0, 0)),
                 pl.BlockSpec((1, D_SHIFT), lambda b, i: (0, 0))]
    args += [mod, norm_g, w_in_bf16, mu]
    widths = (3 * D_RWKV, N_DIR * 2 * LORA, D_RWKV, D_FNET, D_FNET)
    out_specs = [pl.BlockSpec((1, tm, w), lambda b, i: (b, i, 0)) for w in widths]
    out_shape = [jax.ShapeDtypeStruct((B, T, w), F32) for w in widths]
    kern = functools.partial(_in_proj_kernel, has_emb=has_emb, has_halo=has_halo, n_tiles=n_tiles)
    return pl.pallas_call(
        kern, grid=(B, n_tiles), in_specs=in_specs, out_specs=out_specs, out_shape=out_shape,
        compiler_params=pltpu.CompilerParams(dimension_semantics=("arbitrary", "arbitrary"),
                                             vmem_limit_bytes=VMEM_LIMIT),
        name="in_proj",
    )(*args)


def _block_diag_rows(y, head0_lanes):
    return jnp.concatenate([jnp.where(head0_lanes, y, 0.0), jnp.where(head0_lanes, 0.0, y)], axis=0)


def _scan_kernel(*refs, zero_init, write_state, n_chunks):
    it = iter(refs)
    rkv_refs = [next(it), None]
    lwla_refs = [next(it), None]
    rkv_refs[1] = next(it)
    lwla_refs[1] = next(it)
    w0_ref, w2hi_ref, w2lo_ref, a0_ref, a2hi_ref, a2lo_ref = (next(it) for _ in range(6))
    kk_ref, ka_ref, tri_ref, ones_ref = (next(it) for _ in range(4))
    s0_ref = None if zero_init else next(it)
    y_refs = [next(it), next(it)]
    sfin_ref = next(it) if write_state else None
    h_scr = next(it)

    j = pl.program_id(1)

    @pl.when(j == 0)
    def _():
        if zero_init:
            h_scr[...] = jnp.zeros_like(h_scr)
        else:
            h_scr[...] = s0_ref[0]

    C = CHUNK
    row = lax.broadcasted_iota(jnp.int32, (C, LANES), 0)
    lane = lax.broadcasted_iota(jnp.int32, (C, LANES), 1)
    s_idx = jnp.bitwise_and(lane, HEAD_DIM - 1)
    head0 = lane < HEAD_DIM
    eye_pair = (row == s_idx).astype(F32)
    row2 = lax.broadcasted_iota(jnp.int32, (LANES, LANES), 0)
    lane2 = lax.broadcasted_iota(jnp.int32, (LANES, LANES), 1)
    same_head = (row2 < HEAD_DIM) == (lane2 < HEAD_DIM)
    eye2 = row2 == lane2
    decay_scale = math.exp(-0.5)
    n_double = int(math.log2(C)) - 1

    for d in range(N_DIR):
        rkv = rkv_refs[d][0]
        r = rkv[:, :D_RWKV]
        k = rkv[:, D_RWKV:2 * D_RWKV]
        v = rkv[:, 2 * D_RWKV:]
        ll = lwla_refs[d][0]
        z_w = w0_ref[d:d + 1, :] + _dot3(jnp.tanh(ll), w2hi_ref[d], w2lo_ref[d])
        logw = -decay_scale * jax.nn.sigmoid(z_w)
        a = jax.nn.sigmoid(a0_ref[d:d + 1, :] + _dot3(ll, a2hi_ref[d], a2lo_ref[d]))
        kd = k * (1.0 + (a - 1.0) * ka_ref[...])
        kkr = k * kk_ref[...]
        ssq = _dot_exact_rhs(kkr * kkr, ones_ref[...])
        kk = kkr * lax.rsqrt(jnp.maximum(ssq, 1e-24))
        bvec = kk * a
        lw_hi, lw_mid, lw_lo = _split3(logw)
        tri = tri_ref[d]
        cum = (jnp.dot(tri, lw_hi, preferred_element_type=F32)
               + jnp.dot(tri, lw_mid, preferred_element_type=F32)
               + jnp.dot(tri, lw_lo, preferred_element_type=F32))
        cum_prev = cum - logw
        tot = cum[C - 1:C, :] if d == 0 else cum[0:1, :]
        kap_t = kk * jnp.exp(cum_prev)
        r_t = r * jnp.exp(cum)
        e_neg = jnp.exp(-cum)
        b_t = bvec * e_neg
        k_t = kd * e_neg
        e_rem = jnp.exp(tot - cum)
        b_h = bvec * e_rem
        k_h = kd * e_rem
        gam = jnp.exp(tot)
        if d == 0:
            strict, incl = row > s_idx, row >= s_idx
        else:
            strict, incl = row < s_idx, row <= s_idx

        for p in range(N_PAIRS):
            sl = slice(p * LANES, (p + 1) * LANES)
            kap_p, r_p, v_p = kap_t[:, sl], r_t[:, sl], v[:, sl]
            lhs = jnp.concatenate([kap_p, r_p], axis=0)
            rhs = jnp.concatenate([_block_diag_rows(b_t[:, sl], head0),
                                   _block_diag_rows(k_t[:, sl], head0)], axis=0)
            lm = _bdot_nt(lhs, rhs)
            l_b = jnp.where(strict, lm[:C, :LANES], 0.0)
            m_b = jnp.where(incl, lm[C:, :LANES], 0.0)
            l_k = jnp.where(strict, lm[:C, LANES:], 0.0)
            m_k = jnp.where(incl, lm[C:, LANES:], 0.0)
            lmv = _bdot(jnp.concatenate([l_k, m_k], axis=0), _block_diag_rows(v_p, head0))
            l_kv, m_kv = lmv[:C], lmv[C:]
            pw = -l_b
            tinv = eye_pair + pw
            for _ in range(n_double):
                pw = _bdot(pw, _block_diag_rows(pw, head0))
                tinv = tinv + _bdot(tinv, _block_diag_rows(pw, head0))
            wu = _bdot(tinv, jnp.concatenate([_block_diag_rows(kap_p, head0),
                                              _block_diag_rows(l_kv, head0)], axis=1))
            w_m, u_n = wu[:, :LANES], wu[:, LANES:]
            qy = _bdot(m_b, jnp.concatenate([_block_diag_rows(w_m, head0),
                                             _block_diag_rows(u_n, head0)], axis=1))
            r_q = r_p - qy[:, :LANES]
            y_i = m_kv - qy[:, LANES:]
            t2 = jnp.concatenate([b_h[:, sl], k_h[:, sl]], axis=0).T
            ag = _bdot(t2, jnp.concatenate(
                [jnp.concatenate([w_m, -u_n], axis=1),
                 jnp.concatenate([jnp.zeros_like(v_p), v_p], axis=1)], axis=0))
            gam_p = jnp.broadcast_to(gam[:, sl], (LANES, LANES))
            a_bd = jnp.where(same_head, jnp.where(eye2, gam_p, 0.0) - ag[:, :LANES], 0.0)
            g_bd = jnp.where(same_head, ag[:, LANES:], 0.0)
            h_old = h_scr[d, p]
            h_hi, h_lo = _split2(h_old)
            lhs2 = jnp.concatenate([r_q, a_bd], axis=0).astype(BF16)
            yh = (jnp.dot(lhs2, h_hi, preferred_element_type=F32)
                  + jnp.dot(lhs2, h_lo, preferred_element_type=F32))
            y_refs[d][0, :, sl] = y_i + yh[:C]
            h_scr[d, p] = yh[C:] + g_bd

    if write_state:
        @pl.when(j == n_chunks - 1)
        def _():
            sfin_ref[0] = h_scr[...]


def _scan_call(rkv, lwla, wts, s0_bd, write_state):
    B, T, _ = rkv.shape
    nc = T // CHUNK
    zero_init = s0_bd is None
    fwd = lambda b, j: (b, j, 0)
    bwd = lambda b, j: (b, nc - 1 - j, 0)
    full = lambda *shape: pl.BlockSpec(shape, lambda b, j: (0,) * len(shape))
    in_specs = [pl.BlockSpec((1, CHUNK, 3 * D_RWKV), fwd), pl.BlockSpec((1, CHUNK, LANES), fwd),
                pl.BlockSpec((1, CHUNK, 3 * D_RWKV), bwd), pl.BlockSpec((1, CHUNK, LANES), bwd),
                full(N_DIR, D_RWKV), full(N_DIR, LANES, D_RWKV), full(N_DIR, LANES, D_RWKV),
                full(N_DIR, D_RWKV), full(N_DIR, LANES, D_RWKV), full(N_DIR, LANES, D_RWKV),
                full(1, D_RWKV), full(1, D_RWKV), full(N_DIR, CHUNK, CHUNK), full(D_RWKV, D_RWKV)]
    args = [rkv, lwla, rkv, lwla, wts["w0"], wts["w2_hi"], wts["w2_lo"], wts["a0"], wts["a2_hi"],
            wts["a2_lo"], wts["k_k"], wts["k_a"], wts["tri"], wts["ones_bd"]]
    state_block = (1, N_DIR, N_PAIRS, LANES, LANES)
    if not zero_init:
        in_specs.append(pl.BlockSpec(state_block, lambda b, j: (b, 0, 0, 0, 0)))
        args.append(s0_bd)
    out_specs = [pl.BlockSpec((1, CHUNK, D_RWKV), fwd), pl.BlockSpec((1, CHUNK, D_RWKV), bwd)]
    out_shape = [jax.ShapeDtypeStruct((B, T, D_RWKV), F32), jax.ShapeDtypeStruct((B, T, D_RWKV), F32)]
    if write_state:
        out_specs.append(pl.BlockSpec(state_block, lambda b, j: (b, 0, 0, 0, 0)))
        out_shape.append(jax.ShapeDtypeStruct((B,) + state_block[1:], F32))
    kern = functools.partial(_scan_kernel, zero_init=zero_init, write_state=write_state, n_chunks=nc)
    return pl.pallas_call(
        kern, grid=(B, nc), in_specs=in_specs, out_specs=out_specs, out_shape=out_shape,
        scratch_shapes=[pltpu.VMEM((N_DIR, N_PAIRS, LANES, LANES), F32)],
        compiler_params=pltpu.CompilerParams(dimension_semantics=("arbitrary", "arbitrary"),
                                             vmem_limit_bytes=VMEM_LIMIT),
        name="scan",
    )(*args)


def _fnet_kernel(fin_ref, gf_ref, dft_ref, cs_ref, wbd_ref, b_ref, o_ref, g_scr, *, seq_len):
    u = pl.program_id(1)

    @pl.when(u == 0)
    def _():
        for t0 in range(0, seq_len, ROW_TILE):
            fc = _bdot(fin_ref[0, t0:t0 + ROW_TILE, :], cs_ref[...])
            g_scr[t0:t0 + ROW_TILE, :] = fc[:, :D_FNET].astype(BF16)
            g_scr[seq_len + t0:seq_len + t0 + ROW_TILE, :] = fc[:, D_FNET:].astype(BF16)

    f_re = jnp.dot(dft_ref[...], g_scr[...], preferred_element_type=F32)
    f_out = _bdot(f_re, wbd_ref[...]) + b_ref[...]
    o_ref[0] = (f_out * _silu(gf_ref[0])).astype(BF16)


def _fnet_call(fin, gf, dft_bf16, cs_bf16, wbd_bf16, b_fnet):
    B, T, _ = fin.shape
    tu = ROW_TILE
    return pl.pallas_call(
        functools.partial(_fnet_kernel, seq_len=T),
        grid=(B, T // tu),
        in_specs=[pl.BlockSpec((1, T, D_FNET), lambda b, u: (b, 0, 0)),
                  pl.BlockSpec((1, tu, D_FNET), lambda b, u: (b, u, 0)),
                  pl.BlockSpec((tu, 2 * T), lambda b, u: (u, 0)),
                  pl.BlockSpec((D_FNET, 2 * D_FNET), lambda b, u: (0, 0)),
                  pl.BlockSpec((D_FNET, D_FNET), lambda b, u: (0, 0)),
                  pl.BlockSpec((1, D_FNET), lambda b, u: (0, 0))],
        out_specs=pl.BlockSpec((1, tu, D_FNET), lambda b, u: (b, u, 0)),
        out_shape=jax.ShapeDtypeStruct((B, T, D_FNET), BF16),
        scratch_shapes=[pltpu.VMEM((2 * T, D_FNET), BF16)],
        compiler_params=pltpu.CompilerParams(dimension_semantics=("arbitrary", "arbitrary"),
                                             vmem_limit_bytes=VMEM_LIMIT),
        name="fnet",
    )(fin, gf, dft_bf16, cs_bf16, wbd_bf16, b_fnet)


def _out_kernel(*refs, has_emb, final_norm):
    it = iter(refs)
    x_ref = next(it)
    emb_ref = next(it) if has_emb else None
    (mod_ref, yf_ref, yb_ref, rkv_ref, lwla_ref, grec_ref, fo_ref, a0_ref, a2hi_ref, a2lo_ref, ka_ref,
     rk_ref, gng_ref, gnb_ref, avg_ref, ones_ref, wout_ref, fng_ref, o_ref) = (next(it) for _ in range(19))

    y = yf_ref[0] + yb_ref[0]
    mu = _dot_exact_rhs(y, avg_ref[...])
    dlt = y - mu
    var = _dot_exact_rhs(dlt * dlt, avg_ref[...])
    y_n = dlt * lax.rsqrt(var + GN_EPS) * gng_ref[...] + gnb_ref[...]
    rkv = rkv_ref[0]
    r = rkv[:, :D_RWKV]
    k = rkv[:, D_RWKV:2 * D_RWKV]
    v = rkv[:, 2 * D_RWKV:]
    ll = lwla_ref[0]
    a_sum = (jax.nn.sigmoid(a0_ref[0:1, :] + _dot3(ll, a2hi_ref[0], a2lo_ref[0]))
             + jax.nn.sigmoid(a0_ref[1:2, :] + _dot3(ll, a2hi_ref[1], a2lo_ref[1])))
    k_sum = k * (2.0 + (a_sum - 2.0) * ka_ref[...])
    bonus = _dot_exact_rhs(r * k_sum * rk_ref[...], ones_ref[...]) * v
    rec_out = (y_n + bonus) * _silu(grec_ref[0])
    mixed = jnp.concatenate([rec_out.astype(BF16), fo_ref[0]], axis=-1)
    out = jnp.dot(mixed, wout_ref[...], preferred_element_type=F32)
    x = x_ref[0]
    if has_emb:
        x = x + emb_ref[...]
    z = x + mod_ref[0, 2:3, :] * out
    if final_norm:
        ms = jnp.mean(z * z, axis=-1, keepdims=True)
        z = z * lax.rsqrt(ms + NORM_EPS) * fng_ref[...]
    o_ref[0] = z


def _out_call(x, emb, mod, y_f, y_b, rkv, lwla, grec, fo, wts, w_out_bf16, final_norm_g, final_norm):
    B, T, _ = x.shape
    tm = ROW_TILE
    has_emb = emb is not None
    per_batch_mod = mod.shape[0] > 1
    tok = lambda w: pl.BlockSpec((1, tm, w), lambda b, i: (b, i, 0))
    full = lambda *shape: pl.BlockSpec(shape, lambda b, i: (0,) * len(shape))
    in_specs = [tok(D_MODEL)]
    args = [x]
    if has_emb:
        in_specs.append(pl.BlockSpec((tm, D_MODEL), lambda b, i: (i, 0)))
        args.append(emb)
    mod_map = (lambda b, i: (b, 0, 0)) if per_batch_mod else (lambda b, i: (0, 0, 0))
    in_specs += [pl.BlockSpec((1, 3, D_MODEL), mod_map), tok(D_RWKV), tok(D_RWKV), tok(3 * D_RWKV),
                 tok(LANES), tok(D_RWKV), tok(D_FNET),
                 full(N_DIR, D_RWKV), full(N_DIR, LANES, D_RWKV), full(N_DIR, LANES, D_RWKV),
                 full(1, D_RWKV), full(1, D_RWKV), full(1, D_RWKV), full(1, D_RWKV),
                 full(D_RWKV, D_RWKV), full(D_RWKV, D_RWKV), full(D_MODEL, D_MODEL), full(1, D_MODEL)]
    args += [mod, y_f, y_b, rkv, lwla, grec, fo, wts["a0"], wts["a2_hi"], wts["a2_lo"], wts["k_a"],
             wts["r_k"], wts["gn_g"], wts["gn_b"], wts["avg_bd"], wts["ones_bd"], w_out_bf16, final_norm_g]
    return pl.pallas_call(
        functools.partial(_out_kernel, has_emb=has_emb, final_norm=final_norm),
        grid=(B, T // tm), in_specs=in_specs,
        out_specs=pl.BlockSpec((1, tm, D_MODEL), lambda b, i: (b, i, 0)),
        out_shape=jax.ShapeDtypeStruct((B, T, D_MODEL), F32),
        compiler_params=pltpu.CompilerParams(dimension_semantics=("arbitrary", "arbitrary"),
                                             vmem_limit_bytes=VMEM_LIMIT),
        name="out_proj",
    )(*args)


def _dft_table(seq_len):
    idx = np.arange(seq_len, dtype=np.int64)
    ang = 2.0 * np.pi * ((idx[:, None] * idx[None, :]) % seq_len).astype(np.float64) / seq_len
    scale = 1.0 / math.sqrt(seq_len)
    return np.concatenate([np.cos(ang) * scale, -np.sin(ang) * scale], axis=1).astype(np.float32)


def _channel_dft_table():
    n = FNET_GROUP
    idx = np.arange(n, dtype=np.int64)
    ang = 2.0 * np.pi * ((idx[:, None] * idx[None, :]) % n).astype(np.float64) / n
    c = np.cos(ang) / math.sqrt(n)
    s = np.sin(ang) / math.sqrt(n)
    eye = np.eye(D_FNET // n)
    return np.concatenate([np.kron(eye, c), np.kron(eye, s)], axis=1).astype(np.float32)


def _sincos_2d(n_tokens, dtype):
    rows = n_tokens // GRID_W
    pos = jnp.arange(rows * GRID_W)
    row = (pos // GRID_W).astype(F32)
    col = (pos % GRID_W).astype(F32)
    quarter = D_MODEL // 4
    freq = jnp.exp(-math.log(POS_BASE) * jnp.arange(quarter, dtype=F32) / quarter)
    ang_r = row[:, None] * freq
    ang_c = col[:, None] * freq
    emb = jnp.concatenate([jnp.sin(ang_r), jnp.cos(ang_r), jnp.sin(ang_c), jnp.cos(ang_c)], axis=-1)
    return emb.astype(dtype)


def _head_block_matrix(value):
    blk = np.kron(np.eye(N_HEADS), np.ones((HEAD_DIM, HEAD_DIM))) * value
    return jnp.asarray(blk, dtype=BF16)


def _pad_lora(w, row_offset):
    out = jnp.zeros((N_DIR, LANES, w.shape[-1]), F32)
    for d in range(N_DIR):
        out = out.at[d, row_offset + d * LORA:row_offset + (d + 1) * LORA, :].set(w[d])
    hi = out.astype(BF16)
    lo = (out - hi.astype(F32)).astype(BF16)
    return hi, lo


def _layer_weights(l, w0, w2, a0, a2, k_k, k_a, r_k, gn_g, gn_b):
    w2_hi, w2_lo = _pad_lora(w2[l], 0)
    a2_hi, a2_lo = _pad_lora(a2[l], N_DIR * LORA)
    tri_f = np.tril(np.ones((CHUNK, CHUNK)))
    tri = jnp.asarray(np.stack([tri_f, tri_f.T]), dtype=BF16)
    return dict(w0=w0[l], w2_hi=w2_hi, w2_lo=w2_lo, a0=a0[l], a2_hi=a2_hi, a2_lo=a2_lo,
                k_k=k_k[l][None], k_a=k_a[l][None], r_k=r_k[l].reshape(1, D_RWKV),
                gn_g=gn_g[l].reshape(1, D_RWKV), gn_b=gn_b[l].reshape(1, D_RWKV), tri=tri,
                ones_bd=_head_block_matrix(1.0), avg_bd=_head_block_matrix(1.0 / HEAD_DIM))


def _state_to_block_diag(s_f, s_b):
    def one(s):
        h = jnp.swapaxes(s.astype(F32), -1, -2)
        b = h.shape[0]
        h = h.reshape(b, N_PAIRS, PAIR, HEAD_DIM, HEAD_DIM)
        z = jnp.zeros_like(h[:, :, 0])
        top = jnp.concatenate([h[:, :, 0], z], axis=-1)
        bot = jnp.concatenate([z, h[:, :, 1]], axis=-1)
        return jnp.concatenate([top, bot], axis=-2)
    return jnp.stack([one(s_f), one(s_b)], axis=1)


def _block_diag_to_state(s_bd):
    b = s_bd.shape[0]
    h0 = s_bd[..., :HEAD_DIM, :HEAD_DIM]
    h1 = s_bd[..., HEAD_DIM:, HEAD_DIM:]
    h = jnp.stack([h0, h1], axis=3).reshape(b, N_DIR, N_HEADS, HEAD_DIM, HEAD_DIM)
    s = jnp.swapaxes(h, -1, -2)
    return s[:, 0], s[:, 1]


def kernel(x_prompt, x_sample, state_rwkv_fwd, state_rwkv_bwd, c, c_ctx, w_ada, b_ada, norm_g, w_in,
           mu_shift, w0, w2, a0, a2, k_k, k_a, r_k, gn_g, gn_b, w_fnet, b_fnet, w_out, final_norm_g):
    depth = w_in.shape[0]
    n_dec = c.shape[0]
    assert n_dec + 1 <= 8
    bp, tp, _ = x_prompt.shape
    bs, ts, _ = x_sample.shape
    cvec = jnp.zeros((8, D_MODEL), F32).at[0].set(c_ctx).at[1:1 + n_dec].set(c)
    emb = _sincos_2d(ts, x_sample.dtype)
    cs_tab = jnp.asarray(_channel_dft_table()).astype(BF16)
    dft_p = jnp.asarray(_dft_table(tp)).astype(BF16)
    dft_s = jnp.asarray(_dft_table(ts)).astype(BF16)
    fng = final_norm_g[None]

    xp, xs = x_prompt, x_sample
    new_f, new_b = [], []
    for l in range(depth):
        mod = _mod_call(cvec, w_ada[l], b_ada[l][None]).reshape(8, 3, D_MODEL)
        mod_ctx, mod_lat = mod[0:1], mod[1:1 + n_dec]
        wts = _layer_weights(l, w0, w2, a0, a2, k_k, k_a, r_k, gn_g, gn_b)
        w_in_b = w_in[l].astype(BF16)
        w_out_b = w_out[l].astype(BF16)
        wbd = jax.scipy.linalg.block_diag(*[w_fnet[l, g] for g in range(w_fnet.shape[1])]).astype(BF16)
        ng, mu, bf = norm_g[l][None], mu_shift[l][None], b_fnet[l][None]
        emb_l = emb if l == 0 else None

        rkv, lwla, grec, fin, gf = _in_proj_call(xp, None, mod_ctx, ng, w_in_b, mu)
        y_f, y_b, s_bd = _scan_call(rkv, lwla, wts, None, True)
        fo = _fnet_call(fin, gf, dft_p, cs_tab, wbd, bf)
        last = l == depth - 1
        xp = _out_call(xp, None, mod_ctx, y_f, y_b, rkv, lwla, grec, fo, wts, w_out_b, fng, last)
        s_f, s_b = _block_diag_to_state(s_bd)
        new_f.append(s_f)
        new_b.append(s_b)

        rkv, lwla, grec, fin, gf = _in_proj_call(xs, emb_l, mod_lat, ng, w_in_b, mu)
        s0 = _state_to_block_diag(state_rwkv_fwd[:, l], state_rwkv_bwd[:, l])
        y_f, y_b = _scan_call(rkv, lwla, wts, s0, False)
        fo = _fnet_call(fin, gf, dft_s, cs_tab, wbd, bf)
        xs = _out_call(xs, emb_l, mod_lat, y_f, y_b, rkv, lwla, grec, fo, wts, w_out_b, fng, last)
    return (xp, xs, jnp.stack(new_f, axis=1), jnp.stack(new_b, axis=1))
```

```python
import functools
import math

import numpy as np
import jax
import jax.numpy as jnp
from jax import lax
from jax.experimental import pallas as pl
from jax.experimental.pallas import tpu as pltpu

F32 = jnp.float32
BF16 = jnp.bfloat16

D_MODEL = 1024
GRID_W = 64
D_RWKV = 512
D_FNET = D_MODEL - D_RWKV
HEAD_DIM = 64
N_HEADS = D_RWKV // HEAD_DIM
FNET_GROUP = 64
LORA = 32
N_DIR = 2
D_SHIFT = 3 * D_RWKV + N_DIR * 2 * LORA
D_IN = D_SHIFT + D_RWKV + 2 * D_FNET
NORM_EPS = 1e-6
GN_EPS = 64e-5
POS_BASE = 10000.0

LANES = 128
PAIR = LANES // HEAD_DIM
N_PAIRS = N_HEADS // PAIR
CHUNK = 64
ROW_TILE = 256
HALO = 8
VMEM_LIMIT = 48 * 1024 * 1024


def _silu(x):
    return x * jax.nn.sigmoid(x)


def _bdot(a, b):
    return jnp.dot(a.astype(BF16), b.astype(BF16), preferred_element_type=F32)


def _bdot_nt(a, b):
    return lax.dot_general(a.astype(BF16), b.astype(BF16), (((1,), (1,)), ((), ())),
                           preferred_element_type=F32)


def _split2(x):
    hi = x.astype(BF16)
    lo = (x - hi.astype(F32)).astype(BF16)
    return hi, lo


def _split3(x):
    hi = x.astype(BF16)
    r1 = x - hi.astype(F32)
    mid = r1.astype(BF16)
    lo = (r1 - mid.astype(F32)).astype(BF16)
    return hi, mid, lo


def _dot_exact_rhs(x, m_bf16):
    hi, lo = _split2(x)
    return (jnp.dot(hi, m_bf16, preferred_element_type=F32)
            + jnp.dot(lo, m_bf16, preferred_element_type=F32))


def _dot3(a, b_hi, b_lo):
    a_hi, a_lo = _split2(a)
    return (jnp.dot(a_hi, b_hi, preferred_element_type=F32)
            + jnp.dot(a_hi, b_lo, preferred_element_type=F32)
            + jnp.dot(a_lo, b_hi, preferred_element_type=F32))


def _mod_kernel(c_ref, w_ref, b_ref, o_ref):
    s = _silu(c_ref[...])
    w = w_ref[...]
    w_hi, w_lo = _split2(w)
    o_ref[...] = _dot3(s, w_hi, w_lo) + b_ref[...]


def _mod_call(cvec, w_ada, b_ada):
    n_blk = 3
    return pl.pallas_call(
        _mod_kernel,
        grid=(n_blk,),
        in_specs=[pl.BlockSpec((8, D_MODEL), lambda i: (0, 0)),
                  pl.BlockSpec((D_MODEL, D_MODEL), lambda i: (0, i)),
                  pl.BlockSpec((1, D_MODEL), lambda i: (0, i))],
        out_specs=pl.BlockSpec((8, D_MODEL), lambda i: (0, i)),
        out_shape=jax.ShapeDtypeStruct((8, 3 * D_MODEL), F32),
        compiler_params=pltpu.CompilerParams(dimension_semantics=("arbitrary",),
                                             vmem_limit_bytes=VMEM_LIMIT),
        name="mod",
    )(cvec, w_ada, b_ada)


def _modulated_norm(x, g, scale, shift):
    ms = jnp.mean(x * x, axis=-1, keepdims=True)
    y = x * lax.rsqrt(ms + NORM_EPS) * g
    return y * (1.0 + scale) + shift


def _in_proj_kernel(*refs, has_emb, has_halo, n_tiles):
    it = iter(refs)
    x_ref = next(it)
    xp_ref = next(it) if has_halo else None
    xn_ref = next(it) if has_halo else None
    emb_ref = next(it) if has_emb else None
    embp_ref = next(it) if (has_emb and has_halo) else None
    embn_ref = next(it) if (has_emb and has_halo) else None
    mod_ref, g_ref, w_ref, mu_ref = next(it), next(it), next(it), next(it)
    rkv_ref, lwla_ref, grec_ref, fin_ref, gf_ref = next(it), next(it), next(it), next(it), next(it)

    i = pl.program_id(1)
    g = g_ref[...]
    shift = mod_ref[0, 0:1, :]
    scale = mod_ref[0, 1:2, :]
    x = x_ref[0]
    if has_emb:
        x = x + emb_ref[...]
    tm = x.shape[0]
    h = _modulated_norm(x, g, scale, shift)
    p = jnp.dot(h.astype(BF16), w_ref[...], preferred_element_type=F32)
    ps = p[:, :D_SHIFT]
    if has_halo:
        xh = jnp.concatenate([xp_ref[0], xn_ref[0]], axis=0)
        if has_emb:
            xh = xh + jnp.concatenate([embp_ref[...], embn_ref[...]], axis=0)
        hh = _modulated_norm(xh, g, scale, shift)
        ph = jnp.dot(hh.astype(BF16), w_ref[:, :D_SHIFT], preferred_element_type=F32)
        prev_row = jnp.where(i > 0, ph[HALO - 1:HALO, :], 0.0)
        next_row = jnp.where(i < n_tiles - 1, ph[HALO:HALO + 1, :], 0.0)
    else:
        prev_row = jnp.zeros((1, D_SHIFT), F32)
        next_row = jnp.zeros((1, D_SHIFT), F32)
    row = lax.broadcasted_iota(jnp.int32, (tm, D_SHIFT), 0)
    prev = jnp.where(row == 0, prev_row, pltpu.roll(ps, 1, 0))
    nxt = jnp.where(row == tm - 1, next_row, pltpu.roll(ps, tm - 1, 0))
    p_rec = ps + mu_ref[...] * (0.5 * (prev + nxt) - ps)
    rkv_ref[0] = p_rec[:, :3 * D_RWKV]
    lwla_ref[0] = p_rec[:, 3 * D_RWKV:]
    grec_ref[0] = p[:, D_SHIFT:D_SHIFT + D_RWKV]
    fin_ref[0] = p[:, D_SHIFT + D_RWKV:D_SHIFT + D_RWKV + D_FNET]
    gf_ref[0] = p[:, D_SHIFT + D_RWKV + D_FNET:]


def _in_proj_call(x, emb, mod, norm_g, w_in_bf16, mu):
    B, T, _ = x.shape
    tm = ROW_TILE
    n_tiles = T // tm
    has_halo = n_tiles > 1
    has_emb = emb is not None
    per_batch_mod = mod.shape[0] > 1
    blocks_per_tile = tm // HALO
    last_halo_block = T // HALO - 1

    in_specs = [pl.BlockSpec((1, tm, D_MODEL), lambda b, i: (b, i, 0))]
    args = [x]
    if has_halo:
        in_specs += [
            pl.BlockSpec((1, HALO, D_MODEL), lambda b, i: (b, jnp.maximum(i * blocks_per_tile - 1, 0), 0)),
            pl.BlockSpec((1, HALO, D_MODEL),
                         lambda b, i: (b, jnp.minimum((i + 1) * blocks_per_tile, last_halo_block), 0))]
        args += [x, x]
    if has_emb:
        in_specs.append(pl.BlockSpec((tm, D_MODEL), lambda b, i: (i, 0)))
        args.append(emb)
        if has_halo:
            in_specs += [
                pl.BlockSpec((HALO, D_MODEL), lambda b, i: (jnp.maximum(i * blocks_per_tile - 1, 0), 0)),
                pl.BlockSpec((HALO, D_MODEL),
                             lambda b, i: (jnp.minimum((i + 1) * blocks_per_tile, last_halo_block), 0))]
            args += [emb, emb]
    mod_map = (lambda b, i: (b, 0, 0)) if per_batch_mod else (lambda b, i: (0, 0, 0))
    in_specs += [pl.BlockSpec((1, 3, D_MODEL), mod_map),
                 pl.BlockSpec((1, D_MODEL), lambda b, i: (0, 0)),
                 pl.BlockSpec((D_MODEL, D_IN), lambda b, i: (0, 0)),
                 pl.BlockSpec((1, D_SHIFT), lambda b, i: (0, 0))]
    args += [mod, norm_g, w_in_bf16, mu]
    widths = (3 * D_RWKV, N_DIR * 2 * LORA, D_RWKV, D_FNET, D_FNET)
    out_specs = [pl.BlockSpec((1, tm, w), lambda b, i: (b, i, 0)) for w in widths]
    out_shape = [jax.ShapeDtypeStruct((B, T, w), F32) for w in widths]
    kern = functools.partial(_in_proj_kernel, has_emb=has_emb, has_halo=has_halo, n_tiles=n_tiles)
    return pl.pallas_call(
        kern, grid=(B, n_tiles), in_specs=in_specs, out_specs=out_specs, out_shape=out_shape,
        compiler_params=pltpu.CompilerParams(dimension_semantics=("arbitrary", "arbitrary"),
                                             vmem_limit_bytes=VMEM_LIMIT),
        name="in_proj",
    )(*args)


def _block_diag_rows(y, head0_lanes):
    return jnp.concatenate([jnp.where(head0_lanes, y, 0.0), jnp.where(head0_lanes, 0.0, y)], axis=0)


def _scan_kernel(*refs, zero_init, write_state, n_chunks):
    it = iter(refs)
    rkv_refs = [next(it), None]
    lwla_refs = [next(it), None]
    rkv_refs[1] = next(it)
    lwla_refs[1] = next(it)
    w0_ref, w2hi_ref, w2lo_ref, a0_ref, a2hi_ref, a2lo_ref = (next(it) for _ in range(6))
    kk_ref, ka_ref, tri_ref, ones_ref = (next(it) for _ in range(4))
    s0_ref = None if zero_init else next(it)
    y_refs = [next(it), next(it)]
    sfin_ref = next(it) if write_state else None
    h_scr = next(it)

    j = pl.program_id(1)

    @pl.when(j == 0)
    def _():
        if zero_init:
            h_scr[...] = jnp.zeros_like(h_scr)
        else:
            h_scr[...] = s0_ref[0]

    C = CHUNK
    row = lax.broadcasted_iota(jnp.int32, (C, LANES), 0)
    lane = lax.broadcasted_iota(jnp.int32, (C, LANES), 1)
    s_idx = jnp.bitwise_and(lane, HEAD_DIM - 1)
    head0 = lane < HEAD_DIM
    eye_pair = (row == s_idx).astype(F32)
    row2 = lax.broadcasted_iota(jnp.int32, (LANES, LANES), 0)
    lane2 = lax.broadcasted_iota(jnp.int32, (LANES, LANES), 1)
    same_head = (row2 < HEAD_DIM) == (lane2 < HEAD_DIM)
    eye2 = row2 == lane2
    decay_scale = math.exp(-0.5)
    n_double = int(math.log2(C)) - 1

    probs = []
    for d in range(N_DIR):
        rkv = rkv_refs[d][0]
        r = rkv[:, :D_RWKV]
        k = rkv[:, D_RWKV:2 * D_RWKV]
        v = rkv[:, 2 * D_RWKV:]
        ll = lwla_refs[d][0]
        z_w = w0_ref[d:d + 1, :] + _dot3(jnp.tanh(ll), w2hi_ref[d], w2lo_ref[d])
        logw = -decay_scale * jax.nn.sigmoid(z_w)
        a = jax.nn.sigmoid(a0_ref[d:d + 1, :] + _dot3(ll, a2hi_ref[d], a2lo_ref[d]))
        kd = k * (1.0 + (a - 1.0) * ka_ref[...])
        kkr = k * kk_ref[...]
        ssq = _dot_exact_rhs(kkr * kkr, ones_ref[...])
        kk = kkr * lax.rsqrt(jnp.maximum(ssq, 1e-24))
        bvec = kk * a
        lw_hi, lw_mid, lw_lo = _split3(logw)
        tri = tri_ref[d]
        cum = (jnp.dot(tri, lw_hi, preferred_element_type=F32)
               + jnp.dot(tri, lw_mid, preferred_element_type=F32)
               + jnp.dot(tri, lw_lo, preferred_element_type=F32))
        cum_prev = cum - logw
        tot = cum[C - 1:C, :] if d == 0 else cum[0:1, :]
        kap_t = kk * jnp.exp(cum_prev)
        r_t = r * jnp.exp(cum)
        e_neg = jnp.exp(-cum)
        b_t = bvec * e_neg
        k_t = kd * e_neg
        e_rem = jnp.exp(tot - cum)
        b_h = bvec * e_rem
        k_h = kd * e_rem
        gam = jnp.exp(tot)
        if d == 0:
            strict, incl = row > s_idx, row >= s_idx
        else:
            strict, incl = row < s_idx, row <= s_idx

        for p in range(N_PAIRS):
            sl = slice(p * LANES, (p + 1) * LANES)
            probs.append(dict(d=d, p=p, sl=sl, strict=strict, incl=incl, kap=kap_t[:, sl], r=r_t[:, sl],
                              v=v[:, sl], b_t=b_t[:, sl], k_t=k_t[:, sl], b_h=b_h[:, sl], k_h=k_h[:, sl],
                              gam=gam[:, sl]))

    bd = lambda y: _block_diag_rows(y, head0)
    for q in probs:
        lhs = jnp.concatenate([q["kap"], q["r"]], axis=0)
        rhs = jnp.concatenate([bd(q["b_t"]), bd(q["k_t"])], axis=0)
        lm = _bdot_nt(lhs, rhs)
        q["l_b"] = jnp.where(q["strict"], lm[:C, :LANES], 0.0)
        q["m_b"] = jnp.where(q["incl"], lm[C:, :LANES], 0.0)
        q["lm_k"] = jnp.concatenate([jnp.where(q["strict"], lm[:C, LANES:], 0.0),
                                     jnp.where(q["incl"], lm[C:, LANES:], 0.0)], axis=0)
    for q in probs:
        lmv = _bdot(q["lm_k"], bd(q["v"]))
        q["l_kv"], q["m_kv"] = lmv[:C], lmv[C:]
        q["pw"] = -q["l_b"]
        q["tinv"] = eye_pair + q["pw"]
    for _ in range(n_double):
        for q in probs:
            q["pw"] = _bdot(q["pw"], bd(q["pw"]))
        for q in probs:
            q["tinv"] = q["tinv"] + _bdot(q["tinv"], bd(q["pw"]))
    for q in probs:
        wu = _bdot(q["tinv"], jnp.concatenate([bd(q["kap"]), bd(q["l_kv"])], axis=1))
        q["w_m"], q["u_n"] = wu[:, :LANES], wu[:, LANES:]
    for q in probs:
        qy = _bdot(q["m_b"], jnp.concatenate([bd(q["w_m"]), bd(q["u_n"])], axis=1))
        q["r_q"] = q["r"] - qy[:, :LANES]
        q["y_i"] = q["m_kv"] - qy[:, LANES:]
    for q in probs:
        t2 = jnp.concatenate([q["b_h"], q["k_h"]], axis=0).T
        ag = _bdot(t2, jnp.concatenate(
            [jnp.concatenate([q["w_m"], -q["u_n"]], axis=1),
             jnp.concatenate([jnp.zeros_like(q["v"]), q["v"]], axis=1)], axis=0))
        gam_p = jnp.broadcast_to(q["gam"], (LANES, LANES))
        q["a_bd"] = jnp.where(same_head, jnp.where(eye2, gam_p, 0.0) - ag[:, :LANES], 0.0)
        q["g_bd"] = jnp.where(same_head, ag[:, LANES:], 0.0)
    for q in probs:
        h_hi, h_lo = _split2(h_scr[q["d"], q["p"]])
        lhs2 = jnp.concatenate([q["r_q"], q["a_bd"]], axis=0).astype(BF16)
        yh = (jnp.dot(lhs2, h_hi, preferred_element_type=F32)
              + jnp.dot(lhs2, h_lo, preferred_element_type=F32))
        y_refs[q["d"]][0, :, q["sl"]] = q["y_i"] + yh[:C]
        h_scr[q["d"], q["p"]] = yh[C:] + q["g_bd"]

    if write_state:
        @pl.when(j == n_chunks - 1)
        def _():
            sfin_ref[0] = h_scr[...]


def _scan_call(rkv, lwla, wts, s0_bd, write_state):
    B, T, _ = rkv.shape
    nc = T // CHUNK
    zero_init = s0_bd is None
    fwd = lambda b, j: (b, j, 0)
    bwd = lambda b, j: (b, nc - 1 - j, 0)
    full = lambda *shape: pl.BlockSpec(shape, lambda b, j: (0,) * len(shape))
    in_specs = [pl.BlockSpec((1, CHUNK, 3 * D_RWKV), fwd), pl.BlockSpec((1, CHUNK, LANES), fwd),
                pl.BlockSpec((1, CHUNK, 3 * D_RWKV), bwd), pl.BlockSpec((1, CHUNK, LANES), bwd),
                full(N_DIR, D_RWKV), full(N_DIR, LANES, D_RWKV), full(N_DIR, LANES, D_RWKV),
                full(N_DIR, D_RWKV), full(N_DIR, LANES, D_RWKV), full(N_DIR, LANES, D_RWKV),
                full(1, D_RWKV), full(1, D_RWKV), full(N_DIR, CHUNK, CHUNK), full(D_RWKV, D_RWKV)]
    args = [rkv, lwla, rkv, lwla, wts["w0"], wts["w2_hi"], wts["w2_lo"], wts["a0"], wts["a2_hi"],
            wts["a2_lo"], wts["k_k"], wts["k_a"], wts["tri"], wts["ones_bd"]]
    state_block = (1, N_DIR, N_PAIRS, LANES, LANES)
    if not zero_init:
        in_specs.append(pl.BlockSpec(state_block, lambda b, j: (b, 0, 0, 0, 0)))
        args.append(s0_bd)
    out_specs = [pl.BlockSpec((1, CHUNK, D_RWKV), fwd), pl.BlockSpec((1, CHUNK, D_RWKV), bwd)]
    out_shape = [jax.ShapeDtypeStruct((B, T, D_RWKV), F32), jax.ShapeDtypeStruct((B, T, D_RWKV), F32)]
    if write_state:
        out_specs.append(pl.BlockSpec(state_block, lambda b, j: (b, 0, 0, 0, 0)))
        out_shape.append(jax.ShapeDtypeStruct((B,) + state_block[1:], F32))
    kern = functools.partial(_scan_kernel, zero_init=zero_init, write_state=write_state, n_chunks=nc)
    return pl.pallas_call(
        kern, grid=(B, nc), in_specs=in_specs, out_specs=out_specs, out_shape=out_shape,
        scratch_shapes=[pltpu.VMEM((N_DIR, N_PAIRS, LANES, LANES), F32)],
        compiler_params=pltpu.CompilerParams(dimension_semantics=("arbitrary", "arbitrary"),
                                             vmem_limit_bytes=VMEM_LIMIT),
        name="scan",
    )(*args)


def _fnet_kernel(fin_ref, gf_ref, dft_ref, cs_ref, wbd_ref, b_ref, o_ref, g_scr, *, seq_len):
    u = pl.program_id(1)

    @pl.when(u == 0)
    def _():
        for t0 in range(0, seq_len, ROW_TILE):
            fc = _bdot(fin_ref[0, t0:t0 + ROW_TILE, :], cs_ref[...])
            g_scr[t0:t0 + ROW_TILE, :] = fc[:, :D_FNET].astype(BF16)
            g_scr[seq_len + t0:seq_len + t0 + ROW_TILE, :] = fc[:, D_FNET:].astype(BF16)

    f_re = jnp.dot(dft_ref[...], g_scr[...], preferred_element_type=F32)
    f_out = _bdot(f_re, wbd_ref[...]) + b_ref[...]
    o_ref[0] = (f_out * _silu(gf_ref[0])).astype(BF16)


def _fnet_call(fin, gf, dft_bf16, cs_bf16, wbd_bf16, b_fnet):
    B, T, _ = fin.shape
    tu = ROW_TILE
    return pl.pallas_call(
        functools.partial(_fnet_kernel, seq_len=T),
        grid=(B, T // tu),
        in_specs=[pl.BlockSpec((1, T, D_FNET), lambda b, u: (b, 0, 0)),
                  pl.BlockSpec((1, tu, D_FNET), lambda b, u: (b, u, 0)),
                  pl.BlockSpec((tu, 2 * T), lambda b, u: (u, 0)),
                  pl.BlockSpec((D_FNET, 2 * D_FNET), lambda b, u: (0, 0)),
                  pl.BlockSpec((D_FNET, D_FNET), lambda b, u: (0, 0)),
                  pl.BlockSpec((1, D_FNET), lambda b, u: (0, 0))],
        out_specs=pl.BlockSpec((1, tu, D_FNET), lambda b, u: (b, u, 0)),
        out_shape=jax.ShapeDtypeStruct((B, T, D_FNET), BF16),
        scratch_shapes=[pltpu.VMEM((2 * T, D_FNET), BF16)],
        compiler_params=pltpu.CompilerParams(dimension_semantics=("arbitrary", "arbitrary"),
                                             vmem_limit_bytes=VMEM_LIMIT),
        name="fnet",
    )(fin, gf, dft_bf16, cs_bf16, wbd_bf16, b_fnet)


def _out_kernel(*refs, has_emb, final_norm):
    it = iter(refs)
    x_ref = next(it)
    emb_ref = next(it) if has_emb else None
    (mod_ref, yf_ref, yb_ref, rkv_ref, lwla_ref, grec_ref, fo_ref, a0_ref, a2hi_ref, a2lo_ref, ka_ref,
     rk_ref, gng_ref, gnb_ref, avg_ref, ones_ref, wout_ref, fng_ref, o_ref) = (next(it) for _ in range(19))

    y = yf_ref[0] + yb_ref[0]
    mu = _dot_exact_rhs(y, avg_ref[...])
    dlt = y - mu
    var = _dot_exact_rhs(dlt * dlt, avg_ref[...])
    y_n = dlt * lax.rsqrt(var + GN_EPS) * gng_ref[...] + gnb_ref[...]
    rkv = rkv_ref[0]
    r = rkv[:, :D_RWKV]
    k = rkv[:, D_RWKV:2 * D_RWKV]
    v = rkv[:, 2 * D_RWKV:]
    ll = lwla_ref[0]
    a_sum = (jax.nn.sigmoid(a0_ref[0:1, :] + _dot3(ll, a2hi_ref[0], a2lo_ref[0]))
             + jax.nn.sigmoid(a0_ref[1:2, :] + _dot3(ll, a2hi_ref[1], a2lo_ref[1])))
    k_sum = k * (2.0 + (a_sum - 2.0) * ka_ref[...])
    bonus = _dot_exact_rhs(r * k_sum * rk_ref[...], ones_ref[...]) * v
    rec_out = (y_n + bonus) * _silu(grec_ref[0])
    mixed = jnp.concatenate([rec_out.astype(BF16), fo_ref[0]], axis=-1)
    out = jnp.dot(mixed, wout_ref[...], preferred_element_type=F32)
    x = x_ref[0]
    if has_emb:
        x = x + emb_ref[...]
    z = x + mod_ref[0, 2:3, :] * out
    if final_norm:
        ms = jnp.mean(z * z, axis=-1, keepdims=True)
        z = z * lax.rsqrt(ms + NORM_EPS) * fng_ref[...]
    o_ref[0] = z


def _out_call(x, emb, mod, y_f, y_b, rkv, lwla, grec, fo, wts, w_out_bf16, final_norm_g, final_norm):
    B, T, _ = x.shape
    tm = ROW_TILE
    has_emb = emb is not None
    per_batch_mod = mod.shape[0] > 1
    tok = lambda w: pl.BlockSpec((1, tm, w), lambda b, i: (b, i, 0))
    full = lambda *shape: pl.BlockSpec(shape, lambda b, i: (0,) * len(shape))
    in_specs = [tok(D_MODEL)]
    args = [x]
    if has_emb:
        in_specs.append(pl.BlockSpec((tm, D_MODEL), lambda b, i: (i, 0)))
        args.append(emb)
    mod_map = (lambda b, i: (b, 0, 0)) if per_batch_mod else (lambda b, i: (0, 0, 0))
    in_specs += [pl.BlockSpec((1, 3, D_MODEL), mod_map), tok(D_RWKV), tok(D_RWKV), tok(3 * D_RWKV),
                 tok(LANES), tok(D_RWKV), tok(D_FNET),
                 full(N_DIR, D_RWKV), full(N_DIR, LANES, D_RWKV), full(N_DIR, LANES, D_RWKV),
                 full(1, D_RWKV), full(1, D_RWKV), full(1, D_RWKV), full(1, D_RWKV),
                 full(D_RWKV, D_RWKV), full(D_RWKV, D_RWKV), full(D_MODEL, D_MODEL), full(1, D_MODEL)]
    args += [mod, y_f, y_b, rkv, lwla, grec, fo, wts["a0"], wts["a2_hi"], wts["a2_lo"], wts["k_a"],
             wts["r_k"], wts["gn_g"], wts["gn_b"], wts["avg_bd"], wts["ones_bd"], w_out_bf16, final_norm_g]
    return pl.pallas_call(
        functools.partial(_out_kernel, has_emb=has_emb, final_norm=final_norm),
        grid=(B, T // tm), in_specs=in_specs,
        out_specs=pl.BlockSpec((1, tm, D_MODEL), lambda b, i: (b, i, 0)),
        out_shape=jax.ShapeDtypeStruct((B, T, D_MODEL), F32),
        compiler_params=pltpu.CompilerParams(dimension_semantics=("arbitrary", "arbitrary"),
                                             vmem_limit_bytes=VMEM_LIMIT),
        name="out_proj",
    )(*args)


def _dft_table(seq_len):
    idx = np.arange(seq_len, dtype=np.int64)
    ang = 2.0 * np.pi * ((idx[:, None] * idx[None, :]) % seq_len).astype(np.float64) / seq_len
    scale = 1.0 / math.sqrt(seq_len)
    return np.concatenate([np.cos(ang) * scale, -np.sin(ang) * scale], axis=1).astype(np.float32)


def _channel_dft_table():
    n = FNET_GROUP
    idx = np.arange(n, dtype=np.int64)
    ang = 2.0 * np.pi * ((idx[:, None] * idx[None, :]) % n).astype(np.float64) / n
    c = np.cos(ang) / math.sqrt(n)
    s = np.sin(ang) / math.sqrt(n)
    eye = np.eye(D_FNET // n)
    return np.concatenate([np.kron(eye, c), np.kron(eye, s)], axis=1).astype(np.float32)


def _sincos_2d(n_tokens, dtype):
    rows = n_tokens // GRID_W
    pos = jnp.arange(rows * GRID_W)
    row = (pos // GRID_W).astype(F32)
    col = (pos % GRID_W).astype(F32)
    quarter = D_MODEL // 4
    freq = jnp.exp(-math.log(POS_BASE) * jnp.arange(quarter, dtype=F32) / quarter)
    ang_r = row[:, None] * freq
    ang_c = col[:, None] * freq
    emb = jnp.concatenate([jnp.sin(ang_r), jnp.cos(ang_r), jnp.sin(ang_c), jnp.cos(ang_c)], axis=-1)
    return emb.astype(dtype)


def _head_block_matrix(value):
    blk = np.kron(np.eye(N_HEADS), np.ones((HEAD_DIM, HEAD_DIM))) * value
    return jnp.asarray(blk, dtype=BF16)


def _pad_lora(w, row_offset):
    out = jnp.zeros((N_DIR, LANES, w.shape[-1]), F32)
    for d in range(N_DIR):
        out = out.at[d, row_offset + d * LORA:row_offset + (d + 1) * LORA, :].set(w[d])
    hi = out.astype(BF16)
    lo = (out - hi.astype(F32)).astype(BF16)
    return hi, lo


def _layer_weights(l, w0, w2, a0, a2, k_k, k_a, r_k, gn_g, gn_b):
    w2_hi, w2_lo = _pad_lora(w2[l], 0)
    a2_hi, a2_lo = _pad_lora(a2[l], N_DIR * LORA)
    tri_f = np.tril(np.ones((CHUNK, CHUNK)))
    tri = jnp.asarray(np.stack([tri_f, tri_f.T]), dtype=BF16)
    return dict(w0=w0[l], w2_hi=w2_hi, w2_lo=w2_lo, a0=a0[l], a2_hi=a2_hi, a2_lo=a2_lo,
                k_k=k_k[l][None], k_a=k_a[l][None], r_k=r_k[l].reshape(1, D_RWKV),
                gn_g=gn_g[l].reshape(1, D_RWKV), gn_b=gn_b[l].reshape(1, D_RWKV), tri=tri,
                ones_bd=_head_block_matrix(1.0), avg_bd=_head_block_matrix(1.0 / HEAD_DIM))


def _state_to_block_diag(s_f, s_b):
    def one(s):
        h = jnp.swapaxes(s.astype(F32), -1, -2)
        b = h.shape[0]
        h = h.reshape(b, N_PAIRS, PAIR, HEAD_DIM, HEAD_DIM)
        z = jnp.zeros_like(h[:, :, 0])
        top = jnp.concatenate([h[:, :, 0], z], axis=-1)
        bot = jnp.concatenate([z, h[:, :, 1]], axis=-1)
        return jnp.concatenate([top, bot], axis=-2)
    return jnp.stack([one(s_f), one(s_b)], axis=1)


def _block_diag_to_state(s_bd):
    b = s_bd.shape[0]
    h0 = s_bd[..., :HEAD_DIM, :HEAD_DIM]
    h1 = s_bd[..., HEAD_DIM:, HEAD_DIM:]
    h = jnp.stack([h0, h1], axis=3).reshape(b, N_DIR, N_HEADS, HEAD_DIM, HEAD_DIM)
    s = jnp.swapaxes(h, -1, -2)
    return s[:, 0], s[:, 1]


def kernel(x_prompt, x_sample, state_rwkv_fwd, state_rwkv_bwd, c, c_ctx, w_ada, b_ada, norm_g, w_in,
           mu_shift, w0, w2, a0, a2, k_k, k_a, r_k, gn_g, gn_b, w_fnet, b_fnet, w_out, final_norm_g):
    depth = w_in.shape[0]
    n_dec = c.shape[0]
    assert n_dec + 1 <= 8
    bp, tp, _ = x_prompt.shape
    bs, ts, _ = x_sample.shape
    cvec = jnp.zeros((8, D_MODEL), F32).at[0].set(c_ctx).at[1:1 + n_dec].set(c)
    emb = _sincos_2d(ts, x_sample.dtype)
    cs_tab = jnp.asarray(_channel_dft_table()).astype(BF16)
    dft_p = jnp.asarray(_dft_table(tp)).astype(BF16)
    dft_s = jnp.asarray(_dft_table(ts)).astype(BF16)
    fng = final_norm_g[None]

    xp, xs = x_prompt, x_sample
    new_f, new_b = [], []
    for l in range(depth):
        mod = _mod_call(cvec, w_ada[l], b_ada[l][None]).reshape(8, 3, D_MODEL)
        mod_ctx, mod_lat = mod[0:1], mod[1:1 + n_dec]
        wts = _layer_weights(l, w0, w2, a0, a2, k_k, k_a, r_k, gn_g, gn_b)
        w_in_b = w_in[l].astype(BF16)
        w_out_b = w_out[l].astype(BF16)
        wbd = jax.scipy.linalg.block_diag(*[w_fnet[l, g] for g in range(w_fnet.shape[1])]).astype(BF16)
        ng, mu, bf = norm_g[l][None], mu_shift[l][None], b_fnet[l][None]
        emb_l = emb if l == 0 else None

        rkv, lwla, grec, fin, gf = _in_proj_call(xp, None, mod_ctx, ng, w_in_b, mu)
        y_f, y_b, s_bd = _scan_call(rkv, lwla, wts, None, True)
        fo = _fnet_call(fin, gf, dft_p, cs_tab, wbd, bf)
        last = l == depth - 1
        xp = _out_call(xp, None, mod_ctx, y_f, y_b, rkv, lwla, grec, fo, wts, w_out_b, fng, last)
        s_f, s_b = _block_diag_to_state(s_bd)
        new_f.append(s_f)
        new_b.append(s_b)

        rkv, lwla, grec, fin, gf = _in_proj_call(xs, emb_l, mod_lat, ng, w_in_b, mu)
        s0 = _state_to_block_diag(state_rwkv_fwd[:, l], state_rwkv_bwd[:, l])
        y_f, y_b = _scan_call(rkv, lwla, wts, s0, False)
        fo = _fnet_call(fin, gf, dft_s, cs_tab, wbd, bf)
        xs = _out_call(xs, emb_l, mod_lat, y_f, y_b, rkv, lwla, grec, fo, wts, w_out_b, fng, last)
    return (xp, xs, jnp.stack(new_f, axis=1), jnp.stack(new_b, axis=1))
```

```python
import functools
import math

import numpy as np
import jax
import jax.numpy as jnp
from jax import lax
from jax.experimental import pallas as pl
from jax.experimental.pallas import tpu as pltpu

F32 = jnp.float32
BF16 = jnp.bfloat16

D_MODEL = 1024
GRID_W = 64
D_RWKV = 512
D_FNET = D_MODEL - D_RWKV
HEAD_DIM = 64
N_HEADS = D_RWKV // HEAD_DIM
FNET_GROUP = 64
LORA = 32
N_DIR = 2
D_SHIFT = 3 * D_RWKV + N_DIR * 2 * LORA
D_IN = D_SHIFT + D_RWKV + 2 * D_FNET
NORM_EPS = 1e-6
GN_EPS = 64e-5
POS_BASE = 10000.0

LANES = 128
PAIR = LANES // HEAD_DIM
N_PAIRS = N_HEADS // PAIR
CHUNK = 64
SUB = 16
SUB_SHIFT = SUB.bit_length() - 1
SCAN_SEQS = 2
ROW_TILE = 256
HALO = 8
VMEM_LIMIT = 48 * 1024 * 1024


def _silu(x):
    return x * jax.nn.sigmoid(x)


def _bdot(a, b):
    return jnp.dot(a.astype(BF16), b.astype(BF16), preferred_element_type=F32)


def _bdot_nt(a, b):
    return lax.dot_general(a.astype(BF16), b.astype(BF16), (((1,), (1,)), ((), ())),
                           preferred_element_type=F32)


def _split2(x):
    hi = x.astype(BF16)
    lo = (x - hi.astype(F32)).astype(BF16)
    return hi, lo


def _split3(x):
    hi = x.astype(BF16)
    r1 = x - hi.astype(F32)
    mid = r1.astype(BF16)
    lo = (r1 - mid.astype(F32)).astype(BF16)
    return hi, mid, lo


def _dot_exact_rhs(x, m_bf16):
    hi, lo = _split2(x)
    return (jnp.dot(hi, m_bf16, preferred_element_type=F32)
            + jnp.dot(lo, m_bf16, preferred_element_type=F32))


def _dot3(a, b_hi, b_lo):
    a_hi, a_lo = _split2(a)
    return (jnp.dot(a_hi, b_hi, preferred_element_type=F32)
            + jnp.dot(a_hi, b_lo, preferred_element_type=F32)
            + jnp.dot(a_lo, b_hi, preferred_element_type=F32))


def _mod_kernel(c_ref, w_ref, b_ref, o_ref):
    s = _silu(c_ref[...])
    w = w_ref[...]
    w_hi, w_lo = _split2(w)
    o_ref[...] = _dot3(s, w_hi, w_lo) + b_ref[...]


def _mod_call(cvec, w_ada, b_ada):
    n_blk = 3
    return pl.pallas_call(
        _mod_kernel,
        grid=(n_blk,),
        in_specs=[pl.BlockSpec((8, D_MODEL), lambda i: (0, 0)),
                  pl.BlockSpec((D_MODEL, D_MODEL), lambda i: (0, i)),
                  pl.BlockSpec((1, D_MODEL), lambda i: (0, i))],
        out_specs=pl.BlockSpec((8, D_MODEL), lambda i: (0, i)),
        out_shape=jax.ShapeDtypeStruct((8, 3 * D_MODEL), F32),
        compiler_params=pltpu.CompilerParams(dimension_semantics=("arbitrary",),
                                             vmem_limit_bytes=VMEM_LIMIT),
        name="mod",
    )(cvec, w_ada, b_ada)


def _modulated_norm(x, g, scale, shift):
    ms = jnp.mean(x * x, axis=-1, keepdims=True)
    y = x * lax.rsqrt(ms + NORM_EPS) * g
    return y * (1.0 + scale) + shift


def _in_proj_kernel(*refs, has_emb, has_halo, n_tiles):
    it = iter(refs)
    x_ref = next(it)
    xp_ref = next(it) if has_halo else None
    xn_ref = next(it) if has_halo else None
    emb_ref = next(it) if has_emb else None
    embp_ref = next(it) if (has_emb and has_halo) else None
    embn_ref = next(it) if (has_emb and has_halo) else None
    mod_ref, g_ref, w_ref, mu_ref = next(it), next(it), next(it), next(it)
    rkv_ref, lwla_ref, grec_ref, fin_ref, gf_ref = next(it), next(it), next(it), next(it), next(it)

    i = pl.program_id(1)
    g = g_ref[...]
    shift = mod_ref[0, 0:1, :]
    scale = mod_ref[0, 1:2, :]
    x = x_ref[0]
    if has_emb:
        x = x + emb_ref[...]
    tm = x.shape[0]
    h = _modulated_norm(x, g, scale, shift)
    p = jnp.dot(h.astype(BF16), w_ref[...], preferred_element_type=F32)
    ps = p[:, :D_SHIFT]
    if has_halo:
        xh = jnp.concatenate([xp_ref[0], xn_ref[0]], axis=0)
        if has_emb:
            xh = xh + jnp.concatenate([embp_ref[...], embn_ref[...]], axis=0)
        hh = _modulated_norm(xh, g, scale, shift)
        ph = jnp.dot(hh.astype(BF16), w_ref[:, :D_SHIFT], preferred_element_type=F32)
        prev_row = jnp.where(i > 0, ph[HALO - 1:HALO, :], 0.0)
        next_row = jnp.where(i < n_tiles - 1, ph[HALO:HALO + 1, :], 0.0)
    else:
        prev_row = jnp.zeros((1, D_SHIFT), F32)
        next_row = jnp.zeros((1, D_SHIFT), F32)
    row = lax.broadcasted_iota(jnp.int32, (tm, D_SHIFT), 0)
    prev = jnp.where(row == 0, prev_row, pltpu.roll(ps, 1, 0))
    nxt = jnp.where(row == tm - 1, next_row, pltpu.roll(ps, tm - 1, 0))
    p_rec = ps + mu_ref[...] * (0.5 * (prev + nxt) - ps)
    rkv_ref[0] = p_rec[:, :3 * D_RWKV]
    lwla_ref[0] = p_rec[:, 3 * D_RWKV:]
    grec_ref[0] = p[:, D_SHIFT:D_SHIFT + D_RWKV]
    fin_ref[0] = p[:, D_SHIFT + D_RWKV:D_SHIFT + D_RWKV + D_FNET]
    gf_ref[0] = p[:, D_SHIFT + D_RWKV + D_FNET:]


def _in_proj_call(x, emb, mod, norm_g, w_in_bf16, mu):
    B, T, _ = x.shape
    tm = ROW_TILE
    n_tiles = T // tm
    has_halo = n_tiles > 1
    has_emb = emb is not None
    per_batch_mod = mod.shape[0] > 1
    blocks_per_tile = tm // HALO
    last_halo_block = T // HALO - 1

    in_specs = [pl.BlockSpec((1, tm, D_MODEL), lambda b, i: (b, i, 0))]
    args = [x]
    if has_halo:
        in_specs += [
            pl.BlockSpec((1, HALO, D_MODEL), lambda b, i: (b, jnp.maximum(i * blocks_per_tile - 1, 0), 0)),
            pl.BlockSpec((1, HALO, D_MODEL),
                         lambda b, i: (b, jnp.minimum((i + 1) * blocks_per_tile, last_halo_block), 0))]
        args += [x, x]
    if has_emb:
        in_specs.append(pl.BlockSpec((tm, D_MODEL), lambda b, i: (i, 0)))
        args.append(emb)
        if has_halo:
            in_specs += [
                pl.BlockSpec((HALO, D_MODEL), lambda b, i: (jnp.maximum(i * blocks_per_tile - 1, 0), 0)),
                pl.BlockSpec((HALO, D_MODEL),
                             lambda b, i: (jnp.minimum((i + 1) * blocks_per_tile, last_halo_block), 0))]
            args += [emb, emb]
    mod_map = (lambda b, i: (b, 0, 0)) if per_batch_mod else (lambda b, i: (0, 0, 0))
    in_specs += [pl.BlockSpec((1, 3, D_MODEL), mod_map),
                 pl.BlockSpec((1, D_MODEL), lambda b, i: (0, 0)),
                 pl.BlockSpec((D_MODEL, D_IN), lambda b, i: (0, 0)),
                 pl.BlockSpec((1, D_SHIFT), lambda b, i: (0, 0))]
    args += [mod, norm_g, w_in_bf16, mu]
    widths = (3 * D_RWKV, N_DIR * 2 * LORA, D_RWKV, D_FNET, D_FNET)
    out_specs = [pl.BlockSpec((1, tm, w), lambda b, i: (b, i, 0)) for w in widths]
    out_shape = [jax.ShapeDtypeStruct((B, T, w), F32) for w in widths]
    kern = functools.partial(_in_proj_kernel, has_emb=has_emb, has_halo=has_halo, n_tiles=n_tiles)
    return pl.pallas_call(
        kern, grid=(B, n_tiles), in_specs=in_specs, out_specs=out_specs, out_shape=out_shape,
        compiler_params=pltpu.CompilerParams(dimension_semantics=("arbitrary", "arbitrary"),
                                             vmem_limit_bytes=VMEM_LIMIT),
        name="in_proj",
    )(*args)


def _same_block(i, j, size):
    shift = size.bit_length() - 1
    return jnp.right_shift(i, shift) == jnp.right_shift(j, shift)


def _block_diag_rows(y, head0_lanes):
    return jnp.concatenate([jnp.where(head0_lanes, y, 0.0), jnp.where(head0_lanes, 0.0, y)], axis=0)


def _scan_kernel(*refs, zero_init, write_state, n_chunks, n_seq):
    it = iter(refs)
    rkv_refs = [next(it), None]
    lwla_refs = [next(it), None]
    rkv_refs[1] = next(it)
    lwla_refs[1] = next(it)
    w0_ref, w2hi_ref, w2lo_ref, a0_ref, a2hi_ref, a2lo_ref = (next(it) for _ in range(6))
    kk_ref, ka_ref, tri_ref, ones_ref = (next(it) for _ in range(4))
    s0_ref = None if zero_init else next(it)
    y_refs = [next(it), next(it)]
    sfin_ref = next(it) if write_state else None
    h_scr = next(it)

    j = pl.program_id(1)

    @pl.when(j == 0)
    def _():
        if zero_init:
            h_scr[...] = jnp.zeros_like(h_scr)
        else:
            h_scr[...] = s0_ref[...]

    C = CHUNK
    row = lax.broadcasted_iota(jnp.int32, (C, LANES), 0)
    lane = lax.broadcasted_iota(jnp.int32, (C, LANES), 1)
    s_idx = jnp.bitwise_and(lane, HEAD_DIM - 1)
    head0 = lane < HEAD_DIM
    row_c = lax.broadcasted_iota(jnp.int32, (SUB, LANES), 0)
    lane_c = lax.broadcasted_iota(jnp.int32, (SUB, LANES), 1)
    col_c = jnp.bitwise_and(lane_c, SUB - 1)
    blk_c = jnp.right_shift(jnp.bitwise_and(lane_c, HEAD_DIM - 1), SUB_SHIFT)
    lane_blk_c = jnp.right_shift(lane_c, SUB_SHIFT)
    eye_c = (row_c == col_c).astype(F32)

    def bd_c(y):
        return jnp.concatenate([jnp.where(lane_blk_c == g, y, 0.0) for g in range(LANES // SUB)], axis=0)

    row2 = lax.broadcasted_iota(jnp.int32, (LANES, LANES), 0)
    lane2 = lax.broadcasted_iota(jnp.int32, (LANES, LANES), 1)
    same_head = (row2 < HEAD_DIM) == (lane2 < HEAD_DIM)
    eye2 = row2 == lane2
    decay_scale = math.exp(-0.5)

    probs = []
    for n, d in [(n, d) for n in range(n_seq) for d in range(N_DIR)]:
        rkv = rkv_refs[d][n]
        r = rkv[:, :D_RWKV]
        k = rkv[:, D_RWKV:2 * D_RWKV]
        v = rkv[:, 2 * D_RWKV:]
        ll = lwla_refs[d][n]
        z_w = w0_ref[d:d + 1, :] + _dot3(jnp.tanh(ll), w2hi_ref[d], w2lo_ref[d])
        logw = -decay_scale * jax.nn.sigmoid(z_w)
        a = jax.nn.sigmoid(a0_ref[d:d + 1, :] + _dot3(ll, a2hi_ref[d], a2lo_ref[d]))
        kd = k * (1.0 + (a - 1.0) * ka_ref[...])
        kkr = k * kk_ref[...]
        ssq = _dot_exact_rhs(kkr * kkr, ones_ref[...])
        kk = kkr * lax.rsqrt(jnp.maximum(ssq, 1e-24))
        bvec = kk * a
        lw_hi, lw_mid, lw_lo = _split3(logw)
        tri = tri_ref[d]
        cum = (jnp.dot(tri, lw_hi, preferred_element_type=F32)
               + jnp.dot(tri, lw_mid, preferred_element_type=F32)
               + jnp.dot(tri, lw_lo, preferred_element_type=F32))
        cum_prev = cum - logw
        tot = cum[C - 1:C, :] if d == 0 else cum[0:1, :]
        kap_t = kk * jnp.exp(cum_prev)
        r_t = r * jnp.exp(cum)
        e_neg = jnp.exp(-cum)
        b_t = bvec * e_neg
        k_t = kd * e_neg
        e_rem = jnp.exp(tot - cum)
        b_h = bvec * e_rem
        k_h = kd * e_rem
        gam = jnp.exp(tot)
        if d == 0:
            strict, incl = row > s_idx, row >= s_idx
        else:
            strict, incl = row < s_idx, row <= s_idx

        for p in range(N_PAIRS):
            sl = slice(p * LANES, (p + 1) * LANES)
            probs.append(dict(n=n, d=d, p=p, sl=sl, strict=strict, incl=incl, kap=kap_t[:, sl], r=r_t[:, sl],
                              v=v[:, sl], b_t=b_t[:, sl], k_t=k_t[:, sl], b_h=b_h[:, sl], k_h=k_h[:, sl],
                              gam=gam[:, sl]))

    bd = lambda y: _block_diag_rows(y, head0)
    for q in probs:
        lhs = jnp.concatenate([q["kap"], q["r"]], axis=0)
        rhs = jnp.concatenate([bd(q["b_t"]), bd(q["k_t"])], axis=0)
        lm = _bdot_nt(lhs, rhs)
        q["l_b"] = jnp.where(q["strict"], lm[:C, :LANES], 0.0)
        q["m_b"] = jnp.where(q["incl"], lm[C:, :LANES], 0.0)
        q["lm_k"] = jnp.concatenate([jnp.where(q["strict"], lm[:C, LANES:], 0.0),
                                     jnp.where(q["incl"], lm[C:, LANES:], 0.0)], axis=0)
    for q in probs:
        lmv = _bdot(q["lm_k"], bd(q["v"]))
        q["l_kv"], q["m_kv"] = lmv[:C], lmv[C:]
    for q in probs:
        l_b = q["l_b"]
        l_c = l_b[0:SUB]
        for jb in range(1, C // SUB):
            l_c = jnp.where(blk_c == jb, l_b[jb * SUB:(jb + 1) * SUB], l_c)
        q["l_c"] = l_c
        q["t_c"] = eye_c - jnp.where(_same_block(row_c, col_c, 2), l_c, 0.0)
    s = 2
    while s < SUB:
        off_mask = _same_block(row_c, col_c, 2 * s) & ~_same_block(row_c, col_c, s)
        for q in probs:
            q["et"] = _bdot(jnp.where(off_mask, q["l_c"], 0.0), bd_c(q["t_c"]))
        for q in probs:
            q["t_c"] = q["t_c"] - _bdot(q["t_c"], bd_c(q["et"]))
        s *= 2
    for q in probs:
        q["tinv"] = jnp.concatenate([jnp.where(blk_c == jb, q["t_c"], 0.0) for jb in range(C // SUB)], axis=0)
    while s < C:
        off_mask = _same_block(row, s_idx, 2 * s) & ~_same_block(row, s_idx, s)
        for q in probs:
            q["et"] = _bdot(jnp.where(off_mask, q["l_b"], 0.0), bd(q["tinv"]))
        for q in probs:
            q["tinv"] = q["tinv"] - _bdot(q["tinv"], bd(q["et"]))
        s *= 2
    for q in probs:
        wu = _bdot(q["tinv"], jnp.concatenate([bd(q["kap"]), bd(q["l_kv"])], axis=1))
        q["w_m"], q["u_n"] = wu[:, :LANES], wu[:, LANES:]
    for q in probs:
        qy = _bdot(q["m_b"], jnp.concatenate([bd(q["w_m"]), bd(q["u_n"])], axis=1))
        q["r_q"] = q["r"] - qy[:, :LANES]
        q["y_i"] = q["m_kv"] - qy[:, LANES:]
    for q in probs:
        t2 = jnp.concatenate([q["b_h"], q["k_h"]], axis=0).T
        ag = _bdot(t2, jnp.concatenate(
            [jnp.concatenate([q["w_m"], -q["u_n"]], axis=1),
             jnp.concatenate([jnp.zeros_like(q["v"]), q["v"]], axis=1)], axis=0))
        gam_p = jnp.broadcast_to(q["gam"], (LANES, LANES))
        q["a_bd"] = jnp.where(same_head, jnp.where(eye2, gam_p, 0.0) - ag[:, :LANES], 0.0)
        q["g_bd"] = jnp.where(same_head, ag[:, LANES:], 0.0)
    for q in probs:
        h_hi, h_lo = _split2(h_scr[q["n"], q["d"], q["p"]])
        lhs2 = jnp.concatenate([q["r_q"], q["a_bd"]], axis=0).astype(BF16)
        yh = (jnp.dot(lhs2, h_hi, preferred_element_type=F32)
              + jnp.dot(lhs2, h_lo, preferred_element_type=F32))
        y_refs[q["d"]][q["n"], :, q["sl"]] = q["y_i"] + yh[:C]
        h_scr[q["n"], q["d"], q["p"]] = yh[C:] + q["g_bd"]

    if write_state:
        @pl.when(j == n_chunks - 1)
        def _():
            sfin_ref[...] = h_scr[...]


def _scan_call(rkv, lwla, wts, s0_bd, write_state):
    B, T, _ = rkv.shape
    nc = T // CHUNK
    ns = min(SCAN_SEQS, B)
    assert B % ns == 0
    zero_init = s0_bd is None
    fwd = lambda b, j: (b, j, 0)
    bwd = lambda b, j: (b, nc - 1 - j, 0)
    full = lambda *shape: pl.BlockSpec(shape, lambda b, j: (0,) * len(shape))
    in_specs = [pl.BlockSpec((ns, CHUNK, 3 * D_RWKV), fwd), pl.BlockSpec((ns, CHUNK, LANES), fwd),
                pl.BlockSpec((ns, CHUNK, 3 * D_RWKV), bwd), pl.BlockSpec((ns, CHUNK, LANES), bwd),
                full(N_DIR, D_RWKV), full(N_DIR, LANES, D_RWKV), full(N_DIR, LANES, D_RWKV),
                full(N_DIR, D_RWKV), full(N_DIR, LANES, D_RWKV), full(N_DIR, LANES, D_RWKV),
                full(1, D_RWKV), full(1, D_RWKV), full(N_DIR, CHUNK, CHUNK), full(D_RWKV, D_RWKV)]
    args = [rkv, lwla, rkv, lwla, wts["w0"], wts["w2_hi"], wts["w2_lo"], wts["a0"], wts["a2_hi"],
            wts["a2_lo"], wts["k_k"], wts["k_a"], wts["tri"], wts["ones_bd"]]
    state_block = (ns, N_DIR, N_PAIRS, LANES, LANES)
    if not zero_init:
        in_specs.append(pl.BlockSpec(state_block, lambda b, j: (b, 0, 0, 0, 0)))
        args.append(s0_bd)
    out_specs = [pl.BlockSpec((ns, CHUNK, D_RWKV), fwd), pl.BlockSpec((ns, CHUNK, D_RWKV), bwd)]
    out_shape = [jax.ShapeDtypeStruct((B, T, D_RWKV), F32), jax.ShapeDtypeStruct((B, T, D_RWKV), F32)]
    if write_state:
        out_specs.append(pl.BlockSpec(state_block, lambda b, j: (b, 0, 0, 0, 0)))
        out_shape.append(jax.ShapeDtypeStruct((B,) + state_block[1:], F32))
    kern = functools.partial(_scan_kernel, zero_init=zero_init, write_state=write_state, n_chunks=nc,
                             n_seq=ns)
    return pl.pallas_call(
        kern, grid=(B // ns, nc), in_specs=in_specs, out_specs=out_specs, out_shape=out_shape,
        scratch_shapes=[pltpu.VMEM(state_block, F32)],
        compiler_params=pltpu.CompilerParams(dimension_semantics=("arbitrary", "arbitrary"),
                                             vmem_limit_bytes=VMEM_LIMIT),
        name="scan",
    )(*args)


def _fnet_kernel(fin_ref, gf_ref, dft_ref, cs_ref, wbd_ref, b_ref, o_ref, g_scr, *, seq_len):
    u = pl.program_id(1)

    @pl.when(u == 0)
    def _():
        for t0 in range(0, seq_len, ROW_TILE):
            fc = _bdot(fin_ref[0, t0:t0 + ROW_TILE, :], cs_ref[...])
            g_scr[t0:t0 + ROW_TILE, :] = fc[:, :D_FNET].astype(BF16)
            g_scr[seq_len + t0:seq_len + t0 + ROW_TILE, :] = fc[:, D_FNET:].astype(BF16)

    f_re = jnp.dot(dft_ref[...], g_scr[...], preferred_element_type=F32)
    f_out = _bdot(f_re, wbd_ref[...]) + b_ref[...]
    o_ref[0] = (f_out * _silu(gf_ref[0])).astype(BF16)


def _fnet_call(fin, gf, dft_bf16, cs_bf16, wbd_bf16, b_fnet):
    B, T, _ = fin.shape
    tu = ROW_TILE
    return pl.pallas_call(
        functools.partial(_fnet_kernel, seq_len=T),
        grid=(B, T // tu),
        in_specs=[pl.BlockSpec((1, T, D_FNET), lambda b, u: (b, 0, 0)),
                  pl.BlockSpec((1, tu, D_FNET), lambda b, u: (b, u, 0)),
                  pl.BlockSpec((tu, 2 * T), lambda b, u: (u, 0)),
                  pl.BlockSpec((D_FNET, 2 * D_FNET), lambda b, u: (0, 0)),
                  pl.BlockSpec((D_FNET, D_FNET), lambda b, u: (0, 0)),
                  pl.BlockSpec((1, D_FNET), lambda b, u: (0, 0))],
        out_specs=pl.BlockSpec((1, tu, D_FNET), lambda b, u: (b, u, 0)),
        out_shape=jax.ShapeDtypeStruct((B, T, D_FNET), BF16),
        scratch_shapes=[pltpu.VMEM((2 * T, D_FNET), BF16)],
        compiler_params=pltpu.CompilerParams(dimension_semantics=("arbitrary", "arbitrary"),
                                             vmem_limit_bytes=VMEM_LIMIT),
        name="fnet",
    )(fin, gf, dft_bf16, cs_bf16, wbd_bf16, b_fnet)


def _out_kernel(*refs, has_emb, final_norm):
    it = iter(refs)
    x_ref = next(it)
    emb_ref = next(it) if has_emb else None
    (mod_ref, yf_ref, yb_ref, rkv_ref, lwla_ref, grec_ref, fo_ref, a0_ref, a2hi_ref, a2lo_ref, ka_ref,
     rk_ref, gng_ref, gnb_ref, avg_ref, ones_ref, wout_ref, fng_ref, o_ref) = (next(it) for _ in range(19))

    y = yf_ref[0] + yb_ref[0]
    mu = _dot_exact_rhs(y, avg_ref[...])
    dlt = y - mu
    var = _dot_exact_rhs(dlt * dlt, avg_ref[...])
    y_n = dlt * lax.rsqrt(var + GN_EPS) * gng_ref[...] + gnb_ref[...]
    rkv = rkv_ref[0]
    r = rkv[:, :D_RWKV]
    k = rkv[:, D_RWKV:2 * D_RWKV]
    v = rkv[:, 2 * D_RWKV:]
    ll = lwla_ref[0]
    a_sum = (jax.nn.sigmoid(a0_ref[0:1, :] + _dot3(ll, a2hi_ref[0], a2lo_ref[0]))
             + jax.nn.sigmoid(a0_ref[1:2, :] + _dot3(ll, a2hi_ref[1], a2lo_ref[1])))
    k_sum = k * (2.0 + (a_sum - 2.0) * ka_ref[...])
    bonus = _dot_exact_rhs(r * k_sum * rk_ref[...], ones_ref[...]) * v
    rec_out = (y_n + bonus) * _silu(grec_ref[0])
    mixed = jnp.concatenate([rec_out.astype(BF16), fo_ref[0]], axis=-1)
    out = jnp.dot(mixed, wout_ref[...], preferred_element_type=F32)
    x = x_ref[0]
    if has_emb:
        x = x + emb_ref[...]
    z = x + mod_ref[0, 2:3, :] * out
    if final_norm:
        ms = jnp.mean(z * z, axis=-1, keepdims=True)
        z = z * lax.rsqrt(ms + NORM_EPS) * fng_ref[...]
    o_ref[0] = z


def _out_call(x, emb, mod, y_f, y_b, rkv, lwla, grec, fo, wts, w_out_bf16, final_norm_g, final_norm):
    B, T, _ = x.shape
    tm = ROW_TILE
    has_emb = emb is not None
    per_batch_mod = mod.shape[0] > 1
    tok = lambda w: pl.BlockSpec((1, tm, w), lambda b, i: (b, i, 0))
    full = lambda *shape: pl.BlockSpec(shape, lambda b, i: (0,) * len(shape))
    in_specs = [tok(D_MODEL)]
    args = [x]
    if has_emb:
        in_specs.append(pl.BlockSpec((tm, D_MODEL), lambda b, i: (i, 0)))
        args.append(emb)
    mod_map = (lambda b, i: (b, 0, 0)) if per_batch_mod else (lambda b, i: (0, 0, 0))
    in_specs += [pl.BlockSpec((1, 3, D_MODEL), mod_map), tok(D_RWKV), tok(D_RWKV), tok(3 * D_RWKV),
                 tok(LANES), tok(D_RWKV), tok(D_FNET),
                 full(N_DIR, D_RWKV), full(N_DIR, LANES, D_RWKV), full(N_DIR, LANES, D_RWKV),
                 full(1, D_RWKV), full(1, D_RWKV), full(1, D_RWKV), full(1, D_RWKV),
                 full(D_RWKV, D_RWKV), full(D_RWKV, D_RWKV), full(D_MODEL, D_MODEL), full(1, D_MODEL)]
    args += [mod, y_f, y_b, rkv, lwla, grec, fo, wts["a0"], wts["a2_hi"], wts["a2_lo"], wts["k_a"],
             wts["r_k"], wts["gn_g"], wts["gn_b"], wts["avg_bd"], wts["ones_bd"], w_out_bf16, final_norm_g]
    return pl.pallas_call(
        functools.partial(_out_kernel, has_emb=has_emb, final_norm=final_norm),
        grid=(B, T // tm), in_specs=in_specs,
        out_specs=pl.BlockSpec((1, tm, D_MODEL), lambda b, i: (b, i, 0)),
        out_shape=jax.ShapeDtypeStruct((B, T, D_MODEL), F32),
        compiler_params=pltpu.CompilerParams(dimension_semantics=("arbitrary", "arbitrary"),
                                             vmem_limit_bytes=VMEM_LIMIT),
        name="out_proj",
    )(*args)


def _dft_table(seq_len):
    idx = np.arange(seq_len, dtype=np.int64)
    ang = 2.0 * np.pi * ((idx[:, None] * idx[None, :]) % seq_len).astype(np.float64) / seq_len
    scale = 1.0 / math.sqrt(seq_len)
    return np.concatenate([np.cos(ang) * scale, -np.sin(ang) * scale], axis=1).astype(np.float32)


def _channel_dft_table():
    n = FNET_GROUP
    idx = np.arange(n, dtype=np.int64)
    ang = 2.0 * np.pi * ((idx[:, None] * idx[None, :]) % n).astype(np.float64) / n
    c = np.cos(ang) / math.sqrt(n)
    s = np.sin(ang) / math.sqrt(n)
    eye = np.eye(D_FNET // n)
    return np.concatenate([np.kron(eye, c), np.kron(eye, s)], axis=1).astype(np.float32)


def _sincos_2d(n_tokens, dtype):
    rows = n_tokens // GRID_W
    pos = jnp.arange(rows * GRID_W)
    row = (pos // GRID_W).astype(F32)
    col = (pos % GRID_W).astype(F32)
    quarter = D_MODEL // 4
    freq = jnp.exp(-math.log(POS_BASE) * jnp.arange(quarter, dtype=F32) / quarter)
    ang_r = row[:, None] * freq
    ang_c = col[:, None] * freq
    emb = jnp.concatenate([jnp.sin(ang_r), jnp.cos(ang_r), jnp.sin(ang_c), jnp.cos(ang_c)], axis=-1)
    return emb.astype(dtype)


def _head_block_matrix(value):
    blk = np.kron(np.eye(N_HEADS), np.ones((HEAD_DIM, HEAD_DIM))) * value
    return jnp.asarray(blk, dtype=BF16)


def _pad_lora(w, row_offset):
    out = jnp.zeros((N_DIR, LANES, w.shape[-1]), F32)
    for d in range(N_DIR):
        out = out.at[d, row_offset + d * LORA:row_offset + (d + 1) * LORA, :].set(w[d])
    hi = out.astype(BF16)
    lo = (out - hi.astype(F32)).astype(BF16)
    return hi, lo


def _layer_weights(l, w0, w2, a0, a2, k_k, k_a, r_k, gn_g, gn_b):
    w2_hi, w2_lo = _pad_lora(w2[l], 0)
    a2_hi, a2_lo = _pad_lora(a2[l], N_DIR * LORA)
    tri_f = np.tril(np.ones((CHUNK, CHUNK)))
    tri = jnp.asarray(np.stack([tri_f, tri_f.T]), dtype=BF16)
    return dict(w0=w0[l], w2_hi=w2_hi, w2_lo=w2_lo, a0=a0[l], a2_hi=a2_hi, a2_lo=a2_lo,
                k_k=k_k[l][None], k_a=k_a[l][None], r_k=r_k[l].reshape(1, D_RWKV),
                gn_g=gn_g[l].reshape(1, D_RWKV), gn_b=gn_b[l].reshape(1, D_RWKV), tri=tri,
                ones_bd=_head_block_matrix(1.0), avg_bd=_head_block_matrix(1.0 / HEAD_DIM))


def _state_to_block_diag(s_f, s_b):
    def one(s):
        h = jnp.swapaxes(s.astype(F32), -1, -2)
        b = h.shape[0]
        h = h.reshape(b, N_PAIRS, PAIR, HEAD_DIM, HEAD_DIM)
        z = jnp.zeros_like(h[:, :, 0])
        top = jnp.concatenate([h[:, :, 0], z], axis=-1)
        bot = jnp.concatenate([z, h[:, :, 1]], axis=-1)
        return jnp.concatenate([top, bot], axis=-2)
    return jnp.stack([one(s_f), one(s_b)], axis=1)


def _block_diag_to_state(s_bd):
    b = s_bd.shape[0]
    h0 = s_bd[..., :HEAD_DIM, :HEAD_DIM]
    h1 = s_bd[..., HEAD_DIM:, HEAD_DIM:]
    h = jnp.stack([h0, h1], axis=3).reshape(b, N_DIR, N_HEADS, HEAD_DIM, HEAD_DIM)
    s = jnp.swapaxes(h, -1, -2)
    return s[:, 0], s[:, 1]


def kernel(x_prompt, x_sample, state_rwkv_fwd, state_rwkv_bwd, c, c_ctx, w_ada, b_ada, norm_g, w_in,
           mu_shift, w0, w2, a0, a2, k_k, k_a, r_k, gn_g, gn_b, w_fnet, b_fnet, w_out, final_norm_g):
    depth = w_in.shape[0]
    n_dec = c.shape[0]
    assert n_dec + 1 <= 8
    bp, tp, _ = x_prompt.shape
    bs, ts, _ = x_sample.shape
    cvec = jnp.zeros((8, D_MODEL), F32).at[0].set(c_ctx).at[1:1 + n_dec].set(c)
    emb = _sincos_2d(ts, x_sample.dtype)
    cs_tab = jnp.asarray(_channel_dft_table()).astype(BF16)
    dft_p = jnp.asarray(_dft_table(tp)).astype(BF16)
    dft_s = jnp.asarray(_dft_table(ts)).astype(BF16)
    fng = final_norm_g[None]

    xp, xs = x_prompt, x_sample
    new_f, new_b = [], []
    for l in range(depth):
        mod = _mod_call(cvec, w_ada[l], b_ada[l][None]).reshape(8, 3, D_MODEL)
        mod_ctx, mod_lat = mod[0:1], mod[1:1 + n_dec]
        wts = _layer_weights(l, w0, w2, a0, a2, k_k, k_a, r_k, gn_g, gn_b)
        w_in_b = w_in[l].astype(BF16)
        w_out_b = w_out[l].astype(BF16)
        wbd = jax.scipy.linalg.block_diag(*[w_fnet[l, g] for g in range(w_fnet.shape[1])]).astype(BF16)
        ng, mu, bf = norm_g[l][None], mu_shift[l][None], b_fnet[l][None]
        emb_l = emb if l == 0 else None

        rkv, lwla, grec, fin, gf = _in_proj_call(xp, None, mod_ctx, ng, w_in_b, mu)
        y_f, y_b, s_bd = _scan_call(rkv, lwla, wts, None, True)
        fo = _fnet_call(fin, gf, dft_p, cs_tab, wbd, bf)
        last = l == depth - 1
        xp = _out_call(xp, None, mod_ctx, y_f, y_b, rkv, lwla, grec, fo, wts, w_out_b, fng, last)
        s_f, s_b = _block_diag_to_state(s_bd)
        new_f.append(s_f)
        new_b.append(s_b)

        rkv, lwla, grec, fin, gf = _in_proj_call(xs, emb_l, mod_lat, ng, w_in_b, mu)
        s0 = _state_to_block_diag(state_rwkv_fwd[:, l], state_rwkv_bwd[:, l])
        y_f, y_b = _scan_call(rkv, lwla, wts, s0, False)
        fo = _fnet_call(fin, gf, dft_s, cs_tab, wbd, bf)
        xs = _out_call(xs, emb_l, mod_lat, y_f, y_b, rkv, lwla, grec, fo, wts, w_out_b, fng, last)
    return (xp, xs, jnp.stack(new_f, axis=1), jnp.stack(new_b, axis=1))
```

```python
import functools
import math

import numpy as np
import jax
import jax.numpy as jnp
from jax import lax
from jax.experimental import pallas as pl
from jax.experimental.pallas import tpu as pltpu

F32 = jnp.float32
BF16 = jnp.bfloat16

D_MODEL = 1024
GRID_W = 64
D_RWKV = 512
D_FNET = D_MODEL - D_RWKV
HEAD_DIM = 64
N_HEADS = D_RWKV // HEAD_DIM
FNET_GROUP = 64
LORA = 32
N_DIR = 2
D_SHIFT = 3 * D_RWKV + N_DIR * 2 * LORA
D_IN = D_SHIFT + D_RWKV + 2 * D_FNET
NORM_EPS = 1e-6
GN_EPS = 64e-5
POS_BASE = 10000.0

LANES = 128
PAIR = LANES // HEAD_DIM
N_PAIRS = N_HEADS // PAIR
CHUNK = 64
SUB = 16
SUB_SHIFT = SUB.bit_length() - 1
SCAN_SEQS = 2
ROW_TILE = 256
HALO = 8
VMEM_LIMIT = 48 * 1024 * 1024


def _silu(x):
    return x * jax.nn.sigmoid(x)


def _bdot(a, b):
    return jnp.dot(a.astype(BF16), b.astype(BF16), preferred_element_type=F32)


def _bdot_nt(a, b):
    return lax.dot_general(a.astype(BF16), b.astype(BF16), (((1,), (1,)), ((), ())),
                           preferred_element_type=F32)


def _split2(x):
    hi = x.astype(BF16)
    lo = (x - hi.astype(F32)).astype(BF16)
    return hi, lo


def _dot_exact_rhs(x, m_bf16):
    hi, lo = _split2(x)
    return (jnp.dot(hi, m_bf16, preferred_element_type=F32)
            + jnp.dot(lo, m_bf16, preferred_element_type=F32))


def _dot3(a, b_hi, b_lo):
    a_hi, a_lo = _split2(a)
    return (jnp.dot(a_hi, b_hi, preferred_element_type=F32)
            + jnp.dot(a_hi, b_lo, preferred_element_type=F32)
            + jnp.dot(a_lo, b_hi, preferred_element_type=F32))


def _mod_kernel(c_ref, w_ref, b_ref, o_ref):
    s = _silu(c_ref[...])
    w = w_ref[...]
    w_hi, w_lo = _split2(w)
    o_ref[...] = _dot3(s, w_hi, w_lo) + b_ref[...]


def _mod_call(cvec, w_ada, b_ada):
    n_blk = 3
    return pl.pallas_call(
        _mod_kernel,
        grid=(n_blk,),
        in_specs=[pl.BlockSpec((8, D_MODEL), lambda i: (0, 0)),
                  pl.BlockSpec((D_MODEL, D_MODEL), lambda i: (0, i)),
                  pl.BlockSpec((1, D_MODEL), lambda i: (0, i))],
        out_specs=pl.BlockSpec((8, D_MODEL), lambda i: (0, i)),
        out_shape=jax.ShapeDtypeStruct((8, 3 * D_MODEL), F32),
        compiler_params=pltpu.CompilerParams(dimension_semantics=("arbitrary",),
                                             vmem_limit_bytes=VMEM_LIMIT),
        name="mod",
    )(cvec, w_ada, b_ada)


def _modulated_norm(x, g, scale, shift):
    ms = jnp.mean(x * x, axis=-1, keepdims=True)
    y = x * lax.rsqrt(ms + NORM_EPS) * g
    return y * (1.0 + scale) + shift


def _in_proj_kernel(*refs, has_emb, has_halo, n_tiles):
    it = iter(refs)
    x_ref = next(it)
    xp_ref = next(it) if has_halo else None
    xn_ref = next(it) if has_halo else None
    emb_ref = next(it) if has_emb else None
    embp_ref = next(it) if (has_emb and has_halo) else None
    embn_ref = next(it) if (has_emb and has_halo) else None
    mod_ref, g_ref, w_ref, mu_ref = next(it), next(it), next(it), next(it)
    rkv_ref, lwla_ref, grec_ref, fin_ref, gf_ref = next(it), next(it), next(it), next(it), next(it)

    i = pl.program_id(1)
    g = g_ref[...]
    shift = mod_ref[0, 0:1, :]
    scale = mod_ref[0, 1:2, :]
    x = x_ref[0]
    if has_emb:
        x = x + emb_ref[...]
    tm = x.shape[0]
    h = _modulated_norm(x, g, scale, shift)
    p = jnp.dot(h.astype(BF16), w_ref[...], preferred_element_type=F32)
    ps = p[:, :D_SHIFT]
    if has_halo:
        xh = jnp.concatenate([xp_ref[0], xn_ref[0]], axis=0)
        if has_emb:
            xh = xh + jnp.concatenate([embp_ref[...], embn_ref[...]], axis=0)
        hh = _modulated_norm(xh, g, scale, shift)
        ph = jnp.dot(hh.astype(BF16), w_ref[:, :D_SHIFT], preferred_element_type=F32)
        prev_row = jnp.where(i > 0, ph[HALO - 1:HALO, :], 0.0)
        next_row = jnp.where(i < n_tiles - 1, ph[HALO:HALO + 1, :], 0.0)
    else:
        prev_row = jnp.zeros((1, D_SHIFT), F32)
        next_row = jnp.zeros((1, D_SHIFT), F32)
    row = lax.broadcasted_iota(jnp.int32, (tm, D_SHIFT), 0)
    prev = jnp.where(row == 0, prev_row, pltpu.roll(ps, 1, 0))
    nxt = jnp.where(row == tm - 1, next_row, pltpu.roll(ps, tm - 1, 0))
    p_rec = ps + mu_ref[...] * (0.5 * (prev + nxt) - ps)
    rkv_ref[0] = p_rec[:, :3 * D_RWKV]
    lwla_ref[0] = p_rec[:, 3 * D_RWKV:]
    grec_ref[0] = p[:, D_SHIFT:D_SHIFT + D_RWKV]
    fin_ref[0] = p[:, D_SHIFT + D_RWKV:D_SHIFT + D_RWKV + D_FNET]
    gf_ref[0] = p[:, D_SHIFT + D_RWKV + D_FNET:]


def _in_proj_call(x, emb, mod, norm_g, w_in_bf16, mu):
    B, T, _ = x.shape
    tm = ROW_TILE
    n_tiles = T // tm
    has_halo = n_tiles > 1
    has_emb = emb is not None
    per_batch_mod = mod.shape[0] > 1
    blocks_per_tile = tm // HALO
    last_halo_block = T // HALO - 1

    in_specs = [pl.BlockSpec((1, tm, D_MODEL), lambda b, i: (b, i, 0))]
    args = [x]
    if has_halo:
        in_specs += [
            pl.BlockSpec((1, HALO, D_MODEL), lambda b, i: (b, jnp.maximum(i * blocks_per_tile - 1, 0), 0)),
            pl.BlockSpec((1, HALO, D_MODEL),
                         lambda b, i: (b, jnp.minimum((i + 1) * blocks_per_tile, last_halo_block), 0))]
        args += [x, x]
    if has_emb:
        in_specs.append(pl.BlockSpec((tm, D_MODEL), lambda b, i: (i, 0)))
        args.append(emb)
        if has_halo:
            in_specs += [
                pl.BlockSpec((HALO, D_MODEL), lambda b, i: (jnp.maximum(i * blocks_per_tile - 1, 0), 0)),
                pl.BlockSpec((HALO, D_MODEL),
                             lambda b, i: (jnp.minimum((i + 1) * blocks_per_tile, last_halo_block), 0))]
            args += [emb, emb]
    mod_map = (lambda b, i: (b, 0, 0)) if per_batch_mod else (lambda b, i: (0, 0, 0))
    in_specs += [pl.BlockSpec((1, 3, D_MODEL), mod_map),
                 pl.BlockSpec((1, D_MODEL), lambda b, i: (0, 0)),
                 pl.BlockSpec((D_MODEL, D_IN), lambda b, i: (0, 0)),
                 pl.BlockSpec((1, D_SHIFT), lambda b, i: (0, 0))]
    args += [mod, norm_g, w_in_bf16, mu]
    widths = (3 * D_RWKV, N_DIR * 2 * LORA, D_RWKV, D_FNET, D_FNET)
    out_specs = [pl.BlockSpec((1, tm, w), lambda b, i: (b, i, 0)) for w in widths]
    out_shape = [jax.ShapeDtypeStruct((B, T, w), F32) for w in widths]
    kern = functools.partial(_in_proj_kernel, has_emb=has_emb, has_halo=has_halo, n_tiles=n_tiles)
    return pl.pallas_call(
        kern, grid=(B, n_tiles), in_specs=in_specs, out_specs=out_specs, out_shape=out_shape,
        compiler_params=pltpu.CompilerParams(dimension_semantics=("arbitrary", "arbitrary"),
                                             vmem_limit_bytes=VMEM_LIMIT),
        name="in_proj",
    )(*args)


def _same_block(i, j, size):
    shift = size.bit_length() - 1
    return jnp.right_shift(i, shift) == jnp.right_shift(j, shift)


def _block_diag_rows(y, head0_lanes):
    return jnp.concatenate([jnp.where(head0_lanes, y, 0.0), jnp.where(head0_lanes, 0.0, y)], axis=0)


def _scan_kernel(*refs, zero_init, write_state, n_chunks, n_seq):
    it = iter(refs)
    rkv_refs = [next(it), None]
    lwla_refs = [next(it), None]
    rkv_refs[1] = next(it)
    lwla_refs[1] = next(it)
    w0_ref, w2_ref, a0_ref, a2_ref = (next(it) for _ in range(4))
    kk_ref, ka_ref, tri_ref, ones_ref = (next(it) for _ in range(4))
    s0_ref = None if zero_init else next(it)
    y_refs = [next(it), next(it)]
    sfin_refs = [next(it), next(it)] if write_state else None
    h_scr = next(it)

    j = pl.program_id(1)

    @pl.when(j == 0)
    def _():
        if zero_init:
            h_scr[...] = jnp.zeros_like(h_scr)
        else:
            h_scr[...] = s0_ref[...]

    C = CHUNK
    row = lax.broadcasted_iota(jnp.int32, (C, LANES), 0)
    lane = lax.broadcasted_iota(jnp.int32, (C, LANES), 1)
    s_idx = jnp.bitwise_and(lane, HEAD_DIM - 1)
    head0 = lane < HEAD_DIM
    row_c = lax.broadcasted_iota(jnp.int32, (SUB, LANES), 0)
    lane_c = lax.broadcasted_iota(jnp.int32, (SUB, LANES), 1)
    col_c = jnp.bitwise_and(lane_c, SUB - 1)
    blk_c = jnp.right_shift(jnp.bitwise_and(lane_c, HEAD_DIM - 1), SUB_SHIFT)
    lane_blk_c = jnp.right_shift(lane_c, SUB_SHIFT)
    eye_c = (row_c == col_c).astype(F32)

    def bd_c(y):
        return jnp.concatenate([jnp.where(lane_blk_c == g, y, 0.0) for g in range(LANES // SUB)], axis=0)

    row2 = lax.broadcasted_iota(jnp.int32, (LANES, LANES), 0)
    lane2 = lax.broadcasted_iota(jnp.int32, (LANES, LANES), 1)
    same_head = (row2 < HEAD_DIM) == (lane2 < HEAD_DIM)
    eye2 = row2 == lane2
    decay_scale = math.exp(-0.5)

    probs = []
    for n, d in [(n, d) for n in range(n_seq) for d in range(N_DIR)]:
        rkv = rkv_refs[d][n]
        r = rkv[:, :D_RWKV]
        k = rkv[:, D_RWKV:2 * D_RWKV]
        v = rkv[:, 2 * D_RWKV:]
        ll = lwla_refs[d][n]
        z_w = w0_ref[d:d + 1, :] + _bdot(jnp.tanh(ll), w2_ref[d])
        logw = -decay_scale * jax.nn.sigmoid(z_w)
        a = jax.nn.sigmoid(a0_ref[d:d + 1, :] + _bdot(ll, a2_ref[d]))
        kd = k * (1.0 + (a - 1.0) * ka_ref[...])
        kkr = k * kk_ref[...]
        ssq = _bdot(kkr * kkr, ones_ref[...])
        kk = kkr * lax.rsqrt(jnp.maximum(ssq, 1e-24))
        bvec = kk * a
        lw_hi, lw_lo = _split2(logw)
        tri = tri_ref[d]
        cum = (jnp.dot(tri, lw_hi, preferred_element_type=F32)
               + jnp.dot(tri, lw_lo, preferred_element_type=F32))
        cum_prev = cum - logw
        tot = cum[C - 1:C, :] if d == 0 else cum[0:1, :]
        kap_t = kk * jnp.exp(cum_prev)
        r_t = r * jnp.exp(cum)
        e_neg = jnp.exp(-cum)
        gam = jnp.exp(tot)
        b_t = bvec * e_neg
        k_t = kd * e_neg
        e_rem = gam * e_neg
        b_h = bvec * e_rem
        k_h = kd * e_rem
        if d == 0:
            strict, incl = row > s_idx, row >= s_idx
        else:
            strict, incl = row < s_idx, row <= s_idx

        for p in range(N_PAIRS):
            sl = slice(p * LANES, (p + 1) * LANES)
            probs.append(dict(n=n, d=d, p=p, sl=sl, strict=strict, incl=incl, kap=kap_t[:, sl], r=r_t[:, sl],
                              v=v[:, sl], b_t=b_t[:, sl], k_t=k_t[:, sl], b_h=b_h[:, sl], k_h=k_h[:, sl],
                              gam=gam[:, sl]))

    bd = lambda y: _block_diag_rows(y, head0)
    for q in probs:
        lhs = jnp.concatenate([q["kap"], q["r"]], axis=0)
        rhs = jnp.concatenate([bd(q["b_t"]), bd(q["k_t"])], axis=0)
        lm = _bdot_nt(lhs, rhs)
        q["l_b"] = jnp.where(q["strict"], lm[:C, :LANES], 0.0)
        q["m_b"] = jnp.where(q["incl"], lm[C:, :LANES], 0.0)
        q["lm_k"] = jnp.concatenate([jnp.where(q["strict"], lm[:C, LANES:], 0.0),
                                     jnp.where(q["incl"], lm[C:, LANES:], 0.0)], axis=0)
    for q in probs:
        lmv = _bdot(q["lm_k"], bd(q["v"]))
        q["l_kv"], q["m_kv"] = lmv[:C], lmv[C:]
    for q in probs:
        l_b = q["l_b"]
        l_c = l_b[0:SUB]
        for jb in range(1, C // SUB):
            l_c = jnp.where(blk_c == jb, l_b[jb * SUB:(jb + 1) * SUB], l_c)
        q["l_c"] = l_c
        q["t_c"] = eye_c - jnp.where(_same_block(row_c, col_c, 2), l_c, 0.0)
    s = 2
    while s < SUB:
        off_mask = _same_block(row_c, col_c, 2 * s) & ~_same_block(row_c, col_c, s)
        for q in probs:
            q["et"] = _bdot(jnp.where(off_mask, q["l_c"], 0.0), bd_c(q["t_c"]))
        for q in probs:
            q["t_c"] = q["t_c"] - _bdot(q["t_c"], bd_c(q["et"]))
        s *= 2
    for q in probs:
        q["tinv"] = jnp.concatenate([jnp.where(blk_c == jb, q["t_c"], 0.0) for jb in range(C // SUB)], axis=0)
    while s < C:
        off_mask = _same_block(row, s_idx, 2 * s) & ~_same_block(row, s_idx, s)
        for q in probs:
            q["et"] = _bdot(jnp.where(off_mask, q["l_b"], 0.0), bd(q["tinv"]))
        for q in probs:
            q["tinv"] = q["tinv"] - _bdot(q["tinv"], bd(q["et"]))
        s *= 2
    for q in probs:
        wu = _bdot(q["tinv"], jnp.concatenate([bd(q["kap"]), bd(q["l_kv"])], axis=1))
        q["w_m"], q["u_n"] = wu[:, :LANES], wu[:, LANES:]
    for q in probs:
        qy = _bdot(q["m_b"], jnp.concatenate([bd(q["w_m"]), bd(q["u_n"])], axis=1))
        q["r_q"] = q["r"] - qy[:, :LANES]
        q["y_i"] = q["m_kv"] - qy[:, LANES:]
    for q in probs:
        t2 = jnp.concatenate([q["b_h"], q["k_h"]], axis=0).T
        ag = _bdot(t2, jnp.concatenate(
            [jnp.concatenate([q["w_m"], -q["u_n"]], axis=1),
             jnp.concatenate([jnp.zeros_like(q["v"]), q["v"]], axis=1)], axis=0))
        gam_p = jnp.broadcast_to(q["gam"], (LANES, LANES))
        q["a_bd"] = jnp.where(same_head, jnp.where(eye2, gam_p, 0.0) - ag[:, :LANES], 0.0)
        q["g_bd"] = jnp.where(same_head, ag[:, LANES:], 0.0)
    for q in probs:
        h_hi, h_lo = _split2(h_scr[q["n"], q["d"], q["p"]])
        lhs2 = jnp.concatenate([q["r_q"], q["a_bd"]], axis=0).astype(BF16)
        yh = (jnp.dot(lhs2, h_hi, preferred_element_type=F32)
              + jnp.dot(lhs2, h_lo, preferred_element_type=F32))
        y_refs[q["d"]][q["n"], :, q["sl"]] = q["y_i"] + yh[:C]
        h_scr[q["n"], q["d"], q["p"]] = yh[C:] + q["g_bd"]

    if write_state:
        @pl.when(j == n_chunks - 1)
        def _():
            for n, d, p in [(n, d, p) for n in range(n_seq) for d in range(N_DIR) for p in range(N_PAIRS)]:
                ht = h_scr[n, d, p].T
                sfin_refs[d][n, 0, PAIR * p] = ht[:HEAD_DIM, :HEAD_DIM]
                sfin_refs[d][n, 0, PAIR * p + 1] = ht[HEAD_DIM:, HEAD_DIM:]


def _scan_call(rkv, lwla, wts, s0_bd, write_state):
    B, T, _ = rkv.shape
    nc = T // CHUNK
    ns = min(SCAN_SEQS, B)
    assert B % ns == 0
    zero_init = s0_bd is None
    fwd = lambda b, j: (b, j, 0)
    bwd = lambda b, j: (b, nc - 1 - j, 0)
    full = lambda *shape: pl.BlockSpec(shape, lambda b, j: (0,) * len(shape))
    in_specs = [pl.BlockSpec((ns, CHUNK, 3 * D_RWKV), fwd), pl.BlockSpec((ns, CHUNK, LANES), fwd),
                pl.BlockSpec((ns, CHUNK, 3 * D_RWKV), bwd), pl.BlockSpec((ns, CHUNK, LANES), bwd),
                full(N_DIR, D_RWKV), full(N_DIR, LANES, D_RWKV),
                full(N_DIR, D_RWKV), full(N_DIR, LANES, D_RWKV),
                full(1, D_RWKV), full(1, D_RWKV), full(N_DIR, CHUNK, CHUNK), full(D_RWKV, D_RWKV)]
    args = [rkv, lwla, rkv, lwla, wts["w0"], wts["w2"], wts["a0"], wts["a2"],
            wts["k_k"], wts["k_a"], wts["tri"], wts["ones_bd"]]
    state_block = (ns, N_DIR, N_PAIRS, LANES, LANES)
    if not zero_init:
        in_specs.append(pl.BlockSpec(state_block, lambda b, j: (b, 0, 0, 0, 0)))
        args.append(s0_bd)
    out_specs = [pl.BlockSpec((ns, CHUNK, D_RWKV), fwd), pl.BlockSpec((ns, CHUNK, D_RWKV), bwd)]
    out_shape = [jax.ShapeDtypeStruct((B, T, D_RWKV), F32), jax.ShapeDtypeStruct((B, T, D_RWKV), F32)]
    if write_state:
        final_block = (ns, 1, N_HEADS, HEAD_DIM, HEAD_DIM)
        out_specs += [pl.BlockSpec(final_block, lambda b, j: (b, 0, 0, 0, 0))] * N_DIR
        out_shape += [jax.ShapeDtypeStruct((B,) + final_block[1:], F32)] * N_DIR
    kern = functools.partial(_scan_kernel, zero_init=zero_init, write_state=write_state, n_chunks=nc,
                             n_seq=ns)
    return pl.pallas_call(
        kern, grid=(B // ns, nc), in_specs=in_specs, out_specs=out_specs, out_shape=out_shape,
        scratch_shapes=[pltpu.VMEM(state_block, F32)],
        compiler_params=pltpu.CompilerParams(dimension_semantics=("arbitrary", "arbitrary"),
                                             vmem_limit_bytes=VMEM_LIMIT),
        name="scan",
    )(*args)


def _fnet_kernel(fin_ref, gf_ref, dft_ref, cs_ref, wbd_ref, b_ref, o_ref, g_scr, *, seq_len):
    u = pl.program_id(1)

    @pl.when(u == 0)
    def _():
        for t0 in range(0, seq_len, ROW_TILE):
            fc = _bdot(fin_ref[0, t0:t0 + ROW_TILE, :], cs_ref[...])
            g_scr[t0:t0 + ROW_TILE, :] = fc[:, :D_FNET].astype(BF16)
            g_scr[seq_len + t0:seq_len + t0 + ROW_TILE, :] = fc[:, D_FNET:].astype(BF16)

    f_re = jnp.dot(dft_ref[...], g_scr[...], preferred_element_type=F32)
    f_out = _bdot(f_re, wbd_ref[...]) + b_ref[...]
    o_ref[0] = (f_out * _silu(gf_ref[0])).astype(BF16)


def _fnet_call(fin, gf, dft_bf16, cs_bf16, wbd_bf16, b_fnet):
    B, T, _ = fin.shape
    tu = ROW_TILE
    return pl.pallas_call(
        functools.partial(_fnet_kernel, seq_len=T),
        grid=(B, T // tu),
        in_specs=[pl.BlockSpec((1, T, D_FNET), lambda b, u: (b, 0, 0)),
                  pl.BlockSpec((1, tu, D_FNET), lambda b, u: (b, u, 0)),
                  pl.BlockSpec((tu, 2 * T), lambda b, u: (u, 0)),
                  pl.BlockSpec((D_FNET, 2 * D_FNET), lambda b, u: (0, 0)),
                  pl.BlockSpec((D_FNET, D_FNET), lambda b, u: (0, 0)),
                  pl.BlockSpec((1, D_FNET), lambda b, u: (0, 0))],
        out_specs=pl.BlockSpec((1, tu, D_FNET), lambda b, u: (b, u, 0)),
        out_shape=jax.ShapeDtypeStruct((B, T, D_FNET), BF16),
        scratch_shapes=[pltpu.VMEM((2 * T, D_FNET), BF16)],
        compiler_params=pltpu.CompilerParams(dimension_semantics=("arbitrary", "arbitrary"),
                                             vmem_limit_bytes=VMEM_LIMIT),
        name="fnet",
    )(fin, gf, dft_bf16, cs_bf16, wbd_bf16, b_fnet)


def _out_kernel(*refs, has_emb, final_norm):
    it = iter(refs)
    x_ref = next(it)
    emb_ref = next(it) if has_emb else None
    (mod_ref, yf_ref, yb_ref, rkv_ref, lwla_ref, grec_ref, fo_ref, a0_ref, a2_ref, ka_ref,
     rk_ref, gng_ref, gnb_ref, avg_ref, ones_ref, wout_ref, fng_ref, o_ref) = (next(it) for _ in range(18))

    y = yf_ref[0] + yb_ref[0]
    mu = _dot_exact_rhs(y, avg_ref[...])
    dlt = y - mu
    var = _dot_exact_rhs(dlt * dlt, avg_ref[...])
    y_n = dlt * lax.rsqrt(var + GN_EPS) * gng_ref[...] + gnb_ref[...]
    rkv = rkv_ref[0]
    r = rkv[:, :D_RWKV]
    k = rkv[:, D_RWKV:2 * D_RWKV]
    v = rkv[:, 2 * D_RWKV:]
    ll = lwla_ref[0]
    a_sum = (jax.nn.sigmoid(a0_ref[0:1, :] + _bdot(ll, a2_ref[0]))
             + jax.nn.sigmoid(a0_ref[1:2, :] + _bdot(ll, a2_ref[1])))
    k_sum = k * (2.0 + (a_sum - 2.0) * ka_ref[...])
    bonus = _dot_exact_rhs(r * k_sum * rk_ref[...], ones_ref[...]) * v
    rec_out = (y_n + bonus) * _silu(grec_ref[0])
    mixed = jnp.concatenate([rec_out.astype(BF16), fo_ref[0]], axis=-1)
    out = jnp.dot(mixed, wout_ref[...], preferred_element_type=F32)
    x = x_ref[0]
    if has_emb:
        x = x + emb_ref[...]
    z = x + mod_ref[0, 2:3, :] * out
    if final_norm:
        ms = jnp.mean(z * z, axis=-1, keepdims=True)
        z = z * lax.rsqrt(ms + NORM_EPS) * fng_ref[...]
    o_ref[0] = z


def _out_call(x, emb, mod, y_f, y_b, rkv, lwla, grec, fo, wts, w_out_bf16, final_norm_g, final_norm):
    B, T, _ = x.shape
    tm = ROW_TILE
    has_emb = emb is not None
    per_batch_mod = mod.shape[0] > 1
    tok = lambda w: pl.BlockSpec((1, tm, w), lambda b, i: (b, i, 0))
    full = lambda *shape: pl.BlockSpec(shape, lambda b, i: (0,) * len(shape))
    in_specs = [tok(D_MODEL)]
    args = [x]
    if has_emb:
        in_specs.append(pl.BlockSpec((tm, D_MODEL), lambda b, i: (i, 0)))
        args.append(emb)
    mod_map = (lambda b, i: (b, 0, 0)) if per_batch_mod else (lambda b, i: (0, 0, 0))
    in_specs += [pl.BlockSpec((1, 3, D_MODEL), mod_map), tok(D_RWKV), tok(D_RWKV), tok(3 * D_RWKV),
                 tok(LANES), tok(D_RWKV), tok(D_FNET),
                 full(N_DIR, D_RWKV), full(N_DIR, LANES, D_RWKV),
                 full(1, D_RWKV), full(1, D_RWKV), full(1, D_RWKV), full(1, D_RWKV),
                 full(D_RWKV, D_RWKV), full(D_RWKV, D_RWKV), full(D_MODEL, D_MODEL), full(1, D_MODEL)]
    args += [mod, y_f, y_b, rkv, lwla, grec, fo, wts["a0"], wts["a2"], wts["k_a"],
             wts["r_k"], wts["gn_g"], wts["gn_b"], wts["avg_bd"], wts["ones_bd"], w_out_bf16, final_norm_g]
    return pl.pallas_call(
        functools.partial(_out_kernel, has_emb=has_emb, final_norm=final_norm),
        grid=(B, T // tm), in_specs=in_specs,
        out_specs=pl.BlockSpec((1, tm, D_MODEL), lambda b, i: (b, i, 0)),
        out_shape=jax.ShapeDtypeStruct((B, T, D_MODEL), F32),
        compiler_params=pltpu.CompilerParams(dimension_semantics=("arbitrary", "arbitrary"),
                                             vmem_limit_bytes=VMEM_LIMIT),
        name="out_proj",
    )(*args)


def _dft_table(seq_len):
    idx = np.arange(seq_len, dtype=np.int64)
    ang = 2.0 * np.pi * ((idx[:, None] * idx[None, :]) % seq_len).astype(np.float64) / seq_len
    scale = 1.0 / math.sqrt(seq_len)
    return np.concatenate([np.cos(ang) * scale, -np.sin(ang) * scale], axis=1).astype(np.float32)


def _channel_dft_table():
    n = FNET_GROUP
    idx = np.arange(n, dtype=np.int64)
    ang = 2.0 * np.pi * ((idx[:, None] * idx[None, :]) % n).astype(np.float64) / n
    c = np.cos(ang) / math.sqrt(n)
    s = np.sin(ang) / math.sqrt(n)
    eye = np.eye(D_FNET // n)
    return np.concatenate([np.kron(eye, c), np.kron(eye, s)], axis=1).astype(np.float32)


def _sincos_2d(n_tokens):
    rows = n_tokens // GRID_W
    pos = np.arange(rows * GRID_W)
    row = (pos // GRID_W).astype(np.float32)
    col = (pos % GRID_W).astype(np.float32)
    quarter = D_MODEL // 4
    freq = np.exp(np.float32(-math.log(POS_BASE)) * np.arange(quarter, dtype=np.float32) / np.float32(quarter))
    ang_r = row[:, None] * freq
    ang_c = col[:, None] * freq
    return np.concatenate([np.sin(ang_r), np.cos(ang_r), np.sin(ang_c), np.cos(ang_c)], axis=-1).astype(np.float32)


def _head_block_matrix(value):
    blk = np.kron(np.eye(N_HEADS), np.ones((HEAD_DIM, HEAD_DIM))) * value
    return jnp.asarray(blk, dtype=BF16)


def _pad_lora(w, row_offset):
    out = jnp.zeros((N_DIR, LANES, w.shape[-1]), F32)
    for d in range(N_DIR):
        out = out.at[d, row_offset + d * LORA:row_offset + (d + 1) * LORA, :].set(w[d])
    return out.astype(BF16)


def _layer_weights(l, w0, w2, a0, a2, k_k, k_a, r_k, gn_g, gn_b):
    w2_p = _pad_lora(w2[l], 0)
    a2_p = _pad_lora(a2[l], N_DIR * LORA)
    tri_f = np.tril(np.ones((CHUNK, CHUNK)))
    tri = jnp.asarray(np.stack([tri_f, tri_f.T]), dtype=BF16)
    return dict(w0=w0[l], w2=w2_p, a0=a0[l], a2=a2_p,
                k_k=k_k[l][None], k_a=k_a[l][None], r_k=r_k[l].reshape(1, D_RWKV),
                gn_g=gn_g[l].reshape(1, D_RWKV), gn_b=gn_b[l].reshape(1, D_RWKV), tri=tri,
                ones_bd=_head_block_matrix(1.0), avg_bd=_head_block_matrix(1.0 / HEAD_DIM))


def _state_to_block_diag(s_f, s_b):
    def one(s):
        h = jnp.swapaxes(s.astype(F32), -1, -2)
        b = h.shape[0]
        h = h.reshape(b, N_PAIRS, PAIR, HEAD_DIM, HEAD_DIM)
        z = jnp.zeros_like(h[:, :, 0])
        top = jnp.concatenate([h[:, :, 0], z], axis=-1)
        bot = jnp.concatenate([z, h[:, :, 1]], axis=-1)
        return jnp.concatenate([top, bot], axis=-2)
    return jnp.stack([one(s_f), one(s_b)], axis=1)


def kernel(x_prompt, x_sample, state_rwkv_fwd, state_rwkv_bwd, c, c_ctx, w_ada, b_ada, norm_g, w_in,
           mu_shift, w0, w2, a0, a2, k_k, k_a, r_k, gn_g, gn_b, w_fnet, b_fnet, w_out, final_norm_g):
    depth = w_in.shape[0]
    n_dec = c.shape[0]
    assert n_dec + 1 <= 8
    bp, tp, _ = x_prompt.shape
    bs, ts, _ = x_sample.shape
    cvec = jnp.zeros((8, D_MODEL), F32).at[0].set(c_ctx).at[1:1 + n_dec].set(c)
    emb = jnp.asarray(_sincos_2d(ts)).astype(x_sample.dtype)
    cs_tab = jnp.asarray(_channel_dft_table()).astype(BF16)
    dft_p = jnp.asarray(_dft_table(tp)).astype(BF16)
    dft_s = jnp.asarray(_dft_table(ts)).astype(BF16)
    fng = final_norm_g[None]

    xp, xs = x_prompt, x_sample
    new_f, new_b = [], []
    for l in range(depth):
        mod = _mod_call(cvec, w_ada[l], b_ada[l][None]).reshape(8, 3, D_MODEL)
        mod_ctx, mod_lat = mod[0:1], mod[1:1 + n_dec]
        wts = _layer_weights(l, w0, w2, a0, a2, k_k, k_a, r_k, gn_g, gn_b)
        w_in_b = w_in[l].astype(BF16)
        w_out_b = w_out[l].astype(BF16)
        wbd = jax.scipy.linalg.block_diag(*[w_fnet[l, g] for g in range(w_fnet.shape[1])]).astype(BF16)
        ng, mu, bf = norm_g[l][None], mu_shift[l][None], b_fnet[l][None]
        emb_l = emb if l == 0 else None

        rkv, lwla, grec, fin, gf = _in_proj_call(xp, None, mod_ctx, ng, w_in_b, mu)
        y_f, y_b, s_f, s_b = _scan_call(rkv, lwla, wts, None, True)
        fo = _fnet_call(fin, gf, dft_p, cs_tab, wbd, bf)
        last = l == depth - 1
        xp = _out_call(xp, None, mod_ctx, y_f, y_b, rkv, lwla, grec, fo, wts, w_out_b, fng, last)
        new_f.append(s_f)
        new_b.append(s_b)

        rkv, lwla, grec, fin, gf = _in_proj_call(xs, emb_l, mod_lat, ng, w_in_b, mu)
        s0 = _state_to_block_diag(state_rwkv_fwd[:, l], state_rwkv_bwd[:, l])
        y_f, y_b = _scan_call(rkv, lwla, wts, s0, False)
        fo = _fnet_call(fin, gf, dft_s, cs_tab, wbd, bf)
        xs = _out_call(xs, emb_l, mod_lat, y_f, y_b, rkv, lwla, grec, fo, wts, w_out_b, fng, last)
    return (xp, xs, jnp.concatenate(new_f, axis=1), jnp.concatenate(new_b, axis=1))
```

```python
import functools
import math

import numpy as np
import jax
import jax.numpy as jnp
from jax import lax
from jax.experimental import pallas as pl
from jax.experimental.pallas import tpu as pltpu

F32 = jnp.float32
BF16 = jnp.bfloat16

D_MODEL = 1024
GRID_W = 64
D_RWKV = 512
D_FNET = D_MODEL - D_RWKV
HEAD_DIM = 64
N_HEADS = D_RWKV // HEAD_DIM
FNET_GROUP = 64
LORA = 32
N_DIR = 2
D_SHIFT = 3 * D_RWKV + N_DIR * 2 * LORA
D_IN = D_SHIFT + D_RWKV + 2 * D_FNET
NORM_EPS = 1e-6
GN_EPS = 64e-5
POS_BASE = 10000.0

LANES = 128
PAIR = LANES // HEAD_DIM
N_PAIRS = N_HEADS // PAIR
CHUNK = 64
SUB = 16
SUB_SHIFT = SUB.bit_length() - 1
SCAN_SEQS = 2
ROW_TILE = 256
HALO = 8
VMEM_LIMIT = 48 * 1024 * 1024


def _silu(x):
    return x * jax.nn.sigmoid(x)


def _bdot(a, b):
    return jnp.dot(a.astype(BF16), b.astype(BF16), preferred_element_type=F32)


def _bdot_nt(a, b):
    return lax.dot_general(a.astype(BF16), b.astype(BF16), (((1,), (1,)), ((), ())),
                           preferred_element_type=F32)


def _split2(x):
    hi = x.astype(BF16)
    lo = (x - hi.astype(F32)).astype(BF16)
    return hi, lo


def _dot3(a, b_hi, b_lo):
    a_hi, a_lo = _split2(a)
    return (jnp.dot(a_hi, b_hi, preferred_element_type=F32)
            + jnp.dot(a_hi, b_lo, preferred_element_type=F32)
            + jnp.dot(a_lo, b_hi, preferred_element_type=F32))


def _mod_kernel(c_ref, w_ref, b_ref, o_ref):
    s = _silu(c_ref[...])
    w = w_ref[...]
    w_hi, w_lo = _split2(w)
    o_ref[...] = _dot3(s, w_hi, w_lo) + b_ref[...]


def _mod_call(cvec, w_ada, b_ada):
    n_blk = 3
    return pl.pallas_call(
        _mod_kernel,
        grid=(n_blk,),
        in_specs=[pl.BlockSpec((8, D_MODEL), lambda i: (0, 0)),
                  pl.BlockSpec((D_MODEL, D_MODEL), lambda i: (0, i)),
                  pl.BlockSpec((1, D_MODEL), lambda i: (0, i))],
        out_specs=pl.BlockSpec((8, D_MODEL), lambda i: (0, i)),
        out_shape=jax.ShapeDtypeStruct((8, 3 * D_MODEL), F32),
        compiler_params=pltpu.CompilerParams(dimension_semantics=("arbitrary",),
                                             vmem_limit_bytes=VMEM_LIMIT),
        name="mod",
    )(cvec, w_ada, b_ada)


def _modulated_norm(x, g, scale, shift):
    ms = jnp.mean(x * x, axis=-1, keepdims=True)
    y = x * lax.rsqrt(ms + NORM_EPS) * g
    return y * (1.0 + scale) + shift


def _in_proj_kernel(*refs, has_emb, has_halo, n_tiles):
    it = iter(refs)
    x_ref = next(it)
    xp_ref = next(it) if has_halo else None
    xn_ref = next(it) if has_halo else None
    emb_ref = next(it) if has_emb else None
    embp_ref = next(it) if (has_emb and has_halo) else None
    embn_ref = next(it) if (has_emb and has_halo) else None
    mod_ref, g_ref, w_ref, mu_ref = next(it), next(it), next(it), next(it)
    rk_ref, v_ref, lwla_ref, grec_ref, fin_ref, gf_ref = (next(it) for _ in range(6))

    i = pl.program_id(1)
    g = g_ref[...]
    shift = mod_ref[0, 0:1, :]
    scale = mod_ref[0, 1:2, :]
    x = x_ref[0]
    if has_emb:
        x = x + emb_ref[...]
    tm = x.shape[0]
    h = _modulated_norm(x, g, scale, shift)
    p = jnp.dot(h.astype(BF16), w_ref[...], preferred_element_type=F32)
    ps = p[:, :D_SHIFT]
    if has_halo:
        xh = jnp.concatenate([xp_ref[0], xn_ref[0]], axis=0)
        if has_emb:
            xh = xh + jnp.concatenate([embp_ref[...], embn_ref[...]], axis=0)
        hh = _modulated_norm(xh, g, scale, shift)
        ph = jnp.dot(hh.astype(BF16), w_ref[:, :D_SHIFT], preferred_element_type=F32)
        prev_row = jnp.where(i > 0, ph[HALO - 1:HALO, :], 0.0)
        next_row = jnp.where(i < n_tiles - 1, ph[HALO:HALO + 1, :], 0.0)
    else:
        prev_row = jnp.zeros((1, D_SHIFT), F32)
        next_row = jnp.zeros((1, D_SHIFT), F32)
    row = lax.broadcasted_iota(jnp.int32, (tm, D_SHIFT), 0)
    prev = jnp.where(row == 0, prev_row, pltpu.roll(ps, 1, 0))
    nxt = jnp.where(row == tm - 1, next_row, pltpu.roll(ps, tm - 1, 0))
    p_rec = ps + mu_ref[...] * (0.5 * (prev + nxt) - ps)
    rk_ref[0] = p_rec[:, :2 * D_RWKV]
    v_ref[0] = p_rec[:, 2 * D_RWKV:3 * D_RWKV].astype(BF16)
    lwla_ref[0] = p_rec[:, 3 * D_RWKV:]
    grec_ref[0] = p[:, D_SHIFT:D_SHIFT + D_RWKV].astype(BF16)
    fin_ref[0] = p[:, D_SHIFT + D_RWKV:D_SHIFT + D_RWKV + D_FNET].astype(BF16)
    gf_ref[0] = p[:, D_SHIFT + D_RWKV + D_FNET:].astype(BF16)


def _in_proj_call(x, emb, mod, norm_g, w_in_bf16, mu):
    B, T, _ = x.shape
    tm = ROW_TILE
    n_tiles = T // tm
    has_halo = n_tiles > 1
    has_emb = emb is not None
    per_batch_mod = mod.shape[0] > 1
    blocks_per_tile = tm // HALO
    last_halo_block = T // HALO - 1

    in_specs = [pl.BlockSpec((1, tm, D_MODEL), lambda b, i: (b, i, 0))]
    args = [x]
    if has_halo:
        in_specs += [
            pl.BlockSpec((1, HALO, D_MODEL), lambda b, i: (b, jnp.maximum(i * blocks_per_tile - 1, 0), 0)),
            pl.BlockSpec((1, HALO, D_MODEL),
                         lambda b, i: (b, jnp.minimum((i + 1) * blocks_per_tile, last_halo_block), 0))]
        args += [x, x]
    if has_emb:
        in_specs.append(pl.BlockSpec((tm, D_MODEL), lambda b, i: (i, 0)))
        args.append(emb)
        if has_halo:
            in_specs += [
                pl.BlockSpec((HALO, D_MODEL), lambda b, i: (jnp.maximum(i * blocks_per_tile - 1, 0), 0)),
                pl.BlockSpec((HALO, D_MODEL),
                             lambda b, i: (jnp.minimum((i + 1) * blocks_per_tile, last_halo_block), 0))]
            args += [emb, emb]
    mod_map = (lambda b, i: (b, 0, 0)) if per_batch_mod else (lambda b, i: (0, 0, 0))
    in_specs += [pl.BlockSpec((1, 3, D_MODEL), mod_map),
                 pl.BlockSpec((1, D_MODEL), lambda b, i: (0, 0)),
                 pl.BlockSpec((D_MODEL, D_IN), lambda b, i: (0, 0)),
                 pl.BlockSpec((1, D_SHIFT), lambda b, i: (0, 0))]
    args += [mod, norm_g, w_in_bf16, mu]
    outs = ((2 * D_RWKV, F32), (D_RWKV, BF16), (N_DIR * 2 * LORA, F32), (D_RWKV, BF16), (D_FNET, BF16),
            (D_FNET, BF16))
    out_specs = [pl.BlockSpec((1, tm, w), lambda b, i: (b, i, 0)) for w, _ in outs]
    out_shape = [jax.ShapeDtypeStruct((B, T, w), dt) for w, dt in outs]
    kern = functools.partial(_in_proj_kernel, has_emb=has_emb, has_halo=has_halo, n_tiles=n_tiles)
    return pl.pallas_call(
        kern, grid=(B, n_tiles), in_specs=in_specs, out_specs=out_specs, out_shape=out_shape,
        compiler_params=pltpu.CompilerParams(dimension_semantics=("arbitrary", "arbitrary"),
                                             vmem_limit_bytes=VMEM_LIMIT),
        name="in_proj",
    )(*args)


def _same_block(i, j, size):
    shift = size.bit_length() - 1
    return jnp.right_shift(i, shift) == jnp.right_shift(j, shift)


def _block_diag_rows(y, head0_lanes):
    return jnp.concatenate([jnp.where(head0_lanes, y, 0.0), jnp.where(head0_lanes, 0.0, y)], axis=0)


def _scan_kernel(*refs, zero_init, write_state, n_chunks, n_seq):
    it = iter(refs)
    rk_refs, v_refs, lwla_refs = [None, None], [None, None], [None, None]
    for d in range(N_DIR):
        rk_refs[d], v_refs[d], lwla_refs[d] = next(it), next(it), next(it)
    w0_ref, w2_ref, a0_ref, a2_ref = (next(it) for _ in range(4))
    kk_ref, ka_ref, tri_ref, ones_ref = (next(it) for _ in range(4))
    s0_ref = None if zero_init else next(it)
    y_refs = [next(it), next(it)]
    sfin_refs = [next(it), next(it)] if write_state else None
    h_scr = next(it)

    j = pl.program_id(1)

    @pl.when(j == 0)
    def _():
        if zero_init:
            h_scr[...] = jnp.zeros_like(h_scr)
        else:
            h_scr[...] = s0_ref[...]

    C = CHUNK
    row = lax.broadcasted_iota(jnp.int32, (C, LANES), 0)
    lane = lax.broadcasted_iota(jnp.int32, (C, LANES), 1)
    s_idx = jnp.bitwise_and(lane, HEAD_DIM - 1)
    head0 = lane < HEAD_DIM
    row_c = lax.broadcasted_iota(jnp.int32, (SUB, LANES), 0)
    lane_c = lax.broadcasted_iota(jnp.int32, (SUB, LANES), 1)
    col_c = jnp.bitwise_and(lane_c, SUB - 1)
    blk_c = jnp.right_shift(jnp.bitwise_and(lane_c, HEAD_DIM - 1), SUB_SHIFT)
    lane_blk_c = jnp.right_shift(lane_c, SUB_SHIFT)
    eye_c = (row_c == col_c).astype(F32)

    def bd_c(y):
        return jnp.concatenate([jnp.where(lane_blk_c == g, y, 0.0) for g in range(LANES // SUB)], axis=0)

    row2 = lax.broadcasted_iota(jnp.int32, (LANES, LANES), 0)
    lane2 = lax.broadcasted_iota(jnp.int32, (LANES, LANES), 1)
    same_head = (row2 < HEAD_DIM) == (lane2 < HEAD_DIM)
    eye2 = row2 == lane2
    decay_scale = math.exp(-0.5)

    probs = []
    for n, d in [(n, d) for n in range(n_seq) for d in range(N_DIR)]:
        rk = rk_refs[d][n]
        r = rk[:, :D_RWKV]
        k = rk[:, D_RWKV:]
        v = v_refs[d][n]
        ll = lwla_refs[d][n]
        z_w = w0_ref[d:d + 1, :] + _bdot(jnp.tanh(ll), w2_ref[d])
        logw = -decay_scale * jax.nn.sigmoid(z_w)
        a = jax.nn.sigmoid(a0_ref[d:d + 1, :] + _bdot(ll, a2_ref[d]))
        kd = k * (1.0 + (a - 1.0) * ka_ref[...])
        kkr = k * kk_ref[...]
        ssq = _bdot(kkr * kkr, ones_ref[...])
        kk = kkr * lax.rsqrt(jnp.maximum(ssq, 1e-24))
        bvec = kk * a
        lw_hi, lw_lo = _split2(logw)
        tri = tri_ref[d]
        cum = (jnp.dot(tri, lw_hi, preferred_element_type=F32)
               + jnp.dot(tri, lw_lo, preferred_element_type=F32))
        cum_prev = cum - logw
        tot = cum[C - 1:C, :] if d == 0 else cum[0:1, :]
        kap_t = kk * jnp.exp(cum_prev)
        r_t = r * jnp.exp(cum)
        e_neg = jnp.exp(-cum)
        gam = jnp.exp(tot)
        b_t = bvec * e_neg
        k_t = kd * e_neg
        e_rem = gam * e_neg
        b_h = bvec * e_rem
        k_h = kd * e_rem
        if d == 0:
            strict, incl = row > s_idx, row >= s_idx
        else:
            strict, incl = row < s_idx, row <= s_idx

        for p in range(N_PAIRS):
            sl = slice(p * LANES, (p + 1) * LANES)
            probs.append(dict(n=n, d=d, p=p, sl=sl, strict=strict, incl=incl, kap=kap_t[:, sl], r=r_t[:, sl],
                              v=v[:, sl], b_t=b_t[:, sl], k_t=k_t[:, sl], b_h=b_h[:, sl], k_h=k_h[:, sl],
                              gam=gam[:, sl]))

    bd = lambda y: _block_diag_rows(y, head0)
    for q in probs:
        lhs = jnp.concatenate([q["kap"], q["r"]], axis=0)
        rhs = jnp.concatenate([bd(q["b_t"]), bd(q["k_t"])], axis=0)
        lm = _bdot_nt(lhs, rhs)
        q["l_b"] = jnp.where(q["strict"], lm[:C, :LANES], 0.0)
        q["m_b"] = jnp.where(q["incl"], lm[C:, :LANES], 0.0)
        q["lm_k"] = jnp.concatenate([jnp.where(q["strict"], lm[:C, LANES:], 0.0),
                                     jnp.where(q["incl"], lm[C:, LANES:], 0.0)], axis=0)
    for q in probs:
        lmv = _bdot(q["lm_k"], bd(q["v"]))
        q["l_kv"], q["m_kv"] = lmv[:C], lmv[C:]
    for q in probs:
        l_b = q["l_b"]
        l_c = l_b[0:SUB]
        for jb in range(1, C // SUB):
            l_c = jnp.where(blk_c == jb, l_b[jb * SUB:(jb + 1) * SUB], l_c)
        q["l_c"] = l_c
        q["t_c"] = eye_c - jnp.where(_same_block(row_c, col_c, 2), l_c, 0.0)
    s = 2
    while s < SUB:
        off_mask = _same_block(row_c, col_c, 2 * s) & ~_same_block(row_c, col_c, s)
        for q in probs:
            q["et"] = _bdot(jnp.where(off_mask, q["l_c"], 0.0), bd_c(q["t_c"]))
        for q in probs:
            q["t_c"] = q["t_c"] - _bdot(q["t_c"], bd_c(q["et"]))
        s *= 2
    for q in probs:
        q["tinv"] = jnp.concatenate([jnp.where(blk_c == jb, q["t_c"], 0.0) for jb in range(C // SUB)], axis=0)
    while s < C:
        off_mask = _same_block(row, s_idx, 2 * s) & ~_same_block(row, s_idx, s)
        for q in probs:
            q["et"] = _bdot(jnp.where(off_mask, q["l_b"], 0.0), bd(q["tinv"]))
        for q in probs:
            q["tinv"] = q["tinv"] - _bdot(q["tinv"], bd(q["et"]))
        s *= 2
    for q in probs:
        wu = _bdot(q["tinv"], jnp.concatenate([bd(q["kap"]), bd(q["l_kv"])], axis=1))
        q["w_m"], q["u_n"] = wu[:, :LANES], wu[:, LANES:]
    for q in probs:
        qy = _bdot(q["m_b"], jnp.concatenate([bd(q["w_m"]), bd(q["u_n"])], axis=1))
        q["r_q"] = q["r"] - qy[:, :LANES]
        q["y_i"] = q["m_kv"] - qy[:, LANES:]
    for q in probs:
        t2 = jnp.concatenate([q["b_h"], q["k_h"]], axis=0).T
        ag = _bdot(t2, jnp.concatenate(
            [jnp.concatenate([q["w_m"], -q["u_n"]], axis=1).astype(BF16),
             jnp.concatenate([jnp.zeros_like(q["v"]), q["v"]], axis=1)], axis=0))
        gam_p = jnp.broadcast_to(q["gam"], (LANES, LANES))
        q["a_bd"] = jnp.where(same_head, jnp.where(eye2, gam_p, 0.0) - ag[:, :LANES], 0.0)
        q["g_bd"] = jnp.where(same_head, ag[:, LANES:], 0.0)
    for q in probs:
        lhs2 = jnp.concatenate([q["r_q"], q["a_bd"]], axis=0)
        yh = _bdot(lhs2, h_scr[q["n"], q["d"], q["p"]])
        y_refs[q["d"]][q["n"], :, q["sl"]] = q["y_i"] + yh[:C]
        h_scr[q["n"], q["d"], q["p"]] = yh[C:] + q["g_bd"]

    if write_state:
        @pl.when(j == n_chunks - 1)
        def _():
            for n, d, p in [(n, d, p) for n in range(n_seq) for d in range(N_DIR) for p in range(N_PAIRS)]:
                ht = h_scr[n, d, p].T
                sfin_refs[d][n, 0, PAIR * p] = ht[:HEAD_DIM, :HEAD_DIM]
                sfin_refs[d][n, 0, PAIR * p + 1] = ht[HEAD_DIM:, HEAD_DIM:]


def _scan_call(rk, v, lwla, wts, s0_bd, write_state):
    B, T, _ = rk.shape
    nc = T // CHUNK
    ns = min(SCAN_SEQS, B)
    assert B % ns == 0
    zero_init = s0_bd is None
    fwd = lambda b, j: (b, j, 0)
    bwd = lambda b, j: (b, nc - 1 - j, 0)
    full = lambda *shape: pl.BlockSpec(shape, lambda b, j: (0,) * len(shape))
    tok = lambda w, m: pl.BlockSpec((ns, CHUNK, w), m)
    in_specs = [tok(2 * D_RWKV, fwd), tok(D_RWKV, fwd), tok(LANES, fwd),
                tok(2 * D_RWKV, bwd), tok(D_RWKV, bwd), tok(LANES, bwd),
                full(N_DIR, D_RWKV), full(N_DIR, LANES, D_RWKV),
                full(N_DIR, D_RWKV), full(N_DIR, LANES, D_RWKV),
                full(1, D_RWKV), full(1, D_RWKV), full(N_DIR, CHUNK, CHUNK), full(D_RWKV, D_RWKV)]
    args = [rk, v, lwla, rk, v, lwla, wts["w0"], wts["w2"], wts["a0"], wts["a2"],
            wts["k_k"], wts["k_a"], wts["tri"], wts["ones_bd"]]
    state_block = (ns, N_DIR, N_PAIRS, LANES, LANES)
    if not zero_init:
        in_specs.append(pl.BlockSpec(state_block, lambda b, j: (b, 0, 0, 0, 0)))
        args.append(s0_bd)
    out_specs = [pl.BlockSpec((ns, CHUNK, D_RWKV), fwd), pl.BlockSpec((ns, CHUNK, D_RWKV), bwd)]
    out_shape = [jax.ShapeDtypeStruct((B, T, D_RWKV), F32), jax.ShapeDtypeStruct((B, T, D_RWKV), F32)]
    if write_state:
        final_block = (ns, 1, N_HEADS, HEAD_DIM, HEAD_DIM)
        out_specs += [pl.BlockSpec(final_block, lambda b, j: (b, 0, 0, 0, 0))] * N_DIR
        out_shape += [jax.ShapeDtypeStruct((B,) + final_block[1:], F32)] * N_DIR
    kern = functools.partial(_scan_kernel, zero_init=zero_init, write_state=write_state, n_chunks=nc,
                             n_seq=ns)
    return pl.pallas_call(
        kern, grid=(B // ns, nc), in_specs=in_specs, out_specs=out_specs, out_shape=out_shape,
        scratch_shapes=[pltpu.VMEM(state_block, F32)],
        compiler_params=pltpu.CompilerParams(dimension_semantics=("arbitrary", "arbitrary"),
                                             vmem_limit_bytes=VMEM_LIMIT),
        name="scan",
    )(*args)


def _fnet_kernel(fin_ref, gf_ref, dft_ref, cs_ref, wbd_ref, b_ref, o_ref, g_scr, *, seq_len):
    u = pl.program_id(1)

    @pl.when(u == 0)
    def _():
        for t0 in range(0, seq_len, ROW_TILE):
            fc = _bdot(fin_ref[0, t0:t0 + ROW_TILE, :], cs_ref[...])
            g_scr[t0:t0 + ROW_TILE, :] = fc[:, :D_FNET].astype(BF16)
            g_scr[seq_len + t0:seq_len + t0 + ROW_TILE, :] = fc[:, D_FNET:].astype(BF16)

    f_re = jnp.dot(dft_ref[...], g_scr[...], preferred_element_type=F32)
    f_out = _bdot(f_re, wbd_ref[...]) + b_ref[...]
    o_ref[0] = (f_out * _silu(gf_ref[0].astype(F32))).astype(BF16)


def _fnet_call(fin, gf, dft_bf16, cs_bf16, wbd_bf16, b_fnet):
    B, T, _ = fin.shape
    tu = ROW_TILE
    return pl.pallas_call(
        functools.partial(_fnet_kernel, seq_len=T),
        grid=(B, T // tu),
        in_specs=[pl.BlockSpec((1, T, D_FNET), lambda b, u: (b, 0, 0)),
                  pl.BlockSpec((1, tu, D_FNET), lambda b, u: (b, u, 0)),
                  pl.BlockSpec((tu, 2 * T), lambda b, u: (u, 0)),
                  pl.BlockSpec((D_FNET, 2 * D_FNET), lambda b, u: (0, 0)),
                  pl.BlockSpec((D_FNET, D_FNET), lambda b, u: (0, 0)),
                  pl.BlockSpec((1, D_FNET), lambda b, u: (0, 0))],
        out_specs=pl.BlockSpec((1, tu, D_FNET), lambda b, u: (b, u, 0)),
        out_shape=jax.ShapeDtypeStruct((B, T, D_FNET), BF16),
        scratch_shapes=[pltpu.VMEM((2 * T, D_FNET), BF16)],
        compiler_params=pltpu.CompilerParams(dimension_semantics=("arbitrary", "arbitrary"),
                                             vmem_limit_bytes=VMEM_LIMIT),
        name="fnet",
    )(fin, gf, dft_bf16, cs_bf16, wbd_bf16, b_fnet)


def _out_kernel(*refs, has_emb, final_norm):
    it = iter(refs)
    x_ref = next(it)
    emb_ref = next(it) if has_emb else None
    (mod_ref, yf_ref, yb_ref, rk_ref, v_ref, lwla_ref, grec_ref, fo_ref, a0_ref, a2_ref, ka_ref,
     rkw_ref, gng_ref, gnb_ref, avg_ref, ones_ref, wout_ref, fng_ref, o_ref) = (next(it) for _ in range(19))

    y = yf_ref[0] + yb_ref[0]
    mu = _bdot(y, avg_ref[...])
    dlt = y - mu
    var = _bdot(dlt * dlt, avg_ref[...])
    y_n = dlt * lax.rsqrt(var + GN_EPS) * gng_ref[...] + gnb_ref[...]
    rk = rk_ref[0]
    r = rk[:, :D_RWKV]
    k = rk[:, D_RWKV:]
    ll = lwla_ref[0]
    a_sum = (jax.nn.sigmoid(a0_ref[0:1, :] + _bdot(ll, a2_ref[0]))
             + jax.nn.sigmoid(a0_ref[1:2, :] + _bdot(ll, a2_ref[1])))
    k_sum = k * (2.0 + (a_sum - 2.0) * ka_ref[...])
    bonus = _bdot(r * k_sum * rkw_ref[...], ones_ref[...]) * v_ref[0].astype(F32)
    rec_out = (y_n + bonus) * _silu(grec_ref[0].astype(F32))
    mixed = jnp.concatenate([rec_out.astype(BF16), fo_ref[0]], axis=-1)
    out = jnp.dot(mixed, wout_ref[...], preferred_element_type=F32)
    x = x_ref[0]
    if has_emb:
        x = x + emb_ref[...]
    z = x + mod_ref[0, 2:3, :] * out
    if final_norm:
        ms = jnp.mean(z * z, axis=-1, keepdims=True)
        z = z * lax.rsqrt(ms + NORM_EPS) * fng_ref[...]
    o_ref[0] = z


def _out_call(x, emb, mod, y_f, y_b, rk, v, lwla, grec, fo, wts, w_out_bf16, final_norm_g, final_norm):
    B, T, _ = x.shape
    tm = ROW_TILE
    has_emb = emb is not None
    per_batch_mod = mod.shape[0] > 1
    tok = lambda w: pl.BlockSpec((1, tm, w), lambda b, i: (b, i, 0))
    full = lambda *shape: pl.BlockSpec(shape, lambda b, i: (0,) * len(shape))
    in_specs = [tok(D_MODEL)]
    args = [x]
    if has_emb:
        in_specs.append(pl.BlockSpec((tm, D_MODEL), lambda b, i: (i, 0)))
        args.append(emb)
    mod_map = (lambda b, i: (b, 0, 0)) if per_batch_mod else (lambda b, i: (0, 0, 0))
    in_specs += [pl.BlockSpec((1, 3, D_MODEL), mod_map), tok(D_RWKV), tok(D_RWKV), tok(2 * D_RWKV),
                 tok(D_RWKV), tok(LANES), tok(D_RWKV), tok(D_FNET),
                 full(N_DIR, D_RWKV), full(N_DIR, LANES, D_RWKV),
                 full(1, D_RWKV), full(1, D_RWKV), full(1, D_RWKV), full(1, D_RWKV),
                 full(D_RWKV, D_RWKV), full(D_RWKV, D_RWKV), full(D_MODEL, D_MODEL), full(1, D_MODEL)]
    args += [mod, y_f, y_b, rk, v, lwla, grec, fo, wts["a0"], wts["a2"], wts["k_a"],
             wts["r_k"], wts["gn_g"], wts["gn_b"], wts["avg_bd"], wts["ones_bd"], w_out_bf16, final_norm_g]
    return pl.pallas_call(
        functools.partial(_out_kernel, has_emb=has_emb, final_norm=final_norm),
        grid=(B, T // tm), in_specs=in_specs,
        out_specs=pl.BlockSpec((1, tm, D_MODEL), lambda b, i: (b, i, 0)),
        out_shape=jax.ShapeDtypeStruct((B, T, D_MODEL), F32),
        compiler_params=pltpu.CompilerParams(dimension_semantics=("arbitrary", "arbitrary"),
                                             vmem_limit_bytes=VMEM_LIMIT),
        name="out_proj",
    )(*args)


def _dft_table(seq_len):
    idx = np.arange(seq_len, dtype=np.int64)
    ang = 2.0 * np.pi * ((idx[:, None] * idx[None, :]) % seq_len).astype(np.float64) / seq_len
    scale = 1.0 / math.sqrt(seq_len)
    return np.concatenate([np.cos(ang) * scale, -np.sin(ang) * scale], axis=1).astype(np.float32)


def _channel_dft_table():
    n = FNET_GROUP
    idx = np.arange(n, dtype=np.int64)
    ang = 2.0 * np.pi * ((idx[:, None] * idx[None, :]) % n).astype(np.float64) / n
    c = np.cos(ang) / math.sqrt(n)
    s = np.sin(ang) / math.sqrt(n)
    eye = np.eye(D_FNET // n)
    return np.concatenate([np.kron(eye, c), np.kron(eye, s)], axis=1).astype(np.float32)


def _sincos_2d(n_tokens):
    rows = n_tokens // GRID_W
    pos = np.arange(rows * GRID_W)
    row = (pos // GRID_W).astype(np.float32)
    col = (pos % GRID_W).astype(np.float32)
    quarter = D_MODEL // 4
    freq = np.exp(np.float32(-math.log(POS_BASE)) * np.arange(quarter, dtype=np.float32) / np.float32(quarter))
    ang_r = row[:, None] * freq
    ang_c = col[:, None] * freq
    return np.concatenate([np.sin(ang_r), np.cos(ang_r), np.sin(ang_c), np.cos(ang_c)], axis=-1).astype(np.float32)


def _head_block_matrix(value):
    blk = np.kron(np.eye(N_HEADS), np.ones((HEAD_DIM, HEAD_DIM))) * value
    return jnp.asarray(blk, dtype=BF16)


def _pad_lora(w, row_offset):
    out = jnp.zeros((N_DIR, LANES, w.shape[-1]), F32)
    for d in range(N_DIR):
        out = out.at[d, row_offset + d * LORA:row_offset + (d + 1) * LORA, :].set(w[d])
    return out.astype(BF16)


def _layer_weights(l, w0, w2, a0, a2, k_k, k_a, r_k, gn_g, gn_b):
    w2_p = _pad_lora(w2[l], 0)
    a2_p = _pad_lora(a2[l], N_DIR * LORA)
    tri_f = np.tril(np.ones((CHUNK, CHUNK)))
    tri = jnp.asarray(np.stack([tri_f, tri_f.T]), dtype=BF16)
    return dict(w0=w0[l], w2=w2_p, a0=a0[l], a2=a2_p,
                k_k=k_k[l][None], k_a=k_a[l][None], r_k=r_k[l].reshape(1, D_RWKV),
                gn_g=gn_g[l].reshape(1, D_RWKV), gn_b=gn_b[l].reshape(1, D_RWKV), tri=tri,
                ones_bd=_head_block_matrix(1.0), avg_bd=_head_block_matrix(1.0 / HEAD_DIM))


def _state_to_block_diag(s_f, s_b):
    def one(s):
        h = jnp.swapaxes(s.astype(F32), -1, -2)
        b = h.shape[0]
        h = h.reshape(b, N_PAIRS, PAIR, HEAD_DIM, HEAD_DIM)
        z = jnp.zeros_like(h[:, :, 0])
        top = jnp.concatenate([h[:, :, 0], z], axis=-1)
        bot = jnp.concatenate([z, h[:, :, 1]], axis=-1)
        return jnp.concatenate([top, bot], axis=-2)
    return jnp.stack([one(s_f), one(s_b)], axis=1)


def kernel(x_prompt, x_sample, state_rwkv_fwd, state_rwkv_bwd, c, c_ctx, w_ada, b_ada, norm_g, w_in,
           mu_shift, w0, w2, a0, a2, k_k, k_a, r_k, gn_g, gn_b, w_fnet, b_fnet, w_out, final_norm_g):
    depth = w_in.shape[0]
    n_dec = c.shape[0]
    assert n_dec + 1 <= 8
    bp, tp, _ = x_prompt.shape
    bs, ts, _ = x_sample.shape
    cvec = jnp.zeros((8, D_MODEL), F32).at[0].set(c_ctx).at[1:1 + n_dec].set(c)
    emb = jnp.asarray(_sincos_2d(ts)).astype(x_sample.dtype)
    cs_tab = jnp.asarray(_channel_dft_table()).astype(BF16)
    dft_p = jnp.asarray(_dft_table(tp)).astype(BF16)
    dft_s = jnp.asarray(_dft_table(ts)).astype(BF16)
    fng = final_norm_g[None]

    xp, xs = x_prompt, x_sample
    new_f, new_b = [], []
    for l in range(depth):
        mod = _mod_call(cvec, w_ada[l], b_ada[l][None]).reshape(8, 3, D_MODEL)
        mod_ctx, mod_lat = mod[0:1], mod[1:1 + n_dec]
        wts = _layer_weights(l, w0, w2, a0, a2, k_k, k_a, r_k, gn_g, gn_b)
        w_in_b = w_in[l].astype(BF16)
        w_out_b = w_out[l].astype(BF16)
        wbd = jax.scipy.linalg.block_diag(*[w_fnet[l, g] for g in range(w_fnet.shape[1])]).astype(BF16)
        ng, mu, bf = norm_g[l][None], mu_shift[l][None], b_fnet[l][None]
        emb_l = emb if l == 0 else None

        rk, v, lwla, grec, fin, gf = _in_proj_call(xp, None, mod_ctx, ng, w_in_b, mu)
        y_f, y_b, s_f, s_b = _scan_call(rk, v, lwla, wts, None, True)
        fo = _fnet_call(fin, gf, dft_p, cs_tab, wbd, bf)
        last = l == depth - 1
        xp = _out_call(xp, None, mod_ctx, y_f, y_b, rk, v, lwla, grec, fo, wts, w_out_b, fng, last)
        new_f.append(s_f)
        new_b.append(s_b)

        rk, v, lwla, grec, fin, gf = _in_proj_call(xs, emb_l, mod_lat, ng, w_in_b, mu)
        s0 = _state_to_block_diag(state_rwkv_fwd[:, l], state_rwkv_bwd[:, l])
        y_f, y_b = _scan_call(rk, v, lwla, wts, s0, False)
        fo = _fnet_call(fin, gf, dft_s, cs_tab, wbd, bf)
        xs = _out_call(xs, emb_l, mod_lat, y_f, y_b, rk, v, lwla, grec, fo, wts, w_out_b, fng, last)
    return (xp, xs, jnp.concatenate(new_f, axis=1), jnp.concatenate(new_b, axis=1))
```

```python
import functools
import math

import numpy as np
import jax
import jax.numpy as jnp
from jax import lax
from jax.experimental import pallas as pl
from jax.experimental.pallas import tpu as pltpu

F32 = jnp.float32
BF16 = jnp.bfloat16

D_MODEL = 1024
GRID_W = 64
D_RWKV = 512
D_FNET = D_MODEL - D_RWKV
HEAD_DIM = 64
N_HEADS = D_RWKV // HEAD_DIM
FNET_GROUP = 64
LORA = 32
N_DIR = 2
D_SHIFT = 3 * D_RWKV + N_DIR * 2 * LORA
D_IN = D_SHIFT + D_RWKV + 2 * D_FNET
NORM_EPS = 1e-6
GN_EPS = 64e-5
POS_BASE = 10000.0

LANES = 128
PAIR = LANES // HEAD_DIM
N_PAIRS = N_HEADS // PAIR
CHUNK = 64
SUB = 16
SUB_SHIFT = SUB.bit_length() - 1
SCAN_SEQS = 2
ROW_TILE = 512
FNET_ROWS = 1024
HALO = 8
VMEM_LIMIT = 56 * 1024 * 1024


def _silu(x):
    return x * jax.nn.sigmoid(x)


def _bdot(a, b):
    return jnp.dot(a.astype(BF16), b.astype(BF16), preferred_element_type=F32)


def _bdot_nt(a, b):
    return lax.dot_general(a.astype(BF16), b.astype(BF16), (((1,), (1,)), ((), ())),
                           preferred_element_type=F32)


def _split2(x):
    hi = x.astype(BF16)
    lo = (x - hi.astype(F32)).astype(BF16)
    return hi, lo


def _dot3(a, b_hi, b_lo):
    a_hi, a_lo = _split2(a)
    return (jnp.dot(a_hi, b_hi, preferred_element_type=F32)
            + jnp.dot(a_hi, b_lo, preferred_element_type=F32)
            + jnp.dot(a_lo, b_hi, preferred_element_type=F32))


def _mod_kernel(c_ref, w_ref, b_ref, o_ref):
    s = _silu(c_ref[...])
    w = w_ref[...]
    w_hi, w_lo = _split2(w)
    o_ref[...] = _dot3(s, w_hi, w_lo) + b_ref[...]


def _mod_call(cvec, w_ada, b_ada):
    n_blk = 3
    return pl.pallas_call(
        _mod_kernel,
        grid=(n_blk,),
        in_specs=[pl.BlockSpec((8, D_MODEL), lambda i: (0, 0)),
                  pl.BlockSpec((D_MODEL, D_MODEL), lambda i: (0, i)),
                  pl.BlockSpec((1, D_MODEL), lambda i: (0, i))],
        out_specs=pl.BlockSpec((8, D_MODEL), lambda i: (0, i)),
        out_shape=jax.ShapeDtypeStruct((8, 3 * D_MODEL), F32),
        compiler_params=pltpu.CompilerParams(dimension_semantics=("arbitrary",),
                                             vmem_limit_bytes=VMEM_LIMIT),
        name="mod",
    )(cvec, w_ada, b_ada)


def _modulated_norm(x, g, scale, shift):
    ms = jnp.mean(x * x, axis=-1, keepdims=True)
    y = x * lax.rsqrt(ms + NORM_EPS) * g
    return y * (1.0 + scale) + shift


def _in_proj_kernel(*refs, has_emb, has_halo, n_tiles):
    it = iter(refs)
    x_ref = next(it)
    xp_ref = next(it) if has_halo else None
    xn_ref = next(it) if has_halo else None
    emb_ref = next(it) if has_emb else None
    embp_ref = next(it) if (has_emb and has_halo) else None
    embn_ref = next(it) if (has_emb and has_halo) else None
    mod_ref, g_ref, w_ref, mu_ref = next(it), next(it), next(it), next(it)
    rk_ref, v_ref, lwla_ref, grec_ref, fin_ref, gf_ref = (next(it) for _ in range(6))

    i = pl.program_id(1)
    g = g_ref[...]
    shift = mod_ref[0, 0:1, :]
    scale = mod_ref[0, 1:2, :]
    nb, tt, _ = x_ref.shape
    tm = nb * tt
    x = x_ref[...].reshape(tm, D_MODEL)
    if has_emb:
        x = x + emb_ref[...]
    h = _modulated_norm(x, g, scale, shift)
    p = jnp.dot(h.astype(BF16), w_ref[...], preferred_element_type=F32)
    ps = p[:, :D_SHIFT]
    if has_halo:
        xh = jnp.concatenate([xp_ref[0], xn_ref[0]], axis=0)
        if has_emb:
            xh = xh + jnp.concatenate([embp_ref[...], embn_ref[...]], axis=0)
        hh = _modulated_norm(xh, g, scale, shift)
        ph = jnp.dot(hh.astype(BF16), w_ref[:, :D_SHIFT], preferred_element_type=F32)
        prev_row = jnp.where(i > 0, ph[HALO - 1:HALO, :], 0.0)
        next_row = jnp.where(i < n_tiles - 1, ph[HALO:HALO + 1, :], 0.0)
    else:
        prev_row = jnp.zeros((1, D_SHIFT), F32)
        next_row = jnp.zeros((1, D_SHIFT), F32)
    pos = lax.rem(lax.broadcasted_iota(jnp.int32, (tm, D_SHIFT), 0), tt)
    prev = jnp.where(pos == 0, prev_row, pltpu.roll(ps, 1, 0))
    nxt = jnp.where(pos == tt - 1, next_row, pltpu.roll(ps, tm - 1, 0))
    p_rec = ps + mu_ref[...] * (0.5 * (prev + nxt) - ps)

    def put(ref, val):
        ref[...] = val.astype(ref.dtype).reshape(ref.shape)

    put(rk_ref, p_rec[:, :2 * D_RWKV])
    put(v_ref, p_rec[:, 2 * D_RWKV:3 * D_RWKV])
    put(lwla_ref, p_rec[:, 3 * D_RWKV:])
    put(grec_ref, p[:, D_SHIFT:D_SHIFT + D_RWKV])
    put(fin_ref, p[:, D_SHIFT + D_RWKV:D_SHIFT + D_RWKV + D_FNET])
    put(gf_ref, p[:, D_SHIFT + D_RWKV + D_FNET:])


def _in_proj_call(x, emb, mod, norm_g, w_in_bf16, mu):
    B, T, _ = x.shape
    tt = min(T, ROW_TILE)
    nb = max(1, ROW_TILE // T)
    n_tiles = T // tt
    has_halo = n_tiles > 1
    has_emb = emb is not None
    per_batch_mod = mod.shape[0] > 1
    assert B % nb == 0 and T % tt == 0 and (nb == 1 or not (has_emb or per_batch_mod))
    tm = tt
    blocks_per_tile = tm // HALO
    last_halo_block = T // HALO - 1

    in_specs = [pl.BlockSpec((nb, tt, D_MODEL), lambda b, i: (b, i, 0))]
    args = [x]
    if has_halo:
        in_specs += [
            pl.BlockSpec((1, HALO, D_MODEL), lambda b, i: (b, jnp.maximum(i * blocks_per_tile - 1, 0), 0)),
            pl.BlockSpec((1, HALO, D_MODEL),
                         lambda b, i: (b, jnp.minimum((i + 1) * blocks_per_tile, last_halo_block), 0))]
        args += [x, x]
    if has_emb:
        in_specs.append(pl.BlockSpec((tm, D_MODEL), lambda b, i: (i, 0)))
        args.append(emb)
        if has_halo:
            in_specs += [
                pl.BlockSpec((HALO, D_MODEL), lambda b, i: (jnp.maximum(i * blocks_per_tile - 1, 0), 0)),
                pl.BlockSpec((HALO, D_MODEL),
                             lambda b, i: (jnp.minimum((i + 1) * blocks_per_tile, last_halo_block), 0))]
            args += [emb, emb]
    mod_map = (lambda b, i: (b, 0, 0)) if per_batch_mod else (lambda b, i: (0, 0, 0))
    in_specs += [pl.BlockSpec((1, 3, D_MODEL), mod_map),
                 pl.BlockSpec((1, D_MODEL), lambda b, i: (0, 0)),
                 pl.BlockSpec((D_MODEL, D_IN), lambda b, i: (0, 0), pipeline_mode=pl.Buffered(1)),
                 pl.BlockSpec((1, D_SHIFT), lambda b, i: (0, 0))]
    args += [mod, norm_g, w_in_bf16, mu]
    outs = ((2 * D_RWKV, BF16), (D_RWKV, BF16), (N_DIR * 2 * LORA, F32), (D_RWKV, BF16), (D_FNET, BF16),
            (D_FNET, BF16))
    out_specs = [pl.BlockSpec((nb, tt, w), lambda b, i: (b, i, 0)) for w, _ in outs]
    out_shape = [jax.ShapeDtypeStruct((B, T, w), dt) for w, dt in outs]
    kern = functools.partial(_in_proj_kernel, has_emb=has_emb, has_halo=has_halo, n_tiles=n_tiles)
    return pl.pallas_call(
        kern, grid=(B // nb, n_tiles), in_specs=in_specs, out_specs=out_specs, out_shape=out_shape,
        compiler_params=pltpu.CompilerParams(dimension_semantics=("arbitrary", "arbitrary"),
                                             vmem_limit_bytes=VMEM_LIMIT),
        name="in_proj",
    )(*args)


def _same_block(i, j, size):
    shift = size.bit_length() - 1
    return jnp.right_shift(i, shift) == jnp.right_shift(j, shift)


def _block_diag_rows(y, head0_lanes):
    return jnp.concatenate([jnp.where(head0_lanes, y, 0.0), jnp.where(head0_lanes, 0.0, y)], axis=0)


def _scan_kernel(*refs, zero_init, write_state, n_chunks, n_seq):
    it = iter(refs)
    rk_refs, v_refs, lwla_refs = [None, None], [None, None], [None, None]
    for d in range(N_DIR):
        rk_refs[d], v_refs[d], lwla_refs[d] = next(it), next(it), next(it)
    w0_ref, w2_ref, a0_ref, a2_ref = (next(it) for _ in range(4))
    kk_ref, ka_ref, tri_ref, ones_ref = (next(it) for _ in range(4))
    s0_ref = None if zero_init else next(it)
    y_refs = [next(it), next(it)]
    sfin_refs = [next(it), next(it)] if write_state else None
    h_scr = next(it)

    j = pl.program_id(1)

    @pl.when(j == 0)
    def _():
        if zero_init:
            h_scr[...] = jnp.zeros_like(h_scr)
        else:
            h_scr[...] = s0_ref[...]

    C = CHUNK
    row = lax.broadcasted_iota(jnp.int32, (C, LANES), 0)
    lane = lax.broadcasted_iota(jnp.int32, (C, LANES), 1)
    s_idx = jnp.bitwise_and(lane, HEAD_DIM - 1)
    head0 = lane < HEAD_DIM
    row_c = lax.broadcasted_iota(jnp.int32, (SUB, LANES), 0)
    lane_c = lax.broadcasted_iota(jnp.int32, (SUB, LANES), 1)
    col_c = jnp.bitwise_and(lane_c, SUB - 1)
    blk_c = jnp.right_shift(jnp.bitwise_and(lane_c, HEAD_DIM - 1), SUB_SHIFT)
    lane_blk_c = jnp.right_shift(lane_c, SUB_SHIFT)
    eye_c = (row_c == col_c).astype(F32)

    def bd_c(y):
        return jnp.concatenate([jnp.where(lane_blk_c == g, y, 0.0) for g in range(LANES // SUB)], axis=0)

    row2 = lax.broadcasted_iota(jnp.int32, (LANES, LANES), 0)
    lane2 = lax.broadcasted_iota(jnp.int32, (LANES, LANES), 1)
    same_head = (row2 < HEAD_DIM) == (lane2 < HEAD_DIM)
    eye2 = row2 == lane2
    decay_scale = math.exp(-0.5)

    probs = []
    for n, d in [(n, d) for n in range(n_seq) for d in range(N_DIR)]:
        rk = rk_refs[d][n].astype(F32)
        r = rk[:, :D_RWKV]
        k = rk[:, D_RWKV:]
        v = v_refs[d][n]
        ll = lwla_refs[d][n]
        z_w = w0_ref[d:d + 1, :] + _bdot(jnp.tanh(ll), w2_ref[d])
        logw = -decay_scale * jax.nn.sigmoid(z_w)
        a = jax.nn.sigmoid(a0_ref[d:d + 1, :] + _bdot(ll, a2_ref[d]))
        kd = k * (1.0 + (a - 1.0) * ka_ref[...])
        kkr = k * kk_ref[...]
        ssq = _bdot(kkr * kkr, ones_ref[...])
        kk = kkr * lax.rsqrt(jnp.maximum(ssq, 1e-24))
        bvec = kk * a
        lw_hi, lw_lo = _split2(logw)
        tri = tri_ref[d]
        cum = (jnp.dot(tri, lw_hi, preferred_element_type=F32)
               + jnp.dot(tri, lw_lo, preferred_element_type=F32))
        cum_prev = cum - logw
        tot = cum[C - 1:C, :] if d == 0 else cum[0:1, :]
        kap_t = kk * jnp.exp(cum_prev)
        r_t = r * jnp.exp(cum)
        e_neg = jnp.exp(-cum)
        gam = jnp.exp(tot)
        b_t = bvec * e_neg
        k_t = kd * e_neg
        e_rem = gam * e_neg
        b_h = bvec * e_rem
        k_h = kd * e_rem
        if d == 0:
            strict, incl = row > s_idx, row >= s_idx
        else:
            strict, incl = row < s_idx, row <= s_idx

        for p in range(N_PAIRS):
            sl = slice(p * LANES, (p + 1) * LANES)
            probs.append(dict(n=n, d=d, p=p, sl=sl, strict=strict, incl=incl, kap=kap_t[:, sl], r=r_t[:, sl],
                              v=v[:, sl], b_t=b_t[:, sl], k_t=k_t[:, sl], b_h=b_h[:, sl], k_h=k_h[:, sl],
                              gam=gam[:, sl]))

    bd = lambda y: _block_diag_rows(y, head0)
    for q in probs:
        lhs = jnp.concatenate([q["kap"], q["r"]], axis=0)
        rhs = jnp.concatenate([bd(q["b_t"]), bd(q["k_t"])], axis=0)
        lm = _bdot_nt(lhs, rhs)
        q["l_b"] = jnp.where(q["strict"], lm[:C, :LANES], 0.0)
        q["m_b"] = jnp.where(q["incl"], lm[C:, :LANES], 0.0)
        q["lm_k"] = jnp.concatenate([jnp.where(q["strict"], lm[:C, LANES:], 0.0),
                                     jnp.where(q["incl"], lm[C:, LANES:], 0.0)], axis=0)
    for q in probs:
        lmv = _bdot(q["lm_k"], bd(q["v"]))
        q["l_kv"], q["m_kv"] = lmv[:C], lmv[C:]
    for q in probs:
        l_b = q["l_b"]
        l_c = l_b[0:SUB]
        for jb in range(1, C // SUB):
            l_c = jnp.where(blk_c == jb, l_b[jb * SUB:(jb + 1) * SUB], l_c)
        q["l_c"] = l_c
        q["t_c"] = eye_c - jnp.where(_same_block(row_c, col_c, 2), l_c, 0.0)
    s = 2
    while s < SUB:
        off_mask = _same_block(row_c, col_c, 2 * s) & ~_same_block(row_c, col_c, s)
        for q in probs:
            q["et"] = _bdot(jnp.where(off_mask, q["l_c"], 0.0), bd_c(q["t_c"]))
        for q in probs:
            q["t_c"] = q["t_c"] - _bdot(q["t_c"], bd_c(q["et"]))
        s *= 2
    for q in probs:
        q["tinv"] = jnp.concatenate([jnp.where(blk_c == jb, q["t_c"], 0.0) for jb in range(C // SUB)], axis=0)
    while s < C:
        off_mask = _same_block(row, s_idx, 2 * s) & ~_same_block(row, s_idx, s)
        for q in probs:
            q["et"] = _bdot(jnp.where(off_mask, q["l_b"], 0.0), bd(q["tinv"]))
        for q in probs:
            q["tinv"] = q["tinv"] - _bdot(q["tinv"], bd(q["et"]))
        s *= 2
    for q in probs:
        wu = _bdot(q["tinv"], jnp.concatenate([bd(q["kap"]), bd(q["l_kv"])], axis=1))
        q["w_m"], q["u_n"] = wu[:, :LANES], wu[:, LANES:]
    for q in probs:
        qy = _bdot(q["m_b"], jnp.concatenate([bd(q["w_m"]), bd(q["u_n"])], axis=1))
        q["r_q"] = q["r"] - qy[:, :LANES]
        q["y_i"] = q["m_kv"] - qy[:, LANES:]
    for q in probs:
        t2 = jnp.concatenate([q["b_h"], q["k_h"]], axis=0).T
        ag = _bdot(t2, jnp.concatenate(
            [jnp.concatenate([q["w_m"], -q["u_n"]], axis=1).astype(BF16),
             jnp.concatenate([jnp.zeros_like(q["v"]), q["v"]], axis=1)], axis=0))
        gam_p = jnp.broadcast_to(q["gam"], (LANES, LANES))
        q["a_bd"] = jnp.where(same_head, jnp.where(eye2, gam_p, 0.0) - ag[:, :LANES], 0.0)
        q["g_bd"] = jnp.where(same_head, ag[:, LANES:], 0.0)
    for q in probs:
        lhs2 = jnp.concatenate([q["r_q"], q["a_bd"]], axis=0)
        yh = _bdot(lhs2, h_scr[q["n"], q["d"], q["p"]])
        y_refs[q["d"]][q["n"], :, q["sl"]] = (q["y_i"] + yh[:C]).astype(BF16)
        h_scr[q["n"], q["d"], q["p"]] = yh[C:] + q["g_bd"]

    if write_state:
        @pl.when(j == n_chunks - 1)
        def _():
            for n, d, p in [(n, d, p) for n in range(n_seq) for d in range(N_DIR) for p in range(N_PAIRS)]:
                ht = h_scr[n, d, p].T
                sfin_refs[d][n, 0, PAIR * p] = ht[:HEAD_DIM, :HEAD_DIM]
                sfin_refs[d][n, 0, PAIR * p + 1] = ht[HEAD_DIM:, HEAD_DIM:]


def _scan_call(rk, v, lwla, wts, s0_bd, write_state):
    B, T, _ = rk.shape
    nc = T // CHUNK
    ns = min(SCAN_SEQS, B)
    assert B % ns == 0
    zero_init = s0_bd is None
    fwd = lambda b, j: (b, j, 0)
    bwd = lambda b, j: (b, nc - 1 - j, 0)
    full = lambda *shape: pl.BlockSpec(shape, lambda b, j: (0,) * len(shape))
    tok = lambda w, m: pl.BlockSpec((ns, CHUNK, w), m)
    in_specs = [tok(2 * D_RWKV, fwd), tok(D_RWKV, fwd), tok(LANES, fwd),
                tok(2 * D_RWKV, bwd), tok(D_RWKV, bwd), tok(LANES, bwd),
                full(N_DIR, D_RWKV), full(N_DIR, LANES, D_RWKV),
                full(N_DIR, D_RWKV), full(N_DIR, LANES, D_RWKV),
                full(1, D_RWKV), full(1, D_RWKV), full(N_DIR, CHUNK, CHUNK), full(D_RWKV, D_RWKV)]
    args = [rk, v, lwla, rk, v, lwla, wts["w0"], wts["w2"], wts["a0"], wts["a2"],
            wts["k_k"], wts["k_a"], wts["tri"], wts["ones_bd"]]
    state_block = (ns, N_DIR, N_PAIRS, LANES, LANES)
    if not zero_init:
        in_specs.append(pl.BlockSpec(state_block, lambda b, j: (b, 0, 0, 0, 0)))
        args.append(s0_bd)
    out_specs = [pl.BlockSpec((ns, CHUNK, D_RWKV), fwd), pl.BlockSpec((ns, CHUNK, D_RWKV), bwd)]
    out_shape = [jax.ShapeDtypeStruct((B, T, D_RWKV), BF16), jax.ShapeDtypeStruct((B, T, D_RWKV), BF16)]
    if write_state:
        final_block = (ns, 1, N_HEADS, HEAD_DIM, HEAD_DIM)
        out_specs += [pl.BlockSpec(final_block, lambda b, j: (b, 0, 0, 0, 0))] * N_DIR
        out_shape += [jax.ShapeDtypeStruct((B,) + final_block[1:], F32)] * N_DIR
    kern = functools.partial(_scan_kernel, zero_init=zero_init, write_state=write_state, n_chunks=nc,
                             n_seq=ns)
    return pl.pallas_call(
        kern, grid=(B // ns, nc), in_specs=in_specs, out_specs=out_specs, out_shape=out_shape,
        scratch_shapes=[pltpu.VMEM(state_block, F32)],
        compiler_params=pltpu.CompilerParams(dimension_semantics=("arbitrary", "arbitrary"),
                                             vmem_limit_bytes=VMEM_LIMIT),
        name="scan",
    )(*args)


def _fnet_kernel(fin_ref, gf_ref, dft_ref, cs_ref, wbd_ref, b_ref, o_ref, g_scr):
    u = pl.program_id(1)
    nb, seq_len, _ = fin_ref.shape
    tu = o_ref.shape[1]

    @pl.when(u == 0)
    def _():
        step = min(seq_len, ROW_TILE)
        for n in range(nb):
            for t0 in range(0, seq_len, step):
                fc = _bdot(fin_ref[n, t0:t0 + step, :], cs_ref[...])
                g_scr[n, t0:t0 + step, :] = fc[:, :D_FNET].astype(BF16)
                g_scr[n, seq_len + t0:seq_len + t0 + step, :] = fc[:, D_FNET:].astype(BF16)

    f_re = jnp.concatenate([jnp.dot(dft_ref[...], g_scr[n], preferred_element_type=F32) for n in range(nb)],
                           axis=0)
    f_out = _bdot(f_re, wbd_ref[...]) + b_ref[...]
    gate = _silu(gf_ref[...].astype(F32)).reshape(nb * tu, D_FNET)
    o_ref[...] = (f_out * gate).astype(BF16).reshape(nb, tu, D_FNET)


def _fnet_call(fin, gf, dft_bf16, cs_bf16, wbd_bf16, b_fnet):
    B, T, _ = fin.shape
    tu = min(T, ROW_TILE)
    nb = max(1, FNET_ROWS // T)
    assert B % nb == 0 and T % tu == 0
    return pl.pallas_call(
        _fnet_kernel,
        grid=(B // nb, T // tu),
        in_specs=[pl.BlockSpec((nb, T, D_FNET), lambda b, u: (b, 0, 0)),
                  pl.BlockSpec((nb, tu, D_FNET), lambda b, u: (b, u, 0)),
                  pl.BlockSpec((tu, 2 * T), lambda b, u: (u, 0)),
                  pl.BlockSpec((D_FNET, 2 * D_FNET), lambda b, u: (0, 0)),
                  pl.BlockSpec((D_FNET, D_FNET), lambda b, u: (0, 0)),
                  pl.BlockSpec((1, D_FNET), lambda b, u: (0, 0))],
        out_specs=pl.BlockSpec((nb, tu, D_FNET), lambda b, u: (b, u, 0)),
        out_shape=jax.ShapeDtypeStruct((B, T, D_FNET), BF16),
        scratch_shapes=[pltpu.VMEM((nb, 2 * T, D_FNET), BF16)],
        compiler_params=pltpu.CompilerParams(dimension_semantics=("arbitrary", "arbitrary"),
                                             vmem_limit_bytes=VMEM_LIMIT),
        name="fnet",
    )(fin, gf, dft_bf16, cs_bf16, wbd_bf16, b_fnet)


def _out_kernel(*refs, has_emb, final_norm):
    it = iter(refs)
    x_ref = next(it)
    emb_ref = next(it) if has_emb else None
    (mod_ref, yf_ref, yb_ref, rk_ref, v_ref, lwla_ref, grec_ref, fo_ref, a0_ref, a2_ref, ka_ref,
     rkw_ref, gng_ref, gnb_ref, avg_ref, ones_ref, wout_ref, fng_ref, o_ref) = (next(it) for _ in range(19))

    nb, tt, _ = x_ref.shape
    tm = nb * tt
    rows = lambda ref: ref[...].reshape(tm, ref.shape[-1])
    y = rows(yf_ref).astype(F32) + rows(yb_ref).astype(F32)
    mu = _bdot(y, avg_ref[...])
    dlt = y - mu
    var = _bdot(dlt * dlt, avg_ref[...])
    y_n = dlt * lax.rsqrt(var + GN_EPS) * gng_ref[...] + gnb_ref[...]
    rk = rows(rk_ref).astype(F32)
    r = rk[:, :D_RWKV]
    k = rk[:, D_RWKV:]
    ll = rows(lwla_ref)
    a_sum = (jax.nn.sigmoid(a0_ref[0:1, :] + _bdot(ll, a2_ref[0]))
             + jax.nn.sigmoid(a0_ref[1:2, :] + _bdot(ll, a2_ref[1])))
    k_sum = k * (2.0 + (a_sum - 2.0) * ka_ref[...])
    bonus = _bdot(r * k_sum * rkw_ref[...], ones_ref[...]) * rows(v_ref).astype(F32)
    rec_out = (y_n + bonus) * _silu(rows(grec_ref).astype(F32))
    mixed = jnp.concatenate([rec_out.astype(BF16), rows(fo_ref)], axis=-1)
    out = jnp.dot(mixed, wout_ref[...], preferred_element_type=F32)
    x = rows(x_ref)
    if has_emb:
        x = x + emb_ref[...]
    z = x + mod_ref[0, 2:3, :] * out
    if final_norm:
        ms = jnp.mean(z * z, axis=-1, keepdims=True)
        z = z * lax.rsqrt(ms + NORM_EPS) * fng_ref[...]
    o_ref[...] = z.reshape(nb, tt, D_MODEL)


def _out_call(x, emb, mod, y_f, y_b, rk, v, lwla, grec, fo, wts, w_out_bf16, final_norm_g, final_norm):
    B, T, _ = x.shape
    tt = min(T, ROW_TILE)
    nb = max(1, ROW_TILE // T)
    has_emb = emb is not None
    per_batch_mod = mod.shape[0] > 1
    assert B % nb == 0 and T % tt == 0 and (nb == 1 or not (has_emb or per_batch_mod))
    tok = lambda w: pl.BlockSpec((nb, tt, w), lambda b, i: (b, i, 0))
    full = lambda *shape: pl.BlockSpec(shape, lambda b, i: (0,) * len(shape))
    in_specs = [tok(D_MODEL)]
    args = [x]
    if has_emb:
        in_specs.append(pl.BlockSpec((tt, D_MODEL), lambda b, i: (i, 0)))
        args.append(emb)
    mod_map = (lambda b, i: (b, 0, 0)) if per_batch_mod else (lambda b, i: (0, 0, 0))
    in_specs += [pl.BlockSpec((1, 3, D_MODEL), mod_map), tok(D_RWKV), tok(D_RWKV), tok(2 * D_RWKV),
                 tok(D_RWKV), tok(LANES), tok(D_RWKV), tok(D_FNET),
                 full(N_DIR, D_RWKV), full(N_DIR, LANES, D_RWKV),
                 full(1, D_RWKV), full(1, D_RWKV), full(1, D_RWKV), full(1, D_RWKV),
                 full(D_RWKV, D_RWKV), full(D_RWKV, D_RWKV), full(D_MODEL, D_MODEL), full(1, D_MODEL)]
    args += [mod, y_f, y_b, rk, v, lwla, grec, fo, wts["a0"], wts["a2"], wts["k_a"],
             wts["r_k"], wts["gn_g"], wts["gn_b"], wts["avg_bd"], wts["ones_bd"], w_out_bf16, final_norm_g]
    return pl.pallas_call(
        functools.partial(_out_kernel, has_emb=has_emb, final_norm=final_norm),
        grid=(B // nb, T // tt), in_specs=in_specs,
        out_specs=tok(D_MODEL),
        out_shape=jax.ShapeDtypeStruct((B, T, D_MODEL), F32),
        compiler_params=pltpu.CompilerParams(dimension_semantics=("arbitrary", "arbitrary"),
                                             vmem_limit_bytes=VMEM_LIMIT),
        name="out_proj",
    )(*args)


def _dft_table(seq_len):
    idx = np.arange(seq_len, dtype=np.int64)
    ang = 2.0 * np.pi * ((idx[:, None] * idx[None, :]) % seq_len).astype(np.float64) / seq_len
    scale = 1.0 / math.sqrt(seq_len)
    return np.concatenate([np.cos(ang) * scale, -np.sin(ang) * scale], axis=1).astype(np.float32)


def _channel_dft_table():
    n = FNET_GROUP
    idx = np.arange(n, dtype=np.int64)
    ang = 2.0 * np.pi * ((idx[:, None] * idx[None, :]) % n).astype(np.float64) / n
    c = np.cos(ang) / math.sqrt(n)
    s = np.sin(ang) / math.sqrt(n)
    eye = np.eye(D_FNET // n)
    return np.concatenate([np.kron(eye, c), np.kron(eye, s)], axis=1).astype(np.float32)


def _sincos_2d(n_tokens):
    rows = n_tokens // GRID_W
    pos = np.arange(rows * GRID_W)
    row = (pos // GRID_W).astype(np.float32)
    col = (pos % GRID_W).astype(np.float32)
    quarter = D_MODEL // 4
    freq = np.exp(np.float32(-math.log(POS_BASE)) * np.arange(quarter, dtype=np.float32) / np.float32(quarter))
    ang_r = row[:, None] * freq
    ang_c = col[:, None] * freq
    return np.concatenate([np.sin(ang_r), np.cos(ang_r), np.sin(ang_c), np.cos(ang_c)], axis=-1).astype(np.float32)


def _head_block_matrix(value):
    blk = np.kron(np.eye(N_HEADS), np.ones((HEAD_DIM, HEAD_DIM))) * value
    return jnp.asarray(blk, dtype=BF16)


def _pad_lora(w, row_offset):
    out = jnp.zeros((N_DIR, LANES, w.shape[-1]), F32)
    for d in range(N_DIR):
        out = out.at[d, row_offset + d * LORA:row_offset + (d + 1) * LORA, :].set(w[d])
    return out.astype(BF16)


def _layer_weights(l, w0, w2, a0, a2, k_k, k_a, r_k, gn_g, gn_b):
    w2_p = _pad_lora(w2[l], 0)
    a2_p = _pad_lora(a2[l], N_DIR * LORA)
    tri_f = np.tril(np.ones((CHUNK, CHUNK)))
    tri = jnp.asarray(np.stack([tri_f, tri_f.T]), dtype=BF16)
    return dict(w0=w0[l], w2=w2_p, a0=a0[l], a2=a2_p,
                k_k=k_k[l][None], k_a=k_a[l][None], r_k=r_k[l].reshape(1, D_RWKV),
                gn_g=gn_g[l].reshape(1, D_RWKV), gn_b=gn_b[l].reshape(1, D_RWKV), tri=tri,
                ones_bd=_head_block_matrix(1.0), avg_bd=_head_block_matrix(1.0 / HEAD_DIM))


def _state_to_block_diag(s_f, s_b):
    def one(s):
        h = jnp.swapaxes(s.astype(F32), -1, -2)
        b = h.shape[0]
        h = h.reshape(b, N_PAIRS, PAIR, HEAD_DIM, HEAD_DIM)
        z = jnp.zeros_like(h[:, :, 0])
        top = jnp.concatenate([h[:, :, 0], z], axis=-1)
        bot = jnp.concatenate([z, h[:, :, 1]], axis=-1)
        return jnp.concatenate([top, bot], axis=-2)
    return jnp.stack([one(s_f), one(s_b)], axis=1)


def kernel(x_prompt, x_sample, state_rwkv_fwd, state_rwkv_bwd, c, c_ctx, w_ada, b_ada, norm_g, w_in,
           mu_shift, w0, w2, a0, a2, k_k, k_a, r_k, gn_g, gn_b, w_fnet, b_fnet, w_out, final_norm_g):
    depth = w_in.shape[0]
    n_dec = c.shape[0]
    assert n_dec + 1 <= 8
    bp, tp, _ = x_prompt.shape
    bs, ts, _ = x_sample.shape
    cvec = jnp.zeros((8, D_MODEL), F32).at[0].set(c_ctx).at[1:1 + n_dec].set(c)
    emb = jnp.asarray(_sincos_2d(ts)).astype(x_sample.dtype)
    cs_tab = jnp.asarray(_channel_dft_table()).astype(BF16)
    dft_p = jnp.asarray(_dft_table(tp)).astype(BF16)
    dft_s = jnp.asarray(_dft_table(ts)).astype(BF16)
    fng = final_norm_g[None]

    xp, xs = x_prompt, x_sample
    new_f, new_b = [], []
    for l in range(depth):
        mod = _mod_call(cvec, w_ada[l], b_ada[l][None]).reshape(8, 3, D_MODEL)
        mod_ctx, mod_lat = mod[0:1], mod[1:1 + n_dec]
        wts = _layer_weights(l, w0, w2, a0, a2, k_k, k_a, r_k, gn_g, gn_b)
        w_in_b = w_in[l].astype(BF16)
        w_out_b = w_out[l].astype(BF16)
        wbd = jax.scipy.linalg.block_diag(*[w_fnet[l, g] for g in range(w_fnet.shape[1])]).astype(BF16)
        ng, mu, bf = norm_g[l][None], mu_shift[l][None], b_fnet[l][None]
        emb_l = emb if l == 0 else None

        rk, v, lwla, grec, fin, gf = _in_proj_call(xp, None, mod_ctx, ng, w_in_b, mu)
        y_f, y_b, s_f, s_b = _scan_call(rk, v, lwla, wts, None, True)
        fo = _fnet_call(fin, gf, dft_p, cs_tab, wbd, bf)
        last = l == depth - 1
        xp = _out_call(xp, None, mod_ctx, y_f, y_b, rk, v, lwla, grec, fo, wts, w_out_b, fng, last)
        new_f.append(s_f)
        new_b.append(s_b)

        rk, v, lwla, grec, fin, gf = _in_proj_call(xs, emb_l, mod_lat, ng, w_in_b, mu)
        s0 = _state_to_block_diag(state_rwkv_fwd[:, l], state_rwkv_bwd[:, l])
        y_f, y_b = _scan_call(rk, v, lwla, wts, s0, False)
        fo = _fnet_call(fin, gf, dft_s, cs_tab, wbd, bf)
        xs = _out_call(xs, emb_l, mod_lat, y_f, y_b, rk, v, lwla, grec, fo, wts, w_out_b, fng, last)
    return (xp, xs, jnp.concatenate(new_f, axis=1), jnp.concatenate(new_b, axis=1))
```

```python
import functools
import math

import numpy as np
import jax
import jax.numpy as jnp
from jax import lax
from jax.experimental import pallas as pl
from jax.experimental.pallas import tpu as pltpu

F32 = jnp.float32
BF16 = jnp.bfloat16

D_MODEL = 1024
GRID_W = 64
D_RWKV = 512
D_FNET = D_MODEL - D_RWKV
HEAD_DIM = 64
N_HEADS = D_RWKV // HEAD_DIM
FNET_GROUP = 64
LORA = 32
N_DIR = 2
D_SHIFT = 3 * D_RWKV + N_DIR * 2 * LORA
D_IN = D_SHIFT + D_RWKV + 2 * D_FNET
NORM_EPS = 1e-6
GN_EPS = 64e-5
POS_BASE = 10000.0

LANES = 128
PAIR = LANES // HEAD_DIM
N_PAIRS = N_HEADS // PAIR
CHUNK = 64
SUB = 16
SUB_SHIFT = SUB.bit_length() - 1
SCAN_SEQS = 2
ROW_TILE = 512
FNET_ROWS = 1024
HALO = 8
VMEM_LIMIT = 56 * 1024 * 1024


def _silu(x):
    return x * jax.nn.sigmoid(x)


def _bdot(a, b):
    return jnp.dot(a.astype(BF16), b.astype(BF16), preferred_element_type=F32)


def _split2(x):
    hi = x.astype(BF16)
    lo = (x - hi.astype(F32)).astype(BF16)
    return hi, lo


def _dot3(a, b_hi, b_lo):
    a_hi, a_lo = _split2(a)
    return (jnp.dot(a_hi, b_hi, preferred_element_type=F32)
            + jnp.dot(a_hi, b_lo, preferred_element_type=F32)
            + jnp.dot(a_lo, b_hi, preferred_element_type=F32))


def _mod_kernel(c_ref, w_ref, b_ref, o_ref):
    s = _silu(c_ref[...])
    w = w_ref[...]
    w_hi, w_lo = _split2(w)
    o_ref[...] = _dot3(s, w_hi, w_lo) + b_ref[...]


def _mod_call(cvec, w_ada, b_ada):
    n_blk = 3
    return pl.pallas_call(
        _mod_kernel,
        grid=(n_blk,),
        in_specs=[pl.BlockSpec((8, D_MODEL), lambda i: (0, 0)),
                  pl.BlockSpec((D_MODEL, D_MODEL), lambda i: (0, i)),
                  pl.BlockSpec((1, D_MODEL), lambda i: (0, i))],
        out_specs=pl.BlockSpec((8, D_MODEL), lambda i: (0, i)),
        out_shape=jax.ShapeDtypeStruct((8, 3 * D_MODEL), F32),
        compiler_params=pltpu.CompilerParams(dimension_semantics=("arbitrary",),
                                             vmem_limit_bytes=VMEM_LIMIT),
        name="mod",
    )(cvec, w_ada, b_ada)


def _modulated_norm(x, g, scale, shift):
    ms = jnp.mean(x * x, axis=-1, keepdims=True)
    y = x * lax.rsqrt(ms + NORM_EPS) * g
    return y * (1.0 + scale) + shift


def _in_proj_kernel(*refs, has_emb, has_halo, n_tiles):
    it = iter(refs)
    x_ref = next(it)
    xp_ref = next(it) if has_halo else None
    xn_ref = next(it) if has_halo else None
    emb_ref = next(it) if has_emb else None
    embp_ref = next(it) if (has_emb and has_halo) else None
    embn_ref = next(it) if (has_emb and has_halo) else None
    mod_ref, g_ref, w_ref, mu_ref = next(it), next(it), next(it), next(it)
    rk_ref, v_ref, lwla_ref, grec_ref, fin_ref, gf_ref = (next(it) for _ in range(6))

    i = pl.program_id(1)
    g = g_ref[...]
    shift = mod_ref[0, 0:1, :]
    scale = mod_ref[0, 1:2, :]
    nb, tt, _ = x_ref.shape
    tm = nb * tt
    x = x_ref[...].reshape(tm, D_MODEL)
    if has_emb:
        x = x + emb_ref[...]
    h = _modulated_norm(x, g, scale, shift)
    p = jnp.dot(h.astype(BF16), w_ref[...], preferred_element_type=F32)
    ps = p[:, :D_SHIFT]
    if has_halo:
        xh = jnp.concatenate([xp_ref[0], xn_ref[0]], axis=0)
        if has_emb:
            xh = xh + jnp.concatenate([embp_ref[...], embn_ref[...]], axis=0)
        hh = _modulated_norm(xh, g, scale, shift)
        ph = jnp.dot(hh.astype(BF16), w_ref[:, :D_SHIFT], preferred_element_type=F32)
        prev_row = jnp.where(i > 0, ph[HALO - 1:HALO, :], 0.0)
        next_row = jnp.where(i < n_tiles - 1, ph[HALO:HALO + 1, :], 0.0)
    else:
        prev_row = jnp.zeros((1, D_SHIFT), F32)
        next_row = jnp.zeros((1, D_SHIFT), F32)
    pos = lax.rem(lax.broadcasted_iota(jnp.int32, (tm, D_SHIFT), 0), tt)
    prev = jnp.where(pos == 0, prev_row, pltpu.roll(ps, 1, 0))
    nxt = jnp.where(pos == tt - 1, next_row, pltpu.roll(ps, tm - 1, 0))
    p_rec = ps + mu_ref[...] * (0.5 * (prev + nxt) - ps)

    def put(ref, val):
        ref[...] = val.astype(ref.dtype).reshape(ref.shape)

    put(rk_ref, p_rec[:, :2 * D_RWKV])
    put(v_ref, p_rec[:, 2 * D_RWKV:3 * D_RWKV])
    put(lwla_ref, p_rec[:, 3 * D_RWKV:])
    put(grec_ref, p[:, D_SHIFT:D_SHIFT + D_RWKV])
    put(fin_ref, p[:, D_SHIFT + D_RWKV:D_SHIFT + D_RWKV + D_FNET])
    put(gf_ref, p[:, D_SHIFT + D_RWKV + D_FNET:])


def _in_proj_call(x, emb, mod, norm_g, w_in_bf16, mu):
    B, T, _ = x.shape
    tt = min(T, ROW_TILE)
    nb = max(1, ROW_TILE // T)
    n_tiles = T // tt
    has_halo = n_tiles > 1
    has_emb = emb is not None
    per_batch_mod = mod.shape[0] > 1
    assert B % nb == 0 and T % tt == 0 and (nb == 1 or not (has_emb or per_batch_mod))
    tm = tt
    blocks_per_tile = tm // HALO
    last_halo_block = T // HALO - 1

    in_specs = [pl.BlockSpec((nb, tt, D_MODEL), lambda b, i: (b, i, 0))]
    args = [x]
    if has_halo:
        in_specs += [
            pl.BlockSpec((1, HALO, D_MODEL), lambda b, i: (b, jnp.maximum(i * blocks_per_tile - 1, 0), 0)),
            pl.BlockSpec((1, HALO, D_MODEL),
                         lambda b, i: (b, jnp.minimum((i + 1) * blocks_per_tile, last_halo_block), 0))]
        args += [x, x]
    if has_emb:
        in_specs.append(pl.BlockSpec((tm, D_MODEL), lambda b, i: (i, 0)))
        args.append(emb)
        if has_halo:
            in_specs += [
                pl.BlockSpec((HALO, D_MODEL), lambda b, i: (jnp.maximum(i * blocks_per_tile - 1, 0), 0)),
                pl.BlockSpec((HALO, D_MODEL),
                             lambda b, i: (jnp.minimum((i + 1) * blocks_per_tile, last_halo_block), 0))]
            args += [emb, emb]
    mod_map = (lambda b, i: (b, 0, 0)) if per_batch_mod else (lambda b, i: (0, 0, 0))
    in_specs += [pl.BlockSpec((1, 3, D_MODEL), mod_map),
                 pl.BlockSpec((1, D_MODEL), lambda b, i: (0, 0)),
                 pl.BlockSpec((D_MODEL, D_IN), lambda b, i: (0, 0), pipeline_mode=pl.Buffered(1)),
                 pl.BlockSpec((1, D_SHIFT), lambda b, i: (0, 0))]
    args += [mod, norm_g, w_in_bf16, mu]
    outs = ((2 * D_RWKV, BF16), (D_RWKV, BF16), (N_DIR * 2 * LORA, F32), (D_RWKV, BF16), (D_FNET, BF16),
            (D_FNET, BF16))
    out_specs = [pl.BlockSpec((nb, tt, w), lambda b, i: (b, i, 0)) for w, _ in outs]
    out_shape = [jax.ShapeDtypeStruct((B, T, w), dt) for w, dt in outs]
    kern = functools.partial(_in_proj_kernel, has_emb=has_emb, has_halo=has_halo, n_tiles=n_tiles)
    return pl.pallas_call(
        kern, grid=(B // nb, n_tiles), in_specs=in_specs, out_specs=out_specs, out_shape=out_shape,
        compiler_params=pltpu.CompilerParams(dimension_semantics=("arbitrary", "arbitrary"),
                                             vmem_limit_bytes=VMEM_LIMIT),
        name="in_proj",
    )(*args)


def _same_block(i, j, size):
    shift = size.bit_length() - 1
    return jnp.right_shift(i, shift) == jnp.right_shift(j, shift)


def _block_diag_rows(y, head0_lanes):
    return jnp.concatenate([jnp.where(head0_lanes, y, 0.0), jnp.where(head0_lanes, 0.0, y)], axis=0)


def _scan_kernel(*refs, zero_init, write_state, n_chunks, n_seq):
    it = iter(refs)
    rk_refs, v_refs, lwla_refs = [None, None], [None, None], [None, None]
    for d in range(N_DIR):
        rk_refs[d], v_refs[d], lwla_refs[d] = next(it), next(it), next(it)
    w0_ref, w2_ref, a0_ref, a2_ref = (next(it) for _ in range(4))
    kk_ref, ka_ref, tri_ref, ones_ref = (next(it) for _ in range(4))
    s0_ref = None if zero_init else next(it)
    y_refs = [next(it), next(it)]
    sfin_refs = [next(it), next(it)] if write_state else None
    h_scr = next(it)

    j = pl.program_id(1)

    @pl.when(j == 0)
    def _():
        if zero_init:
            h_scr[...] = jnp.zeros_like(h_scr)
        else:
            h_scr[...] = s0_ref[...]

    C = CHUNK
    row = lax.broadcasted_iota(jnp.int32, (C, LANES), 0)
    lane = lax.broadcasted_iota(jnp.int32, (C, LANES), 1)
    s_idx = jnp.bitwise_and(lane, HEAD_DIM - 1)
    head0 = lane < HEAD_DIM
    row_c = lax.broadcasted_iota(jnp.int32, (SUB, LANES), 0)
    lane_c = lax.broadcasted_iota(jnp.int32, (SUB, LANES), 1)
    col_c = jnp.bitwise_and(lane_c, SUB - 1)
    blk_c = jnp.right_shift(jnp.bitwise_and(lane_c, HEAD_DIM - 1), SUB_SHIFT)
    lane_blk_c = jnp.right_shift(lane_c, SUB_SHIFT)
    eye_c = (row_c == col_c).astype(F32)

    def bd_c(y):
        return jnp.concatenate([jnp.where(lane_blk_c == g, y, 0.0) for g in range(LANES // SUB)], axis=0)

    row2 = lax.broadcasted_iota(jnp.int32, (LANES, LANES), 0)
    lane2 = lax.broadcasted_iota(jnp.int32, (LANES, LANES), 1)
    same_head = (row2 < HEAD_DIM) == (lane2 < HEAD_DIM)
    decay_scale = math.exp(-0.5)

    probs = []
    for n, d in [(n, d) for n in range(n_seq) for d in range(N_DIR)]:
        rk = rk_refs[d][n].astype(F32)
        r = rk[:, :D_RWKV]
        k = rk[:, D_RWKV:]
        v = v_refs[d][n]
        ll = lwla_refs[d][n]
        z_w = w0_ref[d:d + 1, :] + _bdot(jnp.tanh(ll), w2_ref[d])
        logw = -decay_scale * jax.nn.sigmoid(z_w)
        a = jax.nn.sigmoid(a0_ref[d:d + 1, :] + _bdot(ll, a2_ref[d]))
        kd = k * (1.0 + (a - 1.0) * ka_ref[...])
        kkr = k * kk_ref[...]
        ssq = _bdot(kkr * kkr, ones_ref[...])
        kk = kkr * lax.rsqrt(jnp.maximum(ssq, 1e-24))
        bvec = kk * a
        lw_hi, lw_lo = _split2(logw)
        tri = tri_ref[d]
        cum = (jnp.dot(tri, lw_hi, preferred_element_type=F32)
               + jnp.dot(tri, lw_lo, preferred_element_type=F32))
        cum_prev = cum - logw
        tot = cum[C - 1:C, :] if d == 0 else cum[0:1, :]
        kap_t = kk * jnp.exp(cum_prev)
        r_t = r * jnp.exp(cum)
        e_neg = jnp.exp(-cum)
        gam = jnp.exp(tot)
        b_t = bvec * e_neg
        k_t = kd * e_neg
        e_rem = gam * e_neg
        b_h = bvec * e_rem
        k_h = kd * e_rem
        if d == 0:
            strict, incl = row > s_idx, row >= s_idx
        else:
            strict, incl = row < s_idx, row <= s_idx

        for p in range(N_PAIRS):
            sl = slice(p * LANES, (p + 1) * LANES)
            dup_t = lambda x: jnp.where(same_head, jnp.concatenate([x[:, sl], x[:, sl]], axis=0).T, 0.0)
            probs.append(dict(n=n, d=d, p=p, sl=sl, strict=strict, incl=incl,
                              lhs=jnp.concatenate([kap_t[:, sl], r_t[:, sl]], axis=0).astype(BF16),
                              v=v[:, sl], w_lm=jnp.concatenate([dup_t(b_t), dup_t(k_t)], axis=1).astype(BF16),
                              t2=jnp.concatenate([b_h[:, sl], k_h[:, sl]], axis=0).T.astype(BF16),
                              gam_col=jnp.broadcast_to(gam[:, sl], (LANES, LANES)).T))

    bd = lambda y: _block_diag_rows(y, head0)
    for q in probs:
        q["h"] = h_scr[q["n"], q["d"], q["p"]]
        lm = jnp.dot(q["lhs"], q["w_lm"], preferred_element_type=F32)
        q["l_b"] = jnp.where(q["strict"], lm[:C, :LANES], 0.0)
        q["m_b"] = jnp.where(q["incl"], lm[C:, :LANES], 0.0)
        q["lm_k"] = jnp.concatenate([jnp.where(q["strict"], lm[:C, LANES:], 0.0),
                                     jnp.where(q["incl"], lm[C:, LANES:], 0.0)], axis=0)
    for q in probs:
        lmv = _bdot(q["lm_k"], bd(q["v"]))
        q["l_kv"], q["m_kv"] = lmv[:C], lmv[C:]
    for q in probs:
        q["kr_h"] = _bdot(q["lhs"], q["h"])
    for q in probs:
        l_b = q["l_b"]
        l_c = l_b[0:SUB]
        for jb in range(1, C // SUB):
            l_c = jnp.where(blk_c == jb, l_b[jb * SUB:(jb + 1) * SUB], l_c)
        q["l_c"] = l_c
        q["t_c"] = eye_c - jnp.where(_same_block(row_c, col_c, 2), l_c, 0.0)
    s = 2
    while s < SUB:
        off_mask = _same_block(row_c, col_c, 2 * s) & ~_same_block(row_c, col_c, s)
        for q in probs:
            q["et"] = _bdot(jnp.where(off_mask, q["l_c"], 0.0), bd_c(q["t_c"]))
        for q in probs:
            q["t_c"] = q["t_c"] - _bdot(q["t_c"], bd_c(q["et"]))
        s *= 2
    for q in probs:
        q["tinv"] = jnp.concatenate([jnp.where(blk_c == jb, q["t_c"], 0.0) for jb in range(C // SUB)], axis=0)
    while s < C:
        off_mask = _same_block(row, s_idx, 2 * s) & ~_same_block(row, s_idx, s)
        for q in probs:
            q["et"] = _bdot(jnp.where(off_mask, q["l_b"], 0.0), bd(q["tinv"]))
        for q in probs:
            q["tinv"] = q["tinv"] - _bdot(q["tinv"], bd(q["et"]))
        s *= 2
    for q in probs:
        q["u_n"] = _bdot(q["tinv"], bd(q["kr_h"][:C] + q["l_kv"]))
    for q in probs:
        y = q["kr_h"][C:] + q["m_kv"] - _bdot(q["m_b"], bd(q["u_n"]))
        y_refs[q["d"]][q["n"], :, q["sl"]] = y.astype(BF16)
    for q in probs:
        upd = _bdot(q["t2"], jnp.concatenate([(-q["u_n"]).astype(BF16), q["v"]], axis=0))
        h_scr[q["n"], q["d"], q["p"]] = q["gam_col"] * q["h"] + jnp.where(same_head, upd, 0.0)

    if write_state:
        @pl.when(j == n_chunks - 1)
        def _():
            for n, d, p in [(n, d, p) for n in range(n_seq) for d in range(N_DIR) for p in range(N_PAIRS)]:
                ht = h_scr[n, d, p].T
                sfin_refs[d][n, 0, PAIR * p] = ht[:HEAD_DIM, :HEAD_DIM]
                sfin_refs[d][n, 0, PAIR * p + 1] = ht[HEAD_DIM:, HEAD_DIM:]


def _scan_call(rk, v, lwla, wts, s0_bd, write_state):
    B, T, _ = rk.shape
    nc = T // CHUNK
    ns = min(SCAN_SEQS, B)
    assert B % ns == 0
    zero_init = s0_bd is None
    fwd = lambda b, j: (b, j, 0)
    bwd = lambda b, j: (b, nc - 1 - j, 0)
    full = lambda *shape: pl.BlockSpec(shape, lambda b, j: (0,) * len(shape))
    tok = lambda w, m: pl.BlockSpec((ns, CHUNK, w), m)
    in_specs = [tok(2 * D_RWKV, fwd), tok(D_RWKV, fwd), tok(LANES, fwd),
                tok(2 * D_RWKV, bwd), tok(D_RWKV, bwd), tok(LANES, bwd),
                full(N_DIR, D_RWKV), full(N_DIR, LANES, D_RWKV),
                full(N_DIR, D_RWKV), full(N_DIR, LANES, D_RWKV),
                full(1, D_RWKV), full(1, D_RWKV), full(N_DIR, CHUNK, CHUNK), full(D_RWKV, D_RWKV)]
    args = [rk, v, lwla, rk, v, lwla, wts["w0"], wts["w2"], wts["a0"], wts["a2"],
            wts["k_k"], wts["k_a"], wts["tri"], wts["ones_bd"]]
    state_block = (ns, N_DIR, N_PAIRS, LANES, LANES)
    if not zero_init:
        in_specs.append(pl.BlockSpec(state_block, lambda b, j: (b, 0, 0, 0, 0)))
        args.append(s0_bd)
    out_specs = [pl.BlockSpec((ns, CHUNK, D_RWKV), fwd), pl.BlockSpec((ns, CHUNK, D_RWKV), bwd)]
    out_shape = [jax.ShapeDtypeStruct((B, T, D_RWKV), BF16), jax.ShapeDtypeStruct((B, T, D_RWKV), BF16)]
    if write_state:
        final_block = (ns, 1, N_HEADS, HEAD_DIM, HEAD_DIM)
        out_specs += [pl.BlockSpec(final_block, lambda b, j: (b, 0, 0, 0, 0))] * N_DIR
        out_shape += [jax.ShapeDtypeStruct((B,) + final_block[1:], F32)] * N_DIR
    kern = functools.partial(_scan_kernel, zero_init=zero_init, write_state=write_state, n_chunks=nc,
                             n_seq=ns)
    return pl.pallas_call(
        kern, grid=(B // ns, nc), in_specs=in_specs, out_specs=out_specs, out_shape=out_shape,
        scratch_shapes=[pltpu.VMEM(state_block, F32)],
        compiler_params=pltpu.CompilerParams(dimension_semantics=("arbitrary", "arbitrary"),
                                             vmem_limit_bytes=VMEM_LIMIT),
        name="scan",
    )(*args)


def _fnet_kernel(fin_ref, gf_ref, dft_ref, cs_ref, wbd_ref, b_ref, o_ref, g_scr):
    u = pl.program_id(1)
    nb, seq_len, _ = fin_ref.shape
    tu = o_ref.shape[1]

    @pl.when(u == 0)
    def _():
        step = min(seq_len, ROW_TILE)
        for n in range(nb):
            for t0 in range(0, seq_len, step):
                fc = _bdot(fin_ref[n, t0:t0 + step, :], cs_ref[...])
                g_scr[n, t0:t0 + step, :] = fc[:, :D_FNET].astype(BF16)
                g_scr[n, seq_len + t0:seq_len + t0 + step, :] = fc[:, D_FNET:].astype(BF16)

    f_re = jnp.concatenate([jnp.dot(dft_ref[...], g_scr[n], preferred_element_type=F32) for n in range(nb)],
                           axis=0)
    f_out = _bdot(f_re, wbd_ref[...]) + b_ref[...]
    gate = _silu(gf_ref[...].astype(F32)).reshape(nb * tu, D_FNET)
    o_ref[...] = (f_out * gate).astype(BF16).reshape(nb, tu, D_FNET)


def _fnet_call(fin, gf, dft_bf16, cs_bf16, wbd_bf16, b_fnet):
    B, T, _ = fin.shape
    tu = min(T, ROW_TILE)
    nb = max(1, FNET_ROWS // T)
    assert B % nb == 0 and T % tu == 0
    return pl.pallas_call(
        _fnet_kernel,
        grid=(B // nb, T // tu),
        in_specs=[pl.BlockSpec((nb, T, D_FNET), lambda b, u: (b, 0, 0)),
                  pl.BlockSpec((nb, tu, D_FNET), lambda b, u: (b, u, 0)),
                  pl.BlockSpec((tu, 2 * T), lambda b, u: (u, 0)),
                  pl.BlockSpec((D_FNET, 2 * D_FNET), lambda b, u: (0, 0)),
                  pl.BlockSpec((D_FNET, D_FNET), lambda b, u: (0, 0)),
                  pl.BlockSpec((1, D_FNET), lambda b, u: (0, 0))],
        out_specs=pl.BlockSpec((nb, tu, D_FNET), lambda b, u: (b, u, 0)),
        out_shape=jax.ShapeDtypeStruct((B, T, D_FNET), BF16),
        scratch_shapes=[pltpu.VMEM((nb, 2 * T, D_FNET), BF16)],
        compiler_params=pltpu.CompilerParams(dimension_semantics=("arbitrary", "arbitrary"),
                                             vmem_limit_bytes=VMEM_LIMIT),
        name="fnet",
    )(fin, gf, dft_bf16, cs_bf16, wbd_bf16, b_fnet)


def _out_kernel(*refs, has_emb, final_norm):
    it = iter(refs)
    x_ref = next(it)
    emb_ref = next(it) if has_emb else None
    (mod_ref, yf_ref, yb_ref, rk_ref, v_ref, lwla_ref, grec_ref, fo_ref, a0_ref, a2_ref, ka_ref,
     rkw_ref, gng_ref, gnb_ref, avg_ref, ones_ref, wout_ref, fng_ref, o_ref) = (next(it) for _ in range(19))

    nb, tt, _ = x_ref.shape
    tm = nb * tt
    rows = lambda ref: ref[...].reshape(tm, ref.shape[-1])
    y = rows(yf_ref).astype(F32) + rows(yb_ref).astype(F32)
    mu = _bdot(y, avg_ref[...])
    dlt = y - mu
    var = _bdot(dlt * dlt, avg_ref[...])
    y_n = dlt * lax.rsqrt(var + GN_EPS) * gng_ref[...] + gnb_ref[...]
    rk = rows(rk_ref).astype(F32)
    r = rk[:, :D_RWKV]
    k = rk[:, D_RWKV:]
    ll = rows(lwla_ref)
    a_sum = (jax.nn.sigmoid(a0_ref[0:1, :] + _bdot(ll, a2_ref[0]))
             + jax.nn.sigmoid(a0_ref[1:2, :] + _bdot(ll, a2_ref[1])))
    k_sum = k * (2.0 + (a_sum - 2.0) * ka_ref[...])
    bonus = _bdot(r * k_sum * rkw_ref[...], ones_ref[...]) * rows(v_ref).astype(F32)
    rec_out = (y_n + bonus) * _silu(rows(grec_ref).astype(F32))
    mixed = jnp.concatenate([rec_out.astype(BF16), rows(fo_ref)], axis=-1)
    out = jnp.dot(mixed, wout_ref[...], preferred_element_type=F32)
    x = rows(x_ref)
    if has_emb:
        x = x + emb_ref[...]
    z = x + mod_ref[0, 2:3, :] * out
    if final_norm:
        ms = jnp.mean(z * z, axis=-1, keepdims=True)
        z = z * lax.rsqrt(ms + NORM_EPS) * fng_ref[...]
    o_ref[...] = z.reshape(nb, tt, D_MODEL)


def _out_call(x, emb, mod, y_f, y_b, rk, v, lwla, grec, fo, wts, w_out_bf16, final_norm_g, final_norm):
    B, T, _ = x.shape
    tt = min(T, ROW_TILE)
    nb = max(1, ROW_TILE // T)
    has_emb = emb is not None
    per_batch_mod = mod.shape[0] > 1
    assert B % nb == 0 and T % tt == 0 and (nb == 1 or not (has_emb or per_batch_mod))
    tok = lambda w: pl.BlockSpec((nb, tt, w), lambda b, i: (b, i, 0))
    full = lambda *shape: pl.BlockSpec(shape, lambda b, i: (0,) * len(shape))
    in_specs = [tok(D_MODEL)]
    args = [x]
    if has_emb:
        in_specs.append(pl.BlockSpec((tt, D_MODEL), lambda b, i: (i, 0)))
        args.append(emb)
    mod_map = (lambda b, i: (b, 0, 0)) if per_batch_mod else (lambda b, i: (0, 0, 0))
    in_specs += [pl.BlockSpec((1, 3, D_MODEL), mod_map), tok(D_RWKV), tok(D_RWKV), tok(2 * D_RWKV),
                 tok(D_RWKV), tok(LANES), tok(D_RWKV), tok(D_FNET),
                 full(N_DIR, D_RWKV), full(N_DIR, LANES, D_RWKV),
                 full(1, D_RWKV), full(1, D_RWKV), full(1, D_RWKV), full(1, D_RWKV),
                 full(D_RWKV, D_RWKV), full(D_RWKV, D_RWKV), full(D_MODEL, D_MODEL), full(1, D_MODEL)]
    args += [mod, y_f, y_b, rk, v, lwla, grec, fo, wts["a0"], wts["a2"], wts["k_a"],
             wts["r_k"], wts["gn_g"], wts["gn_b"], wts["avg_bd"], wts["ones_bd"], w_out_bf16, final_norm_g]
    return pl.pallas_call(
        functools.partial(_out_kernel, has_emb=has_emb, final_norm=final_norm),
        grid=(B // nb, T // tt), in_specs=in_specs,
        out_specs=tok(D_MODEL),
        out_shape=jax.ShapeDtypeStruct((B, T, D_MODEL), F32),
        compiler_params=pltpu.CompilerParams(dimension_semantics=("arbitrary", "arbitrary"),
                                             vmem_limit_bytes=VMEM_LIMIT),
        name="out_proj",
    )(*args)


def _dft_table(seq_len):
    idx = np.arange(seq_len, dtype=np.int64)
    ang = 2.0 * np.pi * ((idx[:, None] * idx[None, :]) % seq_len).astype(np.float64) / seq_len
    scale = 1.0 / math.sqrt(seq_len)
    return np.concatenate([np.cos(ang) * scale, -np.sin(ang) * scale], axis=1).astype(np.float32)


def _channel_dft_table():
    n = FNET_GROUP
    idx = np.arange(n, dtype=np.int64)
    ang = 2.0 * np.pi * ((idx[:, None] * idx[None, :]) % n).astype(np.float64) / n
    c = np.cos(ang) / math.sqrt(n)
    s = np.sin(ang) / math.sqrt(n)
    eye = np.eye(D_FNET // n)
    return np.concatenate([np.kron(eye, c), np.kron(eye, s)], axis=1).astype(np.float32)


def _sincos_2d(n_tokens):
    rows = n_tokens // GRID_W
    pos = np.arange(rows * GRID_W)
    row = (pos // GRID_W).astype(np.float32)
    col = (pos % GRID_W).astype(np.float32)
    quarter = D_MODEL // 4
    freq = np.exp(np.float32(-math.log(POS_BASE)) * np.arange(quarter, dtype=np.float32) / np.float32(quarter))
    ang_r = row[:, None] * freq
    ang_c = col[:, None] * freq
    return np.concatenate([np.sin(ang_r), np.cos(ang_r), np.sin(ang_c), np.cos(ang_c)], axis=-1).astype(np.float32)


def _head_block_matrix(value):
    blk = np.kron(np.eye(N_HEADS), np.ones((HEAD_DIM, HEAD_DIM))) * value
    return jnp.asarray(blk, dtype=BF16)


def _pad_lora(w, row_offset):
    out = jnp.zeros((N_DIR, LANES, w.shape[-1]), F32)
    for d in range(N_DIR):
        out = out.at[d, row_offset + d * LORA:row_offset + (d + 1) * LORA, :].set(w[d])
    return out.astype(BF16)


def _layer_weights(l, w0, w2, a0, a2, k_k, k_a, r_k, gn_g, gn_b):
    w2_p = _pad_lora(w2[l], 0)
    a2_p = _pad_lora(a2[l], N_DIR * LORA)
    tri_f = np.tril(np.ones((CHUNK, CHUNK)))
    tri = jnp.asarray(np.stack([tri_f, tri_f.T]), dtype=BF16)
    return dict(w0=w0[l], w2=w2_p, a0=a0[l], a2=a2_p,
                k_k=k_k[l][None], k_a=k_a[l][None], r_k=r_k[l].reshape(1, D_RWKV),
                gn_g=gn_g[l].reshape(1, D_RWKV), gn_b=gn_b[l].reshape(1, D_RWKV), tri=tri,
                ones_bd=_head_block_matrix(1.0), avg_bd=_head_block_matrix(1.0 / HEAD_DIM))


def _state_to_block_diag(s_f, s_b):
    def one(s):
        h = jnp.swapaxes(s.astype(F32), -1, -2)
        b = h.shape[0]
        h = h.reshape(b, N_PAIRS, PAIR, HEAD_DIM, HEAD_DIM)
        z = jnp.zeros_like(h[:, :, 0])
        top = jnp.concatenate([h[:, :, 0], z], axis=-1)
        bot = jnp.concatenate([z, h[:, :, 1]], axis=-1)
        return jnp.concatenate([top, bot], axis=-2)
    return jnp.stack([one(s_f), one(s_b)], axis=1)


def kernel(x_prompt, x_sample, state_rwkv_fwd, state_rwkv_bwd, c, c_ctx, w_ada, b_ada, norm_g, w_in,
           mu_shift, w0, w2, a0, a2, k_k, k_a, r_k, gn_g, gn_b, w_fnet, b_fnet, w_out, final_norm_g):
    depth = w_in.shape[0]
    n_dec = c.shape[0]
    assert n_dec + 1 <= 8
    bp, tp, _ = x_prompt.shape
    bs, ts, _ = x_sample.shape
    cvec = jnp.zeros((8, D_MODEL), F32).at[0].set(c_ctx).at[1:1 + n_dec].set(c)
    emb = jnp.asarray(_sincos_2d(ts)).astype(x_sample.dtype)
    cs_tab = jnp.asarray(_channel_dft_table()).astype(BF16)
    dft_p = jnp.asarray(_dft_table(tp)).astype(BF16)
    dft_s = jnp.asarray(_dft_table(ts)).astype(BF16)
    fng = final_norm_g[None]

    xp, xs = x_prompt, x_sample
    new_f, new_b = [], []
    for l in range(depth):
        mod = _mod_call(cvec, w_ada[l], b_ada[l][None]).reshape(8, 3, D_MODEL)
        mod_ctx, mod_lat = mod[0:1], mod[1:1 + n_dec]
        wts = _layer_weights(l, w0, w2, a0, a2, k_k, k_a, r_k, gn_g, gn_b)
        w_in_b = w_in[l].astype(BF16)
        w_out_b = w_out[l].astype(BF16)
        wbd = jax.scipy.linalg.block_diag(*[w_fnet[l, g] for g in range(w_fnet.shape[1])]).astype(BF16)
        ng, mu, bf = norm_g[l][None], mu_shift[l][None], b_fnet[l][None]
        emb_l = emb if l == 0 else None

        rk, v, lwla, grec, fin, gf = _in_proj_call(xp, None, mod_ctx, ng, w_in_b, mu)
        y_f, y_b, s_f, s_b = _scan_call(rk, v, lwla, wts, None, True)
        fo = _fnet_call(fin, gf, dft_p, cs_tab, wbd, bf)
        last = l == depth - 1
        xp = _out_call(xp, None, mod_ctx, y_f, y_b, rk, v, lwla, grec, fo, wts, w_out_b, fng, last)
        new_f.append(s_f)
        new_b.append(s_b)

        rk, v, lwla, grec, fin, gf = _in_proj_call(xs, emb_l, mod_lat, ng, w_in_b, mu)
        s0 = _state_to_block_diag(state_rwkv_fwd[:, l], state_rwkv_bwd[:, l])
        y_f, y_b = _scan_call(rk, v, lwla, wts, s0, False)
        fo = _fnet_call(fin, gf, dft_s, cs_tab, wbd, bf)
        xs = _out_call(xs, emb_l, mod_lat, y_f, y_b, rk, v, lwla, grec, fo, wts, w_out_b, fng, last)
    return (xp, xs, jnp.concatenate(new_f, axis=1), jnp.concatenate(new_b, axis=1))
```

```python
import functools
import math

import numpy as np
import jax
import jax.numpy as jnp
from jax import lax
from jax.experimental import pallas as pl
from jax.experimental.pallas import tpu as pltpu

F32 = jnp.float32
BF16 = jnp.bfloat16

D_MODEL = 1024
GRID_W = 64
D_RWKV = 512
D_FNET = D_MODEL - D_RWKV
HEAD_DIM = 64
N_HEADS = D_RWKV // HEAD_DIM
FNET_GROUP = 64
LORA = 32
N_DIR = 2
D_SHIFT = 3 * D_RWKV + N_DIR * 2 * LORA
D_IN = D_SHIFT + D_RWKV + 2 * D_FNET
NORM_EPS = 1e-6
GN_EPS = 64e-5
POS_BASE = 10000.0

LANES = 128
PAIR = LANES // HEAD_DIM
N_PAIRS = N_HEADS // PAIR
CHUNK = 64
SUB = 16
SUB_SHIFT = SUB.bit_length() - 1
SCAN_SEQS = 4
ROW_TILE = 512
FNET_ROWS = 1024
HALO = 8
VMEM_LIMIT = 56 * 1024 * 1024


def _silu(x):
    return x * jax.nn.sigmoid(x)


def _bdot(a, b):
    return jnp.dot(a.astype(BF16), b.astype(BF16), preferred_element_type=F32)


def _split2(x):
    hi = x.astype(BF16)
    lo = (x - hi.astype(F32)).astype(BF16)
    return hi, lo


def _mod_kernel(c_ref, w_ref, b_ref, o_ref):
    s_hi, s_lo = _split2(_silu(c_ref[...]))
    w = w_ref[...].astype(BF16)
    o_ref[...] = (jnp.dot(s_hi, w, preferred_element_type=F32)
                  + jnp.dot(s_lo, w, preferred_element_type=F32) + b_ref[...])


def _mod_call(cvec, w_ada, b_ada):
    n_blk = 3
    return pl.pallas_call(
        _mod_kernel,
        grid=(n_blk,),
        in_specs=[pl.BlockSpec((8, D_MODEL), lambda i: (0, 0)),
                  pl.BlockSpec((D_MODEL, D_MODEL), lambda i: (0, i)),
                  pl.BlockSpec((1, D_MODEL), lambda i: (0, i))],
        out_specs=pl.BlockSpec((8, D_MODEL), lambda i: (0, i)),
        out_shape=jax.ShapeDtypeStruct((8, 3 * D_MODEL), F32),
        compiler_params=pltpu.CompilerParams(dimension_semantics=("arbitrary",),
                                             vmem_limit_bytes=VMEM_LIMIT),
        name="mod",
    )(cvec, w_ada, b_ada)


def _modulated_norm(x, g, scale, shift):
    ms = jnp.mean(x * x, axis=-1, keepdims=True)
    y = x * lax.rsqrt(ms + NORM_EPS) * g
    return y * (1.0 + scale) + shift


def _in_proj_kernel(*refs, has_emb, has_halo, n_tiles):
    it = iter(refs)
    x_ref = next(it)
    xp_ref = next(it) if has_halo else None
    xn_ref = next(it) if has_halo else None
    emb_ref = next(it) if has_emb else None
    embp_ref = next(it) if (has_emb and has_halo) else None
    embn_ref = next(it) if (has_emb and has_halo) else None
    mod_ref, g_ref, w_ref, mu_ref = next(it), next(it), next(it), next(it)
    rk_ref, v_ref, lwla_ref, grec_ref, fin_ref, gf_ref = (next(it) for _ in range(6))

    i = pl.program_id(1)
    g = g_ref[...]
    shift = mod_ref[0, 0:1, :]
    scale = mod_ref[0, 1:2, :]
    nb, tt, _ = x_ref.shape
    tm = nb * tt
    x = x_ref[...].reshape(tm, D_MODEL)
    if has_emb:
        x = x + emb_ref[...]
    h = _modulated_norm(x, g, scale, shift)
    p = jnp.dot(h.astype(BF16), w_ref[...], preferred_element_type=F32)
    ps = p[:, :D_SHIFT]
    if has_halo:
        xh = jnp.concatenate([xp_ref[0], xn_ref[0]], axis=0)
        if has_emb:
            xh = xh + jnp.concatenate([embp_ref[...], embn_ref[...]], axis=0)
        hh = _modulated_norm(xh, g, scale, shift)
        ph = jnp.dot(hh.astype(BF16), w_ref[:, :D_SHIFT], preferred_element_type=F32)
        prev_row = jnp.where(i > 0, ph[HALO - 1:HALO, :], 0.0)
        next_row = jnp.where(i < n_tiles - 1, ph[HALO:HALO + 1, :], 0.0)
    else:
        prev_row = jnp.zeros((1, D_SHIFT), F32)
        next_row = jnp.zeros((1, D_SHIFT), F32)
    pos = lax.rem(lax.broadcasted_iota(jnp.int32, (tm, D_SHIFT), 0), tt)
    prev = jnp.where(pos == 0, prev_row, pltpu.roll(ps, 1, 0))
    nxt = jnp.where(pos == tt - 1, next_row, pltpu.roll(ps, tm - 1, 0))
    p_rec = ps + mu_ref[...] * (0.5 * (prev + nxt) - ps)

    def put(ref, val):
        ref[...] = val.astype(ref.dtype).reshape(ref.shape)

    put(rk_ref, p_rec[:, :2 * D_RWKV])
    put(v_ref, p_rec[:, 2 * D_RWKV:3 * D_RWKV])
    put(lwla_ref, p_rec[:, 3 * D_RWKV:])
    put(grec_ref, p[:, D_SHIFT:D_SHIFT + D_RWKV])
    put(fin_ref, p[:, D_SHIFT + D_RWKV:D_SHIFT + D_RWKV + D_FNET])
    put(gf_ref, p[:, D_SHIFT + D_RWKV + D_FNET:])


def _in_proj_call(x, emb, mod, norm_g, w_in_bf16, mu):
    B, T, _ = x.shape
    tt = min(T, ROW_TILE)
    nb = max(1, ROW_TILE // T)
    n_tiles = T // tt
    has_halo = n_tiles > 1
    has_emb = emb is not None
    per_batch_mod = mod.shape[0] > 1
    assert B % nb == 0 and T % tt == 0 and (nb == 1 or not (has_emb or per_batch_mod))
    tm = tt
    blocks_per_tile = tm // HALO
    last_halo_block = T // HALO - 1

    in_specs = [pl.BlockSpec((nb, tt, D_MODEL), lambda b, i: (b, i, 0))]
    args = [x]
    if has_halo:
        in_specs += [
            pl.BlockSpec((1, HALO, D_MODEL), lambda b, i: (b, jnp.maximum(i * blocks_per_tile - 1, 0), 0)),
            pl.BlockSpec((1, HALO, D_MODEL),
                         lambda b, i: (b, jnp.minimum((i + 1) * blocks_per_tile, last_halo_block), 0))]
        args += [x, x]
    if has_emb:
        in_specs.append(pl.BlockSpec((tm, D_MODEL), lambda b, i: (i, 0)))
        args.append(emb)
        if has_halo:
            in_specs += [
                pl.BlockSpec((HALO, D_MODEL), lambda b, i: (jnp.maximum(i * blocks_per_tile - 1, 0), 0)),
                pl.BlockSpec((HALO, D_MODEL),
                             lambda b, i: (jnp.minimum((i + 1) * blocks_per_tile, last_halo_block), 0))]
            args += [emb, emb]
    mod_map = (lambda b, i: (b, 0, 0)) if per_batch_mod else (lambda b, i: (0, 0, 0))
    in_specs += [pl.BlockSpec((1, 3, D_MODEL), mod_map),
                 pl.BlockSpec((1, D_MODEL), lambda b, i: (0, 0)),
                 pl.BlockSpec((D_MODEL, D_IN), lambda b, i: (0, 0), pipeline_mode=pl.Buffered(1)),
                 pl.BlockSpec((1, D_SHIFT), lambda b, i: (0, 0))]
    args += [mod, norm_g, w_in_bf16, mu]
    outs = ((2 * D_RWKV, BF16), (D_RWKV, BF16), (N_DIR * 2 * LORA, F32), (D_RWKV, BF16), (D_FNET, BF16),
            (D_FNET, BF16))
    out_specs = [pl.BlockSpec((nb, tt, w), lambda b, i: (b, i, 0)) for w, _ in outs]
    out_shape = [jax.ShapeDtypeStruct((B, T, w), dt) for w, dt in outs]
    kern = functools.partial(_in_proj_kernel, has_emb=has_emb, has_halo=has_halo, n_tiles=n_tiles)
    return pl.pallas_call(
        kern, grid=(B // nb, n_tiles), in_specs=in_specs, out_specs=out_specs, out_shape=out_shape,
        compiler_params=pltpu.CompilerParams(dimension_semantics=("arbitrary", "arbitrary"),
                                             vmem_limit_bytes=VMEM_LIMIT),
        name="in_proj",
    )(*args)


def _same_block(i, j, size):
    shift = size.bit_length() - 1
    return jnp.right_shift(i, shift) == jnp.right_shift(j, shift)


def _block_diag_rows(y, head0_lanes):
    return jnp.concatenate([jnp.where(head0_lanes, y, 0.0), jnp.where(head0_lanes, 0.0, y)], axis=0)


def _scan_kernel(*refs, zero_init, write_state, n_chunks, n_seq):
    it = iter(refs)
    rk_refs, v_refs, lwla_refs = [None, None], [None, None], [None, None]
    for d in range(N_DIR):
        rk_refs[d], v_refs[d], lwla_refs[d] = next(it), next(it), next(it)
    w0_ref, w2_ref, a0_ref, a2_ref = (next(it) for _ in range(4))
    kk_ref, ka_ref, tri_ref, ones_ref = (next(it) for _ in range(4))
    s0_ref = None if zero_init else next(it)
    y_refs = [next(it), next(it)]
    sfin_refs = [next(it), next(it)] if write_state else None
    h_scr = next(it)

    j = pl.program_id(1)

    @pl.when(j == 0)
    def _():
        if zero_init:
            h_scr[...] = jnp.zeros_like(h_scr)
        else:
            h_scr[...] = s0_ref[...]

    C = CHUNK
    row = lax.broadcasted_iota(jnp.int32, (C, LANES), 0)
    lane = lax.broadcasted_iota(jnp.int32, (C, LANES), 1)
    s_idx = jnp.bitwise_and(lane, HEAD_DIM - 1)
    head0 = lane < HEAD_DIM
    row_c = lax.broadcasted_iota(jnp.int32, (SUB, LANES), 0)
    lane_c = lax.broadcasted_iota(jnp.int32, (SUB, LANES), 1)
    col_c = jnp.bitwise_and(lane_c, SUB - 1)
    blk_c = jnp.right_shift(jnp.bitwise_and(lane_c, HEAD_DIM - 1), SUB_SHIFT)
    lane_blk_c = jnp.right_shift(lane_c, SUB_SHIFT)
    eye_c = (row_c == col_c).astype(F32)

    def bd_c(y):
        return jnp.concatenate([jnp.where(lane_blk_c == g, y, 0.0) for g in range(LANES // SUB)], axis=0)

    row2 = lax.broadcasted_iota(jnp.int32, (LANES, LANES), 0)
    lane2 = lax.broadcasted_iota(jnp.int32, (LANES, LANES), 1)
    same_head = (row2 < HEAD_DIM) == (lane2 < HEAD_DIM)
    decay_scale = math.exp(-0.5)

    probs = []
    for n, d in [(n, d) for n in range(n_seq) for d in range(N_DIR)]:
        rk = rk_refs[d][n].astype(F32)
        r = rk[:, :D_RWKV]
        k = rk[:, D_RWKV:]
        v = v_refs[d][n]
        ll = lwla_refs[d][n]
        z_w = w0_ref[d:d + 1, :] + _bdot(jnp.tanh(ll), w2_ref[d])
        logw = -decay_scale * jax.nn.sigmoid(z_w)
        a = jax.nn.sigmoid(a0_ref[d:d + 1, :] + _bdot(ll, a2_ref[d]))
        kd = k * (1.0 + (a - 1.0) * ka_ref[...])
        kkr = k * kk_ref[...]
        ssq = _bdot(kkr * kkr, ones_ref[...])
        kk = kkr * lax.rsqrt(jnp.maximum(ssq, 1e-24))
        bvec = kk * a
        lw_hi, lw_lo = _split2(logw)
        tri = tri_ref[d]
        cum = (jnp.dot(tri, lw_hi, preferred_element_type=F32)
               + jnp.dot(tri, lw_lo, preferred_element_type=F32))
        cum_prev = cum - logw
        tot = cum[C - 1:C, :] if d == 0 else cum[0:1, :]
        kap_t = kk * jnp.exp(cum_prev)
        r_t = r * jnp.exp(cum)
        e_neg = jnp.exp(-cum)
        gam = jnp.exp(tot)
        b_t = bvec * e_neg
        k_t = kd * e_neg
        e_rem = gam * e_neg
        b_h = bvec * e_rem
        k_h = kd * e_rem
        if d == 0:
            strict, incl = row > s_idx, row >= s_idx
        else:
            strict, incl = row < s_idx, row <= s_idx

        for p in range(N_PAIRS):
            sl = slice(p * LANES, (p + 1) * LANES)
            dup_t = lambda x: jnp.where(same_head, jnp.concatenate([x[:, sl], x[:, sl]], axis=0).T, 0.0)
            probs.append(dict(n=n, d=d, p=p, sl=sl, strict=strict, incl=incl,
                              lhs=jnp.concatenate([kap_t[:, sl], r_t[:, sl]], axis=0).astype(BF16),
                              v=v[:, sl], w_lm=jnp.concatenate([dup_t(b_t), dup_t(k_t)], axis=1).astype(BF16),
                              t2=jnp.concatenate([b_h[:, sl], k_h[:, sl]], axis=0).T.astype(BF16),
                              gam_col=jnp.broadcast_to(gam[:, sl], (LANES, LANES)).T))

    bd = lambda y: _block_diag_rows(y, head0)
    for q in probs:
        q["h"] = h_scr[q["n"], q["d"], q["p"]]
        lm = jnp.dot(q["lhs"], q["w_lm"], preferred_element_type=F32)
        q["l_b"] = jnp.where(q["strict"], lm[:C, :LANES], 0.0)
        q["m_b"] = jnp.where(q["incl"], lm[C:, :LANES], 0.0)
        q["lm_k"] = jnp.concatenate([jnp.where(q["strict"], lm[:C, LANES:], 0.0),
                                     jnp.where(q["incl"], lm[C:, LANES:], 0.0)], axis=0)
    for q in probs:
        lmv = _bdot(q["lm_k"], bd(q["v"]))
        q["l_kv"], q["m_kv"] = lmv[:C], lmv[C:]
    for q in probs:
        q["kr_h"] = _bdot(q["lhs"], q["h"])
    for q in probs:
        l_b = q["l_b"]
        l_c = l_b[0:SUB]
        for jb in range(1, C // SUB):
            l_c = jnp.where(blk_c == jb, l_b[jb * SUB:(jb + 1) * SUB], l_c)
        q["l_c"] = l_c
        q["t_c"] = eye_c - jnp.where(_same_block(row_c, col_c, 2), l_c, 0.0)
    s = 2
    while s < SUB:
        off_mask = _same_block(row_c, col_c, 2 * s) & ~_same_block(row_c, col_c, s)
        for q in probs:
            q["et"] = _bdot(jnp.where(off_mask, q["l_c"], 0.0), bd_c(q["t_c"]))
        for q in probs:
            q["t_c"] = q["t_c"] - _bdot(q["t_c"], bd_c(q["et"]))
        s *= 2
    for q in probs:
        q["tinv"] = jnp.concatenate([jnp.where(blk_c == jb, q["t_c"], 0.0) for jb in range(C // SUB)], axis=0)
    while s < C:
        off_mask = _same_block(row, s_idx, 2 * s) & ~_same_block(row, s_idx, s)
        for q in probs:
            q["et"] = _bdot(jnp.where(off_mask, q["l_b"], 0.0), bd(q["tinv"]))
        for q in probs:
            q["tinv"] = q["tinv"] - _bdot(q["tinv"], bd(q["et"]))
        s *= 2
    for q in probs:
        q["u_n"] = _bdot(q["tinv"], bd(q["kr_h"][:C] + q["l_kv"]))
    for q in probs:
        y = q["kr_h"][C:] + q["m_kv"] - _bdot(q["m_b"], bd(q["u_n"]))
        y_refs[q["d"]][q["n"], :, q["sl"]] = y.astype(BF16)
    for q in probs:
        upd = _bdot(q["t2"], jnp.concatenate([(-q["u_n"]).astype(BF16), q["v"]], axis=0))
        h_scr[q["n"], q["d"], q["p"]] = q["gam_col"] * q["h"] + jnp.where(same_head, upd, 0.0)

    if write_state:
        @pl.when(j == n_chunks - 1)
        def _():
            for n, d, p in [(n, d, p) for n in range(n_seq) for d in range(N_DIR) for p in range(N_PAIRS)]:
                ht = h_scr[n, d, p].T
                sfin_refs[d][n, 0, PAIR * p] = ht[:HEAD_DIM, :HEAD_DIM]
                sfin_refs[d][n, 0, PAIR * p + 1] = ht[HEAD_DIM:, HEAD_DIM:]


def _scan_call(rk, v, lwla, wts, s0_bd, write_state):
    B, T, _ = rk.shape
    nc = T // CHUNK
    ns = min(SCAN_SEQS, B)
    assert B % ns == 0
    zero_init = s0_bd is None
    fwd = lambda b, j: (b, j, 0)
    bwd = lambda b, j: (b, nc - 1 - j, 0)
    full = lambda *shape: pl.BlockSpec(shape, lambda b, j: (0,) * len(shape))
    tok = lambda w, m: pl.BlockSpec((ns, CHUNK, w), m)
    in_specs = [tok(2 * D_RWKV, fwd), tok(D_RWKV, fwd), tok(LANES, fwd),
                tok(2 * D_RWKV, bwd), tok(D_RWKV, bwd), tok(LANES, bwd),
                full(N_DIR, D_RWKV), full(N_DIR, LANES, D_RWKV),
                full(N_DIR, D_RWKV), full(N_DIR, LANES, D_RWKV),
                full(1, D_RWKV), full(1, D_RWKV), full(N_DIR, CHUNK, CHUNK), full(D_RWKV, D_RWKV)]
    args = [rk, v, lwla, rk, v, lwla, wts["w0"], wts["w2"], wts["a0"], wts["a2"],
            wts["k_k"], wts["k_a"], wts["tri"], wts["ones_bd"]]
    state_block = (ns, N_DIR, N_PAIRS, LANES, LANES)
    if not zero_init:
        in_specs.append(pl.BlockSpec(state_block, lambda b, j: (b, 0, 0, 0, 0)))
        args.append(s0_bd)
    out_specs = [pl.BlockSpec((ns, CHUNK, D_RWKV), fwd), pl.BlockSpec((ns, CHUNK, D_RWKV), bwd)]
    out_shape = [jax.ShapeDtypeStruct((B, T, D_RWKV), BF16), jax.ShapeDtypeStruct((B, T, D_RWKV), BF16)]
    if write_state:
        final_block = (ns, 1, N_HEADS, HEAD_DIM, HEAD_DIM)
        out_specs += [pl.BlockSpec(final_block, lambda b, j: (b, 0, 0, 0, 0))] * N_DIR
        out_shape += [jax.ShapeDtypeStruct((B,) + final_block[1:], F32)] * N_DIR
    kern = functools.partial(_scan_kernel, zero_init=zero_init, write_state=write_state, n_chunks=nc,
                             n_seq=ns)
    return pl.pallas_call(
        kern, grid=(B // ns, nc), in_specs=in_specs, out_specs=out_specs, out_shape=out_shape,
        scratch_shapes=[pltpu.VMEM(state_block, F32)],
        compiler_params=pltpu.CompilerParams(dimension_semantics=("arbitrary", "arbitrary"),
                                             vmem_limit_bytes=VMEM_LIMIT),
        name="scan",
    )(*args)


def _fnet_kernel(fin_ref, gf_ref, dft_ref, cs_ref, wbd_ref, b_ref, o_ref, g_scr):
    u = pl.program_id(1)
    nb, seq_len, _ = fin_ref.shape
    tu = o_ref.shape[1]

    @pl.when(u == 0)
    def _():
        step = min(seq_len, ROW_TILE)
        for n in range(nb):
            for t0 in range(0, seq_len, step):
                fc = _bdot(fin_ref[n, t0:t0 + step, :], cs_ref[...])
                g_scr[n, t0:t0 + step, :] = fc[:, :D_FNET].astype(BF16)
                g_scr[n, seq_len + t0:seq_len + t0 + step, :] = fc[:, D_FNET:].astype(BF16)

    f_re = jnp.concatenate([jnp.dot(dft_ref[...], g_scr[n], preferred_element_type=F32) for n in range(nb)],
                           axis=0)
    f_out = _bdot(f_re, wbd_ref[...]) + b_ref[...]
    gate = _silu(gf_ref[...].astype(F32)).reshape(nb * tu, D_FNET)
    o_ref[...] = (f_out * gate).astype(BF16).reshape(nb, tu, D_FNET)


def _fnet_call(fin, gf, dft_bf16, cs_bf16, wbd_bf16, b_fnet):
    B, T, _ = fin.shape
    tu = min(T, ROW_TILE)
    nb = max(1, FNET_ROWS // T)
    assert B % nb == 0 and T % tu == 0
    return pl.pallas_call(
        _fnet_kernel,
        grid=(B // nb, T // tu),
        in_specs=[pl.BlockSpec((nb, T, D_FNET), lambda b, u: (b, 0, 0)),
                  pl.BlockSpec((nb, tu, D_FNET), lambda b, u: (b, u, 0)),
                  pl.BlockSpec((tu, 2 * T), lambda b, u: (u, 0)),
                  pl.BlockSpec((D_FNET, 2 * D_FNET), lambda b, u: (0, 0)),
                  pl.BlockSpec((D_FNET, D_FNET), lambda b, u: (0, 0)),
                  pl.BlockSpec((1, D_FNET), lambda b, u: (0, 0))],
        out_specs=pl.BlockSpec((nb, tu, D_FNET), lambda b, u: (b, u, 0)),
        out_shape=jax.ShapeDtypeStruct((B, T, D_FNET), BF16),
        scratch_shapes=[pltpu.VMEM((nb, 2 * T, D_FNET), BF16)],
        compiler_params=pltpu.CompilerParams(dimension_semantics=("arbitrary", "arbitrary"),
                                             vmem_limit_bytes=VMEM_LIMIT),
        name="fnet",
    )(fin, gf, dft_bf16, cs_bf16, wbd_bf16, b_fnet)


def _out_kernel(*refs, has_emb, final_norm):
    it = iter(refs)
    x_ref = next(it)
    emb_ref = next(it) if has_emb else None
    (mod_ref, yf_ref, yb_ref, rk_ref, v_ref, lwla_ref, grec_ref, fo_ref, a0_ref, a2_ref, ka_ref,
     rkw_ref, gng_ref, gnb_ref, avg_ref, ones_ref, wout_ref, fng_ref, o_ref) = (next(it) for _ in range(19))

    nb, tt, _ = x_ref.shape
    tm = nb * tt
    rows = lambda ref: ref[...].reshape(tm, ref.shape[-1])
    y = rows(yf_ref).astype(F32) + rows(yb_ref).astype(F32)
    mu = _bdot(y, avg_ref[...])
    dlt = y - mu
    var = _bdot(dlt * dlt, avg_ref[...])
    y_n = dlt * lax.rsqrt(var + GN_EPS) * gng_ref[...] + gnb_ref[...]
    rk = rows(rk_ref).astype(F32)
    r = rk[:, :D_RWKV]
    k = rk[:, D_RWKV:]
    ll = rows(lwla_ref)
    a_sum = (jax.nn.sigmoid(a0_ref[0:1, :] + _bdot(ll, a2_ref[0]))
             + jax.nn.sigmoid(a0_ref[1:2, :] + _bdot(ll, a2_ref[1])))
    k_sum = k * (2.0 + (a_sum - 2.0) * ka_ref[...])
    bonus = _bdot(r * k_sum * rkw_ref[...], ones_ref[...]) * rows(v_ref).astype(F32)
    rec_out = (y_n + bonus) * _silu(rows(grec_ref).astype(F32))
    mixed = jnp.concatenate([rec_out.astype(BF16), rows(fo_ref)], axis=-1)
    out = jnp.dot(mixed, wout_ref[...], preferred_element_type=F32)
    x = rows(x_ref)
    if has_emb:
        x = x + emb_ref[...]
    z = x + mod_ref[0, 2:3, :] * out
    if final_norm:
        ms = jnp.mean(z * z, axis=-1, keepdims=True)
        z = z * lax.rsqrt(ms + NORM_EPS) * fng_ref[...]
    o_ref[...] = z.reshape(nb, tt, D_MODEL)


def _out_call(x, emb, mod, y_f, y_b, rk, v, lwla, grec, fo, wts, w_out_bf16, final_norm_g, final_norm):
    B, T, _ = x.shape
    tt = min(T, ROW_TILE)
    nb = max(1, ROW_TILE // T)
    has_emb = emb is not None
    per_batch_mod = mod.shape[0] > 1
    assert B % nb == 0 and T % tt == 0 and (nb == 1 or not (has_emb or per_batch_mod))
    tok = lambda w: pl.BlockSpec((nb, tt, w), lambda b, i: (b, i, 0))
    full = lambda *shape: pl.BlockSpec(shape, lambda b, i: (0,) * len(shape))
    in_specs = [tok(D_MODEL)]
    args = [x]
    if has_emb:
        in_specs.append(pl.BlockSpec((tt, D_MODEL), lambda b, i: (i, 0)))
        args.append(emb)
    mod_map = (lambda b, i: (b, 0, 0)) if per_batch_mod else (lambda b, i: (0, 0, 0))
    in_specs += [pl.BlockSpec((1, 3, D_MODEL), mod_map), tok(D_RWKV), tok(D_RWKV), tok(2 * D_RWKV),
                 tok(D_RWKV), tok(LANES), tok(D_RWKV), tok(D_FNET),
                 full(N_DIR, D_RWKV), full(N_DIR, LANES, D_RWKV),
                 full(1, D_RWKV), full(1, D_RWKV), full(1, D_RWKV), full(1, D_RWKV),
                 full(D_RWKV, D_RWKV), full(D_RWKV, D_RWKV), full(D_MODEL, D_MODEL), full(1, D_MODEL)]
    args += [mod, y_f, y_b, rk, v, lwla, grec, fo, wts["a0"], wts["a2"], wts["k_a"],
             wts["r_k"], wts["gn_g"], wts["gn_b"], wts["avg_bd"], wts["ones_bd"], w_out_bf16, final_norm_g]
    return pl.pallas_call(
        functools.partial(_out_kernel, has_emb=has_emb, final_norm=final_norm),
        grid=(B // nb, T // tt), in_specs=in_specs,
        out_specs=tok(D_MODEL),
        out_shape=jax.ShapeDtypeStruct((B, T, D_MODEL), F32),
        compiler_params=pltpu.CompilerParams(dimension_semantics=("arbitrary", "arbitrary"),
                                             vmem_limit_bytes=VMEM_LIMIT),
        name="out_proj",
    )(*args)


def _dft_table(seq_len):
    idx = np.arange(seq_len, dtype=np.int64)
    ang = 2.0 * np.pi * ((idx[:, None] * idx[None, :]) % seq_len).astype(np.float64) / seq_len
    scale = 1.0 / math.sqrt(seq_len)
    return np.concatenate([np.cos(ang) * scale, -np.sin(ang) * scale], axis=1).astype(np.float32)


def _channel_dft_table():
    n = FNET_GROUP
    idx = np.arange(n, dtype=np.int64)
    ang = 2.0 * np.pi * ((idx[:, None] * idx[None, :]) % n).astype(np.float64) / n
    c = np.cos(ang) / math.sqrt(n)
    s = np.sin(ang) / math.sqrt(n)
    eye = np.eye(D_FNET // n)
    return np.concatenate([np.kron(eye, c), np.kron(eye, s)], axis=1).astype(np.float32)


def _sincos_2d(n_tokens):
    rows = n_tokens // GRID_W
    pos = np.arange(rows * GRID_W)
    row = (pos // GRID_W).astype(np.float32)
    col = (pos % GRID_W).astype(np.float32)
    quarter = D_MODEL // 4
    freq = np.exp(np.float32(-math.log(POS_BASE)) * np.arange(quarter, dtype=np.float32) / np.float32(quarter))
    ang_r = row[:, None] * freq
    ang_c = col[:, None] * freq
    return np.concatenate([np.sin(ang_r), np.cos(ang_r), np.sin(ang_c), np.cos(ang_c)], axis=-1).astype(np.float32)


def _head_block_matrix(value):
    blk = np.kron(np.eye(N_HEADS), np.ones((HEAD_DIM, HEAD_DIM))) * value
    return jnp.asarray(blk, dtype=BF16)


def _pad_lora(w, row_offset):
    out = jnp.zeros((N_DIR, LANES, w.shape[-1]), F32)
    for d in range(N_DIR):
        out = out.at[d, row_offset + d * LORA:row_offset + (d + 1) * LORA, :].set(w[d])
    return out.astype(BF16)


def _layer_weights(l, w0, w2, a0, a2, k_k, k_a, r_k, gn_g, gn_b):
    w2_p = _pad_lora(w2[l], 0)
    a2_p = _pad_lora(a2[l], N_DIR * LORA)
    tri_f = np.tril(np.ones((CHUNK, CHUNK)))
    tri = jnp.asarray(np.stack([tri_f, tri_f.T]), dtype=BF16)
    return dict(w0=w0[l], w2=w2_p, a0=a0[l], a2=a2_p,
                k_k=k_k[l][None], k_a=k_a[l][None], r_k=r_k[l].reshape(1, D_RWKV),
                gn_g=gn_g[l].reshape(1, D_RWKV), gn_b=gn_b[l].reshape(1, D_RWKV), tri=tri,
                ones_bd=_head_block_matrix(1.0), avg_bd=_head_block_matrix(1.0 / HEAD_DIM))


def _state_to_block_diag(s_f, s_b):
    def one(s):
        h = jnp.swapaxes(s.astype(F32), -1, -2)
        b = h.shape[0]
        h = h.reshape(b, N_PAIRS, PAIR, HEAD_DIM, HEAD_DIM)
        z = jnp.zeros_like(h[:, :, 0])
        top = jnp.concatenate([h[:, :, 0], z], axis=-1)
        bot = jnp.concatenate([z, h[:, :, 1]], axis=-1)
        return jnp.concatenate([top, bot], axis=-2)
    return jnp.stack([one(s_f), one(s_b)], axis=1)


def kernel(x_prompt, x_sample, state_rwkv_fwd, state_rwkv_bwd, c, c_ctx, w_ada, b_ada, norm_g, w_in,
           mu_shift, w0, w2, a0, a2, k_k, k_a, r_k, gn_g, gn_b, w_fnet, b_fnet, w_out, final_norm_g):
    depth = w_in.shape[0]
    n_dec = c.shape[0]
    assert n_dec + 1 <= 8
    bp, tp, _ = x_prompt.shape
    bs, ts, _ = x_sample.shape
    cvec = jnp.zeros((8, D_MODEL), F32).at[0].set(c_ctx).at[1:1 + n_dec].set(c)
    emb = jnp.asarray(_sincos_2d(ts)).astype(x_sample.dtype)
    cs_tab = jnp.asarray(_channel_dft_table()).astype(BF16)
    dft_p = jnp.asarray(_dft_table(tp)).astype(BF16)
    dft_s = jnp.asarray(_dft_table(ts)).astype(BF16)
    fng = final_norm_g[None]

    xp, xs = x_prompt, x_sample
    new_f, new_b = [], []
    for l in range(depth):
        mod = _mod_call(cvec, w_ada[l], b_ada[l][None]).reshape(8, 3, D_MODEL)
        mod_ctx, mod_lat = mod[0:1], mod[1:1 + n_dec]
        wts = _layer_weights(l, w0, w2, a0, a2, k_k, k_a, r_k, gn_g, gn_b)
        w_in_b = w_in[l].astype(BF16)
        w_out_b = w_out[l].astype(BF16)
        wbd = jax.scipy.linalg.block_diag(*[w_fnet[l, g] for g in range(w_fnet.shape[1])]).astype(BF16)
        ng, mu, bf = norm_g[l][None], mu_shift[l][None], b_fnet[l][None]
        emb_l = emb if l == 0 else None

        rk, v, lwla, grec, fin, gf = _in_proj_call(xp, None, mod_ctx, ng, w_in_b, mu)
        y_f, y_b, s_f, s_b = _scan_call(rk, v, lwla, wts, None, True)
        fo = _fnet_call(fin, gf, dft_p, cs_tab, wbd, bf)
        last = l == depth - 1
        xp = _out_call(xp, None, mod_ctx, y_f, y_b, rk, v, lwla, grec, fo, wts, w_out_b, fng, last)
        new_f.append(s_f)
        new_b.append(s_b)

        rk, v, lwla, grec, fin, gf = _in_proj_call(xs, emb_l, mod_lat, ng, w_in_b, mu)
        s0 = _state_to_block_diag(state_rwkv_fwd[:, l], state_rwkv_bwd[:, l])
        y_f, y_b = _scan_call(rk, v, lwla, wts, s0, False)
        fo = _fnet_call(fin, gf, dft_s, cs_tab, wbd, bf)
        xs = _out_call(xs, emb_l, mod_lat, y_f, y_b, rk, v, lwla, grec, fo, wts, w_out_b, fng, last)
    return (xp, xs, jnp.concatenate(new_f, axis=1), jnp.concatenate(new_b, axis=1))
```

```python
import functools
import math

import numpy as np
import jax
import jax.numpy as jnp
from jax import lax
from jax.experimental import pallas as pl
from jax.experimental.pallas import tpu as pltpu

F32 = jnp.float32
BF16 = jnp.bfloat16

D_MODEL = 1024
GRID_W = 64
D_RWKV = 512
D_FNET = D_MODEL - D_RWKV
HEAD_DIM = 64
N_HEADS = D_RWKV // HEAD_DIM
FNET_GROUP = 64
LORA = 32
N_DIR = 2
D_SHIFT = 3 * D_RWKV + N_DIR * 2 * LORA
D_IN = D_SHIFT + D_RWKV + 2 * D_FNET
NORM_EPS = 1e-6
GN_EPS = 64e-5
POS_BASE = 10000.0

LANES = 128
PAIR = LANES // HEAD_DIM
N_PAIRS = N_HEADS // PAIR
CHUNK = 64
SUB = 16
SUB_SHIFT = SUB.bit_length() - 1
SCAN_SEQS = 4
ROW_TILE = 512
FNET_ROWS = 1024
HALO = 8
VMEM_LIMIT = 56 * 1024 * 1024


def _silu(x):
    return x * jax.nn.sigmoid(x)


def _bdot(a, b):
    return jnp.dot(a.astype(BF16), b.astype(BF16), preferred_element_type=F32)


def _split2(x):
    hi = x.astype(BF16)
    lo = (x - hi.astype(F32)).astype(BF16)
    return hi, lo


def _mod_kernel(c_ref, w_ref, b_ref, o_ref):
    s_hi, s_lo = _split2(_silu(c_ref[...]))
    w = w_ref[...].astype(BF16)
    o_ref[...] = (jnp.dot(s_hi, w, preferred_element_type=F32)
                  + jnp.dot(s_lo, w, preferred_element_type=F32) + b_ref[...])


def _mod_call(cvec, w_ada, b_ada):
    n_blk = 3
    return pl.pallas_call(
        _mod_kernel,
        grid=(n_blk,),
        in_specs=[pl.BlockSpec((8, D_MODEL), lambda i: (0, 0)),
                  pl.BlockSpec((D_MODEL, D_MODEL), lambda i: (0, i)),
                  pl.BlockSpec((1, D_MODEL), lambda i: (0, i))],
        out_specs=pl.BlockSpec((8, D_MODEL), lambda i: (0, i)),
        out_shape=jax.ShapeDtypeStruct((8, 3 * D_MODEL), F32),
        compiler_params=pltpu.CompilerParams(dimension_semantics=("arbitrary",),
                                             vmem_limit_bytes=VMEM_LIMIT),
        name="mod",
    )(cvec, w_ada, b_ada)


def _modulated_norm(x, g, scale, shift):
    ms = jnp.mean(x * x, axis=-1, keepdims=True)
    y = x * lax.rsqrt(ms + NORM_EPS) * g
    return y * (1.0 + scale) + shift


def _in_proj_kernel(*refs, has_emb, has_halo, n_tiles):
    it = iter(refs)
    x_ref = next(it)
    xp_ref = next(it) if has_halo else None
    xn_ref = next(it) if has_halo else None
    emb_ref = next(it) if has_emb else None
    embp_ref = next(it) if (has_emb and has_halo) else None
    embn_ref = next(it) if (has_emb and has_halo) else None
    mod_ref, g_ref, w_ref, mu_ref = next(it), next(it), next(it), next(it)
    rk_ref, v_ref, lwla_ref, grec_ref, fin_ref, gf_ref = (next(it) for _ in range(6))

    i = pl.program_id(1)
    g = g_ref[...]
    shift = mod_ref[0, 0:1, :]
    scale = mod_ref[0, 1:2, :]
    nb, tt, _ = x_ref.shape
    tm = nb * tt
    x = x_ref[...].reshape(tm, D_MODEL)
    if has_emb:
        x = x + emb_ref[...]
    h = _modulated_norm(x, g, scale, shift)
    p = jnp.dot(h.astype(BF16), w_ref[...], preferred_element_type=F32)
    ps = p[:, :D_SHIFT]
    if has_halo:
        xh = jnp.concatenate([xp_ref[0], xn_ref[0]], axis=0)
        if has_emb:
            xh = xh + jnp.concatenate([embp_ref[...], embn_ref[...]], axis=0)
        hh = _modulated_norm(xh, g, scale, shift)
        ph = jnp.dot(hh.astype(BF16), w_ref[:, :D_SHIFT], preferred_element_type=F32)
        prev_row = jnp.where(i > 0, ph[HALO - 1:HALO, :], 0.0)
        next_row = jnp.where(i < n_tiles - 1, ph[HALO:HALO + 1, :], 0.0)
    else:
        prev_row = jnp.zeros((1, D_SHIFT), F32)
        next_row = jnp.zeros((1, D_SHIFT), F32)
    pos = lax.rem(lax.broadcasted_iota(jnp.int32, (tm, D_SHIFT), 0), tt)
    prev = jnp.where(pos == 0, prev_row, pltpu.roll(ps, 1, 0))
    nxt = jnp.where(pos == tt - 1, next_row, pltpu.roll(ps, tm - 1, 0))
    p_rec = ps + mu_ref[...] * (0.5 * (prev + nxt) - ps)

    def put(ref, val):
        ref[...] = val.astype(ref.dtype).reshape(ref.shape)

    put(rk_ref, p_rec[:, :2 * D_RWKV])
    put(v_ref, p_rec[:, 2 * D_RWKV:3 * D_RWKV])
    put(lwla_ref, p_rec[:, 3 * D_RWKV:])
    put(grec_ref, p[:, D_SHIFT:D_SHIFT + D_RWKV])
    put(fin_ref, p[:, D_SHIFT + D_RWKV:D_SHIFT + D_RWKV + D_FNET])
    put(gf_ref, p[:, D_SHIFT + D_RWKV + D_FNET:])


def _in_proj_call(x, emb, mod, norm_g, w_in_bf16, mu):
    B, T, _ = x.shape
    tt = min(T, ROW_TILE)
    nb = max(1, ROW_TILE // T)
    n_tiles = T // tt
    has_halo = n_tiles > 1
    has_emb = emb is not None
    per_batch_mod = mod.shape[0] > 1
    assert B % nb == 0 and T % tt == 0 and (nb == 1 or not (has_emb or per_batch_mod))
    tm = tt
    blocks_per_tile = tm // HALO
    last_halo_block = T // HALO - 1

    in_specs = [pl.BlockSpec((nb, tt, D_MODEL), lambda b, i: (b, i, 0))]
    args = [x]
    if has_halo:
        in_specs += [
            pl.BlockSpec((1, HALO, D_MODEL), lambda b, i: (b, jnp.maximum(i * blocks_per_tile - 1, 0), 0)),
            pl.BlockSpec((1, HALO, D_MODEL),
                         lambda b, i: (b, jnp.minimum((i + 1) * blocks_per_tile, last_halo_block), 0))]
        args += [x, x]
    if has_emb:
        in_specs.append(pl.BlockSpec((tm, D_MODEL), lambda b, i: (i, 0)))
        args.append(emb)
        if has_halo:
            in_specs += [
                pl.BlockSpec((HALO, D_MODEL), lambda b, i: (jnp.maximum(i * blocks_per_tile - 1, 0), 0)),
                pl.BlockSpec((HALO, D_MODEL),
                             lambda b, i: (jnp.minimum((i + 1) * blocks_per_tile, last_halo_block), 0))]
            args += [emb, emb]
    mod_map = (lambda b, i: (b, 0, 0)) if per_batch_mod else (lambda b, i: (0, 0, 0))
    in_specs += [pl.BlockSpec((1, 3, D_MODEL), mod_map),
                 pl.BlockSpec((1, D_MODEL), lambda b, i: (0, 0)),
                 pl.BlockSpec((D_MODEL, D_IN), lambda b, i: (0, 0), pipeline_mode=pl.Buffered(1)),
                 pl.BlockSpec((1, D_SHIFT), lambda b, i: (0, 0))]
    args += [mod, norm_g, w_in_bf16, mu]
    outs = ((2 * D_RWKV, BF16), (D_RWKV, BF16), (N_DIR * 2 * LORA, F32), (D_RWKV, BF16), (D_FNET, BF16),
            (D_FNET, BF16))
    out_specs = [pl.BlockSpec((nb, tt, w), lambda b, i: (b, i, 0)) for w, _ in outs]
    out_shape = [jax.ShapeDtypeStruct((B, T, w), dt) for w, dt in outs]
    kern = functools.partial(_in_proj_kernel, has_emb=has_emb, has_halo=has_halo, n_tiles=n_tiles)
    return pl.pallas_call(
        kern, grid=(B // nb, n_tiles), in_specs=in_specs, out_specs=out_specs, out_shape=out_shape,
        compiler_params=pltpu.CompilerParams(dimension_semantics=("arbitrary", "arbitrary"),
                                             vmem_limit_bytes=VMEM_LIMIT),
        name="in_proj",
    )(*args)


def _same_block(i, j, size):
    shift = size.bit_length() - 1
    return jnp.right_shift(i, shift) == jnp.right_shift(j, shift)


def _block_diag_rows(y, head0_lanes):
    return jnp.concatenate([jnp.where(head0_lanes, y, 0.0), jnp.where(head0_lanes, 0.0, y)], axis=0)


def _scan_kernel(*refs, zero_init, write_state, n_chunks, n_seq):
    it = iter(refs)
    rk_refs, v_refs, lwla_refs = [None, None], [None, None], [None, None]
    for d in range(N_DIR):
        rk_refs[d], v_refs[d], lwla_refs[d] = next(it), next(it), next(it)
    w0_ref, w2_ref, a0_ref, a2_ref = (next(it) for _ in range(4))
    kk_ref, ka_ref, tri_ref, ones_ref = (next(it) for _ in range(4))
    s0_ref = None if zero_init else next(it)
    y_refs = [next(it), next(it)]
    sfin_refs = [next(it), next(it)] if write_state else None
    h_scr = next(it)

    j = pl.program_id(1)

    @pl.when(j == 0)
    def _():
        if zero_init:
            h_scr[...] = jnp.zeros_like(h_scr)
        else:
            h_scr[...] = s0_ref[...]

    C = CHUNK
    row = lax.broadcasted_iota(jnp.int32, (C, LANES), 0)
    lane = lax.broadcasted_iota(jnp.int32, (C, LANES), 1)
    s_idx = jnp.bitwise_and(lane, HEAD_DIM - 1)
    head0 = lane < HEAD_DIM
    row_c = lax.broadcasted_iota(jnp.int32, (SUB, LANES), 0)
    lane_c = lax.broadcasted_iota(jnp.int32, (SUB, LANES), 1)
    col_c = jnp.bitwise_and(lane_c, SUB - 1)
    blk_c = jnp.right_shift(jnp.bitwise_and(lane_c, HEAD_DIM - 1), SUB_SHIFT)
    lane_blk_c = jnp.right_shift(lane_c, SUB_SHIFT)
    eye_c = (row_c == col_c).astype(F32)

    def bd_c(y):
        return jnp.concatenate([jnp.where(lane_blk_c == g, y, 0.0) for g in range(LANES // SUB)], axis=0)

    def inverse_size4(l_c, upper):
        r4 = jnp.bitwise_and(row_c, 3)
        c4 = jnp.bitwise_and(col_c, 3)
        down, up = (lambda x, k: pltpu.roll(x, k, 0)), (lambda x, k: pltpu.roll(x, SUB - k, 0))
        right, left = (lambda x, k: pltpu.roll(x, k, 1)), (lambda x, k: pltpu.roll(x, LANES - k, 1))
        s2 = jnp.where(_same_block(row_c, col_c, 2), l_c, 0.0)
        e4 = jnp.where(_same_block(row_c, col_c, 4), l_c, 0.0) - s2
        if upper:
            s_col = jnp.where(r4 == 1, up(s2, 1), up(s2, 2))
            s_row = jnp.where(c4 == 2, right(s2, 1), right(s2, 2))
            e_s = jnp.where(c4 == 3, right(e4, 1) * s_col, 0.0)
            s_e = jnp.where(r4 == 0, s_row * up(e4, 1), 0.0)
            s_e_s = jnp.where((r4 == 0) & (c4 == 3), s_row * up(e_s, 1), 0.0)
        else:
            s_col = jnp.where(r4 == 2, down(s2, 1), down(s2, 2))
            s_row = jnp.where(c4 == 1, left(s2, 1), left(s2, 2))
            e_s = jnp.where(c4 == 0, left(e4, 1) * s_col, 0.0)
            s_e = jnp.where(r4 == 3, s_row * down(e4, 1), 0.0)
            s_e_s = jnp.where((r4 == 3) & (c4 == 0), s_row * down(e_s, 1), 0.0)
        return eye_c - s2 - e4 + e_s + s_e - s_e_s

    row2 = lax.broadcasted_iota(jnp.int32, (LANES, LANES), 0)
    lane2 = lax.broadcasted_iota(jnp.int32, (LANES, LANES), 1)
    same_head = (row2 < HEAD_DIM) == (lane2 < HEAD_DIM)
    decay_scale = math.exp(-0.5)

    probs = []
    for n, d in [(n, d) for n in range(n_seq) for d in range(N_DIR)]:
        rk = rk_refs[d][n].astype(F32)
        r = rk[:, :D_RWKV]
        k = rk[:, D_RWKV:]
        v = v_refs[d][n]
        ll = lwla_refs[d][n]
        z_w = w0_ref[d:d + 1, :] + _bdot(jnp.tanh(ll), w2_ref[d])
        logw = -decay_scale * jax.nn.sigmoid(z_w)
        a = jax.nn.sigmoid(a0_ref[d:d + 1, :] + _bdot(ll, a2_ref[d]))
        kd = k * (1.0 + (a - 1.0) * ka_ref[...])
        kkr = k * kk_ref[...]
        ssq = _bdot(kkr * kkr, ones_ref[...])
        kk = kkr * lax.rsqrt(jnp.maximum(ssq, 1e-24))
        bvec = kk * a
        lw_hi, lw_lo = _split2(logw)
        tri = tri_ref[d]
        cum = (jnp.dot(tri, lw_hi, preferred_element_type=F32)
               + jnp.dot(tri, lw_lo, preferred_element_type=F32))
        cum_prev = cum - logw
        tot = cum[C - 1:C, :] if d == 0 else cum[0:1, :]
        kap_t = kk * jnp.exp(cum_prev)
        r_t = r * jnp.exp(cum)
        e_neg = jnp.exp(-cum)
        gam = jnp.exp(tot)
        b_t = bvec * e_neg
        k_t = kd * e_neg
        e_rem = gam * e_neg
        b_h = bvec * e_rem
        k_h = kd * e_rem
        if d == 0:
            strict, incl = row > s_idx, row >= s_idx
        else:
            strict, incl = row < s_idx, row <= s_idx

        for p in range(N_PAIRS):
            sl = slice(p * LANES, (p + 1) * LANES)
            dup_t = lambda x: jnp.where(same_head, jnp.concatenate([x[:, sl], x[:, sl]], axis=0).T, 0.0)
            probs.append(dict(n=n, d=d, p=p, sl=sl, strict=strict, incl=incl,
                              lhs=jnp.concatenate([kap_t[:, sl], r_t[:, sl]], axis=0).astype(BF16),
                              v=v[:, sl], w_lm=jnp.concatenate([dup_t(b_t), dup_t(k_t)], axis=1).astype(BF16),
                              t2=jnp.concatenate([b_h[:, sl], k_h[:, sl]], axis=0).T.astype(BF16),
                              gam_col=jnp.broadcast_to(gam[:, sl], (LANES, LANES)).T))

    bd = lambda y: _block_diag_rows(y, head0)
    for q in probs:
        q["h"] = h_scr[q["n"], q["d"], q["p"]]
        lm = jnp.dot(q["lhs"], q["w_lm"], preferred_element_type=F32)
        q["l_b"] = jnp.where(q["strict"], lm[:C, :LANES], 0.0)
        q["m_b"] = jnp.where(q["incl"], lm[C:, :LANES], 0.0)
        q["lm_k"] = jnp.concatenate([jnp.where(q["strict"], lm[:C, LANES:], 0.0),
                                     jnp.where(q["incl"], lm[C:, LANES:], 0.0)], axis=0)
    for q in probs:
        lmv = _bdot(q["lm_k"], bd(q["v"]))
        q["l_kv"], q["m_kv"] = lmv[:C], lmv[C:]
    for q in probs:
        q["kr_h"] = _bdot(q["lhs"], q["h"])
    for q in probs:
        l_b = q["l_b"]
        l_c = l_b[0:SUB]
        for jb in range(1, C // SUB):
            l_c = jnp.where(blk_c == jb, l_b[jb * SUB:(jb + 1) * SUB], l_c)
        q["l_c"] = l_c
        q["t_c"] = inverse_size4(l_c, q["d"] == 1)
    s = 4
    while s < SUB:
        off_mask = _same_block(row_c, col_c, 2 * s) & ~_same_block(row_c, col_c, s)
        for q in probs:
            q["et"] = _bdot(jnp.where(off_mask, q["l_c"], 0.0), bd_c(q["t_c"]))
        for q in probs:
            q["t_c"] = q["t_c"] - _bdot(q["t_c"], bd_c(q["et"]))
        s *= 2
    for q in probs:
        q["tinv"] = jnp.concatenate([jnp.where(blk_c == jb, q["t_c"], 0.0) for jb in range(C // SUB)], axis=0)
    while s < C:
        off_mask = _same_block(row, s_idx, 2 * s) & ~_same_block(row, s_idx, s)
        for q in probs:
            q["et"] = _bdot(jnp.where(off_mask, q["l_b"], 0.0), bd(q["tinv"]))
        for q in probs:
            q["tinv"] = q["tinv"] - _bdot(q["tinv"], bd(q["et"]))
        s *= 2
    for q in probs:
        q["u_n"] = _bdot(q["tinv"], bd(q["kr_h"][:C] + q["l_kv"]))
    for q in probs:
        y = q["kr_h"][C:] + q["m_kv"] - _bdot(q["m_b"], bd(q["u_n"]))
        y_refs[q["d"]][q["n"], :, q["sl"]] = y.astype(BF16)
    for q in probs:
        upd = _bdot(q["t2"], jnp.concatenate([(-q["u_n"]).astype(BF16), q["v"]], axis=0))
        h_scr[q["n"], q["d"], q["p"]] = q["gam_col"] * q["h"] + jnp.where(same_head, upd, 0.0)

    if write_state:
        @pl.when(j == n_chunks - 1)
        def _():
            for n, d, p in [(n, d, p) for n in range(n_seq) for d in range(N_DIR) for p in range(N_PAIRS)]:
                ht = h_scr[n, d, p].T
                sfin_refs[d][n, 0, PAIR * p] = ht[:HEAD_DIM, :HEAD_DIM]
                sfin_refs[d][n, 0, PAIR * p + 1] = ht[HEAD_DIM:, HEAD_DIM:]


def _scan_call(rk, v, lwla, wts, s0_bd, write_state):
    B, T, _ = rk.shape
    nc = T // CHUNK
    ns = min(SCAN_SEQS, B)
    assert B % ns == 0
    zero_init = s0_bd is None
    fwd = lambda b, j: (b, j, 0)
    bwd = lambda b, j: (b, nc - 1 - j, 0)
    full = lambda *shape: pl.BlockSpec(shape, lambda b, j: (0,) * len(shape))
    tok = lambda w, m: pl.BlockSpec((ns, CHUNK, w), m)
    in_specs = [tok(2 * D_RWKV, fwd), tok(D_RWKV, fwd), tok(LANES, fwd),
                tok(2 * D_RWKV, bwd), tok(D_RWKV, bwd), tok(LANES, bwd),
                full(N_DIR, D_RWKV), full(N_DIR, LANES, D_RWKV),
                full(N_DIR, D_RWKV), full(N_DIR, LANES, D_RWKV),
                full(1, D_RWKV), full(1, D_RWKV), full(N_DIR, CHUNK, CHUNK), full(D_RWKV, D_RWKV)]
    args = [rk, v, lwla, rk, v, lwla, wts["w0"], wts["w2"], wts["a0"], wts["a2"],
            wts["k_k"], wts["k_a"], wts["tri"], wts["ones_bd"]]
    state_block = (ns, N_DIR, N_PAIRS, LANES, LANES)
    if not zero_init:
        in_specs.append(pl.BlockSpec(state_block, lambda b, j: (b, 0, 0, 0, 0)))
        args.append(s0_bd)
    out_specs = [pl.BlockSpec((ns, CHUNK, D_RWKV), fwd), pl.BlockSpec((ns, CHUNK, D_RWKV), bwd)]
    out_shape = [jax.ShapeDtypeStruct((B, T, D_RWKV), BF16), jax.ShapeDtypeStruct((B, T, D_RWKV), BF16)]
    if write_state:
        final_block = (ns, 1, N_HEADS, HEAD_DIM, HEAD_DIM)
        out_specs += [pl.BlockSpec(final_block, lambda b, j: (b, 0, 0, 0, 0))] * N_DIR
        out_shape += [jax.ShapeDtypeStruct((B,) + final_block[1:], F32)] * N_DIR
    kern = functools.partial(_scan_kernel, zero_init=zero_init, write_state=write_state, n_chunks=nc,
                             n_seq=ns)
    return pl.pallas_call(
        kern, grid=(B // ns, nc), in_specs=in_specs, out_specs=out_specs, out_shape=out_shape,
        scratch_shapes=[pltpu.VMEM(state_block, F32)],
        compiler_params=pltpu.CompilerParams(dimension_semantics=("arbitrary", "arbitrary"),
                                             vmem_limit_bytes=VMEM_LIMIT),
        name="scan",
    )(*args)


def _fnet_kernel(fin_ref, gf_ref, dft_ref, cs_ref, wbd_ref, b_ref, o_ref, g_scr):
    u = pl.program_id(1)
    nb, seq_len, _ = fin_ref.shape
    tu = o_ref.shape[1]

    @pl.when(u == 0)
    def _():
        step = min(seq_len, ROW_TILE)
        for n in range(nb):
            for t0 in range(0, seq_len, step):
                fc = _bdot(fin_ref[n, t0:t0 + step, :], cs_ref[...])
                g_scr[n, t0:t0 + step, :] = fc[:, :D_FNET].astype(BF16)
                g_scr[n, seq_len + t0:seq_len + t0 + step, :] = fc[:, D_FNET:].astype(BF16)

    f_re = jnp.concatenate([jnp.dot(dft_ref[...], g_scr[n], preferred_element_type=F32) for n in range(nb)],
                           axis=0)
    f_out = _bdot(f_re, wbd_ref[...]) + b_ref[...]
    gate = _silu(gf_ref[...].astype(F32)).reshape(nb * tu, D_FNET)
    o_ref[...] = (f_out * gate).astype(BF16).reshape(nb, tu, D_FNET)


def _fnet_call(fin, gf, dft_bf16, cs_bf16, wbd_bf16, b_fnet):
    B, T, _ = fin.shape
    tu = min(T, ROW_TILE)
    nb = max(1, FNET_ROWS // T)
    assert B % nb == 0 and T % tu == 0
    return pl.pallas_call(
        _fnet_kernel,
        grid=(B // nb, T // tu),
        in_specs=[pl.BlockSpec((nb, T, D_FNET), lambda b, u: (b, 0, 0)),
                  pl.BlockSpec((nb, tu, D_FNET), lambda b, u: (b, u, 0)),
                  pl.BlockSpec((tu, 2 * T), lambda b, u: (u, 0)),
                  pl.BlockSpec((D_FNET, 2 * D_FNET), lambda b, u: (0, 0)),
                  pl.BlockSpec((D_FNET, D_FNET), lambda b, u: (0, 0)),
                  pl.BlockSpec((1, D_FNET), lambda b, u: (0, 0))],
        out_specs=pl.BlockSpec((nb, tu, D_FNET), lambda b, u: (b, u, 0)),
        out_shape=jax.ShapeDtypeStruct((B, T, D_FNET), BF16),
        scratch_shapes=[pltpu.VMEM((nb, 2 * T, D_FNET), BF16)],
        compiler_params=pltpu.CompilerParams(dimension_semantics=("arbitrary", "arbitrary"),
                                             vmem_limit_bytes=VMEM_LIMIT),
        name="fnet",
    )(fin, gf, dft_bf16, cs_bf16, wbd_bf16, b_fnet)


def _out_kernel(*refs, has_emb, final_norm):
    it = iter(refs)
    x_ref = next(it)
    emb_ref = next(it) if has_emb else None
    (mod_ref, yf_ref, yb_ref, rk_ref, v_ref, lwla_ref, grec_ref, fo_ref, a0_ref, a2_ref, ka_ref,
     rkw_ref, gng_ref, gnb_ref, avg_ref, ones_ref, wout_ref, fng_ref, o_ref) = (next(it) for _ in range(19))

    nb, tt, _ = x_ref.shape
    tm = nb * tt
    rows = lambda ref: ref[...].reshape(tm, ref.shape[-1])
    y = rows(yf_ref).astype(F32) + rows(yb_ref).astype(F32)
    mu = _bdot(y, avg_ref[...])
    dlt = y - mu
    var = _bdot(dlt * dlt, avg_ref[...])
    y_n = dlt * lax.rsqrt(var + GN_EPS) * gng_ref[...] + gnb_ref[...]
    rk = rows(rk_ref).astype(F32)
    r = rk[:, :D_RWKV]
    k = rk[:, D_RWKV:]
    ll = rows(lwla_ref)
    a_sum = (jax.nn.sigmoid(a0_ref[0:1, :] + _bdot(ll, a2_ref[0]))
             + jax.nn.sigmoid(a0_ref[1:2, :] + _bdot(ll, a2_ref[1])))
    k_sum = k * (2.0 + (a_sum - 2.0) * ka_ref[...])
    bonus = _bdot(r * k_sum * rkw_ref[...], ones_ref[...]) * rows(v_ref).astype(F32)
    rec_out = (y_n + bonus) * _silu(rows(grec_ref).astype(F32))
    mixed = jnp.concatenate([rec_out.astype(BF16), rows(fo_ref)], axis=-1)
    out = jnp.dot(mixed, wout_ref[...], preferred_element_type=F32)
    x = rows(x_ref)
    if has_emb:
        x = x + emb_ref[...]
    z = x + mod_ref[0, 2:3, :] * out
    if final_norm:
        ms = jnp.mean(z * z, axis=-1, keepdims=True)
        z = z * lax.rsqrt(ms + NORM_EPS) * fng_ref[...]
    o_ref[...] = z.reshape(nb, tt, D_MODEL)


def _out_call(x, emb, mod, y_f, y_b, rk, v, lwla, grec, fo, wts, w_out_bf16, final_norm_g, final_norm):
    B, T, _ = x.shape
    tt = min(T, ROW_TILE)
    nb = max(1, ROW_TILE // T)
    has_emb = emb is not None
    per_batch_mod = mod.shape[0] > 1
    assert B % nb == 0 and T % tt == 0 and (nb == 1 or not (has_emb or per_batch_mod))
    tok = lambda w: pl.BlockSpec((nb, tt, w), lambda b, i: (b, i, 0))
    full = lambda *shape: pl.BlockSpec(shape, lambda b, i: (0,) * len(shape))
    in_specs = [tok(D_MODEL)]
    args = [x]
    if has_emb:
        in_specs.append(pl.BlockSpec((tt, D_MODEL), lambda b, i: (i, 0)))
        args.append(emb)
    mod_map = (lambda b, i: (b, 0, 0)) if per_batch_mod else (lambda b, i: (0, 0, 0))
    in_specs += [pl.BlockSpec((1, 3, D_MODEL), mod_map), tok(D_RWKV), tok(D_RWKV), tok(2 * D_RWKV),
                 tok(D_RWKV), tok(LANES), tok(D_RWKV), tok(D_FNET),
                 full(N_DIR, D_RWKV), full(N_DIR, LANES, D_RWKV),
                 full(1, D_RWKV), full(1, D_RWKV), full(1, D_RWKV), full(1, D_RWKV),
                 full(D_RWKV, D_RWKV), full(D_RWKV, D_RWKV), full(D_MODEL, D_MODEL), full(1, D_MODEL)]
    args += [mod, y_f, y_b, rk, v, lwla, grec, fo, wts["a0"], wts["a2"], wts["k_a"],
             wts["r_k"], wts["gn_g"], wts["gn_b"], wts["avg_bd"], wts["ones_bd"], w_out_bf16, final_norm_g]
    return pl.pallas_call(
        functools.partial(_out_kernel, has_emb=has_emb, final_norm=final_norm),
        grid=(B // nb, T // tt), in_specs=in_specs,
        out_specs=tok(D_MODEL),
        out_shape=jax.ShapeDtypeStruct((B, T, D_MODEL), F32),
        compiler_params=pltpu.CompilerParams(dimension_semantics=("arbitrary", "arbitrary"),
                                             vmem_limit_bytes=VMEM_LIMIT),
        name="out_proj",
    )(*args)


def _dft_table(seq_len):
    idx = np.arange(seq_len, dtype=np.int64)
    ang = 2.0 * np.pi * ((idx[:, None] * idx[None, :]) % seq_len).astype(np.float64) / seq_len
    scale = 1.0 / math.sqrt(seq_len)
    return np.concatenate([np.cos(ang) * scale, -np.sin(ang) * scale], axis=1).astype(np.float32)


def _channel_dft_table():
    n = FNET_GROUP
    idx = np.arange(n, dtype=np.int64)
    ang = 2.0 * np.pi * ((idx[:, None] * idx[None, :]) % n).astype(np.float64) / n
    c = np.cos(ang) / math.sqrt(n)
    s = np.sin(ang) / math.sqrt(n)
    eye = np.eye(D_FNET // n)
    return np.concatenate([np.kron(eye, c), np.kron(eye, s)], axis=1).astype(np.float32)


def _sincos_2d(n_tokens):
    rows = n_tokens // GRID_W
    pos = np.arange(rows * GRID_W)
    row = (pos // GRID_W).astype(np.float32)
    col = (pos % GRID_W).astype(np.float32)
    quarter = D_MODEL // 4
    freq = np.exp(np.float32(-math.log(POS_BASE)) * np.arange(quarter, dtype=np.float32) / np.float32(quarter))
    ang_r = row[:, None] * freq
    ang_c = col[:, None] * freq
    return np.concatenate([np.sin(ang_r), np.cos(ang_r), np.sin(ang_c), np.cos(ang_c)], axis=-1).astype(np.float32)


def _head_block_matrix(value):
    blk = np.kron(np.eye(N_HEADS), np.ones((HEAD_DIM, HEAD_DIM))) * value
    return jnp.asarray(blk, dtype=BF16)


def _pad_lora(w, row_offset):
    out = jnp.zeros((N_DIR, LANES, w.shape[-1]), F32)
    for d in range(N_DIR):
        out = out.at[d, row_offset + d * LORA:row_offset + (d + 1) * LORA, :].set(w[d])
    return out.astype(BF16)


def _layer_weights(l, w0, w2, a0, a2, k_k, k_a, r_k, gn_g, gn_b):
    w2_p = _pad_lora(w2[l], 0)
    a2_p = _pad_lora(a2[l], N_DIR * LORA)
    tri_f = np.tril(np.ones((CHUNK, CHUNK)))
    tri = jnp.asarray(np.stack([tri_f, tri_f.T]), dtype=BF16)
    return dict(w0=w0[l], w2=w2_p, a0=a0[l], a2=a2_p,
                k_k=k_k[l][None], k_a=k_a[l][None], r_k=r_k[l].reshape(1, D_RWKV),
                gn_g=gn_g[l].reshape(1, D_RWKV), gn_b=gn_b[l].reshape(1, D_RWKV), tri=tri,
                ones_bd=_head_block_matrix(1.0), avg_bd=_head_block_matrix(1.0 / HEAD_DIM))


def _state_to_block_diag(s_f, s_b):
    def one(s):
        h = jnp.swapaxes(s.astype(F32), -1, -2)
        b = h.shape[0]
        h = h.reshape(b, N_PAIRS, PAIR, HEAD_DIM, HEAD_DIM)
        z = jnp.zeros_like(h[:, :, 0])
        top = jnp.concatenate([h[:, :, 0], z], axis=-1)
        bot = jnp.concatenate([z, h[:, :, 1]], axis=-1)
        return jnp.concatenate([top, bot], axis=-2)
    return jnp.stack([one(s_f), one(s_b)], axis=1)


def kernel(x_prompt, x_sample, state_rwkv_fwd, state_rwkv_bwd, c, c_ctx, w_ada, b_ada, norm_g, w_in,
           mu_shift, w0, w2, a0, a2, k_k, k_a, r_k, gn_g, gn_b, w_fnet, b_fnet, w_out, final_norm_g):
    depth = w_in.shape[0]
    n_dec = c.shape[0]
    assert n_dec + 1 <= 8
    bp, tp, _ = x_prompt.shape
    bs, ts, _ = x_sample.shape
    cvec = jnp.zeros((8, D_MODEL), F32).at[0].set(c_ctx).at[1:1 + n_dec].set(c)
    emb = jnp.asarray(_sincos_2d(ts)).astype(x_sample.dtype)
    cs_tab = jnp.asarray(_channel_dft_table()).astype(BF16)
    dft_p = jnp.asarray(_dft_table(tp)).astype(BF16)
    dft_s = jnp.asarray(_dft_table(ts)).astype(BF16)
    fng = final_norm_g[None]

    xp, xs = x_prompt, x_sample
    new_f, new_b = [], []
    for l in range(depth):
        mod = _mod_call(cvec, w_ada[l], b_ada[l][None]).reshape(8, 3, D_MODEL)
        mod_ctx, mod_lat = mod[0:1], mod[1:1 + n_dec]
        wts = _layer_weights(l, w0, w2, a0, a2, k_k, k_a, r_k, gn_g, gn_b)
        w_in_b = w_in[l].astype(BF16)
        w_out_b = w_out[l].astype(BF16)
        wbd = jax.scipy.linalg.block_diag(*[w_fnet[l, g] for g in range(w_fnet.shape[1])]).astype(BF16)
        ng, mu, bf = norm_g[l][None], mu_shift[l][None], b_fnet[l][None]
        emb_l = emb if l == 0 else None

        rk, v, lwla, grec, fin, gf = _in_proj_call(xp, None, mod_ctx, ng, w_in_b, mu)
        y_f, y_b, s_f, s_b = _scan_call(rk, v, lwla, wts, None, True)
        fo = _fnet_call(fin, gf, dft_p, cs_tab, wbd, bf)
        last = l == depth - 1
        xp = _out_call(xp, None, mod_ctx, y_f, y_b, rk, v, lwla, grec, fo, wts, w_out_b, fng, last)
        new_f.append(s_f)
        new_b.append(s_b)

        rk, v, lwla, grec, fin, gf = _in_proj_call(xs, emb_l, mod_lat, ng, w_in_b, mu)
        s0 = _state_to_block_diag(state_rwkv_fwd[:, l], state_rwkv_bwd[:, l])
        y_f, y_b = _scan_call(rk, v, lwla, wts, s0, False)
        fo = _fnet_call(fin, gf, dft_s, cs_tab, wbd, bf)
        xs = _out_call(xs, emb_l, mod_lat, y_f, y_b, rk, v, lwla, grec, fo, wts, w_out_b, fng, last)
    return (xp, xs, jnp.concatenate(new_f, axis=1), jnp.concatenate(new_b, axis=1))
```

```python
import functools
import math

import numpy as np
import jax
import jax.numpy as jnp
from jax import lax
from jax.experimental import pallas as pl
from jax.experimental.pallas import tpu as pltpu

F32 = jnp.float32
BF16 = jnp.bfloat16

D_MODEL = 1024
GRID_W = 64
D_RWKV = 512
D_FNET = D_MODEL - D_RWKV
HEAD_DIM = 64
N_HEADS = D_RWKV // HEAD_DIM
FNET_GROUP = 64
LORA = 32
N_DIR = 2
D_SHIFT = 3 * D_RWKV + N_DIR * 2 * LORA
D_IN = D_SHIFT + D_RWKV + 2 * D_FNET
NORM_EPS = 1e-6
GN_EPS = 64e-5
POS_BASE = 10000.0

LANES = 128
PAIR = LANES // HEAD_DIM
N_PAIRS = N_HEADS // PAIR
CHUNK = 64
SUB = 16
SUB_SHIFT = SUB.bit_length() - 1
SCAN_SEQS = 4
ROW_TILE = 512
FNET_ROWS = 1024
HALO = 8
VMEM_LIMIT = 56 * 1024 * 1024


def _silu(x):
    return x * jax.nn.sigmoid(x)


def _bdot(a, b):
    return jnp.dot(a.astype(BF16), b.astype(BF16), preferred_element_type=F32)


def _split2(x):
    hi = x.astype(BF16)
    lo = (x - hi.astype(F32)).astype(BF16)
    return hi, lo


def _mod_kernel(c_ref, w_ref, b_ref, o_ref):
    s_hi, s_lo = _split2(_silu(c_ref[...]))
    w = w_ref[...].astype(BF16)
    o_ref[...] = (jnp.dot(s_hi, w, preferred_element_type=F32)
                  + jnp.dot(s_lo, w, preferred_element_type=F32) + b_ref[...])


def _mod_call(cvec, w_ada, b_ada):
    n_blk = 3
    return pl.pallas_call(
        _mod_kernel,
        grid=(n_blk,),
        in_specs=[pl.BlockSpec((8, D_MODEL), lambda i: (0, 0)),
                  pl.BlockSpec((D_MODEL, D_MODEL), lambda i: (0, i)),
                  pl.BlockSpec((1, D_MODEL), lambda i: (0, i))],
        out_specs=pl.BlockSpec((8, D_MODEL), lambda i: (0, i)),
        out_shape=jax.ShapeDtypeStruct((8, 3 * D_MODEL), F32),
        compiler_params=pltpu.CompilerParams(dimension_semantics=("arbitrary",),
                                             vmem_limit_bytes=VMEM_LIMIT),
        name="mod",
    )(cvec, w_ada, b_ada)


def _modulated_norm(x, g, scale, shift):
    ms = jnp.mean(x * x, axis=-1, keepdims=True)
    y = x * lax.rsqrt(ms + NORM_EPS) * g
    return y * (1.0 + scale) + shift


def _in_proj_kernel(*refs, has_emb, has_halo, n_tiles):
    it = iter(refs)
    x_ref = next(it)
    xp_ref = next(it) if has_halo else None
    xn_ref = next(it) if has_halo else None
    emb_ref = next(it) if has_emb else None
    embp_ref = next(it) if (has_emb and has_halo) else None
    embn_ref = next(it) if (has_emb and has_halo) else None
    mod_ref, g_ref, w_ref, mu_ref = next(it), next(it), next(it), next(it)
    rk_ref, v_ref, lwla_ref, grec_ref, fin_ref, gf_ref = (next(it) for _ in range(6))

    i = pl.program_id(1)
    g = g_ref[...]
    shift = mod_ref[0, 0:1, :]
    scale = mod_ref[0, 1:2, :]
    nb, tt, _ = x_ref.shape
    tm = nb * tt
    x = x_ref[...].reshape(tm, D_MODEL)
    if has_emb:
        x = x + emb_ref[...]
    h = _modulated_norm(x, g, scale, shift)
    p = jnp.dot(h.astype(BF16), w_ref[...], preferred_element_type=F32)
    ps = p[:, :D_SHIFT]
    if has_halo:
        xh = jnp.concatenate([xp_ref[0], xn_ref[0]], axis=0)
        if has_emb:
            xh = xh + jnp.concatenate([embp_ref[...], embn_ref[...]], axis=0)
        hh = _modulated_norm(xh, g, scale, shift)
        ph = jnp.dot(hh.astype(BF16), w_ref[:, :D_SHIFT], preferred_element_type=F32)
        prev_row = jnp.where(i > 0, ph[HALO - 1:HALO, :], 0.0)
        next_row = jnp.where(i < n_tiles - 1, ph[HALO:HALO + 1, :], 0.0)
    else:
        prev_row = jnp.zeros((1, D_SHIFT), F32)
        next_row = jnp.zeros((1, D_SHIFT), F32)
    pos = lax.rem(lax.broadcasted_iota(jnp.int32, (tm, D_SHIFT), 0), tt)
    prev = jnp.where(pos == 0, prev_row, pltpu.roll(ps, 1, 0))
    nxt = jnp.where(pos == tt - 1, next_row, pltpu.roll(ps, tm - 1, 0))
    p_rec = ps + mu_ref[...] * (0.5 * (prev + nxt) - ps)

    def put(ref, val):
        ref[...] = val.astype(ref.dtype).reshape(ref.shape)

    put(rk_ref, p_rec[:, :2 * D_RWKV])
    put(v_ref, p_rec[:, 2 * D_RWKV:3 * D_RWKV])
    put(lwla_ref, p_rec[:, 3 * D_RWKV:])
    put(grec_ref, p[:, D_SHIFT:D_SHIFT + D_RWKV])
    put(fin_ref, p[:, D_SHIFT + D_RWKV:D_SHIFT + D_RWKV + D_FNET])
    put(gf_ref, p[:, D_SHIFT + D_RWKV + D_FNET:])


def _in_proj_call(x, emb, mod, norm_g, w_in_bf16, mu):
    B, T, _ = x.shape
    tt = min(T, ROW_TILE)
    nb = max(1, ROW_TILE // T)
    n_tiles = T // tt
    has_halo = n_tiles > 1
    has_emb = emb is not None
    per_batch_mod = mod.shape[0] > 1
    assert B % nb == 0 and T % tt == 0 and (nb == 1 or not (has_emb or per_batch_mod))
    tm = tt
    blocks_per_tile = tm // HALO
    last_halo_block = T // HALO - 1

    in_specs = [pl.BlockSpec((nb, tt, D_MODEL), lambda b, i: (b, i, 0))]
    args = [x]
    if has_halo:
        in_specs += [
            pl.BlockSpec((1, HALO, D_MODEL), lambda b, i: (b, jnp.maximum(i * blocks_per_tile - 1, 0), 0)),
            pl.BlockSpec((1, HALO, D_MODEL),
                         lambda b, i: (b, jnp.minimum((i + 1) * blocks_per_tile, last_halo_block), 0))]
        args += [x, x]
    if has_emb:
        in_specs.append(pl.BlockSpec((tm, D_MODEL), lambda b, i: (i, 0)))
        args.append(emb)
        if has_halo:
            in_specs += [
                pl.BlockSpec((HALO, D_MODEL), lambda b, i: (jnp.maximum(i * blocks_per_tile - 1, 0), 0)),
                pl.BlockSpec((HALO, D_MODEL),
                             lambda b, i: (jnp.minimum((i + 1) * blocks_per_tile, last_halo_block), 0))]
            args += [emb, emb]
    mod_map = (lambda b, i: (b, 0, 0)) if per_batch_mod else (lambda b, i: (0, 0, 0))
    in_specs += [pl.BlockSpec((1, 3, D_MODEL), mod_map),
                 pl.BlockSpec((1, D_MODEL), lambda b, i: (0, 0)),
                 pl.BlockSpec((D_MODEL, D_IN), lambda b, i: (0, 0), pipeline_mode=pl.Buffered(1)),
                 pl.BlockSpec((1, D_SHIFT), lambda b, i: (0, 0))]
    args += [mod, norm_g, w_in_bf16, mu]
    outs = ((2 * D_RWKV, BF16), (D_RWKV, BF16), (N_DIR * 2 * LORA, F32), (D_RWKV, BF16), (D_FNET, BF16),
            (D_FNET, BF16))
    out_specs = [pl.BlockSpec((nb, tt, w), lambda b, i: (b, i, 0)) for w, _ in outs]
    out_shape = [jax.ShapeDtypeStruct((B, T, w), dt) for w, dt in outs]
    kern = functools.partial(_in_proj_kernel, has_emb=has_emb, has_halo=has_halo, n_tiles=n_tiles)
    return pl.pallas_call(
        kern, grid=(B // nb, n_tiles), in_specs=in_specs, out_specs=out_specs, out_shape=out_shape,
        compiler_params=pltpu.CompilerParams(dimension_semantics=("arbitrary", "arbitrary"),
                                             vmem_limit_bytes=VMEM_LIMIT),
        name="in_proj",
    )(*args)


def _same_block(i, j, size):
    shift = size.bit_length() - 1
    return jnp.right_shift(i, shift) == jnp.right_shift(j, shift)


def _block_diag_rows(y, head0_lanes):
    return jnp.concatenate([jnp.where(head0_lanes, y, 0.0), jnp.where(head0_lanes, 0.0, y)], axis=0)


def _scan_kernel(*refs, zero_init, write_state, n_chunks, n_seq):
    it = iter(refs)
    rk_refs, v_refs, lwla_refs = [None, None], [None, None], [None, None]
    for d in range(N_DIR):
        rk_refs[d], v_refs[d], lwla_refs[d] = next(it), next(it), next(it)
    w0_ref, w2_ref, a0_ref, a2_ref = (next(it) for _ in range(4))
    kk_ref, ka_ref, tri_ref, ones_ref = (next(it) for _ in range(4))
    s0_ref = None if zero_init else next(it)
    y_refs = [next(it), next(it)]
    sfin_refs = [next(it), next(it)] if write_state else None
    h_scr = next(it)

    j = pl.program_id(1)

    @pl.when(j == 0)
    def _():
        if zero_init:
            h_scr[...] = jnp.zeros_like(h_scr)
        else:
            h_scr[...] = s0_ref[...]

    C = CHUNK
    row = lax.broadcasted_iota(jnp.int32, (C, LANES), 0)
    lane = lax.broadcasted_iota(jnp.int32, (C, LANES), 1)
    s_idx = jnp.bitwise_and(lane, HEAD_DIM - 1)
    head0 = lane < HEAD_DIM
    row_c = lax.broadcasted_iota(jnp.int32, (SUB, LANES), 0)
    lane_c = lax.broadcasted_iota(jnp.int32, (SUB, LANES), 1)
    col_c = jnp.bitwise_and(lane_c, SUB - 1)
    blk_c = jnp.right_shift(jnp.bitwise_and(lane_c, HEAD_DIM - 1), SUB_SHIFT)
    lane_blk_c = jnp.right_shift(lane_c, SUB_SHIFT)
    eye_c = (row_c == col_c).astype(F32)

    def bd_c(y):
        return jnp.concatenate([jnp.where(lane_blk_c == g, y, 0.0) for g in range(LANES // SUB)], axis=0)

    def inverse_size4(l_c, upper):
        r4 = jnp.bitwise_and(row_c, 3)
        c4 = jnp.bitwise_and(col_c, 3)
        down, up = (lambda x, k: pltpu.roll(x, k, 0)), (lambda x, k: pltpu.roll(x, SUB - k, 0))
        right, left = (lambda x, k: pltpu.roll(x, k, 1)), (lambda x, k: pltpu.roll(x, LANES - k, 1))
        s2 = jnp.where(_same_block(row_c, col_c, 2), l_c, 0.0)
        e4 = jnp.where(_same_block(row_c, col_c, 4), l_c, 0.0) - s2
        if upper:
            s_col = jnp.where(r4 == 1, up(s2, 1), up(s2, 2))
            s_row = jnp.where(c4 == 2, right(s2, 1), right(s2, 2))
            e_s = jnp.where(c4 == 3, right(e4, 1) * s_col, 0.0)
            s_e = jnp.where(r4 == 0, s_row * up(e4, 1), 0.0)
            s_e_s = jnp.where((r4 == 0) & (c4 == 3), s_row * up(e_s, 1), 0.0)
        else:
            s_col = jnp.where(r4 == 2, down(s2, 1), down(s2, 2))
            s_row = jnp.where(c4 == 1, left(s2, 1), left(s2, 2))
            e_s = jnp.where(c4 == 0, left(e4, 1) * s_col, 0.0)
            s_e = jnp.where(r4 == 3, s_row * down(e4, 1), 0.0)
            s_e_s = jnp.where((r4 == 3) & (c4 == 0), s_row * down(e_s, 1), 0.0)
        return eye_c - s2 - e4 + e_s + s_e - s_e_s

    row2 = lax.broadcasted_iota(jnp.int32, (LANES, LANES), 0)
    lane2 = lax.broadcasted_iota(jnp.int32, (LANES, LANES), 1)
    same_head = (row2 < HEAD_DIM) == (lane2 < HEAD_DIM)
    decay_scale = math.exp(-0.5)

    seq_dir = [(n, d) for n in range(n_seq) for d in range(N_DIR)]
    rk_all = {nd: rk_refs[nd[1]][nd[0]].astype(F32) for nd in seq_dir}
    kkr_all = {nd: rk_all[nd][:, D_RWKV:] * kk_ref[...] for nd in seq_dir}
    ssq_rows = _bdot(jnp.concatenate([kkr_all[nd] * kkr_all[nd] for nd in seq_dir], axis=0), ones_ref[...])
    z_w_rows, z_a_rows = [], []
    for d in range(N_DIR):
        ll_rows = jnp.concatenate([lwla_refs[d][n] for n in range(n_seq)], axis=0)
        z_w_rows.append(w0_ref[d:d + 1, :] + _bdot(jnp.tanh(ll_rows), w2_ref[d]))
        z_a_rows.append(a0_ref[d:d + 1, :] + _bdot(ll_rows, a2_ref[d]))

    probs = []
    for i, (n, d) in enumerate(seq_dir):
        rk = rk_all[(n, d)]
        r = rk[:, :D_RWKV]
        k = rk[:, D_RWKV:]
        v = v_refs[d][n]
        logw = -decay_scale * jax.nn.sigmoid(z_w_rows[d][n * C:(n + 1) * C])
        a = jax.nn.sigmoid(z_a_rows[d][n * C:(n + 1) * C])
        kd = k * (1.0 + (a - 1.0) * ka_ref[...])
        kkr = kkr_all[(n, d)]
        ssq = ssq_rows[i * C:(i + 1) * C]
        kk = kkr * lax.rsqrt(jnp.maximum(ssq, 1e-24))
        bvec = kk * a
        lw_hi, lw_lo = _split2(logw)
        tri = tri_ref[d]
        cum = (jnp.dot(tri, lw_hi, preferred_element_type=F32)
               + jnp.dot(tri, lw_lo, preferred_element_type=F32))
        cum_prev = cum - logw
        tot = cum[C - 1:C, :] if d == 0 else cum[0:1, :]
        kap_t = kk * jnp.exp(cum_prev)
        r_t = r * jnp.exp(cum)
        e_neg = jnp.exp(-cum)
        gam = jnp.exp(tot)
        b_t = bvec * e_neg
        k_t = kd * e_neg
        e_rem = gam * e_neg
        b_h = bvec * e_rem
        k_h = kd * e_rem
        if d == 0:
            strict, incl = row > s_idx, row >= s_idx
        else:
            strict, incl = row < s_idx, row <= s_idx

        for p in range(N_PAIRS):
            sl = slice(p * LANES, (p + 1) * LANES)
            dup_t = lambda x: jnp.where(same_head, jnp.concatenate([x[:, sl], x[:, sl]], axis=0).T, 0.0)
            probs.append(dict(n=n, d=d, p=p, sl=sl, strict=strict, incl=incl,
                              lhs=jnp.concatenate([kap_t[:, sl], r_t[:, sl]], axis=0).astype(BF16),
                              v=v[:, sl], w_lm=jnp.concatenate([dup_t(b_t), dup_t(k_t)], axis=1).astype(BF16),
                              t2=jnp.concatenate([b_h[:, sl], k_h[:, sl]], axis=0).T.astype(BF16),
                              gam_col=jnp.broadcast_to(gam[:, sl], (LANES, LANES)).T))

    bd = lambda y: _block_diag_rows(y, head0)
    for q in probs:
        q["h"] = h_scr[q["n"], q["d"], q["p"]]
        lm = jnp.dot(q["lhs"], q["w_lm"], preferred_element_type=F32)
        q["l_b"] = jnp.where(q["strict"], lm[:C, :LANES], 0.0)
        q["m_b"] = jnp.where(q["incl"], lm[C:, :LANES], 0.0)
        q["lm_k"] = jnp.concatenate([jnp.where(q["strict"], lm[:C, LANES:], 0.0),
                                     jnp.where(q["incl"], lm[C:, LANES:], 0.0)], axis=0)
    for q in probs:
        lmv = _bdot(q["lm_k"], bd(q["v"]))
        q["l_kv"], q["m_kv"] = lmv[:C], lmv[C:]
    for q in probs:
        q["kr_h"] = _bdot(q["lhs"], q["h"])
    for q in probs:
        l_b = q["l_b"]
        l_c = l_b[0:SUB]
        for jb in range(1, C // SUB):
            l_c = jnp.where(blk_c == jb, l_b[jb * SUB:(jb + 1) * SUB], l_c)
        q["l_c"] = l_c
        q["t_c"] = inverse_size4(l_c, q["d"] == 1)
    s = 4
    while s < SUB:
        off_mask = _same_block(row_c, col_c, 2 * s) & ~_same_block(row_c, col_c, s)
        for q in probs:
            q["et"] = _bdot(jnp.where(off_mask, q["l_c"], 0.0), bd_c(q["t_c"]))
        for q in probs:
            q["t_c"] = q["t_c"] - _bdot(q["t_c"], bd_c(q["et"]))
        s *= 2
    for q in probs:
        q["tinv"] = jnp.concatenate([jnp.where(blk_c == jb, q["t_c"], 0.0) for jb in range(C // SUB)], axis=0)
    while s < C:
        off_mask = _same_block(row, s_idx, 2 * s) & ~_same_block(row, s_idx, s)
        for q in probs:
            q["et"] = _bdot(jnp.where(off_mask, q["l_b"], 0.0), bd(q["tinv"]))
        for q in probs:
            q["tinv"] = q["tinv"] - _bdot(q["tinv"], bd(q["et"]))
        s *= 2
    for q in probs:
        q["u_n"] = _bdot(q["tinv"], bd(q["kr_h"][:C] + q["l_kv"]))
    for q in probs:
        y = q["kr_h"][C:] + q["m_kv"] - _bdot(q["m_b"], bd(q["u_n"]))
        y_refs[q["d"]][q["n"], :, q["sl"]] = y.astype(BF16)
    for q in probs:
        upd = _bdot(q["t2"], jnp.concatenate([(-q["u_n"]).astype(BF16), q["v"]], axis=0))
        h_scr[q["n"], q["d"], q["p"]] = q["gam_col"] * q["h"] + jnp.where(same_head, upd, 0.0)

    if write_state:
        @pl.when(j == n_chunks - 1)
        def _():
            for n, d, p in [(n, d, p) for n in range(n_seq) for d in range(N_DIR) for p in range(N_PAIRS)]:
                ht = h_scr[n, d, p].T
                sfin_refs[d][n, 0, PAIR * p] = ht[:HEAD_DIM, :HEAD_DIM]
                sfin_refs[d][n, 0, PAIR * p + 1] = ht[HEAD_DIM:, HEAD_DIM:]


def _scan_call(rk, v, lwla, wts, s0_bd, write_state):
    B, T, _ = rk.shape
    nc = T // CHUNK
    ns = min(SCAN_SEQS, B)
    assert B % ns == 0
    zero_init = s0_bd is None
    fwd = lambda b, j: (b, j, 0)
    bwd = lambda b, j: (b, nc - 1 - j, 0)
    full = lambda *shape: pl.BlockSpec(shape, lambda b, j: (0,) * len(shape))
    tok = lambda w, m: pl.BlockSpec((ns, CHUNK, w), m)
    in_specs = [tok(2 * D_RWKV, fwd), tok(D_RWKV, fwd), tok(LANES, fwd),
                tok(2 * D_RWKV, bwd), tok(D_RWKV, bwd), tok(LANES, bwd),
                full(N_DIR, D_RWKV), full(N_DIR, LANES, D_RWKV),
                full(N_DIR, D_RWKV), full(N_DIR, LANES, D_RWKV),
                full(1, D_RWKV), full(1, D_RWKV), full(N_DIR, CHUNK, CHUNK), full(D_RWKV, D_RWKV)]
    args = [rk, v, lwla, rk, v, lwla, wts["w0"], wts["w2"], wts["a0"], wts["a2"],
            wts["k_k"], wts["k_a"], wts["tri"], wts["ones_bd"]]
    state_block = (ns, N_DIR, N_PAIRS, LANES, LANES)
    if not zero_init:
        in_specs.append(pl.BlockSpec(state_block, lambda b, j: (b, 0, 0, 0, 0)))
        args.append(s0_bd)
    out_specs = [pl.BlockSpec((ns, CHUNK, D_RWKV), fwd), pl.BlockSpec((ns, CHUNK, D_RWKV), bwd)]
    out_shape = [jax.ShapeDtypeStruct((B, T, D_RWKV), BF16), jax.ShapeDtypeStruct((B, T, D_RWKV), BF16)]
    if write_state:
        final_block = (ns, 1, N_HEADS, HEAD_DIM, HEAD_DIM)
        out_specs += [pl.BlockSpec(final_block, lambda b, j: (b, 0, 0, 0, 0))] * N_DIR
        out_shape += [jax.ShapeDtypeStruct((B,) + final_block[1:], F32)] * N_DIR
    kern = functools.partial(_scan_kernel, zero_init=zero_init, write_state=write_state, n_chunks=nc,
                             n_seq=ns)
    return pl.pallas_call(
        kern, grid=(B // ns, nc), in_specs=in_specs, out_specs=out_specs, out_shape=out_shape,
        scratch_shapes=[pltpu.VMEM(state_block, F32)],
        compiler_params=pltpu.CompilerParams(dimension_semantics=("arbitrary", "arbitrary"),
                                             vmem_limit_bytes=VMEM_LIMIT),
        name="scan",
    )(*args)


def _fnet_kernel(fin_ref, gf_ref, dft_ref, cs_ref, wbd_ref, b_ref, o_ref, g_scr):
    u = pl.program_id(1)
    nb, seq_len, _ = fin_ref.shape
    tu = o_ref.shape[1]

    @pl.when(u == 0)
    def _():
        step = min(seq_len, ROW_TILE)
        for n in range(nb):
            for t0 in range(0, seq_len, step):
                fc = _bdot(fin_ref[n, t0:t0 + step, :], cs_ref[...])
                g_scr[n, t0:t0 + step, :] = fc[:, :D_FNET].astype(BF16)
                g_scr[n, seq_len + t0:seq_len + t0 + step, :] = fc[:, D_FNET:].astype(BF16)

    f_re = jnp.concatenate([jnp.dot(dft_ref[...], g_scr[n], preferred_element_type=F32) for n in range(nb)],
                           axis=0)
    f_out = _bdot(f_re, wbd_ref[...]) + b_ref[...]
    gate = _silu(gf_ref[...].astype(F32)).reshape(nb * tu, D_FNET)
    o_ref[...] = (f_out * gate).astype(BF16).reshape(nb, tu, D_FNET)


def _fnet_call(fin, gf, dft_bf16, cs_bf16, wbd_bf16, b_fnet):
    B, T, _ = fin.shape
    tu = min(T, ROW_TILE)
    nb = max(1, FNET_ROWS // T)
    assert B % nb == 0 and T % tu == 0
    return pl.pallas_call(
        _fnet_kernel,
        grid=(B // nb, T // tu),
        in_specs=[pl.BlockSpec((nb, T, D_FNET), lambda b, u: (b, 0, 0)),
                  pl.BlockSpec((nb, tu, D_FNET), lambda b, u: (b, u, 0)),
                  pl.BlockSpec((tu, 2 * T), lambda b, u: (u, 0)),
                  pl.BlockSpec((D_FNET, 2 * D_FNET), lambda b, u: (0, 0)),
                  pl.BlockSpec((D_FNET, D_FNET), lambda b, u: (0, 0)),
                  pl.BlockSpec((1, D_FNET), lambda b, u: (0, 0))],
        out_specs=pl.BlockSpec((nb, tu, D_FNET), lambda b, u: (b, u, 0)),
        out_shape=jax.ShapeDtypeStruct((B, T, D_FNET), BF16),
        scratch_shapes=[pltpu.VMEM((nb, 2 * T, D_FNET), BF16)],
        compiler_params=pltpu.CompilerParams(dimension_semantics=("arbitrary", "arbitrary"),
                                             vmem_limit_bytes=VMEM_LIMIT),
        name="fnet",
    )(fin, gf, dft_bf16, cs_bf16, wbd_bf16, b_fnet)


def _out_kernel(*refs, has_emb, final_norm):
    it = iter(refs)
    x_ref = next(it)
    emb_ref = next(it) if has_emb else None
    (mod_ref, yf_ref, yb_ref, rk_ref, v_ref, lwla_ref, grec_ref, fo_ref, a0_ref, a2_ref, ka_ref,
     rkw_ref, gng_ref, gnb_ref, avg_ref, ones_ref, wout_ref, fng_ref, o_ref) = (next(it) for _ in range(19))

    nb, tt, _ = x_ref.shape
    tm = nb * tt
    rows = lambda ref: ref[...].reshape(tm, ref.shape[-1])
    y = rows(yf_ref).astype(F32) + rows(yb_ref).astype(F32)
    mu = _bdot(y, avg_ref[...])
    dlt = y - mu
    var = _bdot(dlt * dlt, avg_ref[...])
    y_n = dlt * lax.rsqrt(var + GN_EPS) * gng_ref[...] + gnb_ref[...]
    rk = rows(rk_ref).astype(F32)
    r = rk[:, :D_RWKV]
    k = rk[:, D_RWKV:]
    ll = rows(lwla_ref)
    a_sum = (jax.nn.sigmoid(a0_ref[0:1, :] + _bdot(ll, a2_ref[0]))
             + jax.nn.sigmoid(a0_ref[1:2, :] + _bdot(ll, a2_ref[1])))
    k_sum = k * (2.0 + (a_sum - 2.0) * ka_ref[...])
    bonus = _bdot(r * k_sum * rkw_ref[...], ones_ref[...]) * rows(v_ref).astype(F32)
    rec_out = (y_n + bonus) * _silu(rows(grec_ref).astype(F32))
    mixed = jnp.concatenate([rec_out.astype(BF16), rows(fo_ref)], axis=-1)
    out = jnp.dot(mixed, wout_ref[...], preferred_element_type=F32)
    x = rows(x_ref)
    if has_emb:
        x = x + emb_ref[...]
    z = x + mod_ref[0, 2:3, :] * out
    if final_norm:
        ms = jnp.mean(z * z, axis=-1, keepdims=True)
        z = z * lax.rsqrt(ms + NORM_EPS) * fng_ref[...]
    o_ref[...] = z.reshape(nb, tt, D_MODEL)


def _out_call(x, emb, mod, y_f, y_b, rk, v, lwla, grec, fo, wts, w_out_bf16, final_norm_g, final_norm):
    B, T, _ = x.shape
    tt = min(T, ROW_TILE)
    nb = max(1, ROW_TILE // T)
    has_emb = emb is not None
    per_batch_mod = mod.shape[0] > 1
    assert B % nb == 0 and T % tt == 0 and (nb == 1 or not (has_emb or per_batch_mod))
    tok = lambda w: pl.BlockSpec((nb, tt, w), lambda b, i: (b, i, 0))
    full = lambda *shape: pl.BlockSpec(shape, lambda b, i: (0,) * len(shape))
    in_specs = [tok(D_MODEL)]
    args = [x]
    if has_emb:
        in_specs.append(pl.BlockSpec((tt, D_MODEL), lambda b, i: (i, 0)))
        args.append(emb)
    mod_map = (lambda b, i: (b, 0, 0)) if per_batch_mod else (lambda b, i: (0, 0, 0))
    in_specs += [pl.BlockSpec((1, 3, D_MODEL), mod_map), tok(D_RWKV), tok(D_RWKV), tok(2 * D_RWKV),
                 tok(D_RWKV), tok(LANES), tok(D_RWKV), tok(D_FNET),
                 full(N_DIR, D_RWKV), full(N_DIR, LANES, D_RWKV),
                 full(1, D_RWKV), full(1, D_RWKV), full(1, D_RWKV), full(1, D_RWKV),
                 full(D_RWKV, D_RWKV), full(D_RWKV, D_RWKV), full(D_MODEL, D_MODEL), full(1, D_MODEL)]
    args += [mod, y_f, y_b, rk, v, lwla, grec, fo, wts["a0"], wts["a2"], wts["k_a"],
             wts["r_k"], wts["gn_g"], wts["gn_b"], wts["avg_bd"], wts["ones_bd"], w_out_bf16, final_norm_g]
    return pl.pallas_call(
        functools.partial(_out_kernel, has_emb=has_emb, final_norm=final_norm),
        grid=(B // nb, T // tt), in_specs=in_specs,
        out_specs=tok(D_MODEL),
        out_shape=jax.ShapeDtypeStruct((B, T, D_MODEL), F32),
        compiler_params=pltpu.CompilerParams(dimension_semantics=("arbitrary", "arbitrary"),
                                             vmem_limit_bytes=VMEM_LIMIT),
        name="out_proj",
    )(*args)


def _dft_table(seq_len):
    idx = np.arange(seq_len, dtype=np.int64)
    ang = 2.0 * np.pi * ((idx[:, None] * idx[None, :]) % seq_len).astype(np.float64) / seq_len
    scale = 1.0 / math.sqrt(seq_len)
    return np.concatenate([np.cos(ang) * scale, -np.sin(ang) * scale], axis=1).astype(np.float32)


def _channel_dft_table():
    n = FNET_GROUP
    idx = np.arange(n, dtype=np.int64)
    ang = 2.0 * np.pi * ((idx[:, None] * idx[None, :]) % n).astype(np.float64) / n
    c = np.cos(ang) / math.sqrt(n)
    s = np.sin(ang) / math.sqrt(n)
    eye = np.eye(D_FNET // n)
    return np.concatenate([np.kron(eye, c), np.kron(eye, s)], axis=1).astype(np.float32)


def _sincos_2d(n_tokens):
    rows = n_tokens // GRID_W
    pos = np.arange(rows * GRID_W)
    row = (pos // GRID_W).astype(np.float32)
    col = (pos % GRID_W).astype(np.float32)
    quarter = D_MODEL // 4
    freq = np.exp(np.float32(-math.log(POS_BASE)) * np.arange(quarter, dtype=np.float32) / np.float32(quarter))
    ang_r = row[:, None] * freq
    ang_c = col[:, None] * freq
    return np.concatenate([np.sin(ang_r), np.cos(ang_r), np.sin(ang_c), np.cos(ang_c)], axis=-1).astype(np.float32)


def _head_block_matrix(value):
    blk = np.kron(np.eye(N_HEADS), np.ones((HEAD_DIM, HEAD_DIM))) * value
    return jnp.asarray(blk, dtype=BF16)


def _pad_lora(w, row_offset):
    out = jnp.zeros((N_DIR, LANES, w.shape[-1]), F32)
    for d in range(N_DIR):
        out = out.at[d, row_offset + d * LORA:row_offset + (d + 1) * LORA, :].set(w[d])
    return out.astype(BF16)


def _layer_weights(l, w0, w2, a0, a2, k_k, k_a, r_k, gn_g, gn_b):
    w2_p = _pad_lora(w2[l], 0)
    a2_p = _pad_lora(a2[l], N_DIR * LORA)
    tri_f = np.tril(np.ones((CHUNK, CHUNK)))
    tri = jnp.asarray(np.stack([tri_f, tri_f.T]), dtype=BF16)
    return dict(w0=w0[l], w2=w2_p, a0=a0[l], a2=a2_p,
                k_k=k_k[l][None], k_a=k_a[l][None], r_k=r_k[l].reshape(1, D_RWKV),
                gn_g=gn_g[l].reshape(1, D_RWKV), gn_b=gn_b[l].reshape(1, D_RWKV), tri=tri,
                ones_bd=_head_block_matrix(1.0), avg_bd=_head_block_matrix(1.0 / HEAD_DIM))


def _state_to_block_diag(s_f, s_b):
    def one(s):
        h = jnp.swapaxes(s.astype(F32), -1, -2)
        b = h.shape[0]
        h = h.reshape(b, N_PAIRS, PAIR, HEAD_DIM, HEAD_DIM)
        z = jnp.zeros_like(h[:, :, 0])
        top = jnp.concatenate([h[:, :, 0], z], axis=-1)
        bot = jnp.concatenate([z, h[:, :, 1]], axis=-1)
        return jnp.concatenate([top, bot], axis=-2)
    return jnp.stack([one(s_f), one(s_b)], axis=1)


def kernel(x_prompt, x_sample, state_rwkv_fwd, state_rwkv_bwd, c, c_ctx, w_ada, b_ada, norm_g, w_in,
           mu_shift, w0, w2, a0, a2, k_k, k_a, r_k, gn_g, gn_b, w_fnet, b_fnet, w_out, final_norm_g):
    depth = w_in.shape[0]
    n_dec = c.shape[0]
    assert n_dec + 1 <= 8
    bp, tp, _ = x_prompt.shape
    bs, ts, _ = x_sample.shape
    cvec = jnp.zeros((8, D_MODEL), F32).at[0].set(c_ctx).at[1:1 + n_dec].set(c)
    emb = jnp.asarray(_sincos_2d(ts)).astype(x_sample.dtype)
    cs_tab = jnp.asarray(_channel_dft_table()).astype(BF16)
    dft_p = jnp.asarray(_dft_table(tp)).astype(BF16)
    dft_s = jnp.asarray(_dft_table(ts)).astype(BF16)
    fng = final_norm_g[None]

    xp, xs = x_prompt, x_sample
    new_f, new_b = [], []
    for l in range(depth):
        mod = _mod_call(cvec, w_ada[l], b_ada[l][None]).reshape(8, 3, D_MODEL)
        mod_ctx, mod_lat = mod[0:1], mod[1:1 + n_dec]
        wts = _layer_weights(l, w0, w2, a0, a2, k_k, k_a, r_k, gn_g, gn_b)
        w_in_b = w_in[l].astype(BF16)
        w_out_b = w_out[l].astype(BF16)
        wbd = jax.scipy.linalg.block_diag(*[w_fnet[l, g] for g in range(w_fnet.shape[1])]).astype(BF16)
        ng, mu, bf = norm_g[l][None], mu_shift[l][None], b_fnet[l][None]
        emb_l = emb if l == 0 else None

        rk, v, lwla, grec, fin, gf = _in_proj_call(xp, None, mod_ctx, ng, w_in_b, mu)
        y_f, y_b, s_f, s_b = _scan_call(rk, v, lwla, wts, None, True)
        fo = _fnet_call(fin, gf, dft_p, cs_tab, wbd, bf)
        last = l == depth - 1
        xp = _out_call(xp, None, mod_ctx, y_f, y_b, rk, v, lwla, grec, fo, wts, w_out_b, fng, last)
        new_f.append(s_f)
        new_b.append(s_b)

        rk, v, lwla, grec, fin, gf = _in_proj_call(xs, emb_l, mod_lat, ng, w_in_b, mu)
        s0 = _state_to_block_diag(state_rwkv_fwd[:, l], state_rwkv_bwd[:, l])
        y_f, y_b = _scan_call(rk, v, lwla, wts, s0, False)
        fo = _fnet_call(fin, gf, dft_s, cs_tab, wbd, bf)
        xs = _out_call(xs, emb_l, mod_lat, y_f, y_b, rk, v, lwla, grec, fo, wts, w_out_b, fng, last)
    return (xp, xs, jnp.concatenate(new_f, axis=1), jnp.concatenate(new_b, axis=1))
```

```python
import functools
import math

import numpy as np
import jax
import jax.numpy as jnp
from jax import lax
from jax.experimental import pallas as pl
from jax.experimental.pallas import tpu as pltpu

F32 = jnp.float32
BF16 = jnp.bfloat16

D_MODEL = 1024
GRID_W = 64
D_RWKV = 512
D_FNET = D_MODEL - D_RWKV
HEAD_DIM = 64
N_HEADS = D_RWKV // HEAD_DIM
FNET_GROUP = 64
LORA = 32
N_DIR = 2
D_SHIFT = 3 * D_RWKV + N_DIR * 2 * LORA
D_IN = D_SHIFT + D_RWKV + 2 * D_FNET
NORM_EPS = 1e-6
GN_EPS = 64e-5
POS_BASE = 10000.0

LANES = 128
PAIR = LANES // HEAD_DIM
N_PAIRS = N_HEADS // PAIR
CHUNK = 64
SUB = 16
SUB_SHIFT = SUB.bit_length() - 1
SCAN_SEQS = 4
ROW_TILE = 512
FNET_ROWS = 1024
HALO = 8
VMEM_LIMIT = 56 * 1024 * 1024


def _silu(x):
    return x * jax.nn.sigmoid(x)


def _bdot(a, b):
    return jnp.dot(a.astype(BF16), b.astype(BF16), preferred_element_type=F32)


def _split2(x):
    hi = x.astype(BF16)
    lo = (x - hi.astype(F32)).astype(BF16)
    return hi, lo


def _mod_kernel(c_ref, w_ref, b_ref, o_ref):
    s_hi, s_lo = _split2(_silu(c_ref[...]))
    w = w_ref[...].astype(BF16)
    o_ref[...] = (jnp.dot(s_hi, w, preferred_element_type=F32)
                  + jnp.dot(s_lo, w, preferred_element_type=F32) + b_ref[...])


def _mod_call(cvec, w_ada, b_ada):
    cols = D_MODEL // 2
    return pl.pallas_call(
        _mod_kernel,
        grid=(3 * D_MODEL // cols,),
        in_specs=[pl.BlockSpec((8, D_MODEL), lambda i: (0, 0)),
                  pl.BlockSpec((D_MODEL, cols), lambda i: (0, i)),
                  pl.BlockSpec((1, cols), lambda i: (0, i))],
        out_specs=pl.BlockSpec((8, cols), lambda i: (0, i)),
        out_shape=jax.ShapeDtypeStruct((8, 3 * D_MODEL), F32),
        compiler_params=pltpu.CompilerParams(dimension_semantics=("arbitrary",),
                                             vmem_limit_bytes=VMEM_LIMIT),
        name="mod",
    )(cvec, w_ada, b_ada)


def _modulated_norm(x, g, scale, shift):
    ms = jnp.mean(x * x, axis=-1, keepdims=True)
    y = x * lax.rsqrt(ms + NORM_EPS) * g
    return y * (1.0 + scale) + shift


def _in_proj_kernel(*refs, has_emb, has_halo, n_tiles):
    it = iter(refs)
    x_ref = next(it)
    xp_ref = next(it) if has_halo else None
    xn_ref = next(it) if has_halo else None
    emb_ref = next(it) if has_emb else None
    embp_ref = next(it) if (has_emb and has_halo) else None
    embn_ref = next(it) if (has_emb and has_halo) else None
    mod_ref, g_ref, w_ref, mu_ref = next(it), next(it), next(it), next(it)
    rk_ref, v_ref, lwla_ref, grec_ref, fin_ref, gf_ref = (next(it) for _ in range(6))

    i = pl.program_id(1)
    g = g_ref[...]
    shift = mod_ref[0, 0:1, :]
    scale = mod_ref[0, 1:2, :]
    nb, tt, _ = x_ref.shape
    tm = nb * tt
    x = x_ref[...].reshape(tm, D_MODEL)
    if has_emb:
        x = x + emb_ref[...]
    h = _modulated_norm(x, g, scale, shift)
    p = jnp.dot(h.astype(BF16), w_ref[...], preferred_element_type=F32)
    ps = p[:, :D_SHIFT]
    if has_halo:
        xh = jnp.concatenate([xp_ref[0], xn_ref[0]], axis=0)
        if has_emb:
            xh = xh + jnp.concatenate([embp_ref[...], embn_ref[...]], axis=0)
        hh = _modulated_norm(xh, g, scale, shift)
        ph = jnp.dot(hh.astype(BF16), w_ref[:, :D_SHIFT], preferred_element_type=F32)
        prev_row = jnp.where(i > 0, ph[HALO - 1:HALO, :], 0.0)
        next_row = jnp.where(i < n_tiles - 1, ph[HALO:HALO + 1, :], 0.0)
    else:
        prev_row = jnp.zeros((1, D_SHIFT), F32)
        next_row = jnp.zeros((1, D_SHIFT), F32)
    pos = lax.rem(lax.broadcasted_iota(jnp.int32, (tm, D_SHIFT), 0), tt)
    prev = jnp.where(pos == 0, prev_row, pltpu.roll(ps, 1, 0))
    nxt = jnp.where(pos == tt - 1, next_row, pltpu.roll(ps, tm - 1, 0))
    p_rec = ps + mu_ref[...] * (0.5 * (prev + nxt) - ps)

    def put(ref, val):
        ref[...] = val.astype(ref.dtype).reshape(ref.shape)

    put(rk_ref, p_rec[:, :2 * D_RWKV])
    put(v_ref, p_rec[:, 2 * D_RWKV:3 * D_RWKV])
    put(lwla_ref, p_rec[:, 3 * D_RWKV:])
    put(grec_ref, p[:, D_SHIFT:D_SHIFT + D_RWKV])
    put(fin_ref, p[:, D_SHIFT + D_RWKV:D_SHIFT + D_RWKV + D_FNET])
    put(gf_ref, p[:, D_SHIFT + D_RWKV + D_FNET:])


def _in_proj_call(x, emb, mod, norm_g, w_in_bf16, mu):
    B, T, _ = x.shape
    tt = min(T, ROW_TILE)
    nb = max(1, ROW_TILE // T)
    n_tiles = T // tt
    has_halo = n_tiles > 1
    has_emb = emb is not None
    per_batch_mod = mod.shape[0] > 1
    assert B % nb == 0 and T % tt == 0 and (nb == 1 or not (has_emb or per_batch_mod))
    tm = tt
    blocks_per_tile = tm // HALO
    last_halo_block = T // HALO - 1

    in_specs = [pl.BlockSpec((nb, tt, D_MODEL), lambda b, i: (b, i, 0))]
    args = [x]
    if has_halo:
        in_specs += [
            pl.BlockSpec((1, HALO, D_MODEL), lambda b, i: (b, jnp.maximum(i * blocks_per_tile - 1, 0), 0)),
            pl.BlockSpec((1, HALO, D_MODEL),
                         lambda b, i: (b, jnp.minimum((i + 1) * blocks_per_tile, last_halo_block), 0))]
        args += [x, x]
    if has_emb:
        in_specs.append(pl.BlockSpec((tm, D_MODEL), lambda b, i: (i, 0)))
        args.append(emb)
        if has_halo:
            in_specs += [
                pl.BlockSpec((HALO, D_MODEL), lambda b, i: (jnp.maximum(i * blocks_per_tile - 1, 0), 0)),
                pl.BlockSpec((HALO, D_MODEL),
                             lambda b, i: (jnp.minimum((i + 1) * blocks_per_tile, last_halo_block), 0))]
            args += [emb, emb]
    mod_map = (lambda b, i: (b, 0, 0)) if per_batch_mod else (lambda b, i: (0, 0, 0))
    in_specs += [pl.BlockSpec((1, 3, D_MODEL), mod_map),
                 pl.BlockSpec((1, D_MODEL), lambda b, i: (0, 0)),
                 pl.BlockSpec((D_MODEL, D_IN), lambda b, i: (0, 0), pipeline_mode=pl.Buffered(1)),
                 pl.BlockSpec((1, D_SHIFT), lambda b, i: (0, 0))]
    args += [mod, norm_g, w_in_bf16, mu]
    outs = ((2 * D_RWKV, BF16), (D_RWKV, BF16), (N_DIR * 2 * LORA, F32), (D_RWKV, BF16), (D_FNET, BF16),
            (D_FNET, BF16))
    out_specs = [pl.BlockSpec((nb, tt, w), lambda b, i: (b, i, 0)) for w, _ in outs]
    out_shape = [jax.ShapeDtypeStruct((B, T, w), dt) for w, dt in outs]
    kern = functools.partial(_in_proj_kernel, has_emb=has_emb, has_halo=has_halo, n_tiles=n_tiles)
    return pl.pallas_call(
        kern, grid=(B // nb, n_tiles), in_specs=in_specs, out_specs=out_specs, out_shape=out_shape,
        compiler_params=pltpu.CompilerParams(dimension_semantics=("arbitrary", "arbitrary"),
                                             vmem_limit_bytes=VMEM_LIMIT),
        name="in_proj",
    )(*args)


def _same_block(i, j, size):
    shift = size.bit_length() - 1
    return jnp.right_shift(i, shift) == jnp.right_shift(j, shift)


def _block_diag_rows(y, head0_lanes):
    return jnp.concatenate([jnp.where(head0_lanes, y, 0.0), jnp.where(head0_lanes, 0.0, y)], axis=0)


def _scan_kernel(*refs, zero_init, write_state, n_chunks, n_seq):
    it = iter(refs)
    rk_refs, v_refs, lwla_refs = [None, None], [None, None], [None, None]
    for d in range(N_DIR):
        rk_refs[d], v_refs[d], lwla_refs[d] = next(it), next(it), next(it)
    w0_ref, w2_ref, a0_ref, a2_ref = (next(it) for _ in range(4))
    kk_ref, ka_ref, tri_ref, ones_ref = (next(it) for _ in range(4))
    s0_ref = None if zero_init else next(it)
    y_refs = [next(it), next(it)]
    sfin_refs = [next(it), next(it)] if write_state else None
    h_scr = next(it)

    j = pl.program_id(1)

    @pl.when(j == 0)
    def _():
        if zero_init:
            h_scr[...] = jnp.zeros_like(h_scr)
        else:
            h_scr[...] = s0_ref[...]

    C = CHUNK
    row = lax.broadcasted_iota(jnp.int32, (C, LANES), 0)
    lane = lax.broadcasted_iota(jnp.int32, (C, LANES), 1)
    s_idx = jnp.bitwise_and(lane, HEAD_DIM - 1)
    head0 = lane < HEAD_DIM
    row_c = lax.broadcasted_iota(jnp.int32, (SUB, LANES), 0)
    lane_c = lax.broadcasted_iota(jnp.int32, (SUB, LANES), 1)
    col_c = jnp.bitwise_and(lane_c, SUB - 1)
    blk_c = jnp.right_shift(jnp.bitwise_and(lane_c, HEAD_DIM - 1), SUB_SHIFT)
    lane_blk_c = jnp.right_shift(lane_c, SUB_SHIFT)
    eye_c = (row_c == col_c).astype(F32)

    def bd_c(y):
        return jnp.concatenate([jnp.where(lane_blk_c == g, y, 0.0) for g in range(LANES // SUB)], axis=0)

    def inverse_size4(l_c, upper):
        r4 = jnp.bitwise_and(row_c, 3)
        c4 = jnp.bitwise_and(col_c, 3)
        down, up = (lambda x, k: pltpu.roll(x, k, 0)), (lambda x, k: pltpu.roll(x, SUB - k, 0))
        right, left = (lambda x, k: pltpu.roll(x, k, 1)), (lambda x, k: pltpu.roll(x, LANES - k, 1))
        s2 = jnp.where(_same_block(row_c, col_c, 2), l_c, 0.0)
        e4 = jnp.where(_same_block(row_c, col_c, 4), l_c, 0.0) - s2
        if upper:
            s_col = jnp.where(r4 == 1, up(s2, 1), up(s2, 2))
            s_row = jnp.where(c4 == 2, right(s2, 1), right(s2, 2))
            e_s = jnp.where(c4 == 3, right(e4, 1) * s_col, 0.0)
            s_e = jnp.where(r4 == 0, s_row * up(e4, 1), 0.0)
            s_e_s = jnp.where((r4 == 0) & (c4 == 3), s_row * up(e_s, 1), 0.0)
        else:
            s_col = jnp.where(r4 == 2, down(s2, 1), down(s2, 2))
            s_row = jnp.where(c4 == 1, left(s2, 1), left(s2, 2))
            e_s = jnp.where(c4 == 0, left(e4, 1) * s_col, 0.0)
            s_e = jnp.where(r4 == 3, s_row * down(e4, 1), 0.0)
            s_e_s = jnp.where((r4 == 3) & (c4 == 0), s_row * down(e_s, 1), 0.0)
        return eye_c - s2 - e4 + e_s + s_e - s_e_s

    row2 = lax.broadcasted_iota(jnp.int32, (LANES, LANES), 0)
    lane2 = lax.broadcasted_iota(jnp.int32, (LANES, LANES), 1)
    same_head = (row2 < HEAD_DIM) == (lane2 < HEAD_DIM)
    decay_scale = math.exp(-0.5)

    seq_dir = [(n, d) for n in range(n_seq) for d in range(N_DIR)]
    rk_all = {nd: rk_refs[nd[1]][nd[0]].astype(F32) for nd in seq_dir}
    kkr_all = {nd: rk_all[nd][:, D_RWKV:] * kk_ref[...] for nd in seq_dir}
    ssq_rows = _bdot(jnp.concatenate([kkr_all[nd] * kkr_all[nd] for nd in seq_dir], axis=0), ones_ref[...])
    z_w_rows, z_a_rows = [], []
    for d in range(N_DIR):
        ll_rows = jnp.concatenate([lwla_refs[d][n] for n in range(n_seq)], axis=0)
        z_w_rows.append(w0_ref[d:d + 1, :] + _bdot(jnp.tanh(ll_rows), w2_ref[d]))
        z_a_rows.append(a0_ref[d:d + 1, :] + _bdot(ll_rows, a2_ref[d]))

    probs = []
    for i, (n, d) in enumerate(seq_dir):
        rk = rk_all[(n, d)]
        r = rk[:, :D_RWKV]
        k = rk[:, D_RWKV:]
        v = v_refs[d][n]
        logw = -decay_scale * jax.nn.sigmoid(z_w_rows[d][n * C:(n + 1) * C])
        a = jax.nn.sigmoid(z_a_rows[d][n * C:(n + 1) * C])
        kd = k * (1.0 + (a - 1.0) * ka_ref[...])
        kkr = kkr_all[(n, d)]
        ssq = ssq_rows[i * C:(i + 1) * C]
        kk = kkr * lax.rsqrt(jnp.maximum(ssq, 1e-24))
        bvec = kk * a
        lw_hi, lw_lo = _split2(logw)
        tri = tri_ref[d]
        cum = (jnp.dot(tri, lw_hi, preferred_element_type=F32)
               + jnp.dot(tri, lw_lo, preferred_element_type=F32))
        cum_prev = cum - logw
        tot = cum[C - 1:C, :] if d == 0 else cum[0:1, :]
        kap_t = kk * jnp.exp(cum_prev)
        r_t = r * jnp.exp(cum)
        e_neg = jnp.exp(-cum)
        gam = jnp.exp(tot)
        b_t = bvec * e_neg
        k_t = kd * e_neg
        e_rem = gam * e_neg
        b_h = bvec * e_rem
        k_h = kd * e_rem
        if d == 0:
            strict, incl = row > s_idx, row >= s_idx
        else:
            strict, incl = row < s_idx, row <= s_idx

        for p in range(N_PAIRS):
            sl = slice(p * LANES, (p + 1) * LANES)
            dup_t = lambda x: jnp.where(same_head, jnp.concatenate([x[:, sl], x[:, sl]], axis=0).T, 0.0)
            probs.append(dict(n=n, d=d, p=p, sl=sl, strict=strict, incl=incl,
                              lhs=jnp.concatenate([kap_t[:, sl], r_t[:, sl]], axis=0).astype(BF16),
                              v=v[:, sl], w_lm=jnp.concatenate([dup_t(b_t), dup_t(k_t)], axis=1).astype(BF16),
                              t2=jnp.concatenate([b_h[:, sl], k_h[:, sl]], axis=0).T.astype(BF16),
                              gam_col=jnp.broadcast_to(gam[:, sl], (LANES, LANES)).T))

    bd = lambda y: _block_diag_rows(y, head0)
    for q in probs:
        q["h"] = h_scr[q["n"], q["d"], q["p"]]
        lm = jnp.dot(q["lhs"], q["w_lm"], preferred_element_type=F32)
        q["l_b"] = jnp.where(q["strict"], lm[:C, :LANES], 0.0)
        q["m_b"] = jnp.where(q["incl"], lm[C:, :LANES], 0.0)
        q["lm_k"] = jnp.concatenate([jnp.where(q["strict"], lm[:C, LANES:], 0.0),
                                     jnp.where(q["incl"], lm[C:, LANES:], 0.0)], axis=0)
    for q in probs:
        lmv = _bdot(q["lm_k"], bd(q["v"]))
        q["l_kv"], q["m_kv"] = lmv[:C], lmv[C:]
    for q in probs:
        q["kr_h"] = _bdot(q["lhs"], q["h"])
    for q in probs:
        l_b = q["l_b"]
        l_c = l_b[0:SUB]
        for jb in range(1, C // SUB):
            l_c = jnp.where(blk_c == jb, l_b[jb * SUB:(jb + 1) * SUB], l_c)
        q["l_c"] = l_c
        q["t_c"] = inverse_size4(l_c, q["d"] == 1)
    s = 4
    while s < SUB:
        off_mask = _same_block(row_c, col_c, 2 * s) & ~_same_block(row_c, col_c, s)
        for q in probs:
            q["et"] = _bdot(jnp.where(off_mask, q["l_c"], 0.0), bd_c(q["t_c"]))
        for q in probs:
            q["t_c"] = q["t_c"] - _bdot(q["t_c"], bd_c(q["et"]))
        s *= 2
    for q in probs:
        q["tinv"] = jnp.concatenate([jnp.where(blk_c == jb, q["t_c"], 0.0) for jb in range(C // SUB)], axis=0)
    while s < C:
        off_mask = _same_block(row, s_idx, 2 * s) & ~_same_block(row, s_idx, s)
        for q in probs:
            q["et"] = _bdot(jnp.where(off_mask, q["l_b"], 0.0), bd(q["tinv"]))
        for q in probs:
            q["tinv"] = q["tinv"] - _bdot(q["tinv"], bd(q["et"]))
        s *= 2
    for q in probs:
        q["u_n"] = _bdot(q["tinv"], bd(q["kr_h"][:C] + q["l_kv"]))
    for q in probs:
        y = q["kr_h"][C:] + q["m_kv"] - _bdot(q["m_b"], bd(q["u_n"]))
        y_refs[q["d"]][q["n"], :, q["sl"]] = y.astype(BF16)
    for q in probs:
        upd = _bdot(q["t2"], jnp.concatenate([(-q["u_n"]).astype(BF16), q["v"]], axis=0))
        h_scr[q["n"], q["d"], q["p"]] = q["gam_col"] * q["h"] + jnp.where(same_head, upd, 0.0)

    if write_state:
        @pl.when(j == n_chunks - 1)
        def _():
            for n, d, p in [(n, d, p) for n in range(n_seq) for d in range(N_DIR) for p in range(N_PAIRS)]:
                ht = h_scr[n, d, p].T
                sfin_refs[d][n, 0, PAIR * p] = ht[:HEAD_DIM, :HEAD_DIM]
                sfin_refs[d][n, 0, PAIR * p + 1] = ht[HEAD_DIM:, HEAD_DIM:]


def _scan_call(rk, v, lwla, wts, s0_bd, write_state):
    B, T, _ = rk.shape
    nc = T // CHUNK
    ns = min(SCAN_SEQS, B)
    assert B % ns == 0
    zero_init = s0_bd is None
    fwd = lambda b, j: (b, j, 0)
    bwd = lambda b, j: (b, nc - 1 - j, 0)
    full = lambda *shape: pl.BlockSpec(shape, lambda b, j: (0,) * len(shape))
    tok = lambda w, m: pl.BlockSpec((ns, CHUNK, w), m)
    in_specs = [tok(2 * D_RWKV, fwd), tok(D_RWKV, fwd), tok(LANES, fwd),
                tok(2 * D_RWKV, bwd), tok(D_RWKV, bwd), tok(LANES, bwd),
                full(N_DIR, D_RWKV), full(N_DIR, LANES, D_RWKV),
                full(N_DIR, D_RWKV), full(N_DIR, LANES, D_RWKV),
                full(1, D_RWKV), full(1, D_RWKV), full(N_DIR, CHUNK, CHUNK), full(D_RWKV, D_RWKV)]
    args = [rk, v, lwla, rk, v, lwla, wts["w0"], wts["w2"], wts["a0"], wts["a2"],
            wts["k_k"], wts["k_a"], wts["tri"], wts["ones_bd"]]
    state_block = (ns, N_DIR, N_PAIRS, LANES, LANES)
    if not zero_init:
        in_specs.append(pl.BlockSpec(state_block, lambda b, j: (b, 0, 0, 0, 0)))
        args.append(s0_bd)
    out_specs = [pl.BlockSpec((ns, CHUNK, D_RWKV), fwd), pl.BlockSpec((ns, CHUNK, D_RWKV), bwd)]
    out_shape = [jax.ShapeDtypeStruct((B, T, D_RWKV), BF16), jax.ShapeDtypeStruct((B, T, D_RWKV), BF16)]
    if write_state:
        final_block = (ns, 1, N_HEADS, HEAD_DIM, HEAD_DIM)
        out_specs += [pl.BlockSpec(final_block, lambda b, j: (b, 0, 0, 0, 0))] * N_DIR
        out_shape += [jax.ShapeDtypeStruct((B,) + final_block[1:], F32)] * N_DIR
    kern = functools.partial(_scan_kernel, zero_init=zero_init, write_state=write_state, n_chunks=nc,
                             n_seq=ns)
    return pl.pallas_call(
        kern, grid=(B // ns, nc), in_specs=in_specs, out_specs=out_specs, out_shape=out_shape,
        scratch_shapes=[pltpu.VMEM(state_block, F32)],
        compiler_params=pltpu.CompilerParams(dimension_semantics=("arbitrary", "arbitrary"),
                                             vmem_limit_bytes=VMEM_LIMIT),
        name="scan",
    )(*args)


def _fnet_kernel(fin_ref, gf_ref, dft_ref, cs_ref, wbd_ref, b_ref, o_ref, g_scr):
    u = pl.program_id(1)
    nb, seq_len, _ = fin_ref.shape
    tu = o_ref.shape[1]

    @pl.when(u == 0)
    def _():
        step = min(seq_len, ROW_TILE)
        for n in range(nb):
            for t0 in range(0, seq_len, step):
                fc = _bdot(fin_ref[n, t0:t0 + step, :], cs_ref[...])
                g_scr[n, t0:t0 + step, :] = fc[:, :D_FNET].astype(BF16)
                g_scr[n, seq_len + t0:seq_len + t0 + step, :] = fc[:, D_FNET:].astype(BF16)

    f_re = jnp.concatenate([jnp.dot(dft_ref[...], g_scr[n], preferred_element_type=F32) for n in range(nb)],
                           axis=0)
    f_out = _bdot(f_re, wbd_ref[...]) + b_ref[...]
    gate = _silu(gf_ref[...].astype(F32)).reshape(nb * tu, D_FNET)
    o_ref[...] = (f_out * gate).astype(BF16).reshape(nb, tu, D_FNET)


def _fnet_call(fin, gf, dft_bf16, cs_bf16, wbd_bf16, b_fnet):
    B, T, _ = fin.shape
    tu = min(T, ROW_TILE)
    nb = max(1, FNET_ROWS // T)
    assert B % nb == 0 and T % tu == 0
    return pl.pallas_call(
        _fnet_kernel,
        grid=(B // nb, T // tu),
        in_specs=[pl.BlockSpec((nb, T, D_FNET), lambda b, u: (b, 0, 0)),
                  pl.BlockSpec((nb, tu, D_FNET), lambda b, u: (b, u, 0)),
                  pl.BlockSpec((tu, 2 * T), lambda b, u: (u, 0)),
                  pl.BlockSpec((D_FNET, 2 * D_FNET), lambda b, u: (0, 0)),
                  pl.BlockSpec((D_FNET, D_FNET), lambda b, u: (0, 0)),
                  pl.BlockSpec((1, D_FNET), lambda b, u: (0, 0))],
        out_specs=pl.BlockSpec((nb, tu, D_FNET), lambda b, u: (b, u, 0)),
        out_shape=jax.ShapeDtypeStruct((B, T, D_FNET), BF16),
        scratch_shapes=[pltpu.VMEM((nb, 2 * T, D_FNET), BF16)],
        compiler_params=pltpu.CompilerParams(dimension_semantics=("arbitrary", "arbitrary"),
                                             vmem_limit_bytes=VMEM_LIMIT),
        name="fnet",
    )(fin, gf, dft_bf16, cs_bf16, wbd_bf16, b_fnet)


def _out_kernel(*refs, has_emb, final_norm):
    it = iter(refs)
    x_ref = next(it)
    emb_ref = next(it) if has_emb else None
    (mod_ref, yf_ref, yb_ref, rk_ref, v_ref, lwla_ref, grec_ref, fo_ref, a0_ref, a2_ref, ka_ref,
     rkw_ref, gng_ref, gnb_ref, avg_ref, ones_ref, wout_ref, fng_ref, o_ref) = (next(it) for _ in range(19))

    nb, tt, _ = x_ref.shape
    tm = nb * tt
    rows = lambda ref: ref[...].reshape(tm, ref.shape[-1])
    y = rows(yf_ref).astype(F32) + rows(yb_ref).astype(F32)
    mu = _bdot(y, avg_ref[...])
    dlt = y - mu
    var = _bdot(dlt * dlt, avg_ref[...])
    y_n = dlt * lax.rsqrt(var + GN_EPS) * gng_ref[...] + gnb_ref[...]
    rk = rows(rk_ref).astype(F32)
    r = rk[:, :D_RWKV]
    k = rk[:, D_RWKV:]
    ll = rows(lwla_ref)
    a_sum = (jax.nn.sigmoid(a0_ref[0:1, :] + _bdot(ll, a2_ref[0]))
             + jax.nn.sigmoid(a0_ref[1:2, :] + _bdot(ll, a2_ref[1])))
    k_sum = k * (2.0 + (a_sum - 2.0) * ka_ref[...])
    bonus = _bdot(r * k_sum * rkw_ref[...], ones_ref[...]) * rows(v_ref).astype(F32)
    rec_out = (y_n + bonus) * _silu(rows(grec_ref).astype(F32))
    mixed = jnp.concatenate([rec_out.astype(BF16), rows(fo_ref)], axis=-1)
    out = jnp.dot(mixed, wout_ref[...], preferred_element_type=F32)
    x = rows(x_ref)
    if has_emb:
        x = x + emb_ref[...]
    z = x + mod_ref[0, 2:3, :] * out
    if final_norm:
        ms = jnp.mean(z * z, axis=-1, keepdims=True)
        z = z * lax.rsqrt(ms + NORM_EPS) * fng_ref[...]
    o_ref[...] = z.reshape(nb, tt, D_MODEL)


def _out_call(x, emb, mod, y_f, y_b, rk, v, lwla, grec, fo, wts, w_out_bf16, final_norm_g, final_norm):
    B, T, _ = x.shape
    tt = min(T, ROW_TILE)
    nb = max(1, ROW_TILE // T)
    has_emb = emb is not None
    per_batch_mod = mod.shape[0] > 1
    assert B % nb == 0 and T % tt == 0 and (nb == 1 or not (has_emb or per_batch_mod))
    tok = lambda w: pl.BlockSpec((nb, tt, w), lambda b, i: (b, i, 0))
    full = lambda *shape: pl.BlockSpec(shape, lambda b, i: (0,) * len(shape))
    in_specs = [tok(D_MODEL)]
    args = [x]
    if has_emb:
        in_specs.append(pl.BlockSpec((tt, D_MODEL), lambda b, i: (i, 0)))
        args.append(emb)
    mod_map = (lambda b, i: (b, 0, 0)) if per_batch_mod else (lambda b, i: (0, 0, 0))
    in_specs += [pl.BlockSpec((1, 3, D_MODEL), mod_map), tok(D_RWKV), tok(D_RWKV), tok(2 * D_RWKV),
                 tok(D_RWKV), tok(LANES), tok(D_RWKV), tok(D_FNET),
                 full(N_DIR, D_RWKV), full(N_DIR, LANES, D_RWKV),
                 full(1, D_RWKV), full(1, D_RWKV), full(1, D_RWKV), full(1, D_RWKV),
                 full(D_RWKV, D_RWKV), full(D_RWKV, D_RWKV), full(D_MODEL, D_MODEL), full(1, D_MODEL)]
    args += [mod, y_f, y_b, rk, v, lwla, grec, fo, wts["a0"], wts["a2"], wts["k_a"],
             wts["r_k"], wts["gn_g"], wts["gn_b"], wts["avg_bd"], wts["ones_bd"], w_out_bf16, final_norm_g]
    return pl.pallas_call(
        functools.partial(_out_kernel, has_emb=has_emb, final_norm=final_norm),
        grid=(B // nb, T // tt), in_specs=in_specs,
        out_specs=tok(D_MODEL),
        out_shape=jax.ShapeDtypeStruct((B, T, D_MODEL), F32),
        compiler_params=pltpu.CompilerParams(dimension_semantics=("arbitrary", "arbitrary"),
                                             vmem_limit_bytes=VMEM_LIMIT),
        name="out_proj",
    )(*args)


def _dft_table(seq_len):
    idx = np.arange(seq_len, dtype=np.int64)
    ang = 2.0 * np.pi * ((idx[:, None] * idx[None, :]) % seq_len).astype(np.float64) / seq_len
    scale = 1.0 / math.sqrt(seq_len)
    return np.concatenate([np.cos(ang) * scale, -np.sin(ang) * scale], axis=1).astype(np.float32)


def _channel_dft_table():
    n = FNET_GROUP
    idx = np.arange(n, dtype=np.int64)
    ang = 2.0 * np.pi * ((idx[:, None] * idx[None, :]) % n).astype(np.float64) / n
    c = np.cos(ang) / math.sqrt(n)
    s = np.sin(ang) / math.sqrt(n)
    eye = np.eye(D_FNET // n)
    return np.concatenate([np.kron(eye, c), np.kron(eye, s)], axis=1).astype(np.float32)


def _sincos_2d(n_tokens):
    rows = n_tokens // GRID_W
    pos = np.arange(rows * GRID_W)
    row = (pos // GRID_W).astype(np.float32)
    col = (pos % GRID_W).astype(np.float32)
    quarter = D_MODEL // 4
    freq = np.exp(np.float32(-math.log(POS_BASE)) * np.arange(quarter, dtype=np.float32) / np.float32(quarter))
    ang_r = row[:, None] * freq
    ang_c = col[:, None] * freq
    return np.concatenate([np.sin(ang_r), np.cos(ang_r), np.sin(ang_c), np.cos(ang_c)], axis=-1).astype(np.float32)


def _head_block_matrix(value):
    blk = np.kron(np.eye(N_HEADS), np.ones((HEAD_DIM, HEAD_DIM))) * value
    return jnp.asarray(blk, dtype=BF16)


def _pad_lora(w, row_offset):
    rows = [jnp.pad(w[d], ((row_offset + d * LORA, LANES - row_offset - (d + 1) * LORA), (0, 0)))
            for d in range(N_DIR)]
    return jnp.stack(rows).astype(BF16)


def _layer_weights(l, w0, w2, a0, a2, k_k, k_a, r_k, gn_g, gn_b):
    w2_p = _pad_lora(w2[l], 0)
    a2_p = _pad_lora(a2[l], N_DIR * LORA)
    tri_f = np.tril(np.ones((CHUNK, CHUNK)))
    tri = jnp.asarray(np.stack([tri_f, tri_f.T]), dtype=BF16)
    return dict(w0=w0[l], w2=w2_p, a0=a0[l], a2=a2_p,
                k_k=k_k[l][None], k_a=k_a[l][None], r_k=r_k[l].reshape(1, D_RWKV),
                gn_g=gn_g[l].reshape(1, D_RWKV), gn_b=gn_b[l].reshape(1, D_RWKV), tri=tri,
                ones_bd=_head_block_matrix(1.0), avg_bd=_head_block_matrix(1.0 / HEAD_DIM))


def _state_to_block_diag(s_f, s_b):
    def one(s):
        h = jnp.swapaxes(s.astype(F32), -1, -2)
        b = h.shape[0]
        h = h.reshape(b, N_PAIRS, PAIR, HEAD_DIM, HEAD_DIM)
        z = jnp.zeros_like(h[:, :, 0])
        top = jnp.concatenate([h[:, :, 0], z], axis=-1)
        bot = jnp.concatenate([z, h[:, :, 1]], axis=-1)
        return jnp.concatenate([top, bot], axis=-2)
    return jnp.stack([one(s_f), one(s_b)], axis=1)


def kernel(x_prompt, x_sample, state_rwkv_fwd, state_rwkv_bwd, c, c_ctx, w_ada, b_ada, norm_g, w_in,
           mu_shift, w0, w2, a0, a2, k_k, k_a, r_k, gn_g, gn_b, w_fnet, b_fnet, w_out, final_norm_g):
    depth = w_in.shape[0]
    n_dec = c.shape[0]
    assert n_dec + 1 <= 8
    bp, tp, _ = x_prompt.shape
    bs, ts, _ = x_sample.shape
    cvec = jnp.concatenate([c_ctx[None], c, jnp.zeros((7 - n_dec, D_MODEL), F32)], axis=0)
    emb = jnp.asarray(_sincos_2d(ts)).astype(x_sample.dtype)
    cs_tab = jnp.asarray(_channel_dft_table()).astype(BF16)
    dft_p = jnp.asarray(_dft_table(tp)).astype(BF16)
    dft_s = jnp.asarray(_dft_table(ts)).astype(BF16)
    fng = final_norm_g[None]

    xp, xs = x_prompt, x_sample
    new_f, new_b = [], []
    for l in range(depth):
        mod = _mod_call(cvec, w_ada[l], b_ada[l][None]).reshape(8, 3, D_MODEL)
        mod_ctx, mod_lat = mod[0:1], mod[1:1 + n_dec]
        wts = _layer_weights(l, w0, w2, a0, a2, k_k, k_a, r_k, gn_g, gn_b)
        w_in_b = w_in[l].astype(BF16)
        w_out_b = w_out[l].astype(BF16)
        n_grp = w_fnet.shape[1]
        wbd = (w_fnet[l][:, :, None, :] * jnp.eye(n_grp, dtype=F32)[:, None, :, None]).reshape(
            D_FNET, D_FNET).astype(BF16)
        ng, mu, bf = norm_g[l][None], mu_shift[l][None], b_fnet[l][None]
        emb_l = emb if l == 0 else None

        rk, v, lwla, grec, fin, gf = _in_proj_call(xp, None, mod_ctx, ng, w_in_b, mu)
        y_f, y_b, s_f, s_b = _scan_call(rk, v, lwla, wts, None, True)
        fo = _fnet_call(fin, gf, dft_p, cs_tab, wbd, bf)
        last = l == depth - 1
        xp = _out_call(xp, None, mod_ctx, y_f, y_b, rk, v, lwla, grec, fo, wts, w_out_b, fng, last)
        new_f.append(s_f)
        new_b.append(s_b)

        rk, v, lwla, grec, fin, gf = _in_proj_call(xs, emb_l, mod_lat, ng, w_in_b, mu)
        s0 = _state_to_block_diag(state_rwkv_fwd[:, l], state_rwkv_bwd[:, l])
        y_f, y_b = _scan_call(rk, v, lwla, wts, s0, False)
        fo = _fnet_call(fin, gf, dft_s, cs_tab, wbd, bf)
        xs = _out_call(xs, emb_l, mod_lat, y_f, y_b, rk, v, lwla, grec, fo, wts, w_out_b, fng, last)
    return (xp, xs, jnp.concatenate(new_f, axis=1), jnp.concatenate(new_b, axis=1))
```

```python
import functools
import math

import numpy as np
import jax
import jax.numpy as jnp
from jax import lax
from jax.experimental import pallas as pl
from jax.experimental.pallas import tpu as pltpu

F32 = jnp.float32
BF16 = jnp.bfloat16

D_MODEL = 1024
GRID_W = 64
D_RWKV = 512
D_FNET = D_MODEL - D_RWKV
HEAD_DIM = 64
N_HEADS = D_RWKV // HEAD_DIM
FNET_GROUP = 64
LORA = 32
N_DIR = 2
D_SHIFT = 3 * D_RWKV + N_DIR * 2 * LORA
D_IN = D_SHIFT + D_RWKV + 2 * D_FNET
NORM_EPS = 1e-6
GN_EPS = 64e-5
POS_BASE = 10000.0

LANES = 128
PAIR = LANES // HEAD_DIM
N_PAIRS = N_HEADS // PAIR
CHUNK = 64
SUB = 16
SUB_SHIFT = SUB.bit_length() - 1
SCAN_SEQS = 4
ROW_TILE = 512
FNET_ROWS = 1024
HALO = 8
VMEM_LIMIT = 56 * 1024 * 1024


def _silu(x):
    return x * jax.nn.sigmoid(x)


def _bdot(a, b):
    return jnp.dot(a.astype(BF16), b.astype(BF16), preferred_element_type=F32)


def _split2(x):
    hi = x.astype(BF16)
    lo = (x - hi.astype(F32)).astype(BF16)
    return hi, lo


def _mod_kernel(c_ref, w_ref, b_ref, o_ref):
    @pl.when(pl.program_id(0) == 0)
    def _():
        o_ref[...] = jnp.broadcast_to(b_ref[...], o_ref.shape)

    s_hi, s_lo = _split2(_silu(c_ref[...]))
    w = w_ref[...].astype(BF16)
    o_ref[...] += jnp.dot(s_hi, w, preferred_element_type=F32) + jnp.dot(s_lo, w, preferred_element_type=F32)


def _mod_call(cvec, w_ada, b_ada):
    rows = D_MODEL // 4
    return pl.pallas_call(
        _mod_kernel,
        grid=(D_MODEL // rows,),
        in_specs=[pl.BlockSpec((8, rows), lambda i: (0, i)),
                  pl.BlockSpec((rows, 3 * D_MODEL), lambda i: (i, 0)),
                  pl.BlockSpec((1, 3 * D_MODEL), lambda i: (0, 0))],
        out_specs=pl.BlockSpec((8, 3 * D_MODEL), lambda i: (0, 0)),
        out_shape=jax.ShapeDtypeStruct((8, 3 * D_MODEL), F32),
        compiler_params=pltpu.CompilerParams(dimension_semantics=("arbitrary",),
                                             vmem_limit_bytes=VMEM_LIMIT),
        name="mod",
    )(cvec, w_ada, b_ada)


def _modulated_norm(x, g, scale, shift):
    ms = jnp.mean(x * x, axis=-1, keepdims=True)
    y = x * lax.rsqrt(ms + NORM_EPS) * g
    return y * (1.0 + scale) + shift


def _in_proj_kernel(*refs, has_emb, has_halo, n_tiles):
    it = iter(refs)
    x_ref = next(it)
    xp_ref = next(it) if has_halo else None
    xn_ref = next(it) if has_halo else None
    emb_ref = next(it) if has_emb else None
    embp_ref = next(it) if (has_emb and has_halo) else None
    embn_ref = next(it) if (has_emb and has_halo) else None
    mod_ref, g_ref, w_ref, mu_ref = next(it), next(it), next(it), next(it)
    rk_ref, v_ref, lwla_ref, grec_ref, fin_ref, gf_ref = (next(it) for _ in range(6))

    i = pl.program_id(1)
    g = g_ref[...]
    shift = mod_ref[0, 0:1, :]
    scale = mod_ref[0, 1:2, :]
    nb, tt, _ = x_ref.shape
    tm = nb * tt
    x = x_ref[...].reshape(tm, D_MODEL)
    if has_emb:
        x = x + emb_ref[...]
    h = _modulated_norm(x, g, scale, shift)
    p = jnp.dot(h.astype(BF16), w_ref[...], preferred_element_type=F32)
    ps = p[:, :D_SHIFT]
    if has_halo:
        xh = jnp.concatenate([xp_ref[0], xn_ref[0]], axis=0)
        if has_emb:
            xh = xh + jnp.concatenate([embp_ref[...], embn_ref[...]], axis=0)
        hh = _modulated_norm(xh, g, scale, shift)
        ph = jnp.dot(hh.astype(BF16), w_ref[:, :D_SHIFT], preferred_element_type=F32)
        prev_row = jnp.where(i > 0, ph[HALO - 1:HALO, :], 0.0)
        next_row = jnp.where(i < n_tiles - 1, ph[HALO:HALO + 1, :], 0.0)
    else:
        prev_row = jnp.zeros((1, D_SHIFT), F32)
        next_row = jnp.zeros((1, D_SHIFT), F32)
    pos = lax.rem(lax.broadcasted_iota(jnp.int32, (tm, D_SHIFT), 0), tt)
    prev = jnp.where(pos == 0, prev_row, pltpu.roll(ps, 1, 0))
    nxt = jnp.where(pos == tt - 1, next_row, pltpu.roll(ps, tm - 1, 0))
    p_rec = ps + mu_ref[...] * (0.5 * (prev + nxt) - ps)

    def put(ref, val):
        ref[...] = val.astype(ref.dtype).reshape(ref.shape)

    put(rk_ref, p_rec[:, :2 * D_RWKV])
    put(v_ref, p_rec[:, 2 * D_RWKV:3 * D_RWKV])
    put(lwla_ref, p_rec[:, 3 * D_RWKV:])
    put(grec_ref, p[:, D_SHIFT:D_SHIFT + D_RWKV])
    put(fin_ref, p[:, D_SHIFT + D_RWKV:D_SHIFT + D_RWKV + D_FNET])
    put(gf_ref, p[:, D_SHIFT + D_RWKV + D_FNET:])


def _in_proj_call(x, emb, mod, norm_g, w_in_bf16, mu):
    B, T, _ = x.shape
    tt = min(T, ROW_TILE)
    nb = max(1, ROW_TILE // T)
    n_tiles = T // tt
    has_halo = n_tiles > 1
    has_emb = emb is not None
    per_batch_mod = mod.shape[0] > 1
    assert B % nb == 0 and T % tt == 0 and (nb == 1 or not (has_emb or per_batch_mod))
    tm = tt
    blocks_per_tile = tm // HALO
    last_halo_block = T // HALO - 1

    in_specs = [pl.BlockSpec((nb, tt, D_MODEL), lambda b, i: (b, i, 0))]
    args = [x]
    if has_halo:
        in_specs += [
            pl.BlockSpec((1, HALO, D_MODEL), lambda b, i: (b, jnp.maximum(i * blocks_per_tile - 1, 0), 0)),
            pl.BlockSpec((1, HALO, D_MODEL),
                         lambda b, i: (b, jnp.minimum((i + 1) * blocks_per_tile, last_halo_block), 0))]
        args += [x, x]
    if has_emb:
        in_specs.append(pl.BlockSpec((tm, D_MODEL), lambda b, i: (i, 0)))
        args.append(emb)
        if has_halo:
            in_specs += [
                pl.BlockSpec((HALO, D_MODEL), lambda b, i: (jnp.maximum(i * blocks_per_tile - 1, 0), 0)),
                pl.BlockSpec((HALO, D_MODEL),
                             lambda b, i: (jnp.minimum((i + 1) * blocks_per_tile, last_halo_block), 0))]
            args += [emb, emb]
    mod_map = (lambda b, i: (b, 0, 0)) if per_batch_mod else (lambda b, i: (0, 0, 0))
    in_specs += [pl.BlockSpec((1, 3, D_MODEL), mod_map),
                 pl.BlockSpec((1, D_MODEL), lambda b, i: (0, 0)),
                 pl.BlockSpec((D_MODEL, D_IN), lambda b, i: (0, 0), pipeline_mode=pl.Buffered(1)),
                 pl.BlockSpec((1, D_SHIFT), lambda b, i: (0, 0))]
    args += [mod, norm_g, w_in_bf16, mu]
    outs = ((2 * D_RWKV, BF16), (D_RWKV, BF16), (N_DIR * 2 * LORA, F32), (D_RWKV, BF16), (D_FNET, BF16),
            (D_FNET, BF16))
    out_specs = [pl.BlockSpec((nb, tt, w), lambda b, i: (b, i, 0)) for w, _ in outs]
    out_shape = [jax.ShapeDtypeStruct((B, T, w), dt) for w, dt in outs]
    kern = functools.partial(_in_proj_kernel, has_emb=has_emb, has_halo=has_halo, n_tiles=n_tiles)
    return pl.pallas_call(
        kern, grid=(B // nb, n_tiles), in_specs=in_specs, out_specs=out_specs, out_shape=out_shape,
        compiler_params=pltpu.CompilerParams(dimension_semantics=("arbitrary", "arbitrary"),
                                             vmem_limit_bytes=VMEM_LIMIT),
        name="in_proj",
    )(*args)


def _same_block(i, j, size):
    shift = size.bit_length() - 1
    return jnp.right_shift(i, shift) == jnp.right_shift(j, shift)


def _block_diag_rows(y, head0_lanes):
    return jnp.concatenate([jnp.where(head0_lanes, y, 0.0), jnp.where(head0_lanes, 0.0, y)], axis=0)


def _scan_kernel(*refs, zero_init, write_state, n_chunks, n_seq):
    it = iter(refs)
    rk_refs, v_refs, lwla_refs = [None, None], [None, None], [None, None]
    for d in range(N_DIR):
        rk_refs[d], v_refs[d], lwla_refs[d] = next(it), next(it), next(it)
    w0_ref, w2_ref, a0_ref, a2_ref = (next(it) for _ in range(4))
    kk_ref, ka_ref, tri_ref, ones_ref = (next(it) for _ in range(4))
    s0_ref = None if zero_init else next(it)
    y_refs = [next(it), next(it)]
    sfin_refs = [next(it), next(it)] if write_state else None
    h_scr = next(it)

    j = pl.program_id(1)

    @pl.when(j == 0)
    def _():
        if zero_init:
            h_scr[...] = jnp.zeros_like(h_scr)
        else:
            h_scr[...] = s0_ref[...]

    C = CHUNK
    row = lax.broadcasted_iota(jnp.int32, (C, LANES), 0)
    lane = lax.broadcasted_iota(jnp.int32, (C, LANES), 1)
    s_idx = jnp.bitwise_and(lane, HEAD_DIM - 1)
    head0 = lane < HEAD_DIM
    row_c = lax.broadcasted_iota(jnp.int32, (SUB, LANES), 0)
    lane_c = lax.broadcasted_iota(jnp.int32, (SUB, LANES), 1)
    col_c = jnp.bitwise_and(lane_c, SUB - 1)
    blk_c = jnp.right_shift(jnp.bitwise_and(lane_c, HEAD_DIM - 1), SUB_SHIFT)
    lane_blk_c = jnp.right_shift(lane_c, SUB_SHIFT)
    eye_c = (row_c == col_c).astype(F32)

    def bd_c(y):
        return jnp.concatenate([jnp.where(lane_blk_c == g, y, 0.0) for g in range(LANES // SUB)], axis=0)

    def inverse_size4(l_c, upper):
        r4 = jnp.bitwise_and(row_c, 3)
        c4 = jnp.bitwise_and(col_c, 3)
        down, up = (lambda x, k: pltpu.roll(x, k, 0)), (lambda x, k: pltpu.roll(x, SUB - k, 0))
        right, left = (lambda x, k: pltpu.roll(x, k, 1)), (lambda x, k: pltpu.roll(x, LANES - k, 1))
        s2 = jnp.where(_same_block(row_c, col_c, 2), l_c, 0.0)
        e4 = jnp.where(_same_block(row_c, col_c, 4), l_c, 0.0) - s2
        if upper:
            s_col = jnp.where(r4 == 1, up(s2, 1), up(s2, 2))
            s_row = jnp.where(c4 == 2, right(s2, 1), right(s2, 2))
            e_s = jnp.where(c4 == 3, right(e4, 1) * s_col, 0.0)
            s_e = jnp.where(r4 == 0, s_row * up(e4, 1), 0.0)
            s_e_s = jnp.where((r4 == 0) & (c4 == 3), s_row * up(e_s, 1), 0.0)
        else:
            s_col = jnp.where(r4 == 2, down(s2, 1), down(s2, 2))
            s_row = jnp.where(c4 == 1, left(s2, 1), left(s2, 2))
            e_s = jnp.where(c4 == 0, left(e4, 1) * s_col, 0.0)
            s_e = jnp.where(r4 == 3, s_row * down(e4, 1), 0.0)
            s_e_s = jnp.where((r4 == 3) & (c4 == 0), s_row * down(e_s, 1), 0.0)
        return eye_c - s2 - e4 + e_s + s_e - s_e_s

    row2 = lax.broadcasted_iota(jnp.int32, (LANES, LANES), 0)
    lane2 = lax.broadcasted_iota(jnp.int32, (LANES, LANES), 1)
    same_head = (row2 < HEAD_DIM) == (lane2 < HEAD_DIM)
    decay_scale = math.exp(-0.5)

    seq_dir = [(n, d) for n in range(n_seq) for d in range(N_DIR)]
    rk_all = {nd: rk_refs[nd[1]][nd[0]].astype(F32) for nd in seq_dir}
    kkr_all = {nd: rk_all[nd][:, D_RWKV:] * kk_ref[...] for nd in seq_dir}
    ssq_rows = _bdot(jnp.concatenate([kkr_all[nd] * kkr_all[nd] for nd in seq_dir], axis=0), ones_ref[...])
    z_w_rows, z_a_rows = [], []
    for d in range(N_DIR):
        ll_rows = jnp.concatenate([lwla_refs[d][n] for n in range(n_seq)], axis=0)
        z_w_rows.append(w0_ref[d:d + 1, :] + _bdot(jnp.tanh(ll_rows), w2_ref[d]))
        z_a_rows.append(a0_ref[d:d + 1, :] + _bdot(ll_rows, a2_ref[d]))

    probs = []
    for i, (n, d) in enumerate(seq_dir):
        rk = rk_all[(n, d)]
        r = rk[:, :D_RWKV]
        k = rk[:, D_RWKV:]
        v = v_refs[d][n]
        logw = -decay_scale * jax.nn.sigmoid(z_w_rows[d][n * C:(n + 1) * C])
        a = jax.nn.sigmoid(z_a_rows[d][n * C:(n + 1) * C])
        kd = k * (1.0 + (a - 1.0) * ka_ref[...])
        kkr = kkr_all[(n, d)]
        ssq = ssq_rows[i * C:(i + 1) * C]
        kk = kkr * lax.rsqrt(jnp.maximum(ssq, 1e-24))
        bvec = kk * a
        lw_hi, lw_lo = _split2(logw)
        tri = tri_ref[d]
        cum = (jnp.dot(tri, lw_hi, preferred_element_type=F32)
               + jnp.dot(tri, lw_lo, preferred_element_type=F32))
        cum_prev = cum - logw
        tot = cum[C - 1:C, :] if d == 0 else cum[0:1, :]
        kap_t = kk * jnp.exp(cum_prev)
        r_t = r * jnp.exp(cum)
        e_neg = jnp.exp(-cum)
        gam = jnp.exp(tot)
        b_t = bvec * e_neg
        k_t = kd * e_neg
        e_rem = gam * e_neg
        b_h = bvec * e_rem
        k_h = kd * e_rem
        if d == 0:
            strict, incl = row > s_idx, row >= s_idx
        else:
            strict, incl = row < s_idx, row <= s_idx

        for p in range(N_PAIRS):
            sl = slice(p * LANES, (p + 1) * LANES)
            dup_t = lambda x: jnp.where(same_head, jnp.concatenate([x[:, sl], x[:, sl]], axis=0).T, 0.0)
            probs.append(dict(n=n, d=d, p=p, sl=sl, strict=strict, incl=incl,
                              lhs=jnp.concatenate([kap_t[:, sl], r_t[:, sl]], axis=0).astype(BF16),
                              v=v[:, sl], w_lm=jnp.concatenate([dup_t(b_t), dup_t(k_t)], axis=1).astype(BF16),
                              t2=jnp.concatenate([b_h[:, sl], k_h[:, sl]], axis=0).T.astype(BF16),
                              gam_col=jnp.broadcast_to(gam[:, sl], (LANES, LANES)).T))

    bd = lambda y: _block_diag_rows(y, head0)
    for q in probs:
        q["h"] = h_scr[q["n"], q["d"], q["p"]]
        lm = jnp.dot(q["lhs"], q["w_lm"], preferred_element_type=F32)
        q["l_b"] = jnp.where(q["strict"], lm[:C, :LANES], 0.0)
        q["m_b"] = jnp.where(q["incl"], lm[C:, :LANES], 0.0)
        q["lm_k"] = jnp.concatenate([jnp.where(q["strict"], lm[:C, LANES:], 0.0),
                                     jnp.where(q["incl"], lm[C:, LANES:], 0.0)], axis=0)
    for q in probs:
        lmv = _bdot(q["lm_k"], bd(q["v"]))
        q["l_kv"], q["m_kv"] = lmv[:C], lmv[C:]
    for q in probs:
        q["kr_h"] = _bdot(q["lhs"], q["h"])
    for q in probs:
        l_b = q["l_b"]
        l_c = l_b[0:SUB]
        for jb in range(1, C // SUB):
            l_c = jnp.where(blk_c == jb, l_b[jb * SUB:(jb + 1) * SUB], l_c)
        q["l_c"] = l_c
        q["t_c"] = inverse_size4(l_c, q["d"] == 1)
    s = 4
    while s < SUB:
        off_mask = _same_block(row_c, col_c, 2 * s) & ~_same_block(row_c, col_c, s)
        for q in probs:
            q["et"] = _bdot(jnp.where(off_mask, q["l_c"], 0.0), bd_c(q["t_c"]))
        for q in probs:
            q["t_c"] = q["t_c"] - _bdot(q["t_c"], bd_c(q["et"]))
        s *= 2
    for q in probs:
        q["tinv"] = jnp.concatenate([jnp.where(blk_c == jb, q["t_c"], 0.0) for jb in range(C // SUB)], axis=0)
    while s < C:
        off_mask = _same_block(row, s_idx, 2 * s) & ~_same_block(row, s_idx, s)
        for q in probs:
            q["et"] = _bdot(jnp.where(off_mask, q["l_b"], 0.0), bd(q["tinv"]))
        for q in probs:
            q["tinv"] = q["tinv"] - _bdot(q["tinv"], bd(q["et"]))
        s *= 2
    for q in probs:
        q["u_n"] = _bdot(q["tinv"], bd(q["kr_h"][:C] + q["l_kv"]))
    for q in probs:
        y = q["kr_h"][C:] + q["m_kv"] - _bdot(q["m_b"], bd(q["u_n"]))
        y_refs[q["d"]][q["n"], :, q["sl"]] = y.astype(BF16)
    for q in probs:
        upd = _bdot(q["t2"], jnp.concatenate([(-q["u_n"]).astype(BF16), q["v"]], axis=0))
        h_scr[q["n"], q["d"], q["p"]] = q["gam_col"] * q["h"] + jnp.where(same_head, upd, 0.0)

    if write_state:
        @pl.when(j == n_chunks - 1)
        def _():
            for n, d, p in [(n, d, p) for n in range(n_seq) for d in range(N_DIR) for p in range(N_PAIRS)]:
                ht = h_scr[n, d, p].T
                sfin_refs[d][n, 0, PAIR * p] = ht[:HEAD_DIM, :HEAD_DIM]
                sfin_refs[d][n, 0, PAIR * p + 1] = ht[HEAD_DIM:, HEAD_DIM:]


def _scan_call(rk, v, lwla, wts, s0_bd, write_state):
    B, T, _ = rk.shape
    nc = T // CHUNK
    ns = min(SCAN_SEQS, B)
    assert B % ns == 0
    zero_init = s0_bd is None
    fwd = lambda b, j: (b, j, 0)
    bwd = lambda b, j: (b, nc - 1 - j, 0)
    full = lambda *shape: pl.BlockSpec(shape, lambda b, j: (0,) * len(shape))
    tok = lambda w, m: pl.BlockSpec((ns, CHUNK, w), m)
    in_specs = [tok(2 * D_RWKV, fwd), tok(D_RWKV, fwd), tok(LANES, fwd),
                tok(2 * D_RWKV, bwd), tok(D_RWKV, bwd), tok(LANES, bwd),
                full(N_DIR, D_RWKV), full(N_DIR, LANES, D_RWKV),
                full(N_DIR, D_RWKV), full(N_DIR, LANES, D_RWKV),
                full(1, D_RWKV), full(1, D_RWKV), full(N_DIR, CHUNK, CHUNK), full(D_RWKV, D_RWKV)]
    args = [rk, v, lwla, rk, v, lwla, wts["w0"], wts["w2"], wts["a0"], wts["a2"],
            wts["k_k"], wts["k_a"], wts["tri"], wts["ones_bd"]]
    state_block = (ns, N_DIR, N_PAIRS, LANES, LANES)
    if not zero_init:
        in_specs.append(pl.BlockSpec(state_block, lambda b, j: (b, 0, 0, 0, 0)))
        args.append(s0_bd)
    out_specs = [pl.BlockSpec((ns, CHUNK, D_RWKV), fwd), pl.BlockSpec((ns, CHUNK, D_RWKV), bwd)]
    out_shape = [jax.ShapeDtypeStruct((B, T, D_RWKV), BF16), jax.ShapeDtypeStruct((B, T, D_RWKV), BF16)]
    if write_state:
        final_block = (ns, 1, N_HEADS, HEAD_DIM, HEAD_DIM)
        out_specs += [pl.BlockSpec(final_block, lambda b, j: (b, 0, 0, 0, 0))] * N_DIR
        out_shape += [jax.ShapeDtypeStruct((B,) + final_block[1:], F32)] * N_DIR
    kern = functools.partial(_scan_kernel, zero_init=zero_init, write_state=write_state, n_chunks=nc,
                             n_seq=ns)
    return pl.pallas_call(
        kern, grid=(B // ns, nc), in_specs=in_specs, out_specs=out_specs, out_shape=out_shape,
        scratch_shapes=[pltpu.VMEM(state_block, F32)],
        compiler_params=pltpu.CompilerParams(dimension_semantics=("arbitrary", "arbitrary"),
                                             vmem_limit_bytes=VMEM_LIMIT),
        name="scan",
    )(*args)


def _fnet_kernel(fin_ref, gf_ref, dft_ref, cs_ref, wbd_ref, b_ref, o_ref, g_scr):
    u = pl.program_id(1)
    nb, seq_len, _ = fin_ref.shape
    tu = o_ref.shape[1]

    @pl.when(u == 0)
    def _():
        step = min(seq_len, ROW_TILE)
        for n in range(nb):
            for t0 in range(0, seq_len, step):
                fc = _bdot(fin_ref[n, t0:t0 + step, :], cs_ref[...])
                g_scr[n, t0:t0 + step, :] = fc[:, :D_FNET].astype(BF16)
                g_scr[n, seq_len + t0:seq_len + t0 + step, :] = fc[:, D_FNET:].astype(BF16)

    f_re = jnp.concatenate([jnp.dot(dft_ref[...], g_scr[n], preferred_element_type=F32) for n in range(nb)],
                           axis=0)
    f_out = _bdot(f_re, wbd_ref[...]) + b_ref[...]
    gate = _silu(gf_ref[...].astype(F32)).reshape(nb * tu, D_FNET)
    o_ref[...] = (f_out * gate).astype(BF16).reshape(nb, tu, D_FNET)


def _fnet_call(fin, gf, dft_bf16, cs_bf16, wbd_bf16, b_fnet):
    B, T, _ = fin.shape
    tu = min(T, ROW_TILE)
    nb = max(1, FNET_ROWS // T)
    assert B % nb == 0 and T % tu == 0
    return pl.pallas_call(
        _fnet_kernel,
        grid=(B // nb, T // tu),
        in_specs=[pl.BlockSpec((nb, T, D_FNET), lambda b, u: (b, 0, 0)),
                  pl.BlockSpec((nb, tu, D_FNET), lambda b, u: (b, u, 0)),
                  pl.BlockSpec((tu, 2 * T), lambda b, u: (u, 0)),
                  pl.BlockSpec((D_FNET, 2 * D_FNET), lambda b, u: (0, 0)),
                  pl.BlockSpec((D_FNET, D_FNET), lambda b, u: (0, 0)),
                  pl.BlockSpec((1, D_FNET), lambda b, u: (0, 0))],
        out_specs=pl.BlockSpec((nb, tu, D_FNET), lambda b, u: (b, u, 0)),
        out_shape=jax.ShapeDtypeStruct((B, T, D_FNET), BF16),
        scratch_shapes=[pltpu.VMEM((nb, 2 * T, D_FNET), BF16)],
        compiler_params=pltpu.CompilerParams(dimension_semantics=("arbitrary", "arbitrary"),
                                             vmem_limit_bytes=VMEM_LIMIT),
        name="fnet",
    )(fin, gf, dft_bf16, cs_bf16, wbd_bf16, b_fnet)


def _out_kernel(*refs, has_emb, final_norm):
    it = iter(refs)
    x_ref = next(it)
    emb_ref = next(it) if has_emb else None
    (mod_ref, yf_ref, yb_ref, rk_ref, v_ref, lwla_ref, grec_ref, fo_ref, a0_ref, a2_ref, ka_ref,
     rkw_ref, gng_ref, gnb_ref, avg_ref, ones_ref, wout_ref, fng_ref, o_ref) = (next(it) for _ in range(19))

    nb, tt, _ = x_ref.shape
    tm = nb * tt
    rows = lambda ref: ref[...].reshape(tm, ref.shape[-1])
    y = rows(yf_ref).astype(F32) + rows(yb_ref).astype(F32)
    mu = _bdot(y, avg_ref[...])
    dlt = y - mu
    var = _bdot(dlt * dlt, avg_ref[...])
    y_n = dlt * lax.rsqrt(var + GN_EPS) * gng_ref[...] + gnb_ref[...]
    rk = rows(rk_ref).astype(F32)
    r = rk[:, :D_RWKV]
    k = rk[:, D_RWKV:]
    ll = rows(lwla_ref)
    a_sum = (jax.nn.sigmoid(a0_ref[0:1, :] + _bdot(ll, a2_ref[0]))
             + jax.nn.sigmoid(a0_ref[1:2, :] + _bdot(ll, a2_ref[1])))
    k_sum = k * (2.0 + (a_sum - 2.0) * ka_ref[...])
    bonus = _bdot(r * k_sum * rkw_ref[...], ones_ref[...]) * rows(v_ref).astype(F32)
    rec_out = (y_n + bonus) * _silu(rows(grec_ref).astype(F32))
    mixed = jnp.concatenate([rec_out.astype(BF16), rows(fo_ref)], axis=-1)
    out = jnp.dot(mixed, wout_ref[...], preferred_element_type=F32)
    x = rows(x_ref)
    if has_emb:
        x = x + emb_ref[...]
    z = x + mod_ref[0, 2:3, :] * out
    if final_norm:
        ms = jnp.mean(z * z, axis=-1, keepdims=True)
        z = z * lax.rsqrt(ms + NORM_EPS) * fng_ref[...]
    o_ref[...] = z.reshape(nb, tt, D_MODEL)


def _out_call(x, emb, mod, y_f, y_b, rk, v, lwla, grec, fo, wts, w_out_bf16, final_norm_g, final_norm):
    B, T, _ = x.shape
    tt = min(T, ROW_TILE)
    nb = max(1, ROW_TILE // T)
    has_emb = emb is not None
    per_batch_mod = mod.shape[0] > 1
    assert B % nb == 0 and T % tt == 0 and (nb == 1 or not (has_emb or per_batch_mod))
    tok = lambda w: pl.BlockSpec((nb, tt, w), lambda b, i: (b, i, 0))
    full = lambda *shape: pl.BlockSpec(shape, lambda b, i: (0,) * len(shape))
    in_specs = [tok(D_MODEL)]
    args = [x]
    if has_emb:
        in_specs.append(pl.BlockSpec((tt, D_MODEL), lambda b, i: (i, 0)))
        args.append(emb)
    mod_map = (lambda b, i: (b, 0, 0)) if per_batch_mod else (lambda b, i: (0, 0, 0))
    in_specs += [pl.BlockSpec((1, 3, D_MODEL), mod_map), tok(D_RWKV), tok(D_RWKV), tok(2 * D_RWKV),
                 tok(D_RWKV), tok(LANES), tok(D_RWKV), tok(D_FNET),
                 full(N_DIR, D_RWKV), full(N_DIR, LANES, D_RWKV),
                 full(1, D_RWKV), full(1, D_RWKV), full(1, D_RWKV), full(1, D_RWKV),
                 full(D_RWKV, D_RWKV), full(D_RWKV, D_RWKV), full(D_MODEL, D_MODEL), full(1, D_MODEL)]
    args += [mod, y_f, y_b, rk, v, lwla, grec, fo, wts["a0"], wts["a2"], wts["k_a"],
             wts["r_k"], wts["gn_g"], wts["gn_b"], wts["avg_bd"], wts["ones_bd"], w_out_bf16, final_norm_g]
    return pl.pallas_call(
        functools.partial(_out_kernel, has_emb=has_emb, final_norm=final_norm),
        grid=(B // nb, T // tt), in_specs=in_specs,
        out_specs=tok(D_MODEL),
        out_shape=jax.ShapeDtypeStruct((B, T, D_MODEL), F32),
        compiler_params=pltpu.CompilerParams(dimension_semantics=("arbitrary", "arbitrary"),
                                             vmem_limit_bytes=VMEM_LIMIT),
        name="out_proj",
    )(*args)


def _dft_table(seq_len):
    idx = np.arange(seq_len, dtype=np.int64)
    ang = 2.0 * np.pi * ((idx[:, None] * idx[None, :]) % seq_len).astype(np.float64) / seq_len
    scale = 1.0 / math.sqrt(seq_len)
    return np.concatenate([np.cos(ang) * scale, -np.sin(ang) * scale], axis=1).astype(np.float32)


def _channel_dft_table():
    n = FNET_GROUP
    idx = np.arange(n, dtype=np.int64)
    ang = 2.0 * np.pi * ((idx[:, None] * idx[None, :]) % n).astype(np.float64) / n
    c = np.cos(ang) / math.sqrt(n)
    s = np.sin(ang) / math.sqrt(n)
    eye = np.eye(D_FNET // n)
    return np.concatenate([np.kron(eye, c), np.kron(eye, s)], axis=1).astype(np.float32)


def _sincos_2d(n_tokens):
    rows = n_tokens // GRID_W
    pos = np.arange(rows * GRID_W)
    row = (pos // GRID_W).astype(np.float32)
    col = (pos % GRID_W).astype(np.float32)
    quarter = D_MODEL // 4
    freq = np.exp(np.float32(-math.log(POS_BASE)) * np.arange(quarter, dtype=np.float32) / np.float32(quarter))
    ang_r = row[:, None] * freq
    ang_c = col[:, None] * freq
    return np.concatenate([np.sin(ang_r), np.cos(ang_r), np.sin(ang_c), np.cos(ang_c)], axis=-1).astype(np.float32)


def _head_block_matrix(value):
    blk = np.kron(np.eye(N_HEADS), np.ones((HEAD_DIM, HEAD_DIM))) * value
    return jnp.asarray(blk, dtype=BF16)


def _pad_lora(w, row_offset):
    rows = [jnp.pad(w[d], ((row_offset + d * LORA, LANES - row_offset - (d + 1) * LORA), (0, 0)))
            for d in range(N_DIR)]
    return jnp.stack(rows).astype(BF16)


def _layer_weights(l, w0, w2, a0, a2, k_k, k_a, r_k, gn_g, gn_b):
    w2_p = _pad_lora(w2[l], 0)
    a2_p = _pad_lora(a2[l], N_DIR * LORA)
    tri_f = np.tril(np.ones((CHUNK, CHUNK)))
    tri = jnp.asarray(np.stack([tri_f, tri_f.T]), dtype=BF16)
    return dict(w0=w0[l], w2=w2_p, a0=a0[l], a2=a2_p,
                k_k=k_k[l][None], k_a=k_a[l][None], r_k=r_k[l].reshape(1, D_RWKV),
                gn_g=gn_g[l].reshape(1, D_RWKV), gn_b=gn_b[l].reshape(1, D_RWKV), tri=tri,
                ones_bd=_head_block_matrix(1.0), avg_bd=_head_block_matrix(1.0 / HEAD_DIM))


def _state_to_block_diag(s_f, s_b):
    def one(s):
        h = jnp.swapaxes(s.astype(F32), -1, -2)
        b = h.shape[0]
        h = h.reshape(b, N_PAIRS, PAIR, HEAD_DIM, HEAD_DIM)
        z = jnp.zeros_like(h[:, :, 0])
        top = jnp.concatenate([h[:, :, 0], z], axis=-1)
        bot = jnp.concatenate([z, h[:, :, 1]], axis=-1)
        return jnp.concatenate([top, bot], axis=-2)
    return jnp.stack([one(s_f), one(s_b)], axis=1)


def kernel(x_prompt, x_sample, state_rwkv_fwd, state_rwkv_bwd, c, c_ctx, w_ada, b_ada, norm_g, w_in,
           mu_shift, w0, w2, a0, a2, k_k, k_a, r_k, gn_g, gn_b, w_fnet, b_fnet, w_out, final_norm_g):
    depth = w_in.shape[0]
    n_dec = c.shape[0]
    assert n_dec + 1 <= 8
    bp, tp, _ = x_prompt.shape
    bs, ts, _ = x_sample.shape
    cvec = jnp.concatenate([c_ctx[None], c, jnp.zeros((7 - n_dec, D_MODEL), F32)], axis=0)
    emb = jnp.asarray(_sincos_2d(ts)).astype(x_sample.dtype)
    cs_tab = jnp.asarray(_channel_dft_table()).astype(BF16)
    dft_p = jnp.asarray(_dft_table(tp)).astype(BF16)
    dft_s = jnp.asarray(_dft_table(ts)).astype(BF16)
    fng = final_norm_g[None]

    xp, xs = x_prompt, x_sample
    new_f, new_b = [], []
    for l in range(depth):
        mod = _mod_call(cvec, w_ada[l], b_ada[l][None]).reshape(8, 3, D_MODEL)
        mod_ctx, mod_lat = mod[0:1], mod[1:1 + n_dec]
        wts = _layer_weights(l, w0, w2, a0, a2, k_k, k_a, r_k, gn_g, gn_b)
        w_in_b = w_in[l].astype(BF16)
        w_out_b = w_out[l].astype(BF16)
        n_grp = w_fnet.shape[1]
        wbd = (w_fnet[l][:, :, None, :] * jnp.eye(n_grp, dtype=F32)[:, None, :, None]).reshape(
            D_FNET, D_FNET).astype(BF16)
        ng, mu, bf = norm_g[l][None], mu_shift[l][None], b_fnet[l][None]
        emb_l = emb if l == 0 else None

        rk, v, lwla, grec, fin, gf = _in_proj_call(xp, None, mod_ctx, ng, w_in_b, mu)
        y_f, y_b, s_f, s_b = _scan_call(rk, v, lwla, wts, None, True)
        fo = _fnet_call(fin, gf, dft_p, cs_tab, wbd, bf)
        last = l == depth - 1
        xp = _out_call(xp, None, mod_ctx, y_f, y_b, rk, v, lwla, grec, fo, wts, w_out_b, fng, last)
        new_f.append(s_f)
        new_b.append(s_b)

        rk, v, lwla, grec, fin, gf = _in_proj_call(xs, emb_l, mod_lat, ng, w_in_b, mu)
        s0 = _state_to_block_diag(state_rwkv_fwd[:, l], state_rwkv_bwd[:, l])
        y_f, y_b = _scan_call(rk, v, lwla, wts, s0, False)
        fo = _fnet_call(fin, gf, dft_s, cs_tab, wbd, bf)
        xs = _out_call(xs, emb_l, mod_lat, y_f, y_b, rk, v, lwla, grec, fo, wts, w_out_b, fng, last)
    return (xp, xs, jnp.concatenate(new_f, axis=1), jnp.concatenate(new_b, axis=1))
```

```python
import functools
import math

import numpy as np
import jax
import jax.numpy as jnp
from jax import lax
from jax.experimental import pallas as pl
from jax.experimental.pallas import tpu as pltpu

F32 = jnp.float32
BF16 = jnp.bfloat16

D_MODEL = 1024
GRID_W = 64
D_RWKV = 512
D_FNET = D_MODEL - D_RWKV
HEAD_DIM = 64
N_HEADS = D_RWKV // HEAD_DIM
FNET_GROUP = 64
LORA = 32
N_DIR = 2
D_SHIFT = 3 * D_RWKV + N_DIR * 2 * LORA
D_IN = D_SHIFT + D_RWKV + 2 * D_FNET
NORM_EPS = 1e-6
GN_EPS = 64e-5
POS_BASE = 10000.0

LANES = 128
PAIR = LANES // HEAD_DIM
N_PAIRS = N_HEADS // PAIR
CHUNK = 64
SUB = 16
SUB_SHIFT = SUB.bit_length() - 1
SCAN_SEQS = 4
ROW_TILE = 512
ROW_SPLIT = 4
FNET_ROWS = 1024
HALO = 8
VMEM_LIMIT = 56 * 1024 * 1024


def _silu(x):
    return x * jax.nn.sigmoid(x)


def _bdot(a, b):
    return jnp.dot(a.astype(BF16), b.astype(BF16), preferred_element_type=F32)


def _split2(x):
    hi = x.astype(BF16)
    lo = (x - hi.astype(F32)).astype(BF16)
    return hi, lo


def _mod_kernel(c_ref, w_ref, b_ref, o_ref):
    @pl.when(pl.program_id(0) == 0)
    def _():
        o_ref[...] = jnp.broadcast_to(b_ref[...], o_ref.shape)

    s_hi, s_lo = _split2(_silu(c_ref[...]))
    w = w_ref[...].astype(BF16)
    o_ref[...] += jnp.dot(s_hi, w, preferred_element_type=F32) + jnp.dot(s_lo, w, preferred_element_type=F32)


def _mod_call(cvec, w_ada, b_ada):
    rows = D_MODEL // 4
    return pl.pallas_call(
        _mod_kernel,
        grid=(D_MODEL // rows,),
        in_specs=[pl.BlockSpec((8, rows), lambda i: (0, i)),
                  pl.BlockSpec((rows, 3 * D_MODEL), lambda i: (i, 0)),
                  pl.BlockSpec((1, 3 * D_MODEL), lambda i: (0, 0))],
        out_specs=pl.BlockSpec((8, 3 * D_MODEL), lambda i: (0, 0)),
        out_shape=jax.ShapeDtypeStruct((8, 3 * D_MODEL), F32),
        compiler_params=pltpu.CompilerParams(dimension_semantics=("arbitrary",),
                                             vmem_limit_bytes=VMEM_LIMIT),
        name="mod",
    )(cvec, w_ada, b_ada)


def _modulated_norm(x, g, scale, shift):
    ms = jnp.mean(x * x, axis=-1, keepdims=True)
    y = x * lax.rsqrt(ms + NORM_EPS) * g
    return y * (1.0 + scale) + shift


def _in_proj_kernel(*refs, has_emb, has_halo, n_tiles):
    it = iter(refs)
    x_ref = next(it)
    xp_ref = next(it) if has_halo else None
    xn_ref = next(it) if has_halo else None
    emb_ref = next(it) if has_emb else None
    embp_ref = next(it) if (has_emb and has_halo) else None
    embn_ref = next(it) if (has_emb and has_halo) else None
    mod_ref, g_ref, w_ref, mu_ref = next(it), next(it), next(it), next(it)
    rk_ref, v_ref, lwla_ref, grec_ref, fin_ref, gf_ref = (next(it) for _ in range(6))

    i = pl.program_id(1)
    g = g_ref[...]
    shift = mod_ref[0, 0:1, :]
    scale = mod_ref[0, 1:2, :]
    nb, tt, _ = x_ref.shape
    tm = nb * tt
    x = x_ref[...].reshape(tm, D_MODEL)
    if has_emb:
        x = x + emb_ref[...]
    part = tm // ROW_SPLIT
    p = jnp.concatenate(
        [jnp.dot(_modulated_norm(x[r0:r0 + part], g, scale, shift).astype(BF16), w_ref[...],
                 preferred_element_type=F32) for r0 in range(0, tm, part)], axis=0)
    ps = p[:, :D_SHIFT]
    if has_halo:
        xh = jnp.concatenate([xp_ref[0], xn_ref[0]], axis=0)
        if has_emb:
            xh = xh + jnp.concatenate([embp_ref[...], embn_ref[...]], axis=0)
        hh = _modulated_norm(xh, g, scale, shift)
        ph = jnp.dot(hh.astype(BF16), w_ref[:, :D_SHIFT], preferred_element_type=F32)
        prev_row = jnp.where(i > 0, ph[HALO - 1:HALO, :], 0.0)
        next_row = jnp.where(i < n_tiles - 1, ph[HALO:HALO + 1, :], 0.0)
    else:
        prev_row = jnp.zeros((1, D_SHIFT), F32)
        next_row = jnp.zeros((1, D_SHIFT), F32)
    pos = lax.rem(lax.broadcasted_iota(jnp.int32, (tm, D_SHIFT), 0), tt)
    prev = jnp.where(pos == 0, prev_row, pltpu.roll(ps, 1, 0))
    nxt = jnp.where(pos == tt - 1, next_row, pltpu.roll(ps, tm - 1, 0))
    p_rec = ps + mu_ref[...] * (0.5 * (prev + nxt) - ps)

    def put(ref, val):
        ref[...] = val.astype(ref.dtype).reshape(ref.shape)

    put(rk_ref, p_rec[:, :2 * D_RWKV])
    put(v_ref, p_rec[:, 2 * D_RWKV:3 * D_RWKV])
    put(lwla_ref, p_rec[:, 3 * D_RWKV:])
    put(grec_ref, p[:, D_SHIFT:D_SHIFT + D_RWKV])
    put(fin_ref, p[:, D_SHIFT + D_RWKV:D_SHIFT + D_RWKV + D_FNET])
    put(gf_ref, p[:, D_SHIFT + D_RWKV + D_FNET:])


def _in_proj_call(x, emb, mod, norm_g, w_in_bf16, mu):
    B, T, _ = x.shape
    tt = min(T, ROW_TILE)
    nb = max(1, ROW_TILE // T)
    n_tiles = T // tt
    has_halo = n_tiles > 1
    has_emb = emb is not None
    per_batch_mod = mod.shape[0] > 1
    assert B % nb == 0 and T % tt == 0 and (nb == 1 or not (has_emb or per_batch_mod))
    tm = tt
    blocks_per_tile = tm // HALO
    last_halo_block = T // HALO - 1

    in_specs = [pl.BlockSpec((nb, tt, D_MODEL), lambda b, i: (b, i, 0))]
    args = [x]
    if has_halo:
        in_specs += [
            pl.BlockSpec((1, HALO, D_MODEL), lambda b, i: (b, jnp.maximum(i * blocks_per_tile - 1, 0), 0)),
            pl.BlockSpec((1, HALO, D_MODEL),
                         lambda b, i: (b, jnp.minimum((i + 1) * blocks_per_tile, last_halo_block), 0))]
        args += [x, x]
    if has_emb:
        in_specs.append(pl.BlockSpec((tm, D_MODEL), lambda b, i: (i, 0)))
        args.append(emb)
        if has_halo:
            in_specs += [
                pl.BlockSpec((HALO, D_MODEL), lambda b, i: (jnp.maximum(i * blocks_per_tile - 1, 0), 0)),
                pl.BlockSpec((HALO, D_MODEL),
                             lambda b, i: (jnp.minimum((i + 1) * blocks_per_tile, last_halo_block), 0))]
            args += [emb, emb]
    mod_map = (lambda b, i: (b, 0, 0)) if per_batch_mod else (lambda b, i: (0, 0, 0))
    in_specs += [pl.BlockSpec((1, 3, D_MODEL), mod_map),
                 pl.BlockSpec((1, D_MODEL), lambda b, i: (0, 0)),
                 pl.BlockSpec((D_MODEL, D_IN), lambda b, i: (0, 0), pipeline_mode=pl.Buffered(1)),
                 pl.BlockSpec((1, D_SHIFT), lambda b, i: (0, 0))]
    args += [mod, norm_g, w_in_bf16, mu]
    outs = ((2 * D_RWKV, BF16), (D_RWKV, BF16), (N_DIR * 2 * LORA, F32), (D_RWKV, BF16), (D_FNET, BF16),
            (D_FNET, BF16))
    out_specs = [pl.BlockSpec((nb, tt, w), lambda b, i: (b, i, 0)) for w, _ in outs]
    out_shape = [jax.ShapeDtypeStruct((B, T, w), dt) for w, dt in outs]
    kern = functools.partial(_in_proj_kernel, has_emb=has_emb, has_halo=has_halo, n_tiles=n_tiles)
    return pl.pallas_call(
        kern, grid=(B // nb, n_tiles), in_specs=in_specs, out_specs=out_specs, out_shape=out_shape,
        compiler_params=pltpu.CompilerParams(dimension_semantics=("arbitrary", "arbitrary"),
                                             vmem_limit_bytes=VMEM_LIMIT),
        name="in_proj",
    )(*args)


def _same_block(i, j, size):
    shift = size.bit_length() - 1
    return jnp.right_shift(i, shift) == jnp.right_shift(j, shift)


def _block_diag_rows(y, head0_lanes):
    return jnp.concatenate([jnp.where(head0_lanes, y, 0.0), jnp.where(head0_lanes, 0.0, y)], axis=0)


def _scan_kernel(*refs, zero_init, write_state, n_chunks, n_seq):
    it = iter(refs)
    rk_refs, v_refs, lwla_refs = [None, None], [None, None], [None, None]
    for d in range(N_DIR):
        rk_refs[d], v_refs[d], lwla_refs[d] = next(it), next(it), next(it)
    w0_ref, w2_ref, a0_ref, a2_ref = (next(it) for _ in range(4))
    kk_ref, ka_ref, tri_ref, ones_ref = (next(it) for _ in range(4))
    s0_ref = None if zero_init else next(it)
    y_refs = [next(it), next(it)]
    sfin_refs = [next(it), next(it)] if write_state else None
    h_scr = next(it)

    j = pl.program_id(1)

    @pl.when(j == 0)
    def _():
        if zero_init:
            h_scr[...] = jnp.zeros_like(h_scr)
        else:
            h_scr[...] = s0_ref[...]

    C = CHUNK
    row = lax.broadcasted_iota(jnp.int32, (C, LANES), 0)
    lane = lax.broadcasted_iota(jnp.int32, (C, LANES), 1)
    s_idx = jnp.bitwise_and(lane, HEAD_DIM - 1)
    head0 = lane < HEAD_DIM
    row_c = lax.broadcasted_iota(jnp.int32, (SUB, LANES), 0)
    lane_c = lax.broadcasted_iota(jnp.int32, (SUB, LANES), 1)
    col_c = jnp.bitwise_and(lane_c, SUB - 1)
    blk_c = jnp.right_shift(jnp.bitwise_and(lane_c, HEAD_DIM - 1), SUB_SHIFT)
    lane_blk_c = jnp.right_shift(lane_c, SUB_SHIFT)
    eye_c = (row_c == col_c).astype(F32)

    def bd_c(y):
        return jnp.concatenate([jnp.where(lane_blk_c == g, y, 0.0) for g in range(LANES // SUB)], axis=0)

    def inverse_size4(l_c, upper):
        r4 = jnp.bitwise_and(row_c, 3)
        c4 = jnp.bitwise_and(col_c, 3)
        down, up = (lambda x, k: pltpu.roll(x, k, 0)), (lambda x, k: pltpu.roll(x, SUB - k, 0))
        right, left = (lambda x, k: pltpu.roll(x, k, 1)), (lambda x, k: pltpu.roll(x, LANES - k, 1))
        s2 = jnp.where(_same_block(row_c, col_c, 2), l_c, 0.0)
        e4 = jnp.where(_same_block(row_c, col_c, 4), l_c, 0.0) - s2
        if upper:
            s_col = jnp.where(r4 == 1, up(s2, 1), up(s2, 2))
            s_row = jnp.where(c4 == 2, right(s2, 1), right(s2, 2))
            e_s = jnp.where(c4 == 3, right(e4, 1) * s_col, 0.0)
            s_e = jnp.where(r4 == 0, s_row * up(e4, 1), 0.0)
            s_e_s = jnp.where((r4 == 0) & (c4 == 3), s_row * up(e_s, 1), 0.0)
        else:
            s_col = jnp.where(r4 == 2, down(s2, 1), down(s2, 2))
            s_row = jnp.where(c4 == 1, left(s2, 1), left(s2, 2))
            e_s = jnp.where(c4 == 0, left(e4, 1) * s_col, 0.0)
            s_e = jnp.where(r4 == 3, s_row * down(e4, 1), 0.0)
            s_e_s = jnp.where((r4 == 3) & (c4 == 0), s_row * down(e_s, 1), 0.0)
        return eye_c - s2 - e4 + e_s + s_e - s_e_s

    row2 = lax.broadcasted_iota(jnp.int32, (LANES, LANES), 0)
    lane2 = lax.broadcasted_iota(jnp.int32, (LANES, LANES), 1)
    same_head = (row2 < HEAD_DIM) == (lane2 < HEAD_DIM)
    decay_scale = math.exp(-0.5)

    seq_dir = [(n, d) for n in range(n_seq) for d in range(N_DIR)]
    rk_all = {nd: rk_refs[nd[1]][nd[0]].astype(F32) for nd in seq_dir}
    kkr_all = {nd: rk_all[nd][:, D_RWKV:] * kk_ref[...] for nd in seq_dir}
    ssq_rows = _bdot(jnp.concatenate([kkr_all[nd] * kkr_all[nd] for nd in seq_dir], axis=0), ones_ref[...])
    z_w_rows, z_a_rows = [], []
    for d in range(N_DIR):
        ll_rows = jnp.concatenate([lwla_refs[d][n] for n in range(n_seq)], axis=0)
        z_w_rows.append(w0_ref[d:d + 1, :] + _bdot(jnp.tanh(ll_rows), w2_ref[d]))
        z_a_rows.append(a0_ref[d:d + 1, :] + _bdot(ll_rows, a2_ref[d]))

    probs = []
    for i, (n, d) in enumerate(seq_dir):
        rk = rk_all[(n, d)]
        r = rk[:, :D_RWKV]
        k = rk[:, D_RWKV:]
        v = v_refs[d][n]
        logw = -decay_scale * jax.nn.sigmoid(z_w_rows[d][n * C:(n + 1) * C])
        a = jax.nn.sigmoid(z_a_rows[d][n * C:(n + 1) * C])
        kd = k * (1.0 + (a - 1.0) * ka_ref[...])
        kkr = kkr_all[(n, d)]
        ssq = ssq_rows[i * C:(i + 1) * C]
        kk = kkr * lax.rsqrt(jnp.maximum(ssq, 1e-24))
        bvec = kk * a
        lw_hi, lw_lo = _split2(logw)
        tri = tri_ref[d]
        cum = (jnp.dot(tri, lw_hi, preferred_element_type=F32)
               + jnp.dot(tri, lw_lo, preferred_element_type=F32))
        cum_prev = cum - logw
        tot = cum[C - 1:C, :] if d == 0 else cum[0:1, :]
        kap_t = kk * jnp.exp(cum_prev)
        r_t = r * jnp.exp(cum)
        e_neg = jnp.exp(-cum)
        gam = jnp.exp(tot)
        b_t = bvec * e_neg
        k_t = kd * e_neg
        e_rem = gam * e_neg
        b_h = bvec * e_rem
        k_h = kd * e_rem
        if d == 0:
            strict, incl = row > s_idx, row >= s_idx
        else:
            strict, incl = row < s_idx, row <= s_idx

        for p in range(N_PAIRS):
            sl = slice(p * LANES, (p + 1) * LANES)
            dup_t = lambda x: jnp.where(same_head, jnp.concatenate([x[:, sl], x[:, sl]], axis=0).T, 0.0)
            probs.append(dict(n=n, d=d, p=p, sl=sl, strict=strict, incl=incl,
                              lhs=jnp.concatenate([kap_t[:, sl], r_t[:, sl]], axis=0).astype(BF16),
                              v=v[:, sl], w_lm=jnp.concatenate([dup_t(b_t), dup_t(k_t)], axis=1).astype(BF16),
                              t2=jnp.concatenate([b_h[:, sl], k_h[:, sl]], axis=0).T.astype(BF16),
                              gam_col=jnp.broadcast_to(gam[:, sl], (LANES, LANES)).T))

    bd = lambda y: _block_diag_rows(y, head0)
    for q in probs:
        q["h"] = h_scr[q["n"], q["d"], q["p"]]
        lm = jnp.dot(q["lhs"], q["w_lm"], preferred_element_type=F32)
        q["l_b"] = jnp.where(q["strict"], lm[:C, :LANES], 0.0)
        q["m_b"] = jnp.where(q["incl"], lm[C:, :LANES], 0.0)
        q["lm_k"] = jnp.concatenate([jnp.where(q["strict"], lm[:C, LANES:], 0.0),
                                     jnp.where(q["incl"], lm[C:, LANES:], 0.0)], axis=0)
    for q in probs:
        lmv = _bdot(q["lm_k"], bd(q["v"]))
        q["l_kv"], q["m_kv"] = lmv[:C], lmv[C:]
    for q in probs:
        q["kr_h"] = _bdot(q["lhs"], q["h"])
    for q in probs:
        l_b = q["l_b"]
        l_c = l_b[0:SUB]
        for jb in range(1, C // SUB):
            l_c = jnp.where(blk_c == jb, l_b[jb * SUB:(jb + 1) * SUB], l_c)
        q["l_c"] = l_c
        q["t_c"] = inverse_size4(l_c, q["d"] == 1)
    s = 4
    while s < SUB:
        off_mask = _same_block(row_c, col_c, 2 * s) & ~_same_block(row_c, col_c, s)
        for q in probs:
            q["et"] = _bdot(jnp.where(off_mask, q["l_c"], 0.0), bd_c(q["t_c"]))
        for q in probs:
            q["t_c"] = q["t_c"] - _bdot(q["t_c"], bd_c(q["et"]))
        s *= 2
    for q in probs:
        q["tinv"] = jnp.concatenate([jnp.where(blk_c == jb, q["t_c"], 0.0) for jb in range(C // SUB)], axis=0)
    while s < C:
        off_mask = _same_block(row, s_idx, 2 * s) & ~_same_block(row, s_idx, s)
        for q in probs:
            q["et"] = _bdot(jnp.where(off_mask, q["l_b"], 0.0), bd(q["tinv"]))
        for q in probs:
            q["tinv"] = q["tinv"] - _bdot(q["tinv"], bd(q["et"]))
        s *= 2
    for q in probs:
        q["u_n"] = _bdot(q["tinv"], bd(q["kr_h"][:C] + q["l_kv"]))
    for q in probs:
        y = q["kr_h"][C:] + q["m_kv"] - _bdot(q["m_b"], bd(q["u_n"]))
        y_refs[q["d"]][q["n"], :, q["sl"]] = y.astype(BF16)
    for q in probs:
        upd = _bdot(q["t2"], jnp.concatenate([(-q["u_n"]).astype(BF16), q["v"]], axis=0))
        h_scr[q["n"], q["d"], q["p"]] = q["gam_col"] * q["h"] + jnp.where(same_head, upd, 0.0)

    if write_state:
        @pl.when(j == n_chunks - 1)
        def _():
            for n, d, p in [(n, d, p) for n in range(n_seq) for d in range(N_DIR) for p in range(N_PAIRS)]:
                ht = h_scr[n, d, p].T
                sfin_refs[d][n, 0, PAIR * p] = ht[:HEAD_DIM, :HEAD_DIM]
                sfin_refs[d][n, 0, PAIR * p + 1] = ht[HEAD_DIM:, HEAD_DIM:]


def _scan_call(rk, v, lwla, wts, s0_bd, write_state):
    B, T, _ = rk.shape
    nc = T // CHUNK
    ns = min(SCAN_SEQS, B)
    assert B % ns == 0
    zero_init = s0_bd is None
    fwd = lambda b, j: (b, j, 0)
    bwd = lambda b, j: (b, nc - 1 - j, 0)
    full = lambda *shape: pl.BlockSpec(shape, lambda b, j: (0,) * len(shape))
    tok = lambda w, m: pl.BlockSpec((ns, CHUNK, w), m)
    in_specs = [tok(2 * D_RWKV, fwd), tok(D_RWKV, fwd), tok(LANES, fwd),
                tok(2 * D_RWKV, bwd), tok(D_RWKV, bwd), tok(LANES, bwd),
                full(N_DIR, D_RWKV), full(N_DIR, LANES, D_RWKV),
                full(N_DIR, D_RWKV), full(N_DIR, LANES, D_RWKV),
                full(1, D_RWKV), full(1, D_RWKV), full(N_DIR, CHUNK, CHUNK), full(D_RWKV, D_RWKV)]
    args = [rk, v, lwla, rk, v, lwla, wts["w0"], wts["w2"], wts["a0"], wts["a2"],
            wts["k_k"], wts["k_a"], wts["tri"], wts["ones_bd"]]
    state_block = (ns, N_DIR, N_PAIRS, LANES, LANES)
    if not zero_init:
        in_specs.append(pl.BlockSpec(state_block, lambda b, j: (b, 0, 0, 0, 0)))
        args.append(s0_bd)
    out_specs = [pl.BlockSpec((ns, CHUNK, D_RWKV), fwd), pl.BlockSpec((ns, CHUNK, D_RWKV), bwd)]
    out_shape = [jax.ShapeDtypeStruct((B, T, D_RWKV), BF16), jax.ShapeDtypeStruct((B, T, D_RWKV), BF16)]
    if write_state:
        final_block = (ns, 1, N_HEADS, HEAD_DIM, HEAD_DIM)
        out_specs += [pl.BlockSpec(final_block, lambda b, j: (b, 0, 0, 0, 0))] * N_DIR
        out_shape += [jax.ShapeDtypeStruct((B,) + final_block[1:], F32)] * N_DIR
    kern = functools.partial(_scan_kernel, zero_init=zero_init, write_state=write_state, n_chunks=nc,
                             n_seq=ns)
    return pl.pallas_call(
        kern, grid=(B // ns, nc), in_specs=in_specs, out_specs=out_specs, out_shape=out_shape,
        scratch_shapes=[pltpu.VMEM(state_block, F32)],
        compiler_params=pltpu.CompilerParams(dimension_semantics=("arbitrary", "arbitrary"),
                                             vmem_limit_bytes=VMEM_LIMIT),
        name="scan",
    )(*args)


def _fnet_kernel(fin_ref, gf_ref, dft_ref, cs_ref, wbd_ref, b_ref, o_ref, g_scr):
    u = pl.program_id(1)
    nb, seq_len, _ = fin_ref.shape
    tu = o_ref.shape[1]

    @pl.when(u == 0)
    def _():
        step = min(seq_len, ROW_TILE)
        for n in range(nb):
            for t0 in range(0, seq_len, step):
                fc = _bdot(fin_ref[n, t0:t0 + step, :], cs_ref[...])
                g_scr[n, t0:t0 + step, :] = fc[:, :D_FNET].astype(BF16)
                g_scr[n, seq_len + t0:seq_len + t0 + step, :] = fc[:, D_FNET:].astype(BF16)

    f_re = jnp.concatenate([jnp.dot(dft_ref[...], g_scr[n], preferred_element_type=F32) for n in range(nb)],
                           axis=0)
    f_out = _bdot(f_re, wbd_ref[...]) + b_ref[...]
    gate = _silu(gf_ref[...].astype(F32)).reshape(nb * tu, D_FNET)
    o_ref[...] = (f_out * gate).astype(BF16).reshape(nb, tu, D_FNET)


def _fnet_call(fin, gf, dft_bf16, cs_bf16, wbd_bf16, b_fnet):
    B, T, _ = fin.shape
    tu = min(T, ROW_TILE)
    nb = max(1, FNET_ROWS // T)
    assert B % nb == 0 and T % tu == 0
    return pl.pallas_call(
        _fnet_kernel,
        grid=(B // nb, T // tu),
        in_specs=[pl.BlockSpec((nb, T, D_FNET), lambda b, u: (b, 0, 0)),
                  pl.BlockSpec((nb, tu, D_FNET), lambda b, u: (b, u, 0)),
                  pl.BlockSpec((tu, 2 * T), lambda b, u: (u, 0)),
                  pl.BlockSpec((D_FNET, 2 * D_FNET), lambda b, u: (0, 0)),
                  pl.BlockSpec((D_FNET, D_FNET), lambda b, u: (0, 0)),
                  pl.BlockSpec((1, D_FNET), lambda b, u: (0, 0))],
        out_specs=pl.BlockSpec((nb, tu, D_FNET), lambda b, u: (b, u, 0)),
        out_shape=jax.ShapeDtypeStruct((B, T, D_FNET), BF16),
        scratch_shapes=[pltpu.VMEM((nb, 2 * T, D_FNET), BF16)],
        compiler_params=pltpu.CompilerParams(dimension_semantics=("arbitrary", "arbitrary"),
                                             vmem_limit_bytes=VMEM_LIMIT),
        name="fnet",
    )(fin, gf, dft_bf16, cs_bf16, wbd_bf16, b_fnet)


def _out_kernel(*refs, has_emb, final_norm):
    it = iter(refs)
    x_ref = next(it)
    emb_ref = next(it) if has_emb else None
    (mod_ref, yf_ref, yb_ref, rk_ref, v_ref, lwla_ref, grec_ref, fo_ref, a0_ref, a2_ref, ka_ref,
     rkw_ref, gng_ref, gnb_ref, avg_ref, ones_ref, wout_ref, fng_ref, o_ref) = (next(it) for _ in range(19))

    nb, tt, _ = x_ref.shape
    tm = nb * tt
    rows = lambda ref: ref[...].reshape(tm, ref.shape[-1])
    y = rows(yf_ref).astype(F32) + rows(yb_ref).astype(F32)
    mu = _bdot(y, avg_ref[...])
    dlt = y - mu
    var = _bdot(dlt * dlt, avg_ref[...])
    y_n = dlt * lax.rsqrt(var + GN_EPS) * gng_ref[...] + gnb_ref[...]
    rk = rows(rk_ref).astype(F32)
    r = rk[:, :D_RWKV]
    k = rk[:, D_RWKV:]
    ll = rows(lwla_ref)
    a_sum = (jax.nn.sigmoid(a0_ref[0:1, :] + _bdot(ll, a2_ref[0]))
             + jax.nn.sigmoid(a0_ref[1:2, :] + _bdot(ll, a2_ref[1])))
    k_sum = k * (2.0 + (a_sum - 2.0) * ka_ref[...])
    bonus = _bdot(r * k_sum * rkw_ref[...], ones_ref[...]) * rows(v_ref).astype(F32)
    rec_out = (y_n + bonus) * _silu(rows(grec_ref).astype(F32))
    mixed = jnp.concatenate([rec_out.astype(BF16), rows(fo_ref)], axis=-1)
    out = jnp.dot(mixed, wout_ref[...], preferred_element_type=F32)
    x = rows(x_ref)
    if has_emb:
        x = x + emb_ref[...]
    z = x + mod_ref[0, 2:3, :] * out
    if final_norm:
        ms = jnp.mean(z * z, axis=-1, keepdims=True)
        z = z * lax.rsqrt(ms + NORM_EPS) * fng_ref[...]
    o_ref[...] = z.reshape(nb, tt, D_MODEL)


def _out_call(x, emb, mod, y_f, y_b, rk, v, lwla, grec, fo, wts, w_out_bf16, final_norm_g, final_norm):
    B, T, _ = x.shape
    tt = min(T, ROW_TILE)
    nb = max(1, ROW_TILE // T)
    has_emb = emb is not None
    per_batch_mod = mod.shape[0] > 1
    assert B % nb == 0 and T % tt == 0 and (nb == 1 or not (has_emb or per_batch_mod))
    tok = lambda w: pl.BlockSpec((nb, tt, w), lambda b, i: (b, i, 0))
    full = lambda *shape: pl.BlockSpec(shape, lambda b, i: (0,) * len(shape))
    in_specs = [tok(D_MODEL)]
    args = [x]
    if has_emb:
        in_specs.append(pl.BlockSpec((tt, D_MODEL), lambda b, i: (i, 0)))
        args.append(emb)
    mod_map = (lambda b, i: (b, 0, 0)) if per_batch_mod else (lambda b, i: (0, 0, 0))
    in_specs += [pl.BlockSpec((1, 3, D_MODEL), mod_map), tok(D_RWKV), tok(D_RWKV), tok(2 * D_RWKV),
                 tok(D_RWKV), tok(LANES), tok(D_RWKV), tok(D_FNET),
                 full(N_DIR, D_RWKV), full(N_DIR, LANES, D_RWKV),
                 full(1, D_RWKV), full(1, D_RWKV), full(1, D_RWKV), full(1, D_RWKV),
                 full(D_RWKV, D_RWKV), full(D_RWKV, D_RWKV), full(D_MODEL, D_MODEL), full(1, D_MODEL)]
    args += [mod, y_f, y_b, rk, v, lwla, grec, fo, wts["a0"], wts["a2"], wts["k_a"],
             wts["r_k"], wts["gn_g"], wts["gn_b"], wts["avg_bd"], wts["ones_bd"], w_out_bf16, final_norm_g]
    return pl.pallas_call(
        functools.partial(_out_kernel, has_emb=has_emb, final_norm=final_norm),
        grid=(B // nb, T // tt), in_specs=in_specs,
        out_specs=tok(D_MODEL),
        out_shape=jax.ShapeDtypeStruct((B, T, D_MODEL), F32),
        compiler_params=pltpu.CompilerParams(dimension_semantics=("arbitrary", "arbitrary"),
                                             vmem_limit_bytes=VMEM_LIMIT),
        name="out_proj",
    )(*args)


def _dft_table(seq_len):
    idx = np.arange(seq_len, dtype=np.int64)
    ang = 2.0 * np.pi * ((idx[:, None] * idx[None, :]) % seq_len).astype(np.float64) / seq_len
    scale = 1.0 / math.sqrt(seq_len)
    return np.concatenate([np.cos(ang) * scale, -np.sin(ang) * scale], axis=1).astype(np.float32)


def _channel_dft_table():
    n = FNET_GROUP
    idx = np.arange(n, dtype=np.int64)
    ang = 2.0 * np.pi * ((idx[:, None] * idx[None, :]) % n).astype(np.float64) / n
    c = np.cos(ang) / math.sqrt(n)
    s = np.sin(ang) / math.sqrt(n)
    eye = np.eye(D_FNET // n)
    return np.concatenate([np.kron(eye, c), np.kron(eye, s)], axis=1).astype(np.float32)


def _sincos_2d(n_tokens):
    rows = n_tokens // GRID_W
    pos = np.arange(rows * GRID_W)
    row = (pos // GRID_W).astype(np.float32)
    col = (pos % GRID_W).astype(np.float32)
    quarter = D_MODEL // 4
    freq = np.exp(np.float32(-math.log(POS_BASE)) * np.arange(quarter, dtype=np.float32) / np.float32(quarter))
    ang_r = row[:, None] * freq
    ang_c = col[:, None] * freq
    return np.concatenate([np.sin(ang_r), np.cos(ang_r), np.sin(ang_c), np.cos(ang_c)], axis=-1).astype(np.float32)


def _head_block_matrix(value):
    blk = np.kron(np.eye(N_HEADS), np.ones((HEAD_DIM, HEAD_DIM))) * value
    return jnp.asarray(blk, dtype=BF16)


def _pad_lora(w, row_offset):
    rows = [jnp.pad(w[d], ((row_offset + d * LORA, LANES - row_offset - (d + 1) * LORA), (0, 0)))
            for d in range(N_DIR)]
    return jnp.stack(rows).astype(BF16)


def _layer_weights(l, w0, w2, a0, a2, k_k, k_a, r_k, gn_g, gn_b):
    w2_p = _pad_lora(w2[l], 0)
    a2_p = _pad_lora(a2[l], N_DIR * LORA)
    tri_f = np.tril(np.ones((CHUNK, CHUNK)))
    tri = jnp.asarray(np.stack([tri_f, tri_f.T]), dtype=BF16)
    return dict(w0=w0[l], w2=w2_p, a0=a0[l], a2=a2_p,
                k_k=k_k[l][None], k_a=k_a[l][None], r_k=r_k[l].reshape(1, D_RWKV),
                gn_g=gn_g[l].reshape(1, D_RWKV), gn_b=gn_b[l].reshape(1, D_RWKV), tri=tri,
                ones_bd=_head_block_matrix(1.0), avg_bd=_head_block_matrix(1.0 / HEAD_DIM))


def _state_to_block_diag(s_f, s_b):
    def one(s):
        h = jnp.swapaxes(s.astype(F32), -1, -2)
        b = h.shape[0]
        h = h.reshape(b, N_PAIRS, PAIR, HEAD_DIM, HEAD_DIM)
        z = jnp.zeros_like(h[:, :, 0])
        top = jnp.concatenate([h[:, :, 0], z], axis=-1)
        bot = jnp.concatenate([z, h[:, :, 1]], axis=-1)
        return jnp.concatenate([top, bot], axis=-2)
    return jnp.stack([one(s_f), one(s_b)], axis=1)


def kernel(x_prompt, x_sample, state_rwkv_fwd, state_rwkv_bwd, c, c_ctx, w_ada, b_ada, norm_g, w_in,
           mu_shift, w0, w2, a0, a2, k_k, k_a, r_k, gn_g, gn_b, w_fnet, b_fnet, w_out, final_norm_g):
    depth = w_in.shape[0]
    n_dec = c.shape[0]
    assert n_dec + 1 <= 8
    bp, tp, _ = x_prompt.shape
    bs, ts, _ = x_sample.shape
    cvec = jnp.concatenate([c_ctx[None], c, jnp.zeros((7 - n_dec, D_MODEL), F32)], axis=0)
    emb = jnp.asarray(_sincos_2d(ts)).astype(x_sample.dtype)
    cs_tab = jnp.asarray(_channel_dft_table()).astype(BF16)
    dft_p = jnp.asarray(_dft_table(tp)).astype(BF16)
    dft_s = jnp.asarray(_dft_table(ts)).astype(BF16)
    fng = final_norm_g[None]

    xp, xs = x_prompt, x_sample
    new_f, new_b = [], []
    for l in range(depth):
        mod = _mod_call(cvec, w_ada[l], b_ada[l][None]).reshape(8, 3, D_MODEL)
        mod_ctx, mod_lat = mod[0:1], mod[1:1 + n_dec]
        wts = _layer_weights(l, w0, w2, a0, a2, k_k, k_a, r_k, gn_g, gn_b)
        w_in_b = w_in[l].astype(BF16)
        w_out_b = w_out[l].astype(BF16)
        n_grp = w_fnet.shape[1]
        wbd = (w_fnet[l][:, :, None, :] * jnp.eye(n_grp, dtype=F32)[:, None, :, None]).reshape(
            D_FNET, D_FNET).astype(BF16)
        ng, mu, bf = norm_g[l][None], mu_shift[l][None], b_fnet[l][None]
        emb_l = emb if l == 0 else None

        rk, v, lwla, grec, fin, gf = _in_proj_call(xp, None, mod_ctx, ng, w_in_b, mu)
        y_f, y_b, s_f, s_b = _scan_call(rk, v, lwla, wts, None, True)
        fo = _fnet_call(fin, gf, dft_p, cs_tab, wbd, bf)
        last = l == depth - 1
        xp = _out_call(xp, None, mod_ctx, y_f, y_b, rk, v, lwla, grec, fo, wts, w_out_b, fng, last)
        new_f.append(s_f)
        new_b.append(s_b)

        rk, v, lwla, grec, fin, gf = _in_proj_call(xs, emb_l, mod_lat, ng, w_in_b, mu)
        s0 = _state_to_block_diag(state_rwkv_fwd[:, l], state_rwkv_bwd[:, l])
        y_f, y_b = _scan_call(rk, v, lwla, wts, s0, False)
        fo = _fnet_call(fin, gf, dft_s, cs_tab, wbd, bf)
        xs = _out_call(xs, emb_l, mod_lat, y_f, y_b, rk, v, lwla, grec, fo, wts, w_out_b, fng, last)
    return (xp, xs, jnp.concatenate(new_f, axis=1), jnp.concatenate(new_b, axis=1))
```

```python
import functools
import math

import numpy as np
import jax
import jax.numpy as jnp
from jax import lax
from jax.experimental import pallas as pl
from jax.experimental.pallas import tpu as pltpu

F32 = jnp.float32
BF16 = jnp.bfloat16

D_MODEL = 1024
GRID_W = 64
D_RWKV = 512
D_FNET = D_MODEL - D_RWKV
HEAD_DIM = 64
N_HEADS = D_RWKV // HEAD_DIM
FNET_GROUP = 64
LORA = 32
N_DIR = 2
D_SHIFT = 3 * D_RWKV + N_DIR * 2 * LORA
D_IN = D_SHIFT + D_RWKV + 2 * D_FNET
NORM_EPS = 1e-6
GN_EPS = 64e-5
POS_BASE = 10000.0

LANES = 128
PAIR = LANES // HEAD_DIM
N_PAIRS = N_HEADS // PAIR
CHUNK = 64
SUB = 16
SUB_SHIFT = SUB.bit_length() - 1
SCAN_SEQS = 8
ROW_TILE = 512
ROW_SPLIT = 4
FNET_ROWS = 1024
HALO = 8
VMEM_LIMIT = 56 * 1024 * 1024


def _silu(x):
    return x * jax.nn.sigmoid(x)


def _bdot(a, b):
    return jnp.dot(a.astype(BF16), b.astype(BF16), preferred_element_type=F32)


def _split2(x):
    hi = x.astype(BF16)
    lo = (x - hi.astype(F32)).astype(BF16)
    return hi, lo


def _mod_kernel(c_ref, w_ref, b_ref, o_ref):
    @pl.when(pl.program_id(0) == 0)
    def _():
        o_ref[...] = jnp.broadcast_to(b_ref[...], o_ref.shape)

    s_hi, s_lo = _split2(_silu(c_ref[...]))
    w = w_ref[...].astype(BF16)
    o_ref[...] += jnp.dot(s_hi, w, preferred_element_type=F32) + jnp.dot(s_lo, w, preferred_element_type=F32)


def _mod_call(cvec, w_ada, b_ada):
    rows = D_MODEL // 4
    return pl.pallas_call(
        _mod_kernel,
        grid=(D_MODEL // rows,),
        in_specs=[pl.BlockSpec((8, rows), lambda i: (0, i)),
                  pl.BlockSpec((rows, 3 * D_MODEL), lambda i: (i, 0)),
                  pl.BlockSpec((1, 3 * D_MODEL), lambda i: (0, 0))],
        out_specs=pl.BlockSpec((8, 3 * D_MODEL), lambda i: (0, 0)),
        out_shape=jax.ShapeDtypeStruct((8, 3 * D_MODEL), F32),
        compiler_params=pltpu.CompilerParams(dimension_semantics=("arbitrary",),
                                             vmem_limit_bytes=VMEM_LIMIT),
        name="mod",
    )(cvec, w_ada, b_ada)


def _modulated_norm(x, g, scale, shift):
    ms = jnp.mean(x * x, axis=-1, keepdims=True)
    y = x * lax.rsqrt(ms + NORM_EPS) * g
    return y * (1.0 + scale) + shift


def _in_proj_kernel(*refs, has_emb, has_halo, n_tiles):
    it = iter(refs)
    x_ref = next(it)
    xp_ref = next(it) if has_halo else None
    xn_ref = next(it) if has_halo else None
    emb_ref = next(it) if has_emb else None
    embp_ref = next(it) if (has_emb and has_halo) else None
    embn_ref = next(it) if (has_emb and has_halo) else None
    mod_ref, g_ref, w_ref, mu_ref = next(it), next(it), next(it), next(it)
    rk_ref, v_ref, lwla_ref, grec_ref, fin_ref, gf_ref = (next(it) for _ in range(6))

    i = pl.program_id(1)
    g = g_ref[...]
    shift = mod_ref[0, 0:1, :]
    scale = mod_ref[0, 1:2, :]
    nb, tt, _ = x_ref.shape
    tm = nb * tt
    x = x_ref[...].reshape(tm, D_MODEL)
    if has_emb:
        x = x + emb_ref[...]
    part = tm // ROW_SPLIT
    p = jnp.concatenate(
        [jnp.dot(_modulated_norm(x[r0:r0 + part], g, scale, shift).astype(BF16), w_ref[...],
                 preferred_element_type=F32) for r0 in range(0, tm, part)], axis=0)
    ps = p[:, :D_SHIFT]
    if has_halo:
        xh = jnp.concatenate([xp_ref[0], xn_ref[0]], axis=0)
        if has_emb:
            xh = xh + jnp.concatenate([embp_ref[...], embn_ref[...]], axis=0)
        hh = _modulated_norm(xh, g, scale, shift)
        ph = jnp.dot(hh.astype(BF16), w_ref[:, :D_SHIFT], preferred_element_type=F32)
        prev_row = jnp.where(i > 0, ph[HALO - 1:HALO, :], 0.0)
        next_row = jnp.where(i < n_tiles - 1, ph[HALO:HALO + 1, :], 0.0)
    else:
        prev_row = jnp.zeros((1, D_SHIFT), F32)
        next_row = jnp.zeros((1, D_SHIFT), F32)
    pos = lax.rem(lax.broadcasted_iota(jnp.int32, (tm, D_SHIFT), 0), tt)
    prev = jnp.where(pos == 0, prev_row, pltpu.roll(ps, 1, 0))
    nxt = jnp.where(pos == tt - 1, next_row, pltpu.roll(ps, tm - 1, 0))
    p_rec = ps + mu_ref[...] * (0.5 * (prev + nxt) - ps)

    def put(ref, val):
        ref[...] = val.astype(ref.dtype).reshape(ref.shape)

    put(rk_ref, p_rec[:, :2 * D_RWKV])
    put(v_ref, p_rec[:, 2 * D_RWKV:3 * D_RWKV])
    put(lwla_ref, p_rec[:, 3 * D_RWKV:])
    put(grec_ref, p[:, D_SHIFT:D_SHIFT + D_RWKV])
    put(fin_ref, p[:, D_SHIFT + D_RWKV:D_SHIFT + D_RWKV + D_FNET])
    put(gf_ref, p[:, D_SHIFT + D_RWKV + D_FNET:])


def _in_proj_call(x, emb, mod, norm_g, w_in_bf16, mu):
    B, T, _ = x.shape
    tt = min(T, ROW_TILE)
    nb = max(1, ROW_TILE // T)
    n_tiles = T // tt
    has_halo = n_tiles > 1
    has_emb = emb is not None
    per_batch_mod = mod.shape[0] > 1
    assert B % nb == 0 and T % tt == 0 and (nb == 1 or not (has_emb or per_batch_mod))
    tm = tt
    blocks_per_tile = tm // HALO
    last_halo_block = T // HALO - 1

    in_specs = [pl.BlockSpec((nb, tt, D_MODEL), lambda b, i: (b, i, 0))]
    args = [x]
    if has_halo:
        in_specs += [
            pl.BlockSpec((1, HALO, D_MODEL), lambda b, i: (b, jnp.maximum(i * blocks_per_tile - 1, 0), 0)),
            pl.BlockSpec((1, HALO, D_MODEL),
                         lambda b, i: (b, jnp.minimum((i + 1) * blocks_per_tile, last_halo_block), 0))]
        args += [x, x]
    if has_emb:
        in_specs.append(pl.BlockSpec((tm, D_MODEL), lambda b, i: (i, 0)))
        args.append(emb)
        if has_halo:
            in_specs += [
                pl.BlockSpec((HALO, D_MODEL), lambda b, i: (jnp.maximum(i * blocks_per_tile - 1, 0), 0)),
                pl.BlockSpec((HALO, D_MODEL),
                             lambda b, i: (jnp.minimum((i + 1) * blocks_per_tile, last_halo_block), 0))]
            args += [emb, emb]
    mod_map = (lambda b, i: (b, 0, 0)) if per_batch_mod else (lambda b, i: (0, 0, 0))
    in_specs += [pl.BlockSpec((1, 3, D_MODEL), mod_map),
                 pl.BlockSpec((1, D_MODEL), lambda b, i: (0, 0)),
                 pl.BlockSpec((D_MODEL, D_IN), lambda b, i: (0, 0), pipeline_mode=pl.Buffered(1)),
                 pl.BlockSpec((1, D_SHIFT), lambda b, i: (0, 0))]
    args += [mod, norm_g, w_in_bf16, mu]
    outs = ((2 * D_RWKV, BF16), (D_RWKV, BF16), (N_DIR * 2 * LORA, F32), (D_RWKV, BF16), (D_FNET, BF16),
            (D_FNET, BF16))
    out_specs = [pl.BlockSpec((nb, tt, w), lambda b, i: (b, i, 0)) for w, _ in outs]
    out_shape = [jax.ShapeDtypeStruct((B, T, w), dt) for w, dt in outs]
    kern = functools.partial(_in_proj_kernel, has_emb=has_emb, has_halo=has_halo, n_tiles=n_tiles)
    return pl.pallas_call(
        kern, grid=(B // nb, n_tiles), in_specs=in_specs, out_specs=out_specs, out_shape=out_shape,
        compiler_params=pltpu.CompilerParams(dimension_semantics=("arbitrary", "arbitrary"),
                                             vmem_limit_bytes=VMEM_LIMIT),
        name="in_proj",
    )(*args)


def _same_block(i, j, size):
    shift = size.bit_length() - 1
    return jnp.right_shift(i, shift) == jnp.right_shift(j, shift)


def _block_diag_rows(y, head0_lanes):
    return jnp.concatenate([jnp.where(head0_lanes, y, 0.0), jnp.where(head0_lanes, 0.0, y)], axis=0)


def _scan_kernel(*refs, zero_init, write_state, n_chunks, n_seq):
    it = iter(refs)
    rk_refs, v_refs, lwla_refs = [None, None], [None, None], [None, None]
    for d in range(N_DIR):
        rk_refs[d], v_refs[d], lwla_refs[d] = next(it), next(it), next(it)
    w0_ref, w2_ref, a0_ref, a2_ref = (next(it) for _ in range(4))
    kk_ref, ka_ref, tri_ref, ones_ref = (next(it) for _ in range(4))
    s0_ref = None if zero_init else next(it)
    y_refs = [next(it), next(it)]
    sfin_refs = [next(it), next(it)] if write_state else None
    h_scr = next(it)

    j = pl.program_id(1)

    @pl.when(j == 0)
    def _():
        if zero_init:
            h_scr[...] = jnp.zeros_like(h_scr)
        else:
            h_scr[...] = s0_ref[...]

    C = CHUNK
    row = lax.broadcasted_iota(jnp.int32, (C, LANES), 0)
    lane = lax.broadcasted_iota(jnp.int32, (C, LANES), 1)
    s_idx = jnp.bitwise_and(lane, HEAD_DIM - 1)
    head0 = lane < HEAD_DIM
    row_c = lax.broadcasted_iota(jnp.int32, (SUB, LANES), 0)
    lane_c = lax.broadcasted_iota(jnp.int32, (SUB, LANES), 1)
    col_c = jnp.bitwise_and(lane_c, SUB - 1)
    blk_c = jnp.right_shift(jnp.bitwise_and(lane_c, HEAD_DIM - 1), SUB_SHIFT)
    lane_blk_c = jnp.right_shift(lane_c, SUB_SHIFT)
    eye_c = (row_c == col_c).astype(F32)

    def bd_c(y):
        return jnp.concatenate([jnp.where(lane_blk_c == g, y, 0.0) for g in range(LANES // SUB)], axis=0)

    def inverse_size4(l_c, upper):
        r4 = jnp.bitwise_and(row_c, 3)
        c4 = jnp.bitwise_and(col_c, 3)
        down, up = (lambda x, k: pltpu.roll(x, k, 0)), (lambda x, k: pltpu.roll(x, SUB - k, 0))
        right, left = (lambda x, k: pltpu.roll(x, k, 1)), (lambda x, k: pltpu.roll(x, LANES - k, 1))
        s2 = jnp.where(_same_block(row_c, col_c, 2), l_c, 0.0)
        e4 = jnp.where(_same_block(row_c, col_c, 4), l_c, 0.0) - s2
        if upper:
            s_col = jnp.where(r4 == 1, up(s2, 1), up(s2, 2))
            s_row = jnp.where(c4 == 2, right(s2, 1), right(s2, 2))
            e_s = jnp.where(c4 == 3, right(e4, 1) * s_col, 0.0)
            s_e = jnp.where(r4 == 0, s_row * up(e4, 1), 0.0)
            s_e_s = jnp.where((r4 == 0) & (c4 == 3), s_row * up(e_s, 1), 0.0)
        else:
            s_col = jnp.where(r4 == 2, down(s2, 1), down(s2, 2))
            s_row = jnp.where(c4 == 1, left(s2, 1), left(s2, 2))
            e_s = jnp.where(c4 == 0, left(e4, 1) * s_col, 0.0)
            s_e = jnp.where(r4 == 3, s_row * down(e4, 1), 0.0)
            s_e_s = jnp.where((r4 == 3) & (c4 == 0), s_row * down(e_s, 1), 0.0)
        return eye_c - s2 - e4 + e_s + s_e - s_e_s

    row2 = lax.broadcasted_iota(jnp.int32, (LANES, LANES), 0)
    lane2 = lax.broadcasted_iota(jnp.int32, (LANES, LANES), 1)
    same_head = (row2 < HEAD_DIM) == (lane2 < HEAD_DIM)
    decay_scale = math.exp(-0.5)

    seq_dir = [(n, d) for n in range(n_seq) for d in range(N_DIR)]
    rk_all = {nd: rk_refs[nd[1]][nd[0]].astype(F32) for nd in seq_dir}
    kkr_all = {nd: rk_all[nd][:, D_RWKV:] * kk_ref[...] for nd in seq_dir}
    ssq_rows = _bdot(jnp.concatenate([kkr_all[nd] * kkr_all[nd] for nd in seq_dir], axis=0), ones_ref[...])
    z_w_rows, z_a_rows = [], []
    for d in range(N_DIR):
        ll_rows = jnp.concatenate([lwla_refs[d][n] for n in range(n_seq)], axis=0)
        z_w_rows.append(w0_ref[d:d + 1, :] + _bdot(jnp.tanh(ll_rows), w2_ref[d]))
        z_a_rows.append(a0_ref[d:d + 1, :] + _bdot(ll_rows, a2_ref[d]))

    probs = []
    for i, (n, d) in enumerate(seq_dir):
        rk = rk_all[(n, d)]
        r = rk[:, :D_RWKV]
        k = rk[:, D_RWKV:]
        v = v_refs[d][n]
        logw = -decay_scale * jax.nn.sigmoid(z_w_rows[d][n * C:(n + 1) * C])
        a = jax.nn.sigmoid(z_a_rows[d][n * C:(n + 1) * C])
        kd = k * (1.0 + (a - 1.0) * ka_ref[...])
        kkr = kkr_all[(n, d)]
        ssq = ssq_rows[i * C:(i + 1) * C]
        kk = kkr * lax.rsqrt(jnp.maximum(ssq, 1e-24))
        bvec = kk * a
        lw_hi, lw_lo = _split2(logw)
        tri = tri_ref[d]
        cum = (jnp.dot(tri, lw_hi, preferred_element_type=F32)
               + jnp.dot(tri, lw_lo, preferred_element_type=F32))
        cum_prev = cum - logw
        tot = cum[C - 1:C, :] if d == 0 else cum[0:1, :]
        kap_t = kk * jnp.exp(cum_prev)
        r_t = r * jnp.exp(cum)
        e_neg = jnp.exp(-cum)
        gam = jnp.exp(tot)
        b_t = bvec * e_neg
        k_t = kd * e_neg
        e_rem = gam * e_neg
        b_h = bvec * e_rem
        k_h = kd * e_rem
        if d == 0:
            strict, incl = row > s_idx, row >= s_idx
        else:
            strict, incl = row < s_idx, row <= s_idx

        for p in range(N_PAIRS):
            sl = slice(p * LANES, (p + 1) * LANES)
            dup_t = lambda x: jnp.where(same_head, jnp.concatenate([x[:, sl], x[:, sl]], axis=0).T, 0.0)
            probs.append(dict(n=n, d=d, p=p, sl=sl, strict=strict, incl=incl,
                              lhs=jnp.concatenate([kap_t[:, sl], r_t[:, sl]], axis=0).astype(BF16),
                              v=v[:, sl], w_lm=jnp.concatenate([dup_t(b_t), dup_t(k_t)], axis=1).astype(BF16),
                              t2=jnp.concatenate([b_h[:, sl], k_h[:, sl]], axis=0).T.astype(BF16),
                              gam_col=jnp.broadcast_to(gam[:, sl], (LANES, LANES)).T))

    bd = lambda y: _block_diag_rows(y, head0)
    for q in probs:
        q["h"] = h_scr[q["n"], q["d"], q["p"]]
        lm = jnp.dot(q["lhs"], q["w_lm"], preferred_element_type=F32)
        q["l_b"] = jnp.where(q["strict"], lm[:C, :LANES], 0.0)
        q["m_b"] = jnp.where(q["incl"], lm[C:, :LANES], 0.0)
        q["lm_k"] = jnp.concatenate([jnp.where(q["strict"], lm[:C, LANES:], 0.0),
                                     jnp.where(q["incl"], lm[C:, LANES:], 0.0)], axis=0)
    for q in probs:
        lmv = _bdot(q["lm_k"], bd(q["v"]))
        q["l_kv"], q["m_kv"] = lmv[:C], lmv[C:]
    for q in probs:
        q["kr_h"] = _bdot(q["lhs"], q["h"])
    for q in probs:
        l_b = q["l_b"]
        l_c = l_b[0:SUB]
        for jb in range(1, C // SUB):
            l_c = jnp.where(blk_c == jb, l_b[jb * SUB:(jb + 1) * SUB], l_c)
        q["l_c"] = l_c
        q["t_c"] = inverse_size4(l_c, q["d"] == 1)
    s = 4
    while s < SUB:
        off_mask = _same_block(row_c, col_c, 2 * s) & ~_same_block(row_c, col_c, s)
        for q in probs:
            q["et"] = _bdot(jnp.where(off_mask, q["l_c"], 0.0), bd_c(q["t_c"]))
        for q in probs:
            q["t_c"] = q["t_c"] - _bdot(q["t_c"], bd_c(q["et"]))
        s *= 2
    for q in probs:
        q["tinv"] = jnp.concatenate([jnp.where(blk_c == jb, q["t_c"], 0.0) for jb in range(C // SUB)], axis=0)
    while s < C:
        off_mask = _same_block(row, s_idx, 2 * s) & ~_same_block(row, s_idx, s)
        for q in probs:
            q["et"] = _bdot(jnp.where(off_mask, q["l_b"], 0.0), bd(q["tinv"]))
        for q in probs:
            q["tinv"] = q["tinv"] - _bdot(q["tinv"], bd(q["et"]))
        s *= 2
    for q in probs:
        q["u_n"] = _bdot(q["tinv"], bd(q["kr_h"][:C] + q["l_kv"]))
    for q in probs:
        y = q["kr_h"][C:] + q["m_kv"] - _bdot(q["m_b"], bd(q["u_n"]))
        y_refs[q["d"]][q["n"], :, q["sl"]] = y.astype(BF16)
    for q in probs:
        upd = _bdot(q["t2"], jnp.concatenate([(-q["u_n"]).astype(BF16), q["v"]], axis=0))
        h_scr[q["n"], q["d"], q["p"]] = q["gam_col"] * q["h"] + jnp.where(same_head, upd, 0.0)

    if write_state:
        @pl.when(j == n_chunks - 1)
        def _():
            for n, d, p in [(n, d, p) for n in range(n_seq) for d in range(N_DIR) for p in range(N_PAIRS)]:
                ht = h_scr[n, d, p].T
                sfin_refs[d][n, 0, PAIR * p] = ht[:HEAD_DIM, :HEAD_DIM]
                sfin_refs[d][n, 0, PAIR * p + 1] = ht[HEAD_DIM:, HEAD_DIM:]


def _scan_call(rk, v, lwla, wts, s0_bd, write_state):
    B, T, _ = rk.shape
    nc = T // CHUNK
    ns = min(SCAN_SEQS, B)
    assert B % ns == 0
    zero_init = s0_bd is None
    fwd = lambda b, j: (b, j, 0)
    bwd = lambda b, j: (b, nc - 1 - j, 0)
    full = lambda *shape: pl.BlockSpec(shape, lambda b, j: (0,) * len(shape))
    tok = lambda w, m: pl.BlockSpec((ns, CHUNK, w), m)
    in_specs = [tok(2 * D_RWKV, fwd), tok(D_RWKV, fwd), tok(LANES, fwd),
                tok(2 * D_RWKV, bwd), tok(D_RWKV, bwd), tok(LANES, bwd),
                full(N_DIR, D_RWKV), full(N_DIR, LANES, D_RWKV),
                full(N_DIR, D_RWKV), full(N_DIR, LANES, D_RWKV),
                full(1, D_RWKV), full(1, D_RWKV), full(N_DIR, CHUNK, CHUNK), full(D_RWKV, D_RWKV)]
    args = [rk, v, lwla, rk, v, lwla, wts["w0"], wts["w2"], wts["a0"], wts["a2"],
            wts["k_k"], wts["k_a"], wts["tri"], wts["ones_bd"]]
    state_block = (ns, N_DIR, N_PAIRS, LANES, LANES)
    if not zero_init:
        in_specs.append(pl.BlockSpec(state_block, lambda b, j: (b, 0, 0, 0, 0)))
        args.append(s0_bd)
    out_specs = [pl.BlockSpec((ns, CHUNK, D_RWKV), fwd), pl.BlockSpec((ns, CHUNK, D_RWKV), bwd)]
    out_shape = [jax.ShapeDtypeStruct((B, T, D_RWKV), BF16), jax.ShapeDtypeStruct((B, T, D_RWKV), BF16)]
    if write_state:
        final_block = (ns, 1, N_HEADS, HEAD_DIM, HEAD_DIM)
        out_specs += [pl.BlockSpec(final_block, lambda b, j: (b, 0, 0, 0, 0))] * N_DIR
        out_shape += [jax.ShapeDtypeStruct((B,) + final_block[1:], F32)] * N_DIR
    kern = functools.partial(_scan_kernel, zero_init=zero_init, write_state=write_state, n_chunks=nc,
                             n_seq=ns)
    return pl.pallas_call(
        kern, grid=(B // ns, nc), in_specs=in_specs, out_specs=out_specs, out_shape=out_shape,
        scratch_shapes=[pltpu.VMEM(state_block, F32)],
        compiler_params=pltpu.CompilerParams(dimension_semantics=("arbitrary", "arbitrary"),
                                             vmem_limit_bytes=VMEM_LIMIT),
        name="scan",
    )(*args)


def _fnet_kernel(fin_ref, gf_ref, dft_ref, cs_ref, wbd_ref, b_ref, o_ref, g_scr):
    u = pl.program_id(1)
    nb, seq_len, _ = fin_ref.shape
    tu = o_ref.shape[1]

    @pl.when(u == 0)
    def _():
        step = min(seq_len, ROW_TILE)
        for n in range(nb):
            for t0 in range(0, seq_len, step):
                fc = _bdot(fin_ref[n, t0:t0 + step, :], cs_ref[...])
                g_scr[n, t0:t0 + step, :] = fc[:, :D_FNET].astype(BF16)
                g_scr[n, seq_len + t0:seq_len + t0 + step, :] = fc[:, D_FNET:].astype(BF16)

    f_re = jnp.concatenate([jnp.dot(dft_ref[...], g_scr[n], preferred_element_type=F32) for n in range(nb)],
                           axis=0)
    f_out = _bdot(f_re, wbd_ref[...]) + b_ref[...]
    gate = _silu(gf_ref[...].astype(F32)).reshape(nb * tu, D_FNET)
    o_ref[...] = (f_out * gate).astype(BF16).reshape(nb, tu, D_FNET)


def _fnet_call(fin, gf, dft_bf16, cs_bf16, wbd_bf16, b_fnet):
    B, T, _ = fin.shape
    tu = min(T, ROW_TILE)
    nb = max(1, FNET_ROWS // T)
    assert B % nb == 0 and T % tu == 0
    return pl.pallas_call(
        _fnet_kernel,
        grid=(B // nb, T // tu),
        in_specs=[pl.BlockSpec((nb, T, D_FNET), lambda b, u: (b, 0, 0)),
                  pl.BlockSpec((nb, tu, D_FNET), lambda b, u: (b, u, 0)),
                  pl.BlockSpec((tu, 2 * T), lambda b, u: (u, 0)),
                  pl.BlockSpec((D_FNET, 2 * D_FNET), lambda b, u: (0, 0)),
                  pl.BlockSpec((D_FNET, D_FNET), lambda b, u: (0, 0)),
                  pl.BlockSpec((1, D_FNET), lambda b, u: (0, 0))],
        out_specs=pl.BlockSpec((nb, tu, D_FNET), lambda b, u: (b, u, 0)),
        out_shape=jax.ShapeDtypeStruct((B, T, D_FNET), BF16),
        scratch_shapes=[pltpu.VMEM((nb, 2 * T, D_FNET), BF16)],
        compiler_params=pltpu.CompilerParams(dimension_semantics=("arbitrary", "arbitrary"),
                                             vmem_limit_bytes=VMEM_LIMIT),
        name="fnet",
    )(fin, gf, dft_bf16, cs_bf16, wbd_bf16, b_fnet)


def _out_kernel(*refs, has_emb, final_norm):
    it = iter(refs)
    x_ref = next(it)
    emb_ref = next(it) if has_emb else None
    (mod_ref, yf_ref, yb_ref, rk_ref, v_ref, lwla_ref, grec_ref, fo_ref, a0_ref, a2_ref, ka_ref,
     rkw_ref, gng_ref, gnb_ref, avg_ref, ones_ref, wout_ref, fng_ref, o_ref) = (next(it) for _ in range(19))

    nb, tt, _ = x_ref.shape
    tm = nb * tt
    rows = lambda ref: ref[...].reshape(tm, ref.shape[-1])
    y = rows(yf_ref).astype(F32) + rows(yb_ref).astype(F32)
    mu = _bdot(y, avg_ref[...])
    dlt = y - mu
    var = _bdot(dlt * dlt, avg_ref[...])
    y_n = dlt * lax.rsqrt(var + GN_EPS) * gng_ref[...] + gnb_ref[...]
    rk = rows(rk_ref).astype(F32)
    r = rk[:, :D_RWKV]
    k = rk[:, D_RWKV:]
    ll = rows(lwla_ref)
    a_sum = (jax.nn.sigmoid(a0_ref[0:1, :] + _bdot(ll, a2_ref[0]))
             + jax.nn.sigmoid(a0_ref[1:2, :] + _bdot(ll, a2_ref[1])))
    k_sum = k * (2.0 + (a_sum - 2.0) * ka_ref[...])
    bonus = _bdot(r * k_sum * rkw_ref[...], ones_ref[...]) * rows(v_ref).astype(F32)
    rec_out = (y_n + bonus) * _silu(rows(grec_ref).astype(F32))
    mixed = jnp.concatenate([rec_out.astype(BF16), rows(fo_ref)], axis=-1)
    out = jnp.dot(mixed, wout_ref[...], preferred_element_type=F32)
    x = rows(x_ref)
    if has_emb:
        x = x + emb_ref[...]
    z = x + mod_ref[0, 2:3, :] * out
    if final_norm:
        ms = jnp.mean(z * z, axis=-1, keepdims=True)
        z = z * lax.rsqrt(ms + NORM_EPS) * fng_ref[...]
    o_ref[...] = z.reshape(nb, tt, D_MODEL)


def _out_call(x, emb, mod, y_f, y_b, rk, v, lwla, grec, fo, wts, w_out_bf16, final_norm_g, final_norm):
    B, T, _ = x.shape
    tt = min(T, ROW_TILE)
    nb = max(1, ROW_TILE // T)
    has_emb = emb is not None
    per_batch_mod = mod.shape[0] > 1
    assert B % nb == 0 and T % tt == 0 and (nb == 1 or not (has_emb or per_batch_mod))
    tok = lambda w: pl.BlockSpec((nb, tt, w), lambda b, i: (b, i, 0))
    full = lambda *shape: pl.BlockSpec(shape, lambda b, i: (0,) * len(shape))
    in_specs = [tok(D_MODEL)]
    args = [x]
    if has_emb:
        in_specs.append(pl.BlockSpec((tt, D_MODEL), lambda b, i: (i, 0)))
        args.append(emb)
    mod_map = (lambda b, i: (b, 0, 0)) if per_batch_mod else (lambda b, i: (0, 0, 0))
    in_specs += [pl.BlockSpec((1, 3, D_MODEL), mod_map), tok(D_RWKV), tok(D_RWKV), tok(2 * D_RWKV),
                 tok(D_RWKV), tok(LANES), tok(D_RWKV), tok(D_FNET),
                 full(N_DIR, D_RWKV), full(N_DIR, LANES, D_RWKV),
                 full(1, D_RWKV), full(1, D_RWKV), full(1, D_RWKV), full(1, D_RWKV),
                 full(D_RWKV, D_RWKV), full(D_RWKV, D_RWKV), full(D_MODEL, D_MODEL), full(1, D_MODEL)]
    args += [mod, y_f, y_b, rk, v, lwla, grec, fo, wts["a0"], wts["a2"], wts["k_a"],
             wts["r_k"], wts["gn_g"], wts["gn_b"], wts["avg_bd"], wts["ones_bd"], w_out_bf16, final_norm_g]
    return pl.pallas_call(
        functools.partial(_out_kernel, has_emb=has_emb, final_norm=final_norm),
        grid=(B // nb, T // tt), in_specs=in_specs,
        out_specs=tok(D_MODEL),
        out_shape=jax.ShapeDtypeStruct((B, T, D_MODEL), F32),
        compiler_params=pltpu.CompilerParams(dimension_semantics=("arbitrary", "arbitrary"),
                                             vmem_limit_bytes=VMEM_LIMIT),
        name="out_proj",
    )(*args)


def _dft_table(seq_len):
    idx = np.arange(seq_len, dtype=np.int64)
    ang = 2.0 * np.pi * ((idx[:, None] * idx[None, :]) % seq_len).astype(np.float64) / seq_len
    scale = 1.0 / math.sqrt(seq_len)
    return np.concatenate([np.cos(ang) * scale, -np.sin(ang) * scale], axis=1).astype(np.float32)


def _channel_dft_table():
    n = FNET_GROUP
    idx = np.arange(n, dtype=np.int64)
    ang = 2.0 * np.pi * ((idx[:, None] * idx[None, :]) % n).astype(np.float64) / n
    c = np.cos(ang) / math.sqrt(n)
    s = np.sin(ang) / math.sqrt(n)
    eye = np.eye(D_FNET // n)
    return np.concatenate([np.kron(eye, c), np.kron(eye, s)], axis=1).astype(np.float32)


def _sincos_2d(n_tokens):
    rows = n_tokens // GRID_W
    pos = np.arange(rows * GRID_W)
    row = (pos // GRID_W).astype(np.float32)
    col = (pos % GRID_W).astype(np.float32)
    quarter = D_MODEL // 4
    freq = np.exp(np.float32(-math.log(POS_BASE)) * np.arange(quarter, dtype=np.float32) / np.float32(quarter))
    ang_r = row[:, None] * freq
    ang_c = col[:, None] * freq
    return np.concatenate([np.sin(ang_r), np.cos(ang_r), np.sin(ang_c), np.cos(ang_c)], axis=-1).astype(np.float32)


def _head_block_matrix(value):
    blk = np.kron(np.eye(N_HEADS), np.ones((HEAD_DIM, HEAD_DIM))) * value
    return jnp.asarray(blk, dtype=BF16)


def _pad_lora(w, row_offset):
    rows = [jnp.pad(w[d], ((row_offset + d * LORA, LANES - row_offset - (d + 1) * LORA), (0, 0)))
            for d in range(N_DIR)]
    return jnp.stack(rows).astype(BF16)


def _layer_weights(l, w0, w2, a0, a2, k_k, k_a, r_k, gn_g, gn_b):
    w2_p = _pad_lora(w2[l], 0)
    a2_p = _pad_lora(a2[l], N_DIR * LORA)
    tri_f = np.tril(np.ones((CHUNK, CHUNK)))
    tri = jnp.asarray(np.stack([tri_f, tri_f.T]), dtype=BF16)
    return dict(w0=w0[l], w2=w2_p, a0=a0[l], a2=a2_p,
                k_k=k_k[l][None], k_a=k_a[l][None], r_k=r_k[l].reshape(1, D_RWKV),
                gn_g=gn_g[l].reshape(1, D_RWKV), gn_b=gn_b[l].reshape(1, D_RWKV), tri=tri,
                ones_bd=_head_block_matrix(1.0), avg_bd=_head_block_matrix(1.0 / HEAD_DIM))


def _state_to_block_diag(s_f, s_b):
    def one(s):
        h = jnp.swapaxes(s.astype(F32), -1, -2)
        b = h.shape[0]
        h = h.reshape(b, N_PAIRS, PAIR, HEAD_DIM, HEAD_DIM)
        z = jnp.zeros_like(h[:, :, 0])
        top = jnp.concatenate([h[:, :, 0], z], axis=-1)
        bot = jnp.concatenate([z, h[:, :, 1]], axis=-1)
        return jnp.concatenate([top, bot], axis=-2)
    return jnp.stack([one(s_f), one(s_b)], axis=1)


def kernel(x_prompt, x_sample, state_rwkv_fwd, state_rwkv_bwd, c, c_ctx, w_ada, b_ada, norm_g, w_in,
           mu_shift, w0, w2, a0, a2, k_k, k_a, r_k, gn_g, gn_b, w_fnet, b_fnet, w_out, final_norm_g):
    depth = w_in.shape[0]
    n_dec = c.shape[0]
    assert n_dec + 1 <= 8
    bp, tp, _ = x_prompt.shape
    bs, ts, _ = x_sample.shape
    cvec = jnp.concatenate([c_ctx[None], c, jnp.zeros((7 - n_dec, D_MODEL), F32)], axis=0)
    emb = jnp.asarray(_sincos_2d(ts)).astype(x_sample.dtype)
    cs_tab = jnp.asarray(_channel_dft_table()).astype(BF16)
    dft_p = jnp.asarray(_dft_table(tp)).astype(BF16)
    dft_s = jnp.asarray(_dft_table(ts)).astype(BF16)
    fng = final_norm_g[None]

    xp, xs = x_prompt, x_sample
    new_f, new_b = [], []
    for l in range(depth):
        mod = _mod_call(cvec, w_ada[l], b_ada[l][None]).reshape(8, 3, D_MODEL)
        mod_ctx, mod_lat = mod[0:1], mod[1:1 + n_dec]
        wts = _layer_weights(l, w0, w2, a0, a2, k_k, k_a, r_k, gn_g, gn_b)
        w_in_b = w_in[l].astype(BF16)
        w_out_b = w_out[l].astype(BF16)
        n_grp = w_fnet.shape[1]
        wbd = (w_fnet[l][:, :, None, :] * jnp.eye(n_grp, dtype=F32)[:, None, :, None]).reshape(
            D_FNET, D_FNET).astype(BF16)
        ng, mu, bf = norm_g[l][None], mu_shift[l][None], b_fnet[l][None]
        emb_l = emb if l == 0 else None

        rk, v, lwla, grec, fin, gf = _in_proj_call(xp, None, mod_ctx, ng, w_in_b, mu)
        y_f, y_b, s_f, s_b = _scan_call(rk, v, lwla, wts, None, True)
        fo = _fnet_call(fin, gf, dft_p, cs_tab, wbd, bf)
        last = l == depth - 1
        xp = _out_call(xp, None, mod_ctx, y_f, y_b, rk, v, lwla, grec, fo, wts, w_out_b, fng, last)
        new_f.append(s_f)
        new_b.append(s_b)

        rk, v, lwla, grec, fin, gf = _in_proj_call(xs, emb_l, mod_lat, ng, w_in_b, mu)
        s0 = _state_to_block_diag(state_rwkv_fwd[:, l], state_rwkv_bwd[:, l])
        y_f, y_b = _scan_call(rk, v, lwla, wts, s0, False)
        fo = _fnet_call(fin, gf, dft_s, cs_tab, wbd, bf)
        xs = _out_call(xs, emb_l, mod_lat, y_f, y_b, rk, v, lwla, grec, fo, wts, w_out_b, fng, last)
    return (xp, xs, jnp.concatenate(new_f, axis=1), jnp.concatenate(new_b, axis=1))
```

```python
import functools
import math

import numpy as np
import jax
import jax.numpy as jnp
from jax import lax
from jax.experimental import pallas as pl
from jax.experimental.pallas import tpu as pltpu

F32 = jnp.float32
BF16 = jnp.bfloat16

D_MODEL = 1024
GRID_W = 64
D_RWKV = 512
D_FNET = D_MODEL - D_RWKV
HEAD_DIM = 64
N_HEADS = D_RWKV // HEAD_DIM
FNET_GROUP = 64
LORA = 32
N_DIR = 2
D_SHIFT = 3 * D_RWKV + N_DIR * 2 * LORA
D_IN = D_SHIFT + D_RWKV + 2 * D_FNET
NORM_EPS = 1e-6
GN_EPS = 64e-5
POS_BASE = 10000.0

LANES = 128
PAIR = LANES // HEAD_DIM
N_PAIRS = N_HEADS // PAIR
CHUNK = 64
SUB = 16
SUB_SHIFT = SUB.bit_length() - 1
SCAN_SEQS = 8
SCAN_SUB_MAX = 2
ROW_TILE = 512
ROW_SPLIT = 4
FNET_ROWS = 1024
HALO = 8
VMEM_LIMIT = 56 * 1024 * 1024


def _silu(x):
    return x * jax.nn.sigmoid(x)


def _bdot(a, b):
    return jnp.dot(a.astype(BF16), b.astype(BF16), preferred_element_type=F32)


def _split2(x):
    hi = x.astype(BF16)
    lo = (x - hi.astype(F32)).astype(BF16)
    return hi, lo


def _mod_kernel(c_ref, w_ref, b_ref, o_ref):
    @pl.when(pl.program_id(0) == 0)
    def _():
        o_ref[...] = jnp.broadcast_to(b_ref[...], o_ref.shape)

    s_hi, s_lo = _split2(_silu(c_ref[...]))
    w = w_ref[...].astype(BF16)
    o_ref[...] += jnp.dot(s_hi, w, preferred_element_type=F32) + jnp.dot(s_lo, w, preferred_element_type=F32)


def _mod_call(cvec, w_ada, b_ada):
    rows = D_MODEL // 4
    return pl.pallas_call(
        _mod_kernel,
        grid=(D_MODEL // rows,),
        in_specs=[pl.BlockSpec((8, rows), lambda i: (0, i)),
                  pl.BlockSpec((rows, 3 * D_MODEL), lambda i: (i, 0)),
                  pl.BlockSpec((1, 3 * D_MODEL), lambda i: (0, 0))],
        out_specs=pl.BlockSpec((8, 3 * D_MODEL), lambda i: (0, 0)),
        out_shape=jax.ShapeDtypeStruct((8, 3 * D_MODEL), F32),
        compiler_params=pltpu.CompilerParams(dimension_semantics=("arbitrary",),
                                             vmem_limit_bytes=VMEM_LIMIT),
        name="mod",
    )(cvec, w_ada, b_ada)


def _modulated_norm(x, g, scale, shift):
    ms = jnp.mean(x * x, axis=-1, keepdims=True)
    y = x * lax.rsqrt(ms + NORM_EPS) * g
    return y * (1.0 + scale) + shift


def _in_proj_kernel(*refs, has_emb, has_halo, n_tiles):
    it = iter(refs)
    x_ref = next(it)
    xp_ref = next(it) if has_halo else None
    xn_ref = next(it) if has_halo else None
    emb_ref = next(it) if has_emb else None
    embp_ref = next(it) if (has_emb and has_halo) else None
    embn_ref = next(it) if (has_emb and has_halo) else None
    mod_ref, g_ref, w_ref, mu_ref = next(it), next(it), next(it), next(it)
    rk_ref, v_ref, lwla_ref, grec_ref, fin_ref, gf_ref = (next(it) for _ in range(6))

    i = pl.program_id(1)
    g = g_ref[...]
    shift = mod_ref[0, 0:1, :]
    scale = mod_ref[0, 1:2, :]
    nb, tt, _ = x_ref.shape
    tm = nb * tt
    x = x_ref[...].reshape(tm, D_MODEL)
    if has_emb:
        x = x + emb_ref[...]
    part = tm // ROW_SPLIT
    p = jnp.concatenate(
        [jnp.dot(_modulated_norm(x[r0:r0 + part], g, scale, shift).astype(BF16), w_ref[...],
                 preferred_element_type=F32) for r0 in range(0, tm, part)], axis=0)
    ps = p[:, :D_SHIFT]
    if has_halo:
        xh = jnp.concatenate([xp_ref[0], xn_ref[0]], axis=0)
        if has_emb:
            xh = xh + jnp.concatenate([embp_ref[...], embn_ref[...]], axis=0)
        hh = _modulated_norm(xh, g, scale, shift)
        ph = jnp.dot(hh.astype(BF16), w_ref[:, :D_SHIFT], preferred_element_type=F32)
        prev_row = jnp.where(i > 0, ph[HALO - 1:HALO, :], 0.0)
        next_row = jnp.where(i < n_tiles - 1, ph[HALO:HALO + 1, :], 0.0)
    else:
        prev_row = jnp.zeros((1, D_SHIFT), F32)
        next_row = jnp.zeros((1, D_SHIFT), F32)
    pos = lax.rem(lax.broadcasted_iota(jnp.int32, (tm, D_SHIFT), 0), tt)
    prev = jnp.where(pos == 0, prev_row, pltpu.roll(ps, 1, 0))
    nxt = jnp.where(pos == tt - 1, next_row, pltpu.roll(ps, tm - 1, 0))
    p_rec = ps + mu_ref[...] * (0.5 * (prev + nxt) - ps)

    def put(ref, val):
        ref[...] = val.astype(ref.dtype).reshape(ref.shape)

    put(rk_ref, p_rec[:, :2 * D_RWKV])
    put(v_ref, p_rec[:, 2 * D_RWKV:3 * D_RWKV])
    put(lwla_ref, p_rec[:, 3 * D_RWKV:])
    put(grec_ref, p[:, D_SHIFT:D_SHIFT + D_RWKV])
    put(fin_ref, p[:, D_SHIFT + D_RWKV:D_SHIFT + D_RWKV + D_FNET])
    put(gf_ref, p[:, D_SHIFT + D_RWKV + D_FNET:])


def _in_proj_call(x, emb, mod, norm_g, w_in_bf16, mu):
    B, T, _ = x.shape
    tt = min(T, ROW_TILE)
    nb = max(1, ROW_TILE // T)
    n_tiles = T // tt
    has_halo = n_tiles > 1
    has_emb = emb is not None
    per_batch_mod = mod.shape[0] > 1
    assert B % nb == 0 and T % tt == 0 and (nb == 1 or not (has_emb or per_batch_mod))
    tm = tt
    blocks_per_tile = tm // HALO
    last_halo_block = T // HALO - 1

    in_specs = [pl.BlockSpec((nb, tt, D_MODEL), lambda b, i: (b, i, 0))]
    args = [x]
    if has_halo:
        in_specs += [
            pl.BlockSpec((1, HALO, D_MODEL), lambda b, i: (b, jnp.maximum(i * blocks_per_tile - 1, 0), 0)),
            pl.BlockSpec((1, HALO, D_MODEL),
                         lambda b, i: (b, jnp.minimum((i + 1) * blocks_per_tile, last_halo_block), 0))]
        args += [x, x]
    if has_emb:
        in_specs.append(pl.BlockSpec((tm, D_MODEL), lambda b, i: (i, 0)))
        args.append(emb)
        if has_halo:
            in_specs += [
                pl.BlockSpec((HALO, D_MODEL), lambda b, i: (jnp.maximum(i * blocks_per_tile - 1, 0), 0)),
                pl.BlockSpec((HALO, D_MODEL),
                             lambda b, i: (jnp.minimum((i + 1) * blocks_per_tile, last_halo_block), 0))]
            args += [emb, emb]
    mod_map = (lambda b, i: (b, 0, 0)) if per_batch_mod else (lambda b, i: (0, 0, 0))
    in_specs += [pl.BlockSpec((1, 3, D_MODEL), mod_map),
                 pl.BlockSpec((1, D_MODEL), lambda b, i: (0, 0)),
                 pl.BlockSpec((D_MODEL, D_IN), lambda b, i: (0, 0), pipeline_mode=pl.Buffered(1)),
                 pl.BlockSpec((1, D_SHIFT), lambda b, i: (0, 0))]
    args += [mod, norm_g, w_in_bf16, mu]
    outs = ((2 * D_RWKV, BF16), (D_RWKV, BF16), (N_DIR * 2 * LORA, F32), (D_RWKV, BF16), (D_FNET, BF16),
            (D_FNET, BF16))
    out_specs = [pl.BlockSpec((nb, tt, w), lambda b, i: (b, i, 0)) for w, _ in outs]
    out_shape = [jax.ShapeDtypeStruct((B, T, w), dt) for w, dt in outs]
    kern = functools.partial(_in_proj_kernel, has_emb=has_emb, has_halo=has_halo, n_tiles=n_tiles)
    return pl.pallas_call(
        kern, grid=(B // nb, n_tiles), in_specs=in_specs, out_specs=out_specs, out_shape=out_shape,
        compiler_params=pltpu.CompilerParams(dimension_semantics=("arbitrary", "arbitrary"),
                                             vmem_limit_bytes=VMEM_LIMIT),
        name="in_proj",
    )(*args)


def _same_block(i, j, size):
    shift = size.bit_length() - 1
    return jnp.right_shift(i, shift) == jnp.right_shift(j, shift)


def _block_diag_rows(y, head0_lanes):
    return jnp.concatenate([jnp.where(head0_lanes, y, 0.0), jnp.where(head0_lanes, 0.0, y)], axis=0)


def _scan_kernel(*refs, zero_init, write_state, n_steps, n_seq, n_sub):
    it = iter(refs)
    rk_refs, v_refs, lwla_refs = [None, None], [None, None], [None, None]
    for d in range(N_DIR):
        rk_refs[d], v_refs[d], lwla_refs[d] = next(it), next(it), next(it)
    w0_ref, w2_ref, a0_ref, a2_ref = (next(it) for _ in range(4))
    kk_ref, ka_ref, tri_ref, ones_ref = (next(it) for _ in range(4))
    s0_ref = None if zero_init else next(it)
    y_refs = [next(it), next(it)]
    sfin_refs = [next(it), next(it)] if write_state else None
    h_scr = next(it)

    j = pl.program_id(1)

    @pl.when(j == 0)
    def _():
        if zero_init:
            h_scr[...] = jnp.zeros_like(h_scr)
        else:
            h_scr[...] = s0_ref[...]

    C = CHUNK
    row = lax.broadcasted_iota(jnp.int32, (C, LANES), 0)
    lane = lax.broadcasted_iota(jnp.int32, (C, LANES), 1)
    s_idx = jnp.bitwise_and(lane, HEAD_DIM - 1)
    head0 = lane < HEAD_DIM
    row_c = lax.broadcasted_iota(jnp.int32, (SUB, LANES), 0)
    lane_c = lax.broadcasted_iota(jnp.int32, (SUB, LANES), 1)
    col_c = jnp.bitwise_and(lane_c, SUB - 1)
    blk_c = jnp.right_shift(jnp.bitwise_and(lane_c, HEAD_DIM - 1), SUB_SHIFT)
    lane_blk_c = jnp.right_shift(lane_c, SUB_SHIFT)
    eye_c = (row_c == col_c).astype(F32)

    def bd_c(y):
        return jnp.concatenate([jnp.where(lane_blk_c == g, y, 0.0) for g in range(LANES // SUB)], axis=0)

    def inverse_size4(l_c, upper):
        r4 = jnp.bitwise_and(row_c, 3)
        c4 = jnp.bitwise_and(col_c, 3)
        down, up = (lambda x, k: pltpu.roll(x, k, 0)), (lambda x, k: pltpu.roll(x, SUB - k, 0))
        right, left = (lambda x, k: pltpu.roll(x, k, 1)), (lambda x, k: pltpu.roll(x, LANES - k, 1))
        s2 = jnp.where(_same_block(row_c, col_c, 2), l_c, 0.0)
        e4 = jnp.where(_same_block(row_c, col_c, 4), l_c, 0.0) - s2
        if upper:
            s_col = jnp.where(r4 == 1, up(s2, 1), up(s2, 2))
            s_row = jnp.where(c4 == 2, right(s2, 1), right(s2, 2))
            e_s = jnp.where(c4 == 3, right(e4, 1) * s_col, 0.0)
            s_e = jnp.where(r4 == 0, s_row * up(e4, 1), 0.0)
            s_e_s = jnp.where((r4 == 0) & (c4 == 3), s_row * up(e_s, 1), 0.0)
        else:
            s_col = jnp.where(r4 == 2, down(s2, 1), down(s2, 2))
            s_row = jnp.where(c4 == 1, left(s2, 1), left(s2, 2))
            e_s = jnp.where(c4 == 0, left(e4, 1) * s_col, 0.0)
            s_e = jnp.where(r4 == 3, s_row * down(e4, 1), 0.0)
            s_e_s = jnp.where((r4 == 3) & (c4 == 0), s_row * down(e_s, 1), 0.0)
        return eye_c - s2 - e4 + e_s + s_e - s_e_s

    row2 = lax.broadcasted_iota(jnp.int32, (LANES, LANES), 0)
    lane2 = lax.broadcasted_iota(jnp.int32, (LANES, LANES), 1)
    same_head = (row2 < HEAD_DIM) == (lane2 < HEAD_DIM)
    decay_scale = math.exp(-0.5)

    row0 = lambda o, d: (o if d == 0 else n_sub - 1 - o) * C
    keys = [(o, n, d) for o in range(n_sub) for n in range(n_seq) for d in range(N_DIR)]
    rk_all = {(o, n, d): rk_refs[d][n, row0(o, d):row0(o, d) + C, :].astype(F32) for o, n, d in keys}
    kkr_all = {key: rk_all[key][:, D_RWKV:] * kk_ref[...] for key in keys}
    ssq_rows = _bdot(jnp.concatenate([kkr_all[key] * kkr_all[key] for key in keys], axis=0), ones_ref[...])
    dir_keys = [[key for key in keys if key[2] == d] for d in range(N_DIR)]
    z_w_rows, z_a_rows = [], []
    for d in range(N_DIR):
        ll_rows = jnp.concatenate([lwla_refs[d][n, row0(o, d):row0(o, d) + C, :] for o, n, _ in dir_keys[d]], axis=0)
        z_w_rows.append(w0_ref[d:d + 1, :] + _bdot(jnp.tanh(ll_rows), w2_ref[d]))
        z_a_rows.append(a0_ref[d:d + 1, :] + _bdot(ll_rows, a2_ref[d]))

    sets = [[] for _ in range(n_sub)]
    for i, (o, n, d) in enumerate(keys):
        rk = rk_all[(o, n, d)]
        r = rk[:, :D_RWKV]
        k = rk[:, D_RWKV:]
        v = v_refs[d][n, row0(o, d):row0(o, d) + C, :]
        i_d = dir_keys[d].index((o, n, d))
        logw = -decay_scale * jax.nn.sigmoid(z_w_rows[d][i_d * C:(i_d + 1) * C])
        a = jax.nn.sigmoid(z_a_rows[d][i_d * C:(i_d + 1) * C])
        kd = k * (1.0 + (a - 1.0) * ka_ref[...])
        kkr = kkr_all[(o, n, d)]
        ssq = ssq_rows[i * C:(i + 1) * C]
        kk = kkr * lax.rsqrt(jnp.maximum(ssq, 1e-24))
        bvec = kk * a
        lw_hi, lw_lo = _split2(logw)
        tri = tri_ref[d]
        cum = (jnp.dot(tri, lw_hi, preferred_element_type=F32)
               + jnp.dot(tri, lw_lo, preferred_element_type=F32))
        cum_prev = cum - logw
        tot = cum[C - 1:C, :] if d == 0 else cum[0:1, :]
        kap_t = kk * jnp.exp(cum_prev)
        r_t = r * jnp.exp(cum)
        e_neg = jnp.exp(-cum)
        gam = jnp.exp(tot)
        b_t = bvec * e_neg
        k_t = kd * e_neg
        e_rem = gam * e_neg
        b_h = bvec * e_rem
        k_h = kd * e_rem
        if d == 0:
            strict, incl = row > s_idx, row >= s_idx
        else:
            strict, incl = row < s_idx, row <= s_idx

        for p in range(N_PAIRS):
            sl = slice(p * LANES, (p + 1) * LANES)
            dup_t = lambda x: jnp.where(same_head, jnp.concatenate([x[:, sl], x[:, sl]], axis=0).T, 0.0)
            sets[o].append(dict(
                n=n, d=d, p=p, sl=sl, rows=slice(row0(o, d), row0(o, d) + C), strict=strict, incl=incl,
                prev=None if o == 0 else sets[o - 1][len(sets[o])],
                lhs=jnp.concatenate([kap_t[:, sl], r_t[:, sl]], axis=0).astype(BF16),
                v=v[:, sl], w_lm=jnp.concatenate([dup_t(b_t), dup_t(k_t)], axis=1).astype(BF16),
                t2=jnp.concatenate([b_h[:, sl], k_h[:, sl]], axis=0).T.astype(BF16),
                gam_col=jnp.broadcast_to(gam[:, sl], (LANES, LANES)).T))

    bd = lambda y: _block_diag_rows(y, head0)

    def st_scores(probs):
        for q in probs:
            lm = jnp.dot(q["lhs"], q["w_lm"], preferred_element_type=F32)
            q["l_b"] = jnp.where(q["strict"], lm[:C, :LANES], 0.0)
            q["m_b"] = jnp.where(q["incl"], lm[C:, :LANES], 0.0)
            q["lm_k"] = jnp.concatenate([jnp.where(q["strict"], lm[:C, LANES:], 0.0),
                                         jnp.where(q["incl"], lm[C:, LANES:], 0.0)], axis=0)

    def st_values(probs):
        for q in probs:
            lmv = _bdot(q["lm_k"], bd(q["v"]))
            q["l_kv"], q["m_kv"] = lmv[:C], lmv[C:]
            l_b = q["l_b"]
            l_c = l_b[0:SUB]
            for jb in range(1, C // SUB):
                l_c = jnp.where(blk_c == jb, l_b[jb * SUB:(jb + 1) * SUB], l_c)
            q["l_c"] = l_c
            q["t_c"] = inverse_size4(l_c, q["d"] == 1)

    def st_compact_a(s):
        def run(probs):
            off_mask = _same_block(row_c, col_c, 2 * s) & ~_same_block(row_c, col_c, s)
            for q in probs:
                q["et"] = _bdot(jnp.where(off_mask, q["l_c"], 0.0), bd_c(q["t_c"]))
        return run

    def st_compact_b(last):
        def run(probs):
            for q in probs:
                q["t_c"] = q["t_c"] - _bdot(q["t_c"], bd_c(q["et"]))
                if last:
                    q["tinv"] = jnp.concatenate([jnp.where(blk_c == jb, q["t_c"], 0.0) for jb in range(C // SUB)],
                                                axis=0)
        return run

    def st_full_a(s):
        def run(probs):
            off_mask = _same_block(row, s_idx, 2 * s) & ~_same_block(row, s_idx, s)
            for q in probs:
                q["et"] = _bdot(jnp.where(off_mask, q["l_b"], 0.0), bd(q["tinv"]))
        return run

    def st_full_b(probs):
        for q in probs:
            q["tinv"] = q["tinv"] - _bdot(q["tinv"], bd(q["et"]))

    def st_state_read(probs):
        for q in probs:
            q["h"] = h_scr[q["n"], q["d"], q["p"]] if q["prev"] is None else q["prev"]["h_new"]
            q["kr_h"] = _bdot(q["lhs"], q["h"])

    def st_solve(probs):
        for q in probs:
            q["u_n"] = _bdot(q["tinv"], bd(q["kr_h"][:C] + q["l_kv"]))

    def st_output(probs):
        for q in probs:
            y = q["kr_h"][C:] + q["m_kv"] - _bdot(q["m_b"], bd(q["u_n"]))
            y_refs[q["d"]][q["n"], q["rows"], q["sl"]] = y.astype(BF16)

    def st_state_write(probs):
        for q in probs:
            upd = _bdot(q["t2"], jnp.concatenate([(-q["u_n"]).astype(BF16), q["v"]], axis=0))
            q["h_new"] = q["gam_col"] * q["h"] + jnp.where(same_head, upd, 0.0)

    stages = [st_scores, st_values] + ([st_state_read] if n_sub == 1 else [])
    s = 4
    while s < SUB:
        stages += [st_compact_a(s), st_compact_b(2 * s == SUB)]
        s *= 2
    while s < C:
        stages += [st_full_a(s), st_full_b]
        s *= 2
    stages += ([] if n_sub == 1 else [st_state_read]) + [st_solve, st_output, st_state_write]
    skew = 4
    for t in range(len(stages) + skew * (n_sub - 1)):
        for o in range(n_sub):
            if 0 <= t - skew * o < len(stages):
                stages[t - skew * o](sets[o])
    for q in sets[-1]:
        h_scr[q["n"], q["d"], q["p"]] = q["h_new"]

    if write_state:
        @pl.when(j == n_steps - 1)
        def _():
            for n, d, p in [(n, d, p) for n in range(n_seq) for d in range(N_DIR) for p in range(N_PAIRS)]:
                ht = h_scr[n, d, p].T
                sfin_refs[d][n, 0, PAIR * p] = ht[:HEAD_DIM, :HEAD_DIM]
                sfin_refs[d][n, 0, PAIR * p + 1] = ht[HEAD_DIM:, HEAD_DIM:]


def _scan_call(rk, v, lwla, wts, s0_bd, write_state):
    B, T, _ = rk.shape
    ns = min(SCAN_SEQS, B)
    n_sub = max(1, min(SCAN_SEQS // ns, SCAN_SUB_MAX))
    rows = n_sub * CHUNK
    nc = T // rows
    assert B % ns == 0 and T % rows == 0
    zero_init = s0_bd is None
    fwd = lambda b, j: (b, j, 0)
    bwd = lambda b, j: (b, nc - 1 - j, 0)
    full = lambda *shape: pl.BlockSpec(shape, lambda b, j: (0,) * len(shape))
    tok = lambda w, m: pl.BlockSpec((ns, rows, w), m)
    in_specs = [tok(2 * D_RWKV, fwd), tok(D_RWKV, fwd), tok(LANES, fwd),
                tok(2 * D_RWKV, bwd), tok(D_RWKV, bwd), tok(LANES, bwd),
                full(N_DIR, D_RWKV), full(N_DIR, LANES, D_RWKV),
                full(N_DIR, D_RWKV), full(N_DIR, LANES, D_RWKV),
                full(1, D_RWKV), full(1, D_RWKV), full(N_DIR, CHUNK, CHUNK), full(D_RWKV, D_RWKV)]
    args = [rk, v, lwla, rk, v, lwla, wts["w0"], wts["w2"], wts["a0"], wts["a2"],
            wts["k_k"], wts["k_a"], wts["tri"], wts["ones_bd"]]
    state_block = (ns, N_DIR, N_PAIRS, LANES, LANES)
    if not zero_init:
        in_specs.append(pl.BlockSpec(state_block, lambda b, j: (b, 0, 0, 0, 0)))
        args.append(s0_bd)
    out_specs = [tok(D_RWKV, fwd), tok(D_RWKV, bwd)]
    out_shape = [jax.ShapeDtypeStruct((B, T, D_RWKV), BF16), jax.ShapeDtypeStruct((B, T, D_RWKV), BF16)]
    if write_state:
        final_block = (ns, 1, N_HEADS, HEAD_DIM, HEAD_DIM)
        out_specs += [pl.BlockSpec(final_block, lambda b, j: (b, 0, 0, 0, 0))] * N_DIR
        out_shape += [jax.ShapeDtypeStruct((B,) + final_block[1:], F32)] * N_DIR
    kern = functools.partial(_scan_kernel, zero_init=zero_init, write_state=write_state, n_steps=nc,
                             n_seq=ns, n_sub=n_sub)
    return pl.pallas_call(
        kern, grid=(B // ns, nc), in_specs=in_specs, out_specs=out_specs, out_shape=out_shape,
        scratch_shapes=[pltpu.VMEM(state_block, F32)],
        compiler_params=pltpu.CompilerParams(dimension_semantics=("arbitrary", "arbitrary"),
                                             vmem_limit_bytes=VMEM_LIMIT),
        name="scan",
    )(*args)


def _fnet_kernel(fin_ref, gf_ref, dft_ref, cs_ref, wbd_ref, b_ref, o_ref, g_scr):
    u = pl.program_id(1)
    nb, seq_len, _ = fin_ref.shape
    tu = o_ref.shape[1]

    @pl.when(u == 0)
    def _():
        step = min(seq_len, ROW_TILE)
        for n in range(nb):
            for t0 in range(0, seq_len, step):
                fc = _bdot(fin_ref[n, t0:t0 + step, :], cs_ref[...])
                g_scr[n, t0:t0 + step, :] = fc[:, :D_FNET].astype(BF16)
                g_scr[n, seq_len + t0:seq_len + t0 + step, :] = fc[:, D_FNET:].astype(BF16)

    f_re = jnp.concatenate([jnp.dot(dft_ref[...], g_scr[n], preferred_element_type=F32) for n in range(nb)],
                           axis=0)
    f_out = _bdot(f_re, wbd_ref[...]) + b_ref[...]
    gate = _silu(gf_ref[...].astype(F32)).reshape(nb * tu, D_FNET)
    o_ref[...] = (f_out * gate).astype(BF16).reshape(nb, tu, D_FNET)


def _fnet_call(fin, gf, dft_bf16, cs_bf16, wbd_bf16, b_fnet):
    B, T, _ = fin.shape
    tu = min(T, FNET_ROWS)
    nb = max(1, FNET_ROWS // T)
    assert B % nb == 0 and T % tu == 0
    return pl.pallas_call(
        _fnet_kernel,
        grid=(B // nb, T // tu),
        in_specs=[pl.BlockSpec((nb, T, D_FNET), lambda b, u: (b, 0, 0)),
                  pl.BlockSpec((nb, tu, D_FNET), lambda b, u: (b, u, 0)),
                  pl.BlockSpec((tu, 2 * T), lambda b, u: (u, 0)),
                  pl.BlockSpec((D_FNET, 2 * D_FNET), lambda b, u: (0, 0)),
                  pl.BlockSpec((D_FNET, D_FNET), lambda b, u: (0, 0)),
                  pl.BlockSpec((1, D_FNET), lambda b, u: (0, 0))],
        out_specs=pl.BlockSpec((nb, tu, D_FNET), lambda b, u: (b, u, 0)),
        out_shape=jax.ShapeDtypeStruct((B, T, D_FNET), BF16),
        scratch_shapes=[pltpu.VMEM((nb, 2 * T, D_FNET), BF16)],
        compiler_params=pltpu.CompilerParams(dimension_semantics=("arbitrary", "arbitrary"),
                                             vmem_limit_bytes=VMEM_LIMIT),
        name="fnet",
    )(fin, gf, dft_bf16, cs_bf16, wbd_bf16, b_fnet)


def _out_kernel(*refs, has_emb, final_norm):
    it = iter(refs)
    x_ref = next(it)
    emb_ref = next(it) if has_emb else None
    (mod_ref, yf_ref, yb_ref, rk_ref, v_ref, lwla_ref, grec_ref, fo_ref, a0_ref, a2_ref, ka_ref,
     rkw_ref, gng_ref, gnb_ref, avg_ref, ones_ref, wout_ref, fng_ref, o_ref) = (next(it) for _ in range(19))

    nb, tt, _ = x_ref.shape
    tm = nb * tt
    rows = lambda ref: ref[...].reshape(tm, ref.shape[-1])
    y = rows(yf_ref).astype(F32) + rows(yb_ref).astype(F32)
    mu = _bdot(y, avg_ref[...])
    dlt = y - mu
    var = _bdot(dlt * dlt, avg_ref[...])
    y_n = dlt * lax.rsqrt(var + GN_EPS) * gng_ref[...] + gnb_ref[...]
    rk = rows(rk_ref).astype(F32)
    r = rk[:, :D_RWKV]
    k = rk[:, D_RWKV:]
    ll = rows(lwla_ref)
    a_sum = (jax.nn.sigmoid(a0_ref[0:1, :] + _bdot(ll, a2_ref[0]))
             + jax.nn.sigmoid(a0_ref[1:2, :] + _bdot(ll, a2_ref[1])))
    k_sum = k * (2.0 + (a_sum - 2.0) * ka_ref[...])
    bonus = _bdot(r * k_sum * rkw_ref[...], ones_ref[...]) * rows(v_ref).astype(F32)
    rec_out = (y_n + bonus) * _silu(rows(grec_ref).astype(F32))
    mixed = jnp.concatenate([rec_out.astype(BF16), rows(fo_ref)], axis=-1)
    out = jnp.dot(mixed, wout_ref[...], preferred_element_type=F32)
    x = rows(x_ref)
    if has_emb:
        x = x + emb_ref[...]
    z = x + mod_ref[0, 2:3, :] * out
    if final_norm:
        ms = jnp.mean(z * z, axis=-1, keepdims=True)
        z = z * lax.rsqrt(ms + NORM_EPS) * fng_ref[...]
    o_ref[...] = z.reshape(nb, tt, D_MODEL)


def _out_call(x, emb, mod, y_f, y_b, rk, v, lwla, grec, fo, wts, w_out_bf16, final_norm_g, final_norm):
    B, T, _ = x.shape
    tt = min(T, ROW_TILE)
    nb = max(1, ROW_TILE // T)
    has_emb = emb is not None
    per_batch_mod = mod.shape[0] > 1
    assert B % nb == 0 and T % tt == 0 and (nb == 1 or not (has_emb or per_batch_mod))
    tok = lambda w: pl.BlockSpec((nb, tt, w), lambda b, i: (b, i, 0))
    full = lambda *shape: pl.BlockSpec(shape, lambda b, i: (0,) * len(shape))
    in_specs = [tok(D_MODEL)]
    args = [x]
    if has_emb:
        in_specs.append(pl.BlockSpec((tt, D_MODEL), lambda b, i: (i, 0)))
        args.append(emb)
    mod_map = (lambda b, i: (b, 0, 0)) if per_batch_mod else (lambda b, i: (0, 0, 0))
    in_specs += [pl.BlockSpec((1, 3, D_MODEL), mod_map), tok(D_RWKV), tok(D_RWKV), tok(2 * D_RWKV),
                 tok(D_RWKV), tok(LANES), tok(D_RWKV), tok(D_FNET),
                 full(N_DIR, D_RWKV), full(N_DIR, LANES, D_RWKV),
                 full(1, D_RWKV), full(1, D_RWKV), full(1, D_RWKV), full(1, D_RWKV),
                 full(D_RWKV, D_RWKV), full(D_RWKV, D_RWKV), full(D_MODEL, D_MODEL), full(1, D_MODEL)]
    args += [mod, y_f, y_b, rk, v, lwla, grec, fo, wts["a0"], wts["a2"], wts["k_a"],
             wts["r_k"], wts["gn_g"], wts["gn_b"], wts["avg_bd"], wts["ones_bd"], w_out_bf16, final_norm_g]
    return pl.pallas_call(
        functools.partial(_out_kernel, has_emb=has_emb, final_norm=final_norm),
        grid=(B // nb, T // tt), in_specs=in_specs,
        out_specs=tok(D_MODEL),
        out_shape=jax.ShapeDtypeStruct((B, T, D_MODEL), F32),
        compiler_params=pltpu.CompilerParams(dimension_semantics=("arbitrary", "arbitrary"),
                                             vmem_limit_bytes=VMEM_LIMIT),
        name="out_proj",
    )(*args)


def _dft_table(seq_len):
    idx = np.arange(seq_len, dtype=np.int64)
    ang = 2.0 * np.pi * ((idx[:, None] * idx[None, :]) % seq_len).astype(np.float64) / seq_len
    scale = 1.0 / math.sqrt(seq_len)
    return np.concatenate([np.cos(ang) * scale, -np.sin(ang) * scale], axis=1).astype(np.float32)


def _channel_dft_table():
    n = FNET_GROUP
    idx = np.arange(n, dtype=np.int64)
    ang = 2.0 * np.pi * ((idx[:, None] * idx[None, :]) % n).astype(np.float64) / n
    c = np.cos(ang) / math.sqrt(n)
    s = np.sin(ang) / math.sqrt(n)
    eye = np.eye(D_FNET // n)
    return np.concatenate([np.kron(eye, c), np.kron(eye, s)], axis=1).astype(np.float32)


def _sincos_2d(n_tokens):
    rows = n_tokens // GRID_W
    pos = np.arange(rows * GRID_W)
    row = (pos // GRID_W).astype(np.float32)
    col = (pos % GRID_W).astype(np.float32)
    quarter = D_MODEL // 4
    freq = np.exp(np.float32(-math.log(POS_BASE)) * np.arange(quarter, dtype=np.float32) / np.float32(quarter))
    ang_r = row[:, None] * freq
    ang_c = col[:, None] * freq
    return np.concatenate([np.sin(ang_r), np.cos(ang_r), np.sin(ang_c), np.cos(ang_c)], axis=-1).astype(np.float32)


def _head_block_matrix(value):
    blk = np.kron(np.eye(N_HEADS), np.ones((HEAD_DIM, HEAD_DIM))) * value
    return jnp.asarray(blk, dtype=BF16)


def _pad_lora(w, row_offset):
    rows = [jnp.pad(w[d], ((row_offset + d * LORA, LANES - row_offset - (d + 1) * LORA), (0, 0)))
            for d in range(N_DIR)]
    return jnp.stack(rows).astype(BF16)


def _layer_weights(l, w0, w2, a0, a2, k_k, k_a, r_k, gn_g, gn_b):
    w2_p = _pad_lora(w2[l], 0)
    a2_p = _pad_lora(a2[l], N_DIR * LORA)
    tri_f = np.tril(np.ones((CHUNK, CHUNK)))
    tri = jnp.asarray(np.stack([tri_f, tri_f.T]), dtype=BF16)
    return dict(w0=w0[l], w2=w2_p, a0=a0[l], a2=a2_p,
                k_k=k_k[l][None], k_a=k_a[l][None], r_k=r_k[l].reshape(1, D_RWKV),
                gn_g=gn_g[l].reshape(1, D_RWKV), gn_b=gn_b[l].reshape(1, D_RWKV), tri=tri,
                ones_bd=_head_block_matrix(1.0), avg_bd=_head_block_matrix(1.0 / HEAD_DIM))


def _state_to_block_diag(s_f, s_b):
    def one(s):
        h = jnp.swapaxes(s.astype(F32), -1, -2)
        b = h.shape[0]
        h = h.reshape(b, N_PAIRS, PAIR, HEAD_DIM, HEAD_DIM)
        z = jnp.zeros_like(h[:, :, 0])
        top = jnp.concatenate([h[:, :, 0], z], axis=-1)
        bot = jnp.concatenate([z, h[:, :, 1]], axis=-1)
        return jnp.concatenate([top, bot], axis=-2)
    return jnp.stack([one(s_f), one(s_b)], axis=1)


def kernel(x_prompt, x_sample, state_rwkv_fwd, state_rwkv_bwd, c, c_ctx, w_ada, b_ada, norm_g, w_in,
           mu_shift, w0, w2, a0, a2, k_k, k_a, r_k, gn_g, gn_b, w_fnet, b_fnet, w_out, final_norm_g):
    depth = w_in.shape[0]
    n_dec = c.shape[0]
    assert n_dec + 1 <= 8
    bp, tp, _ = x_prompt.shape
    bs, ts, _ = x_sample.shape
    cvec = jnp.concatenate([c_ctx[None], c, jnp.zeros((7 - n_dec, D_MODEL), F32)], axis=0)
    emb = jnp.asarray(_sincos_2d(ts)).astype(x_sample.dtype)
    cs_tab = jnp.asarray(_channel_dft_table()).astype(BF16)
    dft_p = jnp.asarray(_dft_table(tp)).astype(BF16)
    dft_s = jnp.asarray(_dft_table(ts)).astype(BF16)
    fng = final_norm_g[None]

    xp, xs = x_prompt, x_sample
    new_f, new_b = [], []
    for l in range(depth):
        mod = _mod_call(cvec, w_ada[l], b_ada[l][None]).reshape(8, 3, D_MODEL)
        mod_ctx, mod_lat = mod[0:1], mod[1:1 + n_dec]
        wts = _layer_weights(l, w0, w2, a0, a2, k_k, k_a, r_k, gn_g, gn_b)
        w_in_b = w_in[l].astype(BF16)
        w_out_b = w_out[l].astype(BF16)
        n_grp = w_fnet.shape[1]
        wbd = (w_fnet[l][:, :, None, :] * jnp.eye(n_grp, dtype=F32)[:, None, :, None]).reshape(
            D_FNET, D_FNET).astype(BF16)
        ng, mu, bf = norm_g[l][None], mu_shift[l][None], b_fnet[l][None]
        emb_l = emb if l == 0 else None

        rk, v, lwla, grec, fin, gf = _in_proj_call(xp, None, mod_ctx, ng, w_in_b, mu)
        y_f, y_b, s_f, s_b = _scan_call(rk, v, lwla, wts, None, True)
        fo = _fnet_call(fin, gf, dft_p, cs_tab, wbd, bf)
        last = l == depth - 1
        xp = _out_call(xp, None, mod_ctx, y_f, y_b, rk, v, lwla, grec, fo, wts, w_out_b, fng, last)
        new_f.append(s_f)
        new_b.append(s_b)

        rk, v, lwla, grec, fin, gf = _in_proj_call(xs, emb_l, mod_lat, ng, w_in_b, mu)
        s0 = _state_to_block_diag(state_rwkv_fwd[:, l], state_rwkv_bwd[:, l])
        y_f, y_b = _scan_call(rk, v, lwla, wts, s0, False)
        fo = _fnet_call(fin, gf, dft_s, cs_tab, wbd, bf)
        xs = _out_call(xs, emb_l, mod_lat, y_f, y_b, rk, v, lwla, grec, fo, wts, w_out_b, fng, last)
    return (xp, xs, jnp.concatenate(new_f, axis=1), jnp.concatenate(new_b, axis=1))
```

```python
import functools
import math

import numpy as np
import jax
import jax.numpy as jnp
from jax import lax
from jax.experimental import pallas as pl
from jax.experimental.pallas import tpu as pltpu

F32 = jnp.float32
BF16 = jnp.bfloat16

D_MODEL = 1024
GRID_W = 64
D_RWKV = 512
D_FNET = D_MODEL - D_RWKV
HEAD_DIM = 64
N_HEADS = D_RWKV // HEAD_DIM
FNET_GROUP = 64
LORA = 32
N_DIR = 2
D_SHIFT = 3 * D_RWKV + N_DIR * 2 * LORA
D_IN = D_SHIFT + D_RWKV + 2 * D_FNET
NORM_EPS = 1e-6
GN_EPS = 64e-5
POS_BASE = 10000.0

LANES = 128
PAIR = LANES // HEAD_DIM
N_PAIRS = N_HEADS // PAIR
CHUNK = 64
SUB = 16
SUB_SHIFT = SUB.bit_length() - 1
SCAN_SEQS = 4
SCAN_SUB = 2
ROW_TILE = 512
ROW_SPLIT = 4
FNET_ROWS = 1024
HALO = 8
VMEM_LIMIT = 56 * 1024 * 1024


def _silu(x):
    return x * jax.nn.sigmoid(x)


def _bdot(a, b):
    return jnp.dot(a.astype(BF16), b.astype(BF16), preferred_element_type=F32)


def _split2(x):
    hi = x.astype(BF16)
    lo = (x - hi.astype(F32)).astype(BF16)
    return hi, lo


def _mod_kernel(c_ref, w_ref, b_ref, o_ref):
    @pl.when(pl.program_id(0) == 0)
    def _():
        o_ref[...] = jnp.broadcast_to(b_ref[...], o_ref.shape)

    s_hi, s_lo = _split2(_silu(c_ref[...]))
    w = w_ref[...].astype(BF16)
    o_ref[...] += jnp.dot(s_hi, w, preferred_element_type=F32) + jnp.dot(s_lo, w, preferred_element_type=F32)


def _mod_call(cvec, w_ada, b_ada):
    rows = D_MODEL // 4
    return pl.pallas_call(
        _mod_kernel,
        grid=(D_MODEL // rows,),
        in_specs=[pl.BlockSpec((8, rows), lambda i: (0, i)),
                  pl.BlockSpec((rows, 3 * D_MODEL), lambda i: (i, 0)),
                  pl.BlockSpec((1, 3 * D_MODEL), lambda i: (0, 0))],
        out_specs=pl.BlockSpec((8, 3 * D_MODEL), lambda i: (0, 0)),
        out_shape=jax.ShapeDtypeStruct((8, 3 * D_MODEL), F32),
        compiler_params=pltpu.CompilerParams(dimension_semantics=("arbitrary",),
                                             vmem_limit_bytes=VMEM_LIMIT),
        name="mod",
    )(cvec, w_ada, b_ada)


def _modulated_norm(x, g, scale, shift):
    ms = jnp.mean(x * x, axis=-1, keepdims=True)
    y = x * lax.rsqrt(ms + NORM_EPS) * g
    return y * (1.0 + scale) + shift


def _in_proj_kernel(*refs, has_emb, has_halo, n_tiles):
    it = iter(refs)
    x_ref = next(it)
    xp_ref = next(it) if has_halo else None
    xn_ref = next(it) if has_halo else None
    emb_ref = next(it) if has_emb else None
    embp_ref = next(it) if (has_emb and has_halo) else None
    embn_ref = next(it) if (has_emb and has_halo) else None
    mod_ref, g_ref, w_ref, mu_ref = next(it), next(it), next(it), next(it)
    rk_ref, v_ref, lwla_ref, grec_ref, fin_ref, gf_ref = (next(it) for _ in range(6))

    i = pl.program_id(1)
    g = g_ref[...]
    shift = mod_ref[0, 0:1, :]
    scale = mod_ref[0, 1:2, :]
    nb, tt, _ = x_ref.shape
    tm = nb * tt
    x = x_ref[...].reshape(tm, D_MODEL)
    if has_emb:
        x = x + emb_ref[...]
    part = tm // ROW_SPLIT
    p = jnp.concatenate(
        [jnp.dot(_modulated_norm(x[r0:r0 + part], g, scale, shift).astype(BF16), w_ref[...],
                 preferred_element_type=F32) for r0 in range(0, tm, part)], axis=0)
    ps = p[:, :D_SHIFT]
    if has_halo:
        xh = jnp.concatenate([xp_ref[0], xn_ref[0]], axis=0)
        if has_emb:
            xh = xh + jnp.concatenate([embp_ref[...], embn_ref[...]], axis=0)
        hh = _modulated_norm(xh, g, scale, shift)
        ph = jnp.dot(hh.astype(BF16), w_ref[:, :D_SHIFT], preferred_element_type=F32)
        prev_row = jnp.where(i > 0, ph[HALO - 1:HALO, :], 0.0)
        next_row = jnp.where(i < n_tiles - 1, ph[HALO:HALO + 1, :], 0.0)
    else:
        prev_row = jnp.zeros((1, D_SHIFT), F32)
        next_row = jnp.zeros((1, D_SHIFT), F32)
    pos = lax.rem(lax.broadcasted_iota(jnp.int32, (tm, D_SHIFT), 0), tt)
    prev = jnp.where(pos == 0, prev_row, pltpu.roll(ps, 1, 0))
    nxt = jnp.where(pos == tt - 1, next_row, pltpu.roll(ps, tm - 1, 0))
    p_rec = ps + mu_ref[...] * (0.5 * (prev + nxt) - ps)

    def put(ref, val):
        ref[...] = val.astype(ref.dtype).reshape(ref.shape)

    put(rk_ref, p_rec[:, :2 * D_RWKV])
    put(v_ref, p_rec[:, 2 * D_RWKV:3 * D_RWKV])
    put(lwla_ref, p_rec[:, 3 * D_RWKV:])
    put(grec_ref, p[:, D_SHIFT:D_SHIFT + D_RWKV])
    put(fin_ref, p[:, D_SHIFT + D_RWKV:D_SHIFT + D_RWKV + D_FNET])
    put(gf_ref, p[:, D_SHIFT + D_RWKV + D_FNET:])


def _in_proj_call(x, emb, mod, norm_g, w_in_bf16, mu):
    B, T, _ = x.shape
    tt = min(T, ROW_TILE)
    nb = max(1, ROW_TILE // T)
    n_tiles = T // tt
    has_halo = n_tiles > 1
    has_emb = emb is not None
    per_batch_mod = mod.shape[0] > 1
    assert B % nb == 0 and T % tt == 0 and (nb == 1 or not (has_emb or per_batch_mod))
    tm = tt
    blocks_per_tile = tm // HALO
    last_halo_block = T // HALO - 1

    in_specs = [pl.BlockSpec((nb, tt, D_MODEL), lambda b, i: (b, i, 0))]
    args = [x]
    if has_halo:
        in_specs += [
            pl.BlockSpec((1, HALO, D_MODEL), lambda b, i: (b, jnp.maximum(i * blocks_per_tile - 1, 0), 0)),
            pl.BlockSpec((1, HALO, D_MODEL),
                         lambda b, i: (b, jnp.minimum((i + 1) * blocks_per_tile, last_halo_block), 0))]
        args += [x, x]
    if has_emb:
        in_specs.append(pl.BlockSpec((tm, D_MODEL), lambda b, i: (i, 0)))
        args.append(emb)
        if has_halo:
            in_specs += [
                pl.BlockSpec((HALO, D_MODEL), lambda b, i: (jnp.maximum(i * blocks_per_tile - 1, 0), 0)),
                pl.BlockSpec((HALO, D_MODEL),
                             lambda b, i: (jnp.minimum((i + 1) * blocks_per_tile, last_halo_block), 0))]
            args += [emb, emb]
    mod_map = (lambda b, i: (b, 0, 0)) if per_batch_mod else (lambda b, i: (0, 0, 0))
    in_specs += [pl.BlockSpec((1, 3, D_MODEL), mod_map),
                 pl.BlockSpec((1, D_MODEL), lambda b, i: (0, 0)),
                 pl.BlockSpec((D_MODEL, D_IN), lambda b, i: (0, 0), pipeline_mode=pl.Buffered(1)),
                 pl.BlockSpec((1, D_SHIFT), lambda b, i: (0, 0))]
    args += [mod, norm_g, w_in_bf16, mu]
    outs = ((2 * D_RWKV, BF16), (D_RWKV, BF16), (N_DIR * 2 * LORA, F32), (D_RWKV, BF16), (D_FNET, BF16),
            (D_FNET, BF16))
    out_specs = [pl.BlockSpec((nb, tt, w), lambda b, i: (b, i, 0)) for w, _ in outs]
    out_shape = [jax.ShapeDtypeStruct((B, T, w), dt) for w, dt in outs]
    kern = functools.partial(_in_proj_kernel, has_emb=has_emb, has_halo=has_halo, n_tiles=n_tiles)
    return pl.pallas_call(
        kern, grid=(B // nb, n_tiles), in_specs=in_specs, out_specs=out_specs, out_shape=out_shape,
        compiler_params=pltpu.CompilerParams(dimension_semantics=("arbitrary", "arbitrary"),
                                             vmem_limit_bytes=VMEM_LIMIT),
        name="in_proj",
    )(*args)


def _same_block(i, j, size):
    shift = size.bit_length() - 1
    return jnp.right_shift(i, shift) == jnp.right_shift(j, shift)


def _block_diag_rows(y, head0_lanes):
    return jnp.concatenate([jnp.where(head0_lanes, y, 0.0), jnp.where(head0_lanes, 0.0, y)], axis=0)


def _scan_kernel(*refs, zero_init, write_state, n_steps, n_seq, n_sub):
    it = iter(refs)
    rk_refs, v_refs, lwla_refs = [None, None], [None, None], [None, None]
    for d in range(N_DIR):
        rk_refs[d], v_refs[d], lwla_refs[d] = next(it), next(it), next(it)
    w0_ref, w2_ref, a0_ref, a2_ref = (next(it) for _ in range(4))
    kk_ref, ka_ref, tri_ref, ones_ref = (next(it) for _ in range(4))
    s0_ref = None if zero_init else next(it)
    y_refs = [next(it), next(it)]
    sfin_refs = [next(it), next(it)] if write_state else None
    h_scr = next(it)

    j = pl.program_id(1)

    @pl.when(j == 0)
    def _():
        if zero_init:
            h_scr[...] = jnp.zeros_like(h_scr)
        else:
            h_scr[...] = s0_ref[...]

    C = CHUNK
    row = lax.broadcasted_iota(jnp.int32, (C, LANES), 0)
    lane = lax.broadcasted_iota(jnp.int32, (C, LANES), 1)
    s_idx = jnp.bitwise_and(lane, HEAD_DIM - 1)
    head0 = lane < HEAD_DIM
    row_c = lax.broadcasted_iota(jnp.int32, (SUB, LANES), 0)
    lane_c = lax.broadcasted_iota(jnp.int32, (SUB, LANES), 1)
    col_c = jnp.bitwise_and(lane_c, SUB - 1)
    blk_c = jnp.right_shift(jnp.bitwise_and(lane_c, HEAD_DIM - 1), SUB_SHIFT)
    lane_blk_c = jnp.right_shift(lane_c, SUB_SHIFT)
    eye_c = (row_c == col_c).astype(F32)

    def bd_c(y):
        return jnp.concatenate([jnp.where(lane_blk_c == g, y, 0.0) for g in range(LANES // SUB)], axis=0)

    def inverse_size4(l_c, upper):
        r4 = jnp.bitwise_and(row_c, 3)
        c4 = jnp.bitwise_and(col_c, 3)
        down, up = (lambda x, k: pltpu.roll(x, k, 0)), (lambda x, k: pltpu.roll(x, SUB - k, 0))
        right, left = (lambda x, k: pltpu.roll(x, k, 1)), (lambda x, k: pltpu.roll(x, LANES - k, 1))
        s2 = jnp.where(_same_block(row_c, col_c, 2), l_c, 0.0)
        e4 = jnp.where(_same_block(row_c, col_c, 4), l_c, 0.0) - s2
        if upper:
            s_col = jnp.where(r4 == 1, up(s2, 1), up(s2, 2))
            s_row = jnp.where(c4 == 2, right(s2, 1), right(s2, 2))
            e_s = jnp.where(c4 == 3, right(e4, 1) * s_col, 0.0)
            s_e = jnp.where(r4 == 0, s_row * up(e4, 1), 0.0)
            s_e_s = jnp.where((r4 == 0) & (c4 == 3), s_row * up(e_s, 1), 0.0)
        else:
            s_col = jnp.where(r4 == 2, down(s2, 1), down(s2, 2))
            s_row = jnp.where(c4 == 1, left(s2, 1), left(s2, 2))
            e_s = jnp.where(c4 == 0, left(e4, 1) * s_col, 0.0)
            s_e = jnp.where(r4 == 3, s_row * down(e4, 1), 0.0)
            s_e_s = jnp.where((r4 == 3) & (c4 == 0), s_row * down(e_s, 1), 0.0)
        return eye_c - s2 - e4 + e_s + s_e - s_e_s

    row2 = lax.broadcasted_iota(jnp.int32, (LANES, LANES), 0)
    lane2 = lax.broadcasted_iota(jnp.int32, (LANES, LANES), 1)
    same_head = (row2 < HEAD_DIM) == (lane2 < HEAD_DIM)
    decay_scale = math.exp(-0.5)

    row0 = lambda o, d: (o if d == 0 else n_sub - 1 - o) * C
    keys = [(o, n, d) for o in range(n_sub) for n in range(n_seq) for d in range(N_DIR)]
    rk_all = {(o, n, d): rk_refs[d][n, row0(o, d):row0(o, d) + C, :].astype(F32) for o, n, d in keys}
    kkr_all = {key: rk_all[key][:, D_RWKV:] * kk_ref[...] for key in keys}
    ssq_rows = _bdot(jnp.concatenate([kkr_all[key] * kkr_all[key] for key in keys], axis=0), ones_ref[...])
    dir_keys = [[key for key in keys if key[2] == d] for d in range(N_DIR)]
    z_w_rows, z_a_rows = [], []
    for d in range(N_DIR):
        ll_rows = jnp.concatenate([lwla_refs[d][n, row0(o, d):row0(o, d) + C, :] for o, n, _ in dir_keys[d]], axis=0)
        z_w_rows.append(w0_ref[d:d + 1, :] + _bdot(jnp.tanh(ll_rows), w2_ref[d]))
        z_a_rows.append(a0_ref[d:d + 1, :] + _bdot(ll_rows, a2_ref[d]))

    sets = [[] for _ in range(n_sub)]
    for i, (o, n, d) in enumerate(keys):
        rk = rk_all[(o, n, d)]
        r = rk[:, :D_RWKV]
        k = rk[:, D_RWKV:]
        v = v_refs[d][n, row0(o, d):row0(o, d) + C, :]
        i_d = dir_keys[d].index((o, n, d))
        logw = -decay_scale * jax.nn.sigmoid(z_w_rows[d][i_d * C:(i_d + 1) * C])
        a = jax.nn.sigmoid(z_a_rows[d][i_d * C:(i_d + 1) * C])
        kd = k * (1.0 + (a - 1.0) * ka_ref[...])
        kkr = kkr_all[(o, n, d)]
        ssq = ssq_rows[i * C:(i + 1) * C]
        kk = kkr * lax.rsqrt(jnp.maximum(ssq, 1e-24))
        bvec = kk * a
        lw_hi, lw_lo = _split2(logw)
        tri = tri_ref[d]
        cum = (jnp.dot(tri, lw_hi, preferred_element_type=F32)
               + jnp.dot(tri, lw_lo, preferred_element_type=F32))
        cum_prev = cum - logw
        tot = cum[C - 1:C, :] if d == 0 else cum[0:1, :]
        kap_t = kk * jnp.exp(cum_prev)
        r_t = r * jnp.exp(cum)
        e_neg = jnp.exp(-cum)
        gam = jnp.exp(tot)
        b_t = bvec * e_neg
        k_t = kd * e_neg
        e_rem = gam * e_neg
        b_h = bvec * e_rem
        k_h = kd * e_rem
        if d == 0:
            strict, incl = row > s_idx, row >= s_idx
        else:
            strict, incl = row < s_idx, row <= s_idx

        for p in range(N_PAIRS):
            sl = slice(p * LANES, (p + 1) * LANES)
            dup_t = lambda x: jnp.where(same_head, jnp.concatenate([x[:, sl], x[:, sl]], axis=0).T, 0.0)
            sets[o].append(dict(
                n=n, d=d, p=p, sl=sl, rows=slice(row0(o, d), row0(o, d) + C), strict=strict, incl=incl,
                prev=None if o == 0 else sets[o - 1][len(sets[o])],
                lhs=jnp.concatenate([kap_t[:, sl], r_t[:, sl]], axis=0).astype(BF16),
                v=v[:, sl], w_lm=jnp.concatenate([dup_t(b_t), dup_t(k_t)], axis=1).astype(BF16),
                t2=jnp.concatenate([b_h[:, sl], k_h[:, sl]], axis=0).T.astype(BF16),
                gam_col=jnp.broadcast_to(gam[:, sl], (LANES, LANES)).T))

    bd = lambda y: _block_diag_rows(y, head0)

    def st_scores(probs):
        for q in probs:
            lm = jnp.dot(q["lhs"], q["w_lm"], preferred_element_type=F32)
            q["l_b"] = jnp.where(q["strict"], lm[:C, :LANES], 0.0)
            q["m_b"] = jnp.where(q["incl"], lm[C:, :LANES], 0.0)
            q["lm_k"] = jnp.concatenate([jnp.where(q["strict"], lm[:C, LANES:], 0.0),
                                         jnp.where(q["incl"], lm[C:, LANES:], 0.0)], axis=0)

    def st_values(probs):
        for q in probs:
            lmv = _bdot(q["lm_k"], bd(q["v"]))
            q["l_kv"], q["m_kv"] = lmv[:C], lmv[C:]
            l_b = q["l_b"]
            l_c = l_b[0:SUB]
            for jb in range(1, C // SUB):
                l_c = jnp.where(blk_c == jb, l_b[jb * SUB:(jb + 1) * SUB], l_c)
            q["l_c"] = l_c
            q["t_c"] = inverse_size4(l_c, q["d"] == 1)

    def st_compact_a(s):
        def run(probs):
            off_mask = _same_block(row_c, col_c, 2 * s) & ~_same_block(row_c, col_c, s)
            for q in probs:
                q["et"] = _bdot(jnp.where(off_mask, q["l_c"], 0.0), bd_c(q["t_c"]))
        return run

    def st_compact_b(last):
        def run(probs):
            for q in probs:
                q["t_c"] = q["t_c"] - _bdot(q["t_c"], bd_c(q["et"]))
                if last:
                    q["tinv"] = jnp.concatenate([jnp.where(blk_c == jb, q["t_c"], 0.0) for jb in range(C // SUB)],
                                                axis=0)
        return run

    def st_full_a(s):
        def run(probs):
            off_mask = _same_block(row, s_idx, 2 * s) & ~_same_block(row, s_idx, s)
            for q in probs:
                q["et"] = _bdot(jnp.where(off_mask, q["l_b"], 0.0), bd(q["tinv"]))
        return run

    def st_full_b(probs):
        for q in probs:
            q["tinv"] = q["tinv"] - _bdot(q["tinv"], bd(q["et"]))

    def st_state_read(probs):
        for q in probs:
            q["h"] = h_scr[q["n"], q["d"], q["p"]] if q["prev"] is None else q["prev"]["h_new"]
            q["kr_h"] = _bdot(q["lhs"], q["h"])

    def st_solve(probs):
        for q in probs:
            q["u_n"] = _bdot(q["tinv"], bd(q["kr_h"][:C] + q["l_kv"]))

    def st_output(probs):
        for q in probs:
            y = q["kr_h"][C:] + q["m_kv"] - _bdot(q["m_b"], bd(q["u_n"]))
            y_refs[q["d"]][q["n"], q["rows"], q["sl"]] = y.astype(BF16)

    def st_state_write(probs):
        for q in probs:
            upd = _bdot(q["t2"], jnp.concatenate([(-q["u_n"]).astype(BF16), q["v"]], axis=0))
            q["h_new"] = q["gam_col"] * q["h"] + jnp.where(same_head, upd, 0.0)

    stages = [st_scores, st_values] + ([st_state_read] if n_sub == 1 else [])
    s = 4
    while s < SUB:
        stages += [st_compact_a(s), st_compact_b(2 * s == SUB)]
        s *= 2
    while s < C:
        stages += [st_full_a(s), st_full_b]
        s *= 2
    stages += ([] if n_sub == 1 else [st_state_read]) + [st_solve, st_output, st_state_write]
    skew = 4
    for t in range(len(stages) + skew * (n_sub - 1)):
        for o in range(n_sub):
            if 0 <= t - skew * o < len(stages):
                stages[t - skew * o](sets[o])
    for q in sets[-1]:
        h_scr[q["n"], q["d"], q["p"]] = q["h_new"]

    if write_state:
        @pl.when(j == n_steps - 1)
        def _():
            for n, d, p in [(n, d, p) for n in range(n_seq) for d in range(N_DIR) for p in range(N_PAIRS)]:
                ht = h_scr[n, d, p].T
                sfin_refs[d][n, 0, PAIR * p] = ht[:HEAD_DIM, :HEAD_DIM]
                sfin_refs[d][n, 0, PAIR * p + 1] = ht[HEAD_DIM:, HEAD_DIM:]


def _scan_call(rk, v, lwla, wts, s0_bd, write_state):
    B, T, _ = rk.shape
    n_sub = min(SCAN_SUB, T // CHUNK)
    ns = min(SCAN_SEQS, B)
    rows = n_sub * CHUNK
    nc = T // rows
    assert B % ns == 0 and T % rows == 0
    zero_init = s0_bd is None
    fwd = lambda b, j: (b, j, 0)
    bwd = lambda b, j: (b, nc - 1 - j, 0)
    full = lambda *shape: pl.BlockSpec(shape, lambda b, j: (0,) * len(shape))
    tok = lambda w, m: pl.BlockSpec((ns, rows, w), m)
    in_specs = [tok(2 * D_RWKV, fwd), tok(D_RWKV, fwd), tok(LANES, fwd),
                tok(2 * D_RWKV, bwd), tok(D_RWKV, bwd), tok(LANES, bwd),
                full(N_DIR, D_RWKV), full(N_DIR, LANES, D_RWKV),
                full(N_DIR, D_RWKV), full(N_DIR, LANES, D_RWKV),
                full(1, D_RWKV), full(1, D_RWKV), full(N_DIR, CHUNK, CHUNK), full(D_RWKV, D_RWKV)]
    args = [rk, v, lwla, rk, v, lwla, wts["w0"], wts["w2"], wts["a0"], wts["a2"],
            wts["k_k"], wts["k_a"], wts["tri"], wts["ones_bd"]]
    state_block = (ns, N_DIR, N_PAIRS, LANES, LANES)
    if not zero_init:
        in_specs.append(pl.BlockSpec(state_block, lambda b, j: (b, 0, 0, 0, 0)))
        args.append(s0_bd)
    out_specs = [tok(D_RWKV, fwd), tok(D_RWKV, bwd)]
    out_shape = [jax.ShapeDtypeStruct((B, T, D_RWKV), BF16), jax.ShapeDtypeStruct((B, T, D_RWKV), BF16)]
    if write_state:
        final_block = (ns, 1, N_HEADS, HEAD_DIM, HEAD_DIM)
        out_specs += [pl.BlockSpec(final_block, lambda b, j: (b, 0, 0, 0, 0))] * N_DIR
        out_shape += [jax.ShapeDtypeStruct((B,) + final_block[1:], F32)] * N_DIR
    kern = functools.partial(_scan_kernel, zero_init=zero_init, write_state=write_state, n_steps=nc,
                             n_seq=ns, n_sub=n_sub)
    return pl.pallas_call(
        kern, grid=(B // ns, nc), in_specs=in_specs, out_specs=out_specs, out_shape=out_shape,
        scratch_shapes=[pltpu.VMEM(state_block, F32)],
        compiler_params=pltpu.CompilerParams(dimension_semantics=("arbitrary", "arbitrary"),
                                             vmem_limit_bytes=VMEM_LIMIT),
        name="scan",
    )(*args)


def _fnet_kernel(fin_ref, gf_ref, dft_ref, cs_ref, wbd_ref, b_ref, o_ref, g_scr):
    u = pl.program_id(1)
    nb, seq_len, _ = fin_ref.shape
    tu = o_ref.shape[1]

    @pl.when(u == 0)
    def _():
        step = min(seq_len, ROW_TILE)
        for n in range(nb):
            for t0 in range(0, seq_len, step):
                fc = _bdot(fin_ref[n, t0:t0 + step, :], cs_ref[...])
                g_scr[n, t0:t0 + step, :] = fc[:, :D_FNET].astype(BF16)
                g_scr[n, seq_len + t0:seq_len + t0 + step, :] = fc[:, D_FNET:].astype(BF16)

    f_re = jnp.concatenate([jnp.dot(dft_ref[...], g_scr[n], preferred_element_type=F32) for n in range(nb)],
                           axis=0)
    f_out = _bdot(f_re, wbd_ref[...]) + b_ref[...]
    gate = _silu(gf_ref[...].astype(F32)).reshape(nb * tu, D_FNET)
    o_ref[...] = (f_out * gate).astype(BF16).reshape(nb, tu, D_FNET)


def _fnet_call(fin, gf, dft_bf16, cs_bf16, wbd_bf16, b_fnet):
    B, T, _ = fin.shape
    tu = min(T, FNET_ROWS)
    nb = max(1, FNET_ROWS // T)
    assert B % nb == 0 and T % tu == 0
    return pl.pallas_call(
        _fnet_kernel,
        grid=(B // nb, T // tu),
        in_specs=[pl.BlockSpec((nb, T, D_FNET), lambda b, u: (b, 0, 0)),
                  pl.BlockSpec((nb, tu, D_FNET), lambda b, u: (b, u, 0)),
                  pl.BlockSpec((tu, 2 * T), lambda b, u: (u, 0)),
                  pl.BlockSpec((D_FNET, 2 * D_FNET), lambda b, u: (0, 0)),
                  pl.BlockSpec((D_FNET, D_FNET), lambda b, u: (0, 0)),
                  pl.BlockSpec((1, D_FNET), lambda b, u: (0, 0))],
        out_specs=pl.BlockSpec((nb, tu, D_FNET), lambda b, u: (b, u, 0)),
        out_shape=jax.ShapeDtypeStruct((B, T, D_FNET), BF16),
        scratch_shapes=[pltpu.VMEM((nb, 2 * T, D_FNET), BF16)],
        compiler_params=pltpu.CompilerParams(dimension_semantics=("arbitrary", "arbitrary"),
                                             vmem_limit_bytes=VMEM_LIMIT),
        name="fnet",
    )(fin, gf, dft_bf16, cs_bf16, wbd_bf16, b_fnet)


def _out_kernel(*refs, has_emb, final_norm):
    it = iter(refs)
    x_ref = next(it)
    emb_ref = next(it) if has_emb else None
    (mod_ref, yf_ref, yb_ref, rk_ref, v_ref, lwla_ref, grec_ref, fo_ref, a0_ref, a2_ref, ka_ref,
     rkw_ref, gng_ref, gnb_ref, avg_ref, ones_ref, wout_ref, fng_ref, o_ref) = (next(it) for _ in range(19))

    nb, tt, _ = x_ref.shape
    tm = nb * tt
    rows = lambda ref: ref[...].reshape(tm, ref.shape[-1])
    y = rows(yf_ref).astype(F32) + rows(yb_ref).astype(F32)
    mu = _bdot(y, avg_ref[...])
    dlt = y - mu
    var = _bdot(dlt * dlt, avg_ref[...])
    y_n = dlt * lax.rsqrt(var + GN_EPS) * gng_ref[...] + gnb_ref[...]
    rk = rows(rk_ref).astype(F32)
    r = rk[:, :D_RWKV]
    k = rk[:, D_RWKV:]
    ll = rows(lwla_ref)
    a_sum = (jax.nn.sigmoid(a0_ref[0:1, :] + _bdot(ll, a2_ref[0]))
             + jax.nn.sigmoid(a0_ref[1:2, :] + _bdot(ll, a2_ref[1])))
    k_sum = k * (2.0 + (a_sum - 2.0) * ka_ref[...])
    bonus = _bdot(r * k_sum * rkw_ref[...], ones_ref[...]) * rows(v_ref).astype(F32)
    rec_out = (y_n + bonus) * _silu(rows(grec_ref).astype(F32))
    mixed = jnp.concatenate([rec_out.astype(BF16), rows(fo_ref)], axis=-1)
    out = jnp.dot(mixed, wout_ref[...], preferred_element_type=F32)
    x = rows(x_ref)
    if has_emb:
        x = x + emb_ref[...]
    z = x + mod_ref[0, 2:3, :] * out
    if final_norm:
        ms = jnp.mean(z * z, axis=-1, keepdims=True)
        z = z * lax.rsqrt(ms + NORM_EPS) * fng_ref[...]
    o_ref[...] = z.reshape(nb, tt, D_MODEL)


def _out_call(x, emb, mod, y_f, y_b, rk, v, lwla, grec, fo, wts, w_out_bf16, final_norm_g, final_norm):
    B, T, _ = x.shape
    tt = min(T, ROW_TILE)
    nb = max(1, ROW_TILE // T)
    has_emb = emb is not None
    per_batch_mod = mod.shape[0] > 1
    assert B % nb == 0 and T % tt == 0 and (nb == 1 or not (has_emb or per_batch_mod))
    tok = lambda w: pl.BlockSpec((nb, tt, w), lambda b, i: (b, i, 0))
    full = lambda *shape: pl.BlockSpec(shape, lambda b, i: (0,) * len(shape))
    in_specs = [tok(D_MODEL)]
    args = [x]
    if has_emb:
        in_specs.append(pl.BlockSpec((tt, D_MODEL), lambda b, i: (i, 0)))
        args.append(emb)
    mod_map = (lambda b, i: (b, 0, 0)) if per_batch_mod else (lambda b, i: (0, 0, 0))
    in_specs += [pl.BlockSpec((1, 3, D_MODEL), mod_map), tok(D_RWKV), tok(D_RWKV), tok(2 * D_RWKV),
                 tok(D_RWKV), tok(LANES), tok(D_RWKV), tok(D_FNET),
                 full(N_DIR, D_RWKV), full(N_DIR, LANES, D_RWKV),
                 full(1, D_RWKV), full(1, D_RWKV), full(1, D_RWKV), full(1, D_RWKV),
                 full(D_RWKV, D_RWKV), full(D_RWKV, D_RWKV), full(D_MODEL, D_MODEL), full(1, D_MODEL)]
    args += [mod, y_f, y_b, rk, v, lwla, grec, fo, wts["a0"], wts["a2"], wts["k_a"],
             wts["r_k"], wts["gn_g"], wts["gn_b"], wts["avg_bd"], wts["ones_bd"], w_out_bf16, final_norm_g]
    return pl.pallas_call(
        functools.partial(_out_kernel, has_emb=has_emb, final_norm=final_norm),
        grid=(B // nb, T // tt), in_specs=in_specs,
        out_specs=tok(D_MODEL),
        out_shape=jax.ShapeDtypeStruct((B, T, D_MODEL), F32),
        compiler_params=pltpu.CompilerParams(dimension_semantics=("arbitrary", "arbitrary"),
                                             vmem_limit_bytes=VMEM_LIMIT),
        name="out_proj",
    )(*args)


def _dft_table(seq_len):
    idx = np.arange(seq_len, dtype=np.int64)
    ang = 2.0 * np.pi * ((idx[:, None] * idx[None, :]) % seq_len).astype(np.float64) / seq_len
    scale = 1.0 / math.sqrt(seq_len)
    return np.concatenate([np.cos(ang) * scale, -np.sin(ang) * scale], axis=1).astype(np.float32)


def _channel_dft_table():
    n = FNET_GROUP
    idx = np.arange(n, dtype=np.int64)
    ang = 2.0 * np.pi * ((idx[:, None] * idx[None, :]) % n).astype(np.float64) / n
    c = np.cos(ang) / math.sqrt(n)
    s = np.sin(ang) / math.sqrt(n)
    eye = np.eye(D_FNET // n)
    return np.concatenate([np.kron(eye, c), np.kron(eye, s)], axis=1).astype(np.float32)


def _sincos_2d(n_tokens):
    rows = n_tokens // GRID_W
    pos = np.arange(rows * GRID_W)
    row = (pos // GRID_W).astype(np.float32)
    col = (pos % GRID_W).astype(np.float32)
    quarter = D_MODEL // 4
    freq = np.exp(np.float32(-math.log(POS_BASE)) * np.arange(quarter, dtype=np.float32) / np.float32(quarter))
    ang_r = row[:, None] * freq
    ang_c = col[:, None] * freq
    return np.concatenate([np.sin(ang_r), np.cos(ang_r), np.sin(ang_c), np.cos(ang_c)], axis=-1).astype(np.float32)


def _head_block_matrix(value):
    blk = np.kron(np.eye(N_HEADS), np.ones((HEAD_DIM, HEAD_DIM))) * value
    return jnp.asarray(blk, dtype=BF16)


def _pad_lora(w, row_offset):
    rows = [jnp.pad(w[d], ((row_offset + d * LORA, LANES - row_offset - (d + 1) * LORA), (0, 0)))
            for d in range(N_DIR)]
    return jnp.stack(rows).astype(BF16)


def _layer_weights(l, w0, w2, a0, a2, k_k, k_a, r_k, gn_g, gn_b):
    w2_p = _pad_lora(w2[l], 0)
    a2_p = _pad_lora(a2[l], N_DIR * LORA)
    tri_f = np.tril(np.ones((CHUNK, CHUNK)))
    tri = jnp.asarray(np.stack([tri_f, tri_f.T]), dtype=BF16)
    return dict(w0=w0[l], w2=w2_p, a0=a0[l], a2=a2_p,
                k_k=k_k[l][None], k_a=k_a[l][None], r_k=r_k[l].reshape(1, D_RWKV),
                gn_g=gn_g[l].reshape(1, D_RWKV), gn_b=gn_b[l].reshape(1, D_RWKV), tri=tri,
                ones_bd=_head_block_matrix(1.0), avg_bd=_head_block_matrix(1.0 / HEAD_DIM))


def _state_to_block_diag(s_f, s_b):
    def one(s):
        h = jnp.swapaxes(s.astype(F32), -1, -2)
        b = h.shape[0]
        h = h.reshape(b, N_PAIRS, PAIR, HEAD_DIM, HEAD_DIM)
        z = jnp.zeros_like(h[:, :, 0])
        top = jnp.concatenate([h[:, :, 0], z], axis=-1)
        bot = jnp.concatenate([z, h[:, :, 1]], axis=-1)
        return jnp.concatenate([top, bot], axis=-2)
    return jnp.stack([one(s_f), one(s_b)], axis=1)


def kernel(x_prompt, x_sample, state_rwkv_fwd, state_rwkv_bwd, c, c_ctx, w_ada, b_ada, norm_g, w_in,
           mu_shift, w0, w2, a0, a2, k_k, k_a, r_k, gn_g, gn_b, w_fnet, b_fnet, w_out, final_norm_g):
    depth = w_in.shape[0]
    n_dec = c.shape[0]
    assert n_dec + 1 <= 8
    bp, tp, _ = x_prompt.shape
    bs, ts, _ = x_sample.shape
    cvec = jnp.concatenate([c_ctx[None], c, jnp.zeros((7 - n_dec, D_MODEL), F32)], axis=0)
    emb = jnp.asarray(_sincos_2d(ts)).astype(x_sample.dtype)
    cs_tab = jnp.asarray(_channel_dft_table()).astype(BF16)
    dft_p = jnp.asarray(_dft_table(tp)).astype(BF16)
    dft_s = jnp.asarray(_dft_table(ts)).astype(BF16)
    fng = final_norm_g[None]

    xp, xs = x_prompt, x_sample
    new_f, new_b = [], []
    for l in range(depth):
        mod = _mod_call(cvec, w_ada[l], b_ada[l][None]).reshape(8, 3, D_MODEL)
        mod_ctx, mod_lat = mod[0:1], mod[1:1 + n_dec]
        wts = _layer_weights(l, w0, w2, a0, a2, k_k, k_a, r_k, gn_g, gn_b)
        w_in_b = w_in[l].astype(BF16)
        w_out_b = w_out[l].astype(BF16)
        n_grp = w_fnet.shape[1]
        wbd = (w_fnet[l][:, :, None, :] * jnp.eye(n_grp, dtype=F32)[:, None, :, None]).reshape(
            D_FNET, D_FNET).astype(BF16)
        ng, mu, bf = norm_g[l][None], mu_shift[l][None], b_fnet[l][None]
        emb_l = emb if l == 0 else None

        rk, v, lwla, grec, fin, gf = _in_proj_call(xp, None, mod_ctx, ng, w_in_b, mu)
        y_f, y_b, s_f, s_b = _scan_call(rk, v, lwla, wts, None, True)
        fo = _fnet_call(fin, gf, dft_p, cs_tab, wbd, bf)
        last = l == depth - 1
        xp = _out_call(xp, None, mod_ctx, y_f, y_b, rk, v, lwla, grec, fo, wts, w_out_b, fng, last)
        new_f.append(s_f)
        new_b.append(s_b)

        rk, v, lwla, grec, fin, gf = _in_proj_call(xs, emb_l, mod_lat, ng, w_in_b, mu)
        s0 = _state_to_block_diag(state_rwkv_fwd[:, l], state_rwkv_bwd[:, l])
        y_f, y_b = _scan_call(rk, v, lwla, wts, s0, False)
        fo = _fnet_call(fin, gf, dft_s, cs_tab, wbd, bf)
        xs = _out_call(xs, emb_l, mod_lat, y_f, y_b, rk, v, lwla, grec, fo, wts, w_out_b, fng, last)
    return (xp, xs, jnp.concatenate(new_f, axis=1), jnp.concatenate(new_b, axis=1))
```

```python
import functools
import math

import numpy as np
import jax
import jax.numpy as jnp
from jax import lax
from jax.experimental import pallas as pl
from jax.experimental.pallas import tpu as pltpu

F32 = jnp.float32
BF16 = jnp.bfloat16

D_MODEL = 1024
GRID_W = 64
D_RWKV = 512
D_FNET = D_MODEL - D_RWKV
HEAD_DIM = 64
N_HEADS = D_RWKV // HEAD_DIM
FNET_GROUP = 64
LORA = 32
N_DIR = 2
D_SHIFT = 3 * D_RWKV + N_DIR * 2 * LORA
D_IN = D_SHIFT + D_RWKV + 2 * D_FNET
NORM_EPS = 1e-6
GN_EPS = 64e-5
POS_BASE = 10000.0

LANES = 128
PAIR = LANES // HEAD_DIM
N_PAIRS = N_HEADS // PAIR
CHUNK = 64
SUB = 16
SUB_SHIFT = SUB.bit_length() - 1
SCAN_WORK = 8
SCAN_SUB = 4
ROW_TILE = 512
ROW_SPLIT = 4
FNET_ROWS = 1024
HALO = 8
VMEM_LIMIT = 56 * 1024 * 1024


def _silu(x):
    return x * jax.nn.sigmoid(x)


def _bdot(a, b):
    return jnp.dot(a.astype(BF16), b.astype(BF16), preferred_element_type=F32)


def _split2(x):
    hi = x.astype(BF16)
    lo = (x - hi.astype(F32)).astype(BF16)
    return hi, lo


def _mod_kernel(c_ref, w_ref, b_ref, o_ref):
    @pl.when(pl.program_id(0) == 0)
    def _():
        o_ref[...] = jnp.broadcast_to(b_ref[...], o_ref.shape)

    s_hi, s_lo = _split2(_silu(c_ref[...]))
    w = w_ref[...].astype(BF16)
    o_ref[...] += jnp.dot(s_hi, w, preferred_element_type=F32) + jnp.dot(s_lo, w, preferred_element_type=F32)


def _mod_call(cvec, w_ada, b_ada):
    rows = D_MODEL // 4
    return pl.pallas_call(
        _mod_kernel,
        grid=(D_MODEL // rows,),
        in_specs=[pl.BlockSpec((8, rows), lambda i: (0, i)),
                  pl.BlockSpec((rows, 3 * D_MODEL), lambda i: (i, 0)),
                  pl.BlockSpec((1, 3 * D_MODEL), lambda i: (0, 0))],
        out_specs=pl.BlockSpec((8, 3 * D_MODEL), lambda i: (0, 0)),
        out_shape=jax.ShapeDtypeStruct((8, 3 * D_MODEL), F32),
        compiler_params=pltpu.CompilerParams(dimension_semantics=("arbitrary",),
                                             vmem_limit_bytes=VMEM_LIMIT),
        name="mod",
    )(cvec, w_ada, b_ada)


def _modulated_norm(x, g, scale, shift):
    ms = jnp.mean(x * x, axis=-1, keepdims=True)
    y = x * lax.rsqrt(ms + NORM_EPS) * g
    return y * (1.0 + scale) + shift


def _in_proj_kernel(*refs, has_emb, has_halo, n_tiles):
    it = iter(refs)
    x_ref = next(it)
    xp_ref = next(it) if has_halo else None
    xn_ref = next(it) if has_halo else None
    emb_ref = next(it) if has_emb else None
    embp_ref = next(it) if (has_emb and has_halo) else None
    embn_ref = next(it) if (has_emb and has_halo) else None
    mod_ref, g_ref, w_ref, mu_ref = next(it), next(it), next(it), next(it)
    rk_ref, v_ref, lwla_ref, grec_ref, fin_ref, gf_ref = (next(it) for _ in range(6))

    i = pl.program_id(1)
    g = g_ref[...]
    shift = mod_ref[0, 0:1, :]
    scale = mod_ref[0, 1:2, :]
    nb, tt, _ = x_ref.shape
    tm = nb * tt
    x = x_ref[...].reshape(tm, D_MODEL)
    if has_emb:
        x = x + emb_ref[...]
    part = tm // ROW_SPLIT
    p = jnp.concatenate(
        [jnp.dot(_modulated_norm(x[r0:r0 + part], g, scale, shift).astype(BF16), w_ref[...],
                 preferred_element_type=F32) for r0 in range(0, tm, part)], axis=0)
    ps = p[:, :D_SHIFT]
    if has_halo:
        xh = jnp.concatenate([xp_ref[0], xn_ref[0]], axis=0)
        if has_emb:
            xh = xh + jnp.concatenate([embp_ref[...], embn_ref[...]], axis=0)
        hh = _modulated_norm(xh, g, scale, shift)
        ph = jnp.dot(hh.astype(BF16), w_ref[:, :D_SHIFT], preferred_element_type=F32)
        prev_row = jnp.where(i > 0, ph[HALO - 1:HALO, :], 0.0)
        next_row = jnp.where(i < n_tiles - 1, ph[HALO:HALO + 1, :], 0.0)
    else:
        prev_row = jnp.zeros((1, D_SHIFT), F32)
        next_row = jnp.zeros((1, D_SHIFT), F32)
    pos = lax.rem(lax.broadcasted_iota(jnp.int32, (tm, D_SHIFT), 0), tt)
    prev = jnp.where(pos == 0, prev_row, pltpu.roll(ps, 1, 0))
    nxt = jnp.where(pos == tt - 1, next_row, pltpu.roll(ps, tm - 1, 0))
    p_rec = ps + mu_ref[...] * (0.5 * (prev + nxt) - ps)

    def put(ref, val):
        ref[...] = val.astype(ref.dtype).reshape(ref.shape)

    put(rk_ref, p_rec[:, :2 * D_RWKV])
    put(v_ref, p_rec[:, 2 * D_RWKV:3 * D_RWKV])
    put(lwla_ref, p_rec[:, 3 * D_RWKV:])
    put(grec_ref, p[:, D_SHIFT:D_SHIFT + D_RWKV])
    put(fin_ref, p[:, D_SHIFT + D_RWKV:D_SHIFT + D_RWKV + D_FNET])
    put(gf_ref, p[:, D_SHIFT + D_RWKV + D_FNET:])


def _in_proj_call(x, emb, mod, norm_g, w_in_bf16, mu):
    B, T, _ = x.shape
    tt = min(T, ROW_TILE)
    nb = max(1, ROW_TILE // T)
    n_tiles = T // tt
    has_halo = n_tiles > 1
    has_emb = emb is not None
    per_batch_mod = mod.shape[0] > 1
    assert B % nb == 0 and T % tt == 0 and (nb == 1 or not (has_emb or per_batch_mod))
    tm = tt
    blocks_per_tile = tm // HALO
    last_halo_block = T // HALO - 1

    in_specs = [pl.BlockSpec((nb, tt, D_MODEL), lambda b, i: (b, i, 0))]
    args = [x]
    if has_halo:
        in_specs += [
            pl.BlockSpec((1, HALO, D_MODEL), lambda b, i: (b, jnp.maximum(i * blocks_per_tile - 1, 0), 0)),
            pl.BlockSpec((1, HALO, D_MODEL),
                         lambda b, i: (b, jnp.minimum((i + 1) * blocks_per_tile, last_halo_block), 0))]
        args += [x, x]
    if has_emb:
        in_specs.append(pl.BlockSpec((tm, D_MODEL), lambda b, i: (i, 0)))
        args.append(emb)
        if has_halo:
            in_specs += [
                pl.BlockSpec((HALO, D_MODEL), lambda b, i: (jnp.maximum(i * blocks_per_tile - 1, 0), 0)),
                pl.BlockSpec((HALO, D_MODEL),
                             lambda b, i: (jnp.minimum((i + 1) * blocks_per_tile, last_halo_block), 0))]
            args += [emb, emb]
    mod_map = (lambda b, i: (b, 0, 0)) if per_batch_mod else (lambda b, i: (0, 0, 0))
    in_specs += [pl.BlockSpec((1, 3, D_MODEL), mod_map),
                 pl.BlockSpec((1, D_MODEL), lambda b, i: (0, 0)),
                 pl.BlockSpec((D_MODEL, D_IN), lambda b, i: (0, 0), pipeline_mode=pl.Buffered(1)),
                 pl.BlockSpec((1, D_SHIFT), lambda b, i: (0, 0))]
    args += [mod, norm_g, w_in_bf16, mu]
    outs = ((2 * D_RWKV, BF16), (D_RWKV, BF16), (N_DIR * 2 * LORA, F32), (D_RWKV, BF16), (D_FNET, BF16),
            (D_FNET, BF16))
    out_specs = [pl.BlockSpec((nb, tt, w), lambda b, i: (b, i, 0)) for w, _ in outs]
    out_shape = [jax.ShapeDtypeStruct((B, T, w), dt) for w, dt in outs]
    kern = functools.partial(_in_proj_kernel, has_emb=has_emb, has_halo=has_halo, n_tiles=n_tiles)
    return pl.pallas_call(
        kern, grid=(B // nb, n_tiles), in_specs=in_specs, out_specs=out_specs, out_shape=out_shape,
        compiler_params=pltpu.CompilerParams(dimension_semantics=("arbitrary", "arbitrary"),
                                             vmem_limit_bytes=VMEM_LIMIT),
        name="in_proj",
    )(*args)


def _same_block(i, j, size):
    shift = size.bit_length() - 1
    return jnp.right_shift(i, shift) == jnp.right_shift(j, shift)


def _block_diag_rows(y, head0_lanes):
    return jnp.concatenate([jnp.where(head0_lanes, y, 0.0), jnp.where(head0_lanes, 0.0, y)], axis=0)


def _scan_kernel(*refs, zero_init, write_state, n_steps, n_seq, n_sub):
    it = iter(refs)
    rk_refs, v_refs, lwla_refs = [None, None], [None, None], [None, None]
    for d in range(N_DIR):
        rk_refs[d], v_refs[d], lwla_refs[d] = next(it), next(it), next(it)
    w0_ref, w2_ref, a0_ref, a2_ref = (next(it) for _ in range(4))
    kk_ref, ka_ref, tri_ref, ones_ref = (next(it) for _ in range(4))
    s0_ref = None if zero_init else next(it)
    y_refs = [next(it), next(it)]
    sfin_refs = [next(it), next(it)] if write_state else None
    h_scr = next(it)

    j = pl.program_id(1)

    @pl.when(j == 0)
    def _():
        if zero_init:
            h_scr[...] = jnp.zeros_like(h_scr)
        else:
            h_scr[...] = s0_ref[...]

    C = CHUNK
    row = lax.broadcasted_iota(jnp.int32, (C, LANES), 0)
    lane = lax.broadcasted_iota(jnp.int32, (C, LANES), 1)
    s_idx = jnp.bitwise_and(lane, HEAD_DIM - 1)
    head0 = lane < HEAD_DIM
    row_c = lax.broadcasted_iota(jnp.int32, (SUB, LANES), 0)
    lane_c = lax.broadcasted_iota(jnp.int32, (SUB, LANES), 1)
    col_c = jnp.bitwise_and(lane_c, SUB - 1)
    blk_c = jnp.right_shift(jnp.bitwise_and(lane_c, HEAD_DIM - 1), SUB_SHIFT)
    lane_blk_c = jnp.right_shift(lane_c, SUB_SHIFT)
    eye_c = (row_c == col_c).astype(F32)

    def bd_c(y):
        return jnp.concatenate([jnp.where(lane_blk_c == g, y, 0.0) for g in range(LANES // SUB)], axis=0)

    def inverse_size4(l_c, upper):
        r4 = jnp.bitwise_and(row_c, 3)
        c4 = jnp.bitwise_and(col_c, 3)
        down, up = (lambda x, k: pltpu.roll(x, k, 0)), (lambda x, k: pltpu.roll(x, SUB - k, 0))
        right, left = (lambda x, k: pltpu.roll(x, k, 1)), (lambda x, k: pltpu.roll(x, LANES - k, 1))
        s2 = jnp.where(_same_block(row_c, col_c, 2), l_c, 0.0)
        e4 = jnp.where(_same_block(row_c, col_c, 4), l_c, 0.0) - s2
        if upper:
            s_col = jnp.where(r4 == 1, up(s2, 1), up(s2, 2))
            s_row = jnp.where(c4 == 2, right(s2, 1), right(s2, 2))
            e_s = jnp.where(c4 == 3, right(e4, 1) * s_col, 0.0)
            s_e = jnp.where(r4 == 0, s_row * up(e4, 1), 0.0)
            s_e_s = jnp.where((r4 == 0) & (c4 == 3), s_row * up(e_s, 1), 0.0)
        else:
            s_col = jnp.where(r4 == 2, down(s2, 1), down(s2, 2))
            s_row = jnp.where(c4 == 1, left(s2, 1), left(s2, 2))
            e_s = jnp.where(c4 == 0, left(e4, 1) * s_col, 0.0)
            s_e = jnp.where(r4 == 3, s_row * down(e4, 1), 0.0)
            s_e_s = jnp.where((r4 == 3) & (c4 == 0), s_row * down(e_s, 1), 0.0)
        return eye_c - s2 - e4 + e_s + s_e - s_e_s

    row2 = lax.broadcasted_iota(jnp.int32, (LANES, LANES), 0)
    lane2 = lax.broadcasted_iota(jnp.int32, (LANES, LANES), 1)
    same_head = (row2 < HEAD_DIM) == (lane2 < HEAD_DIM)
    decay_scale = math.exp(-0.5)

    row0 = lambda o, d: (o if d == 0 else n_sub - 1 - o) * C
    keys = [(o, n, d) for o in range(n_sub) for n in range(n_seq) for d in range(N_DIR)]
    rk_all = {(o, n, d): rk_refs[d][n, row0(o, d):row0(o, d) + C, :].astype(F32) for o, n, d in keys}
    kkr_all = {key: rk_all[key][:, D_RWKV:] * kk_ref[...] for key in keys}
    ssq_rows = _bdot(jnp.concatenate([kkr_all[key] * kkr_all[key] for key in keys], axis=0), ones_ref[...])
    dir_keys = [[key for key in keys if key[2] == d] for d in range(N_DIR)]
    z_w_rows, z_a_rows = [], []
    for d in range(N_DIR):
        ll_rows = jnp.concatenate([lwla_refs[d][n, row0(o, d):row0(o, d) + C, :] for o, n, _ in dir_keys[d]], axis=0)
        z_w_rows.append(w0_ref[d:d + 1, :] + _bdot(jnp.tanh(ll_rows), w2_ref[d]))
        z_a_rows.append(a0_ref[d:d + 1, :] + _bdot(ll_rows, a2_ref[d]))

    sets = [[] for _ in range(n_sub)]
    for i, (o, n, d) in enumerate(keys):
        rk = rk_all[(o, n, d)]
        r = rk[:, :D_RWKV]
        k = rk[:, D_RWKV:]
        v = v_refs[d][n, row0(o, d):row0(o, d) + C, :]
        i_d = dir_keys[d].index((o, n, d))
        logw = -decay_scale * jax.nn.sigmoid(z_w_rows[d][i_d * C:(i_d + 1) * C])
        a = jax.nn.sigmoid(z_a_rows[d][i_d * C:(i_d + 1) * C])
        kd = k * (1.0 + (a - 1.0) * ka_ref[...])
        kkr = kkr_all[(o, n, d)]
        ssq = ssq_rows[i * C:(i + 1) * C]
        kk = kkr * lax.rsqrt(jnp.maximum(ssq, 1e-24))
        bvec = kk * a
        lw_hi, lw_lo = _split2(logw)
        tri = tri_ref[d]
        cum = (jnp.dot(tri, lw_hi, preferred_element_type=F32)
               + jnp.dot(tri, lw_lo, preferred_element_type=F32))
        cum_prev = cum - logw
        tot = cum[C - 1:C, :] if d == 0 else cum[0:1, :]
        kap_t = kk * jnp.exp(cum_prev)
        r_t = r * jnp.exp(cum)
        e_neg = jnp.exp(-cum)
        gam = jnp.exp(tot)
        b_t = bvec * e_neg
        k_t = kd * e_neg
        e_rem = gam * e_neg
        b_h = bvec * e_rem
        k_h = kd * e_rem
        if d == 0:
            strict, incl = row > s_idx, row >= s_idx
        else:
            strict, incl = row < s_idx, row <= s_idx

        for p in range(N_PAIRS):
            sl = slice(p * LANES, (p + 1) * LANES)
            dup_t = lambda x: jnp.where(same_head, jnp.concatenate([x[:, sl], x[:, sl]], axis=0).T, 0.0)
            sets[o].append(dict(
                n=n, d=d, p=p, sl=sl, rows=slice(row0(o, d), row0(o, d) + C), strict=strict, incl=incl,
                prev=None if o == 0 else sets[o - 1][len(sets[o])],
                lhs=jnp.concatenate([kap_t[:, sl], r_t[:, sl]], axis=0).astype(BF16),
                v=v[:, sl], w_lm=jnp.concatenate([dup_t(b_t), dup_t(k_t)], axis=1).astype(BF16),
                t2=jnp.concatenate([b_h[:, sl], k_h[:, sl]], axis=0).T.astype(BF16),
                gam_col=jnp.broadcast_to(gam[:, sl], (LANES, LANES)).T))

    bd = lambda y: _block_diag_rows(y, head0)

    def st_scores(probs):
        for q in probs:
            lm = jnp.dot(q["lhs"], q["w_lm"], preferred_element_type=F32)
            q["l_b"] = jnp.where(q["strict"], lm[:C, :LANES], 0.0)
            q["m_b"] = jnp.where(q["incl"], lm[C:, :LANES], 0.0)
            q["lm_k"] = jnp.concatenate([jnp.where(q["strict"], lm[:C, LANES:], 0.0),
                                         jnp.where(q["incl"], lm[C:, LANES:], 0.0)], axis=0)

    def st_values(probs):
        for q in probs:
            lmv = _bdot(q["lm_k"], bd(q["v"]))
            q["l_kv"], q["m_kv"] = lmv[:C], lmv[C:]
            l_b = q["l_b"]
            l_c = l_b[0:SUB]
            for jb in range(1, C // SUB):
                l_c = jnp.where(blk_c == jb, l_b[jb * SUB:(jb + 1) * SUB], l_c)
            q["l_c"] = l_c
            q["t_c"] = inverse_size4(l_c, q["d"] == 1)

    def st_compact_a(s):
        def run(probs):
            off_mask = _same_block(row_c, col_c, 2 * s) & ~_same_block(row_c, col_c, s)
            for q in probs:
                q["et"] = _bdot(jnp.where(off_mask, q["l_c"], 0.0), bd_c(q["t_c"]))
        return run

    def st_compact_b(last):
        def run(probs):
            for q in probs:
                q["t_c"] = q["t_c"] - _bdot(q["t_c"], bd_c(q["et"]))
                if last:
                    q["tinv"] = jnp.concatenate([jnp.where(blk_c == jb, q["t_c"], 0.0) for jb in range(C // SUB)],
                                                axis=0)
        return run

    def st_full_a(s):
        def run(probs):
            off_mask = _same_block(row, s_idx, 2 * s) & ~_same_block(row, s_idx, s)
            for q in probs:
                q["et"] = _bdot(jnp.where(off_mask, q["l_b"], 0.0), bd(q["tinv"]))
        return run

    def st_full_b(probs):
        for q in probs:
            q["tinv"] = q["tinv"] - _bdot(q["tinv"], bd(q["et"]))

    def st_state_read(probs):
        for q in probs:
            q["h"] = h_scr[q["n"], q["d"], q["p"]] if q["prev"] is None else q["prev"]["h_new"]
            q["kr_h"] = _bdot(q["lhs"], q["h"])

    def st_solve(probs):
        for q in probs:
            q["u_n"] = _bdot(q["tinv"], bd(q["kr_h"][:C] + q["l_kv"]))

    def st_output(probs):
        for q in probs:
            y = q["kr_h"][C:] + q["m_kv"] - _bdot(q["m_b"], bd(q["u_n"]))
            y_refs[q["d"]][q["n"], q["rows"], q["sl"]] = y.astype(BF16)

    def st_state_write(probs):
        for q in probs:
            upd = _bdot(q["t2"], jnp.concatenate([(-q["u_n"]).astype(BF16), q["v"]], axis=0))
            q["h_new"] = q["gam_col"] * q["h"] + jnp.where(same_head, upd, 0.0)

    stages = [st_scores, st_values] + ([st_state_read] if n_sub == 1 else [])
    s = 4
    while s < SUB:
        stages += [st_compact_a(s), st_compact_b(2 * s == SUB)]
        s *= 2
    while s < C:
        stages += [st_full_a(s), st_full_b]
        s *= 2
    stages += ([] if n_sub == 1 else [st_state_read]) + [st_solve, st_output, st_state_write]
    skew = 4
    for t in range(len(stages) + skew * (n_sub - 1)):
        for o in range(n_sub):
            if 0 <= t - skew * o < len(stages):
                stages[t - skew * o](sets[o])
    for q in sets[-1]:
        h_scr[q["n"], q["d"], q["p"]] = q["h_new"]

    if write_state:
        @pl.when(j == n_steps - 1)
        def _():
            for n, d, p in [(n, d, p) for n in range(n_seq) for d in range(N_DIR) for p in range(N_PAIRS)]:
                ht = h_scr[n, d, p].T
                sfin_refs[d][n, 0, PAIR * p] = ht[:HEAD_DIM, :HEAD_DIM]
                sfin_refs[d][n, 0, PAIR * p + 1] = ht[HEAD_DIM:, HEAD_DIM:]


def _scan_call(rk, v, lwla, wts, s0_bd, write_state):
    B, T, _ = rk.shape
    n_sub = min(SCAN_SUB, T // CHUNK)
    ns = min(max(1, SCAN_WORK // n_sub), B)
    rows = n_sub * CHUNK
    nc = T // rows
    assert B % ns == 0 and T % rows == 0
    zero_init = s0_bd is None
    fwd = lambda b, j: (b, j, 0)
    bwd = lambda b, j: (b, nc - 1 - j, 0)
    full = lambda *shape: pl.BlockSpec(shape, lambda b, j: (0,) * len(shape))
    tok = lambda w, m: pl.BlockSpec((ns, rows, w), m)
    in_specs = [tok(2 * D_RWKV, fwd), tok(D_RWKV, fwd), tok(LANES, fwd),
                tok(2 * D_RWKV, bwd), tok(D_RWKV, bwd), tok(LANES, bwd),
                full(N_DIR, D_RWKV), full(N_DIR, LANES, D_RWKV),
                full(N_DIR, D_RWKV), full(N_DIR, LANES, D_RWKV),
                full(1, D_RWKV), full(1, D_RWKV), full(N_DIR, CHUNK, CHUNK), full(D_RWKV, D_RWKV)]
    args = [rk, v, lwla, rk, v, lwla, wts["w0"], wts["w2"], wts["a0"], wts["a2"],
            wts["k_k"], wts["k_a"], wts["tri"], wts["ones_bd"]]
    state_block = (ns, N_DIR, N_PAIRS, LANES, LANES)
    if not zero_init:
        in_specs.append(pl.BlockSpec(state_block, lambda b, j: (b, 0, 0, 0, 0)))
        args.append(s0_bd)
    out_specs = [tok(D_RWKV, fwd), tok(D_RWKV, bwd)]
    out_shape = [jax.ShapeDtypeStruct((B, T, D_RWKV), BF16), jax.ShapeDtypeStruct((B, T, D_RWKV), BF16)]
    if write_state:
        final_block = (ns, 1, N_HEADS, HEAD_DIM, HEAD_DIM)
        out_specs += [pl.BlockSpec(final_block, lambda b, j: (b, 0, 0, 0, 0))] * N_DIR
        out_shape += [jax.ShapeDtypeStruct((B,) + final_block[1:], F32)] * N_DIR
    kern = functools.partial(_scan_kernel, zero_init=zero_init, write_state=write_state, n_steps=nc,
                             n_seq=ns, n_sub=n_sub)
    return pl.pallas_call(
        kern, grid=(B // ns, nc), in_specs=in_specs, out_specs=out_specs, out_shape=out_shape,
        scratch_shapes=[pltpu.VMEM(state_block, F32)],
        compiler_params=pltpu.CompilerParams(dimension_semantics=("arbitrary", "arbitrary"),
                                             vmem_limit_bytes=VMEM_LIMIT),
        name="scan",
    )(*args)


def _fnet_kernel(fin_ref, gf_ref, dft_ref, cs_ref, wbd_ref, b_ref, o_ref, g_scr):
    u = pl.program_id(1)
    nb, seq_len, _ = fin_ref.shape
    tu = o_ref.shape[1]

    @pl.when(u == 0)
    def _():
        step = min(seq_len, ROW_TILE)
        for n in range(nb):
            for t0 in range(0, seq_len, step):
                fc = _bdot(fin_ref[n, t0:t0 + step, :], cs_ref[...])
                g_scr[n, t0:t0 + step, :] = fc[:, :D_FNET].astype(BF16)
                g_scr[n, seq_len + t0:seq_len + t0 + step, :] = fc[:, D_FNET:].astype(BF16)

    f_re = jnp.concatenate([jnp.dot(dft_ref[...], g_scr[n], preferred_element_type=F32) for n in range(nb)],
                           axis=0)
    f_out = _bdot(f_re, wbd_ref[...]) + b_ref[...]
    gate = _silu(gf_ref[...].astype(F32)).reshape(nb * tu, D_FNET)
    o_ref[...] = (f_out * gate).astype(BF16).reshape(nb, tu, D_FNET)


def _fnet_call(fin, gf, dft_bf16, cs_bf16, wbd_bf16, b_fnet):
    B, T, _ = fin.shape
    tu = min(T, FNET_ROWS)
    nb = max(1, FNET_ROWS // T)
    assert B % nb == 0 and T % tu == 0
    return pl.pallas_call(
        _fnet_kernel,
        grid=(B // nb, T // tu),
        in_specs=[pl.BlockSpec((nb, T, D_FNET), lambda b, u: (b, 0, 0)),
                  pl.BlockSpec((nb, tu, D_FNET), lambda b, u: (b, u, 0)),
                  pl.BlockSpec((tu, 2 * T), lambda b, u: (u, 0)),
                  pl.BlockSpec((D_FNET, 2 * D_FNET), lambda b, u: (0, 0)),
                  pl.BlockSpec((D_FNET, D_FNET), lambda b, u: (0, 0)),
                  pl.BlockSpec((1, D_FNET), lambda b, u: (0, 0))],
        out_specs=pl.BlockSpec((nb, tu, D_FNET), lambda b, u: (b, u, 0)),
        out_shape=jax.ShapeDtypeStruct((B, T, D_FNET), BF16),
        scratch_shapes=[pltpu.VMEM((nb, 2 * T, D_FNET), BF16)],
        compiler_params=pltpu.CompilerParams(dimension_semantics=("arbitrary", "arbitrary"),
                                             vmem_limit_bytes=VMEM_LIMIT),
        name="fnet",
    )(fin, gf, dft_bf16, cs_bf16, wbd_bf16, b_fnet)


def _out_kernel(*refs, has_emb, final_norm):
    it = iter(refs)
    x_ref = next(it)
    emb_ref = next(it) if has_emb else None
    (mod_ref, yf_ref, yb_ref, rk_ref, v_ref, lwla_ref, grec_ref, fo_ref, a0_ref, a2_ref, ka_ref,
     rkw_ref, gng_ref, gnb_ref, avg_ref, ones_ref, wout_ref, fng_ref, o_ref) = (next(it) for _ in range(19))

    nb, tt, _ = x_ref.shape
    tm = nb * tt
    rows = lambda ref: ref[...].reshape(tm, ref.shape[-1])
    y = rows(yf_ref).astype(F32) + rows(yb_ref).astype(F32)
    mu = _bdot(y, avg_ref[...])
    dlt = y - mu
    var = _bdot(dlt * dlt, avg_ref[...])
    y_n = dlt * lax.rsqrt(var + GN_EPS) * gng_ref[...] + gnb_ref[...]
    rk = rows(rk_ref).astype(F32)
    r = rk[:, :D_RWKV]
    k = rk[:, D_RWKV:]
    ll = rows(lwla_ref)
    a_sum = (jax.nn.sigmoid(a0_ref[0:1, :] + _bdot(ll, a2_ref[0]))
             + jax.nn.sigmoid(a0_ref[1:2, :] + _bdot(ll, a2_ref[1])))
    k_sum = k * (2.0 + (a_sum - 2.0) * ka_ref[...])
    bonus = _bdot(r * k_sum * rkw_ref[...], ones_ref[...]) * rows(v_ref).astype(F32)
    rec_out = (y_n + bonus) * _silu(rows(grec_ref).astype(F32))
    mixed = jnp.concatenate([rec_out.astype(BF16), rows(fo_ref)], axis=-1)
    out = jnp.dot(mixed, wout_ref[...], preferred_element_type=F32)
    x = rows(x_ref)
    if has_emb:
        x = x + emb_ref[...]
    z = x + mod_ref[0, 2:3, :] * out
    if final_norm:
        ms = jnp.mean(z * z, axis=-1, keepdims=True)
        z = z * lax.rsqrt(ms + NORM_EPS) * fng_ref[...]
    o_ref[...] = z.reshape(nb, tt, D_MODEL)


def _out_call(x, emb, mod, y_f, y_b, rk, v, lwla, grec, fo, wts, w_out_bf16, final_norm_g, final_norm):
    B, T, _ = x.shape
    tt = min(T, ROW_TILE)
    nb = max(1, ROW_TILE // T)
    has_emb = emb is not None
    per_batch_mod = mod.shape[0] > 1
    assert B % nb == 0 and T % tt == 0 and (nb == 1 or not (has_emb or per_batch_mod))
    tok = lambda w: pl.BlockSpec((nb, tt, w), lambda b, i: (b, i, 0))
    full = lambda *shape: pl.BlockSpec(shape, lambda b, i: (0,) * len(shape))
    in_specs = [tok(D_MODEL)]
    args = [x]
    if has_emb:
        in_specs.append(pl.BlockSpec((tt, D_MODEL), lambda b, i: (i, 0)))
        args.append(emb)
    mod_map = (lambda b, i: (b, 0, 0)) if per_batch_mod else (lambda b, i: (0, 0, 0))
    in_specs += [pl.BlockSpec((1, 3, D_MODEL), mod_map), tok(D_RWKV), tok(D_RWKV), tok(2 * D_RWKV),
                 tok(D_RWKV), tok(LANES), tok(D_RWKV), tok(D_FNET),
                 full(N_DIR, D_RWKV), full(N_DIR, LANES, D_RWKV),
                 full(1, D_RWKV), full(1, D_RWKV), full(1, D_RWKV), full(1, D_RWKV),
                 full(D_RWKV, D_RWKV), full(D_RWKV, D_RWKV), full(D_MODEL, D_MODEL), full(1, D_MODEL)]
    args += [mod, y_f, y_b, rk, v, lwla, grec, fo, wts["a0"], wts["a2"], wts["k_a"],
             wts["r_k"], wts["gn_g"], wts["gn_b"], wts["avg_bd"], wts["ones_bd"], w_out_bf16, final_norm_g]
    return pl.pallas_call(
        functools.partial(_out_kernel, has_emb=has_emb, final_norm=final_norm),
        grid=(B // nb, T // tt), in_specs=in_specs,
        out_specs=tok(D_MODEL),
        out_shape=jax.ShapeDtypeStruct((B, T, D_MODEL), F32),
        compiler_params=pltpu.CompilerParams(dimension_semantics=("arbitrary", "arbitrary"),
                                             vmem_limit_bytes=VMEM_LIMIT),
        name="out_proj",
    )(*args)


def _dft_table(seq_len):
    idx = np.arange(seq_len, dtype=np.int64)
    ang = 2.0 * np.pi * ((idx[:, None] * idx[None, :]) % seq_len).astype(np.float64) / seq_len
    scale = 1.0 / math.sqrt(seq_len)
    return np.concatenate([np.cos(ang) * scale, -np.sin(ang) * scale], axis=1).astype(np.float32)


def _channel_dft_table():
    n = FNET_GROUP
    idx = np.arange(n, dtype=np.int64)
    ang = 2.0 * np.pi * ((idx[:, None] * idx[None, :]) % n).astype(np.float64) / n
    c = np.cos(ang) / math.sqrt(n)
    s = np.sin(ang) / math.sqrt(n)
    eye = np.eye(D_FNET // n)
    return np.concatenate([np.kron(eye, c), np.kron(eye, s)], axis=1).astype(np.float32)


def _sincos_2d(n_tokens):
    rows = n_tokens // GRID_W
    pos = np.arange(rows * GRID_W)
    row = (pos // GRID_W).astype(np.float32)
    col = (pos % GRID_W).astype(np.float32)
    quarter = D_MODEL // 4
    freq = np.exp(np.float32(-math.log(POS_BASE)) * np.arange(quarter, dtype=np.float32) / np.float32(quarter))
    ang_r = row[:, None] * freq
    ang_c = col[:, None] * freq
    return np.concatenate([np.sin(ang_r), np.cos(ang_r), np.sin(ang_c), np.cos(ang_c)], axis=-1).astype(np.float32)


def _head_block_matrix(value):
    blk = np.kron(np.eye(N_HEADS), np.ones((HEAD_DIM, HEAD_DIM))) * value
    return jnp.asarray(blk, dtype=BF16)


def _pad_lora(w, row_offset):
    rows = [jnp.pad(w[d], ((row_offset + d * LORA, LANES - row_offset - (d + 1) * LORA), (0, 0)))
            for d in range(N_DIR)]
    return jnp.stack(rows).astype(BF16)


def _layer_weights(l, w0, w2, a0, a2, k_k, k_a, r_k, gn_g, gn_b):
    w2_p = _pad_lora(w2[l], 0)
    a2_p = _pad_lora(a2[l], N_DIR * LORA)
    tri_f = np.tril(np.ones((CHUNK, CHUNK)))
    tri = jnp.asarray(np.stack([tri_f, tri_f.T]), dtype=BF16)
    return dict(w0=w0[l], w2=w2_p, a0=a0[l], a2=a2_p,
                k_k=k_k[l][None], k_a=k_a[l][None], r_k=r_k[l].reshape(1, D_RWKV),
                gn_g=gn_g[l].reshape(1, D_RWKV), gn_b=gn_b[l].reshape(1, D_RWKV), tri=tri,
                ones_bd=_head_block_matrix(1.0), avg_bd=_head_block_matrix(1.0 / HEAD_DIM))


def _state_to_block_diag(s_f, s_b):
    def one(s):
        h = jnp.swapaxes(s.astype(F32), -1, -2)
        b = h.shape[0]
        h = h.reshape(b, N_PAIRS, PAIR, HEAD_DIM, HEAD_DIM)
        z = jnp.zeros_like(h[:, :, 0])
        top = jnp.concatenate([h[:, :, 0], z], axis=-1)
        bot = jnp.concatenate([z, h[:, :, 1]], axis=-1)
        return jnp.concatenate([top, bot], axis=-2)
    return jnp.stack([one(s_f), one(s_b)], axis=1)


def kernel(x_prompt, x_sample, state_rwkv_fwd, state_rwkv_bwd, c, c_ctx, w_ada, b_ada, norm_g, w_in,
           mu_shift, w0, w2, a0, a2, k_k, k_a, r_k, gn_g, gn_b, w_fnet, b_fnet, w_out, final_norm_g):
    depth = w_in.shape[0]
    n_dec = c.shape[0]
    assert n_dec + 1 <= 8
    bp, tp, _ = x_prompt.shape
    bs, ts, _ = x_sample.shape
    cvec = jnp.concatenate([c_ctx[None], c, jnp.zeros((7 - n_dec, D_MODEL), F32)], axis=0)
    emb = jnp.asarray(_sincos_2d(ts)).astype(x_sample.dtype)
    cs_tab = jnp.asarray(_channel_dft_table()).astype(BF16)
    dft_p = jnp.asarray(_dft_table(tp)).astype(BF16)
    dft_s = jnp.asarray(_dft_table(ts)).astype(BF16)
    fng = final_norm_g[None]

    xp, xs = x_prompt, x_sample
    new_f, new_b = [], []
    for l in range(depth):
        mod = _mod_call(cvec, w_ada[l], b_ada[l][None]).reshape(8, 3, D_MODEL)
        mod_ctx, mod_lat = mod[0:1], mod[1:1 + n_dec]
        wts = _layer_weights(l, w0, w2, a0, a2, k_k, k_a, r_k, gn_g, gn_b)
        w_in_b = w_in[l].astype(BF16)
        w_out_b = w_out[l].astype(BF16)
        n_grp = w_fnet.shape[1]
        wbd = (w_fnet[l][:, :, None, :] * jnp.eye(n_grp, dtype=F32)[:, None, :, None]).reshape(
            D_FNET, D_FNET).astype(BF16)
        ng, mu, bf = norm_g[l][None], mu_shift[l][None], b_fnet[l][None]
        emb_l = emb if l == 0 else None

        rk, v, lwla, grec, fin, gf = _in_proj_call(xp, None, mod_ctx, ng, w_in_b, mu)
        y_f, y_b, s_f, s_b = _scan_call(rk, v, lwla, wts, None, True)
        fo = _fnet_call(fin, gf, dft_p, cs_tab, wbd, bf)
        last = l == depth - 1
        xp = _out_call(xp, None, mod_ctx, y_f, y_b, rk, v, lwla, grec, fo, wts, w_out_b, fng, last)
        new_f.append(s_f)
        new_b.append(s_b)

        rk, v, lwla, grec, fin, gf = _in_proj_call(xs, emb_l, mod_lat, ng, w_in_b, mu)
        s0 = _state_to_block_diag(state_rwkv_fwd[:, l], state_rwkv_bwd[:, l])
        y_f, y_b = _scan_call(rk, v, lwla, wts, s0, False)
        fo = _fnet_call(fin, gf, dft_s, cs_tab, wbd, bf)
        xs = _out_call(xs, emb_l, mod_lat, y_f, y_b, rk, v, lwla, grec, fo, wts, w_out_b, fng, last)
    return (xp, xs, jnp.concatenate(new_f, axis=1), jnp.concatenate(new_b, axis=1))
```

```python
import functools
import math

import numpy as np
import jax
import jax.numpy as jnp
from jax import lax
from jax.experimental import pallas as pl
from jax.experimental.pallas import tpu as pltpu

F32 = jnp.float32
BF16 = jnp.bfloat16

D_MODEL = 1024
GRID_W = 64
D_RWKV = 512
D_FNET = D_MODEL - D_RWKV
HEAD_DIM = 64
N_HEADS = D_RWKV // HEAD_DIM
FNET_GROUP = 64
LORA = 32
N_DIR = 2
D_SHIFT = 3 * D_RWKV + N_DIR * 2 * LORA
D_IN = D_SHIFT + D_RWKV + 2 * D_FNET
NORM_EPS = 1e-6
GN_EPS = 64e-5
POS_BASE = 10000.0

LANES = 128
PAIR = LANES // HEAD_DIM
N_PAIRS = N_HEADS // PAIR
CHUNK = 64
SUB = 16
SUB_SHIFT = SUB.bit_length() - 1
SCAN_WORK = 8
SCAN_SUB = 4
ROW_TILE = 1024
ROW_SPLIT = 4
FNET_ROWS = 1024
HALO = 8
VMEM_LIMIT = 56 * 1024 * 1024


def _silu(x):
    return x * jax.nn.sigmoid(x)


def _bdot(a, b):
    return jnp.dot(a.astype(BF16), b.astype(BF16), preferred_element_type=F32)


def _split2(x):
    hi = x.astype(BF16)
    lo = (x - hi.astype(F32)).astype(BF16)
    return hi, lo


def _mod_kernel(c_ref, w_ref, b_ref, o_ref):
    @pl.when(pl.program_id(0) == 0)
    def _():
        o_ref[...] = jnp.broadcast_to(b_ref[...], o_ref.shape)

    s_hi, s_lo = _split2(_silu(c_ref[...]))
    w = w_ref[...].astype(BF16)
    o_ref[...] += jnp.dot(s_hi, w, preferred_element_type=F32) + jnp.dot(s_lo, w, preferred_element_type=F32)


def _mod_call(cvec, w_ada, b_ada):
    rows = D_MODEL // 4
    return pl.pallas_call(
        _mod_kernel,
        grid=(D_MODEL // rows,),
        in_specs=[pl.BlockSpec((8, rows), lambda i: (0, i)),
                  pl.BlockSpec((rows, 3 * D_MODEL), lambda i: (i, 0)),
                  pl.BlockSpec((1, 3 * D_MODEL), lambda i: (0, 0))],
        out_specs=pl.BlockSpec((8, 3 * D_MODEL), lambda i: (0, 0)),
        out_shape=jax.ShapeDtypeStruct((8, 3 * D_MODEL), F32),
        compiler_params=pltpu.CompilerParams(dimension_semantics=("arbitrary",),
                                             vmem_limit_bytes=VMEM_LIMIT),
        name="mod",
    )(cvec, w_ada, b_ada)


def _modulated_norm(x, g, scale, shift):
    ms = jnp.mean(x * x, axis=-1, keepdims=True)
    y = x * lax.rsqrt(ms + NORM_EPS) * g
    return y * (1.0 + scale) + shift


def _in_proj_kernel(*refs, has_emb, has_halo, n_tiles):
    it = iter(refs)
    x_ref = next(it)
    xp_ref = next(it) if has_halo else None
    xn_ref = next(it) if has_halo else None
    emb_ref = next(it) if has_emb else None
    embp_ref = next(it) if (has_emb and has_halo) else None
    embn_ref = next(it) if (has_emb and has_halo) else None
    mod_ref, g_ref, w_ref, mu_ref = next(it), next(it), next(it), next(it)
    rk_ref, v_ref, lwla_ref, grec_ref, fin_ref, gf_ref = (next(it) for _ in range(6))

    i = pl.program_id(1)
    g = g_ref[...]
    shift = mod_ref[0, 0:1, :]
    scale = mod_ref[0, 1:2, :]
    nb, tt, _ = x_ref.shape
    tm = nb * tt
    x = x_ref[...].reshape(tm, D_MODEL)
    if has_emb:
        x = x + emb_ref[...]
    part = tm // ROW_SPLIT
    p = jnp.concatenate(
        [jnp.dot(_modulated_norm(x[r0:r0 + part], g, scale, shift).astype(BF16), w_ref[...],
                 preferred_element_type=F32) for r0 in range(0, tm, part)], axis=0)
    ps = p[:, :D_SHIFT]
    if has_halo:
        xh = jnp.concatenate([xp_ref[0], xn_ref[0]], axis=0)
        if has_emb:
            xh = xh + jnp.concatenate([embp_ref[...], embn_ref[...]], axis=0)
        hh = _modulated_norm(xh, g, scale, shift)
        ph = jnp.dot(hh.astype(BF16), w_ref[:, :D_SHIFT], preferred_element_type=F32)
        prev_row = jnp.where(i > 0, ph[HALO - 1:HALO, :], 0.0)
        next_row = jnp.where(i < n_tiles - 1, ph[HALO:HALO + 1, :], 0.0)
    else:
        prev_row = jnp.zeros((1, D_SHIFT), F32)
        next_row = jnp.zeros((1, D_SHIFT), F32)
    pos = lax.rem(lax.broadcasted_iota(jnp.int32, (tm, D_SHIFT), 0), tt)
    prev = jnp.where(pos == 0, prev_row, pltpu.roll(ps, 1, 0))
    nxt = jnp.where(pos == tt - 1, next_row, pltpu.roll(ps, tm - 1, 0))
    p_rec = ps + mu_ref[...] * (0.5 * (prev + nxt) - ps)

    def put(ref, val):
        ref[...] = val.astype(ref.dtype).reshape(ref.shape)

    put(rk_ref, p_rec[:, :2 * D_RWKV])
    put(v_ref, p_rec[:, 2 * D_RWKV:3 * D_RWKV])
    put(lwla_ref, p_rec[:, 3 * D_RWKV:])
    put(grec_ref, p[:, D_SHIFT:D_SHIFT + D_RWKV])
    put(fin_ref, p[:, D_SHIFT + D_RWKV:D_SHIFT + D_RWKV + D_FNET])
    put(gf_ref, p[:, D_SHIFT + D_RWKV + D_FNET:])


def _in_proj_call(x, emb, mod, norm_g, w_in_bf16, mu):
    B, T, _ = x.shape
    tt = min(T, ROW_TILE)
    nb = max(1, ROW_TILE // T)
    n_tiles = T // tt
    has_halo = n_tiles > 1
    has_emb = emb is not None
    per_batch_mod = mod.shape[0] > 1
    assert B % nb == 0 and T % tt == 0 and (nb == 1 or not (has_emb or per_batch_mod))
    tm = tt
    blocks_per_tile = tm // HALO
    last_halo_block = T // HALO - 1

    in_specs = [pl.BlockSpec((nb, tt, D_MODEL), lambda b, i: (b, i, 0))]
    args = [x]
    if has_halo:
        in_specs += [
            pl.BlockSpec((1, HALO, D_MODEL), lambda b, i: (b, jnp.maximum(i * blocks_per_tile - 1, 0), 0)),
            pl.BlockSpec((1, HALO, D_MODEL),
                         lambda b, i: (b, jnp.minimum((i + 1) * blocks_per_tile, last_halo_block), 0))]
        args += [x, x]
    if has_emb:
        in_specs.append(pl.BlockSpec((tm, D_MODEL), lambda b, i: (i, 0)))
        args.append(emb)
        if has_halo:
            in_specs += [
                pl.BlockSpec((HALO, D_MODEL), lambda b, i: (jnp.maximum(i * blocks_per_tile - 1, 0), 0)),
                pl.BlockSpec((HALO, D_MODEL),
                             lambda b, i: (jnp.minimum((i + 1) * blocks_per_tile, last_halo_block), 0))]
            args += [emb, emb]
    mod_map = (lambda b, i: (b, 0, 0)) if per_batch_mod else (lambda b, i: (0, 0, 0))
    in_specs += [pl.BlockSpec((1, 3, D_MODEL), mod_map),
                 pl.BlockSpec((1, D_MODEL), lambda b, i: (0, 0)),
                 pl.BlockSpec((D_MODEL, D_IN), lambda b, i: (0, 0), pipeline_mode=pl.Buffered(1)),
                 pl.BlockSpec((1, D_SHIFT), lambda b, i: (0, 0))]
    args += [mod, norm_g, w_in_bf16, mu]
    outs = ((2 * D_RWKV, BF16), (D_RWKV, BF16), (N_DIR * 2 * LORA, F32), (D_RWKV, BF16), (D_FNET, BF16),
            (D_FNET, BF16))
    out_specs = [pl.BlockSpec((nb, tt, w), lambda b, i: (b, i, 0)) for w, _ in outs]
    out_shape = [jax.ShapeDtypeStruct((B, T, w), dt) for w, dt in outs]
    kern = functools.partial(_in_proj_kernel, has_emb=has_emb, has_halo=has_halo, n_tiles=n_tiles)
    return pl.pallas_call(
        kern, grid=(B // nb, n_tiles), in_specs=in_specs, out_specs=out_specs, out_shape=out_shape,
        compiler_params=pltpu.CompilerParams(dimension_semantics=("arbitrary", "arbitrary"),
                                             vmem_limit_bytes=VMEM_LIMIT),
        name="in_proj",
    )(*args)


def _same_block(i, j, size):
    shift = size.bit_length() - 1
    return jnp.right_shift(i, shift) == jnp.right_shift(j, shift)


def _block_diag_rows(y, head0_lanes):
    return jnp.concatenate([jnp.where(head0_lanes, y, 0.0), jnp.where(head0_lanes, 0.0, y)], axis=0)


def _scan_kernel(*refs, zero_init, write_state, n_steps, n_seq, n_sub):
    it = iter(refs)
    rk_refs, v_refs, lwla_refs = [None, None], [None, None], [None, None]
    for d in range(N_DIR):
        rk_refs[d], v_refs[d], lwla_refs[d] = next(it), next(it), next(it)
    w0_ref, w2_ref, a0_ref, a2_ref = (next(it) for _ in range(4))
    kk_ref, ka_ref, tri_ref, ones_ref = (next(it) for _ in range(4))
    s0_ref = None if zero_init else next(it)
    y_refs = [next(it), next(it)]
    sfin_refs = [next(it), next(it)] if write_state else None
    h_scr = next(it)

    j = pl.program_id(1)

    @pl.when(j == 0)
    def _():
        if zero_init:
            h_scr[...] = jnp.zeros_like(h_scr)
        else:
            h_scr[...] = s0_ref[...]

    C = CHUNK
    row = lax.broadcasted_iota(jnp.int32, (C, LANES), 0)
    lane = lax.broadcasted_iota(jnp.int32, (C, LANES), 1)
    s_idx = jnp.bitwise_and(lane, HEAD_DIM - 1)
    head0 = lane < HEAD_DIM
    row_c = lax.broadcasted_iota(jnp.int32, (SUB, LANES), 0)
    lane_c = lax.broadcasted_iota(jnp.int32, (SUB, LANES), 1)
    col_c = jnp.bitwise_and(lane_c, SUB - 1)
    blk_c = jnp.right_shift(jnp.bitwise_and(lane_c, HEAD_DIM - 1), SUB_SHIFT)
    lane_blk_c = jnp.right_shift(lane_c, SUB_SHIFT)
    eye_c = (row_c == col_c).astype(F32)

    def bd_c(y):
        return jnp.concatenate([jnp.where(lane_blk_c == g, y, 0.0) for g in range(LANES // SUB)], axis=0)

    def inverse_size4(l_c, upper):
        r4 = jnp.bitwise_and(row_c, 3)
        c4 = jnp.bitwise_and(col_c, 3)
        down, up = (lambda x, k: pltpu.roll(x, k, 0)), (lambda x, k: pltpu.roll(x, SUB - k, 0))
        right, left = (lambda x, k: pltpu.roll(x, k, 1)), (lambda x, k: pltpu.roll(x, LANES - k, 1))
        s2 = jnp.where(_same_block(row_c, col_c, 2), l_c, 0.0)
        e4 = jnp.where(_same_block(row_c, col_c, 4), l_c, 0.0) - s2
        if upper:
            s_col = jnp.where(r4 == 1, up(s2, 1), up(s2, 2))
            s_row = jnp.where(c4 == 2, right(s2, 1), right(s2, 2))
            e_s = jnp.where(c4 == 3, right(e4, 1) * s_col, 0.0)
            s_e = jnp.where(r4 == 0, s_row * up(e4, 1), 0.0)
            s_e_s = jnp.where((r4 == 0) & (c4 == 3), s_row * up(e_s, 1), 0.0)
        else:
            s_col = jnp.where(r4 == 2, down(s2, 1), down(s2, 2))
            s_row = jnp.where(c4 == 1, left(s2, 1), left(s2, 2))
            e_s = jnp.where(c4 == 0, left(e4, 1) * s_col, 0.0)
            s_e = jnp.where(r4 == 3, s_row * down(e4, 1), 0.0)
            s_e_s = jnp.where((r4 == 3) & (c4 == 0), s_row * down(e_s, 1), 0.0)
        return eye_c - s2 - e4 + e_s + s_e - s_e_s

    row2 = lax.broadcasted_iota(jnp.int32, (LANES, LANES), 0)
    lane2 = lax.broadcasted_iota(jnp.int32, (LANES, LANES), 1)
    same_head = (row2 < HEAD_DIM) == (lane2 < HEAD_DIM)
    decay_scale = math.exp(-0.5)

    row0 = lambda o, d: (o if d == 0 else n_sub - 1 - o) * C
    keys = [(o, n, d) for o in range(n_sub) for n in range(n_seq) for d in range(N_DIR)]
    rk_all = {(o, n, d): rk_refs[d][n, row0(o, d):row0(o, d) + C, :].astype(F32) for o, n, d in keys}
    kkr_all = {key: rk_all[key][:, D_RWKV:] * kk_ref[...] for key in keys}
    ssq_rows = _bdot(jnp.concatenate([kkr_all[key] * kkr_all[key] for key in keys], axis=0), ones_ref[...])
    dir_keys = [[key for key in keys if key[2] == d] for d in range(N_DIR)]
    z_w_rows, z_a_rows = [], []
    for d in range(N_DIR):
        ll_rows = jnp.concatenate([lwla_refs[d][n, row0(o, d):row0(o, d) + C, :] for o, n, _ in dir_keys[d]], axis=0)
        z_w_rows.append(w0_ref[d:d + 1, :] + _bdot(jnp.tanh(ll_rows), w2_ref[d]))
        z_a_rows.append(a0_ref[d:d + 1, :] + _bdot(ll_rows, a2_ref[d]))

    sets = [[] for _ in range(n_sub)]
    for i, (o, n, d) in enumerate(keys):
        rk = rk_all[(o, n, d)]
        r = rk[:, :D_RWKV]
        k = rk[:, D_RWKV:]
        v = v_refs[d][n, row0(o, d):row0(o, d) + C, :]
        i_d = dir_keys[d].index((o, n, d))
        logw = -decay_scale * jax.nn.sigmoid(z_w_rows[d][i_d * C:(i_d + 1) * C])
        a = jax.nn.sigmoid(z_a_rows[d][i_d * C:(i_d + 1) * C])
        kd = k * (1.0 + (a - 1.0) * ka_ref[...])
        kkr = kkr_all[(o, n, d)]
        ssq = ssq_rows[i * C:(i + 1) * C]
        kk = kkr * lax.rsqrt(jnp.maximum(ssq, 1e-24))
        bvec = kk * a
        lw_hi, lw_lo = _split2(logw)
        tri = tri_ref[d]
        cum = (jnp.dot(tri, lw_hi, preferred_element_type=F32)
               + jnp.dot(tri, lw_lo, preferred_element_type=F32))
        cum_prev = cum - logw
        tot = cum[C - 1:C, :] if d == 0 else cum[0:1, :]
        kap_t = kk * jnp.exp(cum_prev)
        r_t = r * jnp.exp(cum)
        e_neg = jnp.exp(-cum)
        gam = jnp.exp(tot)
        b_t = bvec * e_neg
        k_t = kd * e_neg
        e_rem = gam * e_neg
        b_h = bvec * e_rem
        k_h = kd * e_rem
        if d == 0:
            strict, incl = row > s_idx, row >= s_idx
        else:
            strict, incl = row < s_idx, row <= s_idx

        for p in range(N_PAIRS):
            sl = slice(p * LANES, (p + 1) * LANES)
            dup_t = lambda x: jnp.where(same_head, jnp.concatenate([x[:, sl], x[:, sl]], axis=0).T, 0.0)
            sets[o].append(dict(
                n=n, d=d, p=p, sl=sl, rows=slice(row0(o, d), row0(o, d) + C), strict=strict, incl=incl,
                prev=None if o == 0 else sets[o - 1][len(sets[o])],
                lhs=jnp.concatenate([kap_t[:, sl], r_t[:, sl]], axis=0).astype(BF16),
                v=v[:, sl], w_lm=jnp.concatenate([dup_t(b_t), dup_t(k_t)], axis=1).astype(BF16),
                t2=jnp.concatenate([b_h[:, sl], k_h[:, sl]], axis=0).T.astype(BF16),
                gam_col=jnp.broadcast_to(gam[:, sl], (LANES, LANES)).T))

    bd = lambda y: _block_diag_rows(y, head0)

    def st_scores(probs):
        for q in probs:
            lm = jnp.dot(q["lhs"], q["w_lm"], preferred_element_type=F32)
            q["l_b"] = jnp.where(q["strict"], lm[:C, :LANES], 0.0)
            q["m_b"] = jnp.where(q["incl"], lm[C:, :LANES], 0.0)
            q["lm_k"] = jnp.concatenate([jnp.where(q["strict"], lm[:C, LANES:], 0.0),
                                         jnp.where(q["incl"], lm[C:, LANES:], 0.0)], axis=0)

    def st_values(probs):
        for q in probs:
            lmv = _bdot(q["lm_k"], bd(q["v"]))
            q["l_kv"], q["m_kv"] = lmv[:C], lmv[C:]
            l_b = q["l_b"]
            l_c = l_b[0:SUB]
            for jb in range(1, C // SUB):
                l_c = jnp.where(blk_c == jb, l_b[jb * SUB:(jb + 1) * SUB], l_c)
            q["l_c"] = l_c
            q["t_c"] = inverse_size4(l_c, q["d"] == 1)

    def st_compact_a(s):
        def run(probs):
            off_mask = _same_block(row_c, col_c, 2 * s) & ~_same_block(row_c, col_c, s)
            for q in probs:
                q["et"] = _bdot(jnp.where(off_mask, q["l_c"], 0.0), bd_c(q["t_c"]))
        return run

    def st_compact_b(last):
        def run(probs):
            for q in probs:
                q["t_c"] = q["t_c"] - _bdot(q["t_c"], bd_c(q["et"]))
                if last:
                    q["tinv"] = jnp.concatenate([jnp.where(blk_c == jb, q["t_c"], 0.0) for jb in range(C // SUB)],
                                                axis=0)
        return run

    def st_full_a(s):
        def run(probs):
            off_mask = _same_block(row, s_idx, 2 * s) & ~_same_block(row, s_idx, s)
            for q in probs:
                q["et"] = _bdot(jnp.where(off_mask, q["l_b"], 0.0), bd(q["tinv"]))
        return run

    def st_full_b(probs):
        for q in probs:
            q["tinv"] = q["tinv"] - _bdot(q["tinv"], bd(q["et"]))

    def st_state_read(probs):
        for q in probs:
            q["h"] = h_scr[q["n"], q["d"], q["p"]] if q["prev"] is None else q["prev"]["h_new"]
            q["kr_h"] = _bdot(q["lhs"], q["h"])

    def st_solve(probs):
        for q in probs:
            q["u_n"] = _bdot(q["tinv"], bd(q["kr_h"][:C] + q["l_kv"]))

    def st_output(probs):
        for q in probs:
            y = q["kr_h"][C:] + q["m_kv"] - _bdot(q["m_b"], bd(q["u_n"]))
            y_refs[q["d"]][q["n"], q["rows"], q["sl"]] = y.astype(BF16)

    def st_state_write(probs):
        for q in probs:
            upd = _bdot(q["t2"], jnp.concatenate([(-q["u_n"]).astype(BF16), q["v"]], axis=0))
            q["h_new"] = q["gam_col"] * q["h"] + jnp.where(same_head, upd, 0.0)

    stages = [st_scores, st_values] + ([st_state_read] if n_sub == 1 else [])
    s = 4
    while s < SUB:
        stages += [st_compact_a(s), st_compact_b(2 * s == SUB)]
        s *= 2
    while s < C:
        stages += [st_full_a(s), st_full_b]
        s *= 2
    stages += ([] if n_sub == 1 else [st_state_read]) + [st_solve, st_output, st_state_write]
    skew = 4
    for t in range(len(stages) + skew * (n_sub - 1)):
        for o in range(n_sub):
            if 0 <= t - skew * o < len(stages):
                stages[t - skew * o](sets[o])
    for q in sets[-1]:
        h_scr[q["n"], q["d"], q["p"]] = q["h_new"]

    if write_state:
        @pl.when(j == n_steps - 1)
        def _():
            for n, d, p in [(n, d, p) for n in range(n_seq) for d in range(N_DIR) for p in range(N_PAIRS)]:
                ht = h_scr[n, d, p].T
                sfin_refs[d][n, 0, PAIR * p] = ht[:HEAD_DIM, :HEAD_DIM]
                sfin_refs[d][n, 0, PAIR * p + 1] = ht[HEAD_DIM:, HEAD_DIM:]


def _scan_call(rk, v, lwla, wts, s0_bd, write_state):
    B, T, _ = rk.shape
    n_sub = min(SCAN_SUB, T // CHUNK)
    ns = min(max(1, SCAN_WORK // n_sub), B)
    rows = n_sub * CHUNK
    nc = T // rows
    assert B % ns == 0 and T % rows == 0
    zero_init = s0_bd is None
    fwd = lambda b, j: (b, j, 0)
    bwd = lambda b, j: (b, nc - 1 - j, 0)
    full = lambda *shape: pl.BlockSpec(shape, lambda b, j: (0,) * len(shape))
    tok = lambda w, m: pl.BlockSpec((ns, rows, w), m)
    in_specs = [tok(2 * D_RWKV, fwd), tok(D_RWKV, fwd), tok(LANES, fwd),
                tok(2 * D_RWKV, bwd), tok(D_RWKV, bwd), tok(LANES, bwd),
                full(N_DIR, D_RWKV), full(N_DIR, LANES, D_RWKV),
                full(N_DIR, D_RWKV), full(N_DIR, LANES, D_RWKV),
                full(1, D_RWKV), full(1, D_RWKV), full(N_DIR, CHUNK, CHUNK), full(D_RWKV, D_RWKV)]
    args = [rk, v, lwla, rk, v, lwla, wts["w0"], wts["w2"], wts["a0"], wts["a2"],
            wts["k_k"], wts["k_a"], wts["tri"], wts["ones_bd"]]
    state_block = (ns, N_DIR, N_PAIRS, LANES, LANES)
    if not zero_init:
        in_specs.append(pl.BlockSpec(state_block, lambda b, j: (b, 0, 0, 0, 0)))
        args.append(s0_bd)
    out_specs = [tok(D_RWKV, fwd), tok(D_RWKV, bwd)]
    out_shape = [jax.ShapeDtypeStruct((B, T, D_RWKV), BF16), jax.ShapeDtypeStruct((B, T, D_RWKV), BF16)]
    if write_state:
        final_block = (ns, 1, N_HEADS, HEAD_DIM, HEAD_DIM)
        out_specs += [pl.BlockSpec(final_block, lambda b, j: (b, 0, 0, 0, 0))] * N_DIR
        out_shape += [jax.ShapeDtypeStruct((B,) + final_block[1:], F32)] * N_DIR
    kern = functools.partial(_scan_kernel, zero_init=zero_init, write_state=write_state, n_steps=nc,
                             n_seq=ns, n_sub=n_sub)
    return pl.pallas_call(
        kern, grid=(B // ns, nc), in_specs=in_specs, out_specs=out_specs, out_shape=out_shape,
        scratch_shapes=[pltpu.VMEM(state_block, F32)],
        compiler_params=pltpu.CompilerParams(dimension_semantics=("arbitrary", "arbitrary"),
                                             vmem_limit_bytes=VMEM_LIMIT),
        name="scan",
    )(*args)


def _fnet_kernel(fin_ref, gf_ref, dft_ref, cs_ref, wbd_ref, b_ref, o_ref, g_scr):
    u = pl.program_id(1)
    nb, seq_len, _ = fin_ref.shape
    tu = o_ref.shape[1]

    @pl.when(u == 0)
    def _():
        step = min(seq_len, ROW_TILE)
        for n in range(nb):
            for t0 in range(0, seq_len, step):
                fc = _bdot(fin_ref[n, t0:t0 + step, :], cs_ref[...])
                g_scr[n, t0:t0 + step, :] = fc[:, :D_FNET].astype(BF16)
                g_scr[n, seq_len + t0:seq_len + t0 + step, :] = fc[:, D_FNET:].astype(BF16)

    f_re = jnp.concatenate([jnp.dot(dft_ref[...], g_scr[n], preferred_element_type=F32) for n in range(nb)],
                           axis=0)
    f_out = _bdot(f_re, wbd_ref[...]) + b_ref[...]
    gate = _silu(gf_ref[...].astype(F32)).reshape(nb * tu, D_FNET)
    o_ref[...] = (f_out * gate).astype(BF16).reshape(nb, tu, D_FNET)


def _fnet_call(fin, gf, dft_bf16, cs_bf16, wbd_bf16, b_fnet):
    B, T, _ = fin.shape
    tu = min(T, FNET_ROWS)
    nb = max(1, FNET_ROWS // T)
    assert B % nb == 0 and T % tu == 0
    return pl.pallas_call(
        _fnet_kernel,
        grid=(B // nb, T // tu),
        in_specs=[pl.BlockSpec((nb, T, D_FNET), lambda b, u: (b, 0, 0)),
                  pl.BlockSpec((nb, tu, D_FNET), lambda b, u: (b, u, 0)),
                  pl.BlockSpec((tu, 2 * T), lambda b, u: (u, 0)),
                  pl.BlockSpec((D_FNET, 2 * D_FNET), lambda b, u: (0, 0)),
                  pl.BlockSpec((D_FNET, D_FNET), lambda b, u: (0, 0)),
                  pl.BlockSpec((1, D_FNET), lambda b, u: (0, 0))],
        out_specs=pl.BlockSpec((nb, tu, D_FNET), lambda b, u: (b, u, 0)),
        out_shape=jax.ShapeDtypeStruct((B, T, D_FNET), BF16),
        scratch_shapes=[pltpu.VMEM((nb, 2 * T, D_FNET), BF16)],
        compiler_params=pltpu.CompilerParams(dimension_semantics=("arbitrary", "arbitrary"),
                                             vmem_limit_bytes=VMEM_LIMIT),
        name="fnet",
    )(fin, gf, dft_bf16, cs_bf16, wbd_bf16, b_fnet)


def _out_kernel(*refs, has_emb, final_norm):
    it = iter(refs)
    x_ref = next(it)
    emb_ref = next(it) if has_emb else None
    (mod_ref, yf_ref, yb_ref, rk_ref, v_ref, lwla_ref, grec_ref, fo_ref, a0_ref, a2_ref, ka_ref,
     rkw_ref, gng_ref, gnb_ref, avg_ref, ones_ref, wout_ref, fng_ref, o_ref) = (next(it) for _ in range(19))

    nb, tt, _ = x_ref.shape
    tm = nb * tt
    rows = lambda ref: ref[...].reshape(tm, ref.shape[-1])
    y = rows(yf_ref).astype(F32) + rows(yb_ref).astype(F32)
    mu = _bdot(y, avg_ref[...])
    dlt = y - mu
    var = _bdot(dlt * dlt, avg_ref[...])
    y_n = dlt * lax.rsqrt(var + GN_EPS) * gng_ref[...] + gnb_ref[...]
    rk = rows(rk_ref).astype(F32)
    r = rk[:, :D_RWKV]
    k = rk[:, D_RWKV:]
    ll = rows(lwla_ref)
    a_sum = (jax.nn.sigmoid(a0_ref[0:1, :] + _bdot(ll, a2_ref[0]))
             + jax.nn.sigmoid(a0_ref[1:2, :] + _bdot(ll, a2_ref[1])))
    k_sum = k * (2.0 + (a_sum - 2.0) * ka_ref[...])
    bonus = _bdot(r * k_sum * rkw_ref[...], ones_ref[...]) * rows(v_ref).astype(F32)
    rec_out = (y_n + bonus) * _silu(rows(grec_ref).astype(F32))
    mixed = jnp.concatenate([rec_out.astype(BF16), rows(fo_ref)], axis=-1)
    out = jnp.dot(mixed, wout_ref[...], preferred_element_type=F32)
    x = rows(x_ref)
    if has_emb:
        x = x + emb_ref[...]
    z = x + mod_ref[0, 2:3, :] * out
    if final_norm:
        ms = jnp.mean(z * z, axis=-1, keepdims=True)
        z = z * lax.rsqrt(ms + NORM_EPS) * fng_ref[...]
    o_ref[...] = z.reshape(nb, tt, D_MODEL)


def _out_call(x, emb, mod, y_f, y_b, rk, v, lwla, grec, fo, wts, w_out_bf16, final_norm_g, final_norm):
    B, T, _ = x.shape
    tt = min(T, ROW_TILE)
    nb = max(1, ROW_TILE // T)
    has_emb = emb is not None
    per_batch_mod = mod.shape[0] > 1
    assert B % nb == 0 and T % tt == 0 and (nb == 1 or not (has_emb or per_batch_mod))
    tok = lambda w: pl.BlockSpec((nb, tt, w), lambda b, i: (b, i, 0))
    full = lambda *shape: pl.BlockSpec(shape, lambda b, i: (0,) * len(shape))
    in_specs = [tok(D_MODEL)]
    args = [x]
    if has_emb:
        in_specs.append(pl.BlockSpec((tt, D_MODEL), lambda b, i: (i, 0)))
        args.append(emb)
    mod_map = (lambda b, i: (b, 0, 0)) if per_batch_mod else (lambda b, i: (0, 0, 0))
    in_specs += [pl.BlockSpec((1, 3, D_MODEL), mod_map), tok(D_RWKV), tok(D_RWKV), tok(2 * D_RWKV),
                 tok(D_RWKV), tok(LANES), tok(D_RWKV), tok(D_FNET),
                 full(N_DIR, D_RWKV), full(N_DIR, LANES, D_RWKV),
                 full(1, D_RWKV), full(1, D_RWKV), full(1, D_RWKV), full(1, D_RWKV),
                 full(D_RWKV, D_RWKV), full(D_RWKV, D_RWKV), full(D_MODEL, D_MODEL), full(1, D_MODEL)]
    args += [mod, y_f, y_b, rk, v, lwla, grec, fo, wts["a0"], wts["a2"], wts["k_a"],
             wts["r_k"], wts["gn_g"], wts["gn_b"], wts["avg_bd"], wts["ones_bd"], w_out_bf16, final_norm_g]
    return pl.pallas_call(
        functools.partial(_out_kernel, has_emb=has_emb, final_norm=final_norm),
        grid=(B // nb, T // tt), in_specs=in_specs,
        out_specs=tok(D_MODEL),
        out_shape=jax.ShapeDtypeStruct((B, T, D_MODEL), F32),
        compiler_params=pltpu.CompilerParams(dimension_semantics=("arbitrary", "arbitrary"),
                                             vmem_limit_bytes=VMEM_LIMIT),
        name="out_proj",
    )(*args)


def _dft_table(seq_len):
    idx = np.arange(seq_len, dtype=np.int64)
    ang = 2.0 * np.pi * ((idx[:, None] * idx[None, :]) % seq_len).astype(np.float64) / seq_len
    scale = 1.0 / math.sqrt(seq_len)
    return np.concatenate([np.cos(ang) * scale, -np.sin(ang) * scale], axis=1).astype(np.float32)


def _channel_dft_table():
    n = FNET_GROUP
    idx = np.arange(n, dtype=np.int64)
    ang = 2.0 * np.pi * ((idx[:, None] * idx[None, :]) % n).astype(np.float64) / n
    c = np.cos(ang) / math.sqrt(n)
    s = np.sin(ang) / math.sqrt(n)
    eye = np.eye(D_FNET // n)
    return np.concatenate([np.kron(eye, c), np.kron(eye, s)], axis=1).astype(np.float32)


def _sincos_2d(n_tokens):
    rows = n_tokens // GRID_W
    pos = np.arange(rows * GRID_W)
    row = (pos // GRID_W).astype(np.float32)
    col = (pos % GRID_W).astype(np.float32)
    quarter = D_MODEL // 4
    freq = np.exp(np.float32(-math.log(POS_BASE)) * np.arange(quarter, dtype=np.float32) / np.float32(quarter))
    ang_r = row[:, None] * freq
    ang_c = col[:, None] * freq
    return np.concatenate([np.sin(ang_r), np.cos(ang_r), np.sin(ang_c), np.cos(ang_c)], axis=-1).astype(np.float32)


def _head_block_matrix(value):
    blk = np.kron(np.eye(N_HEADS), np.ones((HEAD_DIM, HEAD_DIM))) * value
    return jnp.asarray(blk, dtype=BF16)


def _pad_lora(w, row_offset):
    rows = [jnp.pad(w[d], ((row_offset + d * LORA, LANES - row_offset - (d + 1) * LORA), (0, 0)))
            for d in range(N_DIR)]
    return jnp.stack(rows).astype(BF16)


def _layer_weights(l, w0, w2, a0, a2, k_k, k_a, r_k, gn_g, gn_b):
    w2_p = _pad_lora(w2[l], 0)
    a2_p = _pad_lora(a2[l], N_DIR * LORA)
    tri_f = np.tril(np.ones((CHUNK, CHUNK)))
    tri = jnp.asarray(np.stack([tri_f, tri_f.T]), dtype=BF16)
    return dict(w0=w0[l], w2=w2_p, a0=a0[l], a2=a2_p,
                k_k=k_k[l][None], k_a=k_a[l][None], r_k=r_k[l].reshape(1, D_RWKV),
                gn_g=gn_g[l].reshape(1, D_RWKV), gn_b=gn_b[l].reshape(1, D_RWKV), tri=tri,
                ones_bd=_head_block_matrix(1.0), avg_bd=_head_block_matrix(1.0 / HEAD_DIM))


def _state_to_block_diag(s_f, s_b):
    def one(s):
        h = jnp.swapaxes(s.astype(F32), -1, -2)
        b = h.shape[0]
        h = h.reshape(b, N_PAIRS, PAIR, HEAD_DIM, HEAD_DIM)
        z = jnp.zeros_like(h[:, :, 0])
        top = jnp.concatenate([h[:, :, 0], z], axis=-1)
        bot = jnp.concatenate([z, h[:, :, 1]], axis=-1)
        return jnp.concatenate([top, bot], axis=-2)
    return jnp.stack([one(s_f), one(s_b)], axis=1)


def kernel(x_prompt, x_sample, state_rwkv_fwd, state_rwkv_bwd, c, c_ctx, w_ada, b_ada, norm_g, w_in,
           mu_shift, w0, w2, a0, a2, k_k, k_a, r_k, gn_g, gn_b, w_fnet, b_fnet, w_out, final_norm_g):
    depth = w_in.shape[0]
    n_dec = c.shape[0]
    assert n_dec + 1 <= 8
    bp, tp, _ = x_prompt.shape
    bs, ts, _ = x_sample.shape
    cvec = jnp.concatenate([c_ctx[None], c, jnp.zeros((7 - n_dec, D_MODEL), F32)], axis=0)
    emb = jnp.asarray(_sincos_2d(ts)).astype(x_sample.dtype)
    cs_tab = jnp.asarray(_channel_dft_table()).astype(BF16)
    dft_p = jnp.asarray(_dft_table(tp)).astype(BF16)
    dft_s = jnp.asarray(_dft_table(ts)).astype(BF16)
    fng = final_norm_g[None]

    xp, xs = x_prompt, x_sample
    new_f, new_b = [], []
    for l in range(depth):
        mod = _mod_call(cvec, w_ada[l], b_ada[l][None]).reshape(8, 3, D_MODEL)
        mod_ctx, mod_lat = mod[0:1], mod[1:1 + n_dec]
        wts = _layer_weights(l, w0, w2, a0, a2, k_k, k_a, r_k, gn_g, gn_b)
        w_in_b = w_in[l].astype(BF16)
        w_out_b = w_out[l].astype(BF16)
        n_grp = w_fnet.shape[1]
        wbd = (w_fnet[l][:, :, None, :] * jnp.eye(n_grp, dtype=F32)[:, None, :, None]).reshape(
            D_FNET, D_FNET).astype(BF16)
        ng, mu, bf = norm_g[l][None], mu_shift[l][None], b_fnet[l][None]
        emb_l = emb if l == 0 else None

        rk, v, lwla, grec, fin, gf = _in_proj_call(xp, None, mod_ctx, ng, w_in_b, mu)
        y_f, y_b, s_f, s_b = _scan_call(rk, v, lwla, wts, None, True)
        fo = _fnet_call(fin, gf, dft_p, cs_tab, wbd, bf)
        last = l == depth - 1
        xp = _out_call(xp, None, mod_ctx, y_f, y_b, rk, v, lwla, grec, fo, wts, w_out_b, fng, last)
        new_f.append(s_f)
        new_b.append(s_b)

        rk, v, lwla, grec, fin, gf = _in_proj_call(xs, emb_l, mod_lat, ng, w_in_b, mu)
        s0 = _state_to_block_diag(state_rwkv_fwd[:, l], state_rwkv_bwd[:, l])
        y_f, y_b = _scan_call(rk, v, lwla, wts, s0, False)
        fo = _fnet_call(fin, gf, dft_s, cs_tab, wbd, bf)
        xs = _out_call(xs, emb_l, mod_lat, y_f, y_b, rk, v, lwla, grec, fo, wts, w_out_b, fng, last)
    return (xp, xs, jnp.concatenate(new_f, axis=1), jnp.concatenate(new_b, axis=1))
```

```python
import functools
import math

import numpy as np
import jax
import jax.numpy as jnp
from jax import lax
from jax.experimental import pallas as pl
from jax.experimental.pallas import tpu as pltpu

F32 = jnp.float32
BF16 = jnp.bfloat16

D_MODEL = 1024
GRID_W = 64
D_RWKV = 512
D_FNET = D_MODEL - D_RWKV
HEAD_DIM = 64
N_HEADS = D_RWKV // HEAD_DIM
FNET_GROUP = 64
LORA = 32
N_DIR = 2
D_SHIFT = 3 * D_RWKV + N_DIR * 2 * LORA
D_IN = D_SHIFT + D_RWKV + 2 * D_FNET
NORM_EPS = 1e-6
KK_EPS = 1e-12
GN_EPS = 64e-5
POS_BASE = 10000.0

LANES = 128
SUBLANES = 8
PAIR = LANES // HEAD_DIM
N_PAIRS = N_HEADS // PAIR
CHUNK = 64
SUB = 16
SUB_SHIFT = SUB.bit_length() - 1
SCAN_WORK = 8
SCAN_SUB = 4
ROW_TILE = 512
ROW_SPLIT = 4
FNET_ROWS = 1024
HALO = 8
VMEM_LIMIT = 56 * 1024 * 1024


def _silu(x):
    return x * jax.nn.sigmoid(x)


def _bdot(a, b):
    return jnp.dot(a.astype(BF16), b.astype(BF16), preferred_element_type=F32)


def _split2(x):
    hi = x.astype(BF16)
    lo = (x - hi.astype(F32)).astype(BF16)
    return hi, lo


def _mod_kernel(c_ref, w_ref, b_ref, o_ref):
    @pl.when(pl.program_id(0) == 0)
    def _():
        o_ref[...] = jnp.broadcast_to(b_ref[...], o_ref.shape)

    s_hi, s_lo = _split2(_silu(c_ref[...]))
    w = w_ref[...].astype(BF16)
    o_ref[...] += jnp.dot(s_hi, w, preferred_element_type=F32) + jnp.dot(s_lo, w, preferred_element_type=F32)


def _mod_call(cvec, w_ada, b_ada):
    rows = D_MODEL // 4
    return pl.pallas_call(
        _mod_kernel,
        grid=(D_MODEL // rows,),
        in_specs=[pl.BlockSpec((SUBLANES, rows), lambda i: (0, i)),
                  pl.BlockSpec((rows, 3 * D_MODEL), lambda i: (i, 0)),
                  pl.BlockSpec((1, 3 * D_MODEL), lambda i: (0, 0))],
        out_specs=pl.BlockSpec((SUBLANES, 3 * D_MODEL), lambda i: (0, 0)),
        out_shape=jax.ShapeDtypeStruct((SUBLANES, 3 * D_MODEL), F32),
        compiler_params=pltpu.CompilerParams(dimension_semantics=("arbitrary",),
                                             vmem_limit_bytes=VMEM_LIMIT),
        name="mod",
    )(cvec, w_ada, b_ada)


def _modulated_norm(x, g, scale, shift):
    ms = jnp.mean(x * x, axis=-1, keepdims=True)
    y = x * lax.rsqrt(ms + NORM_EPS) * g
    return y * (1.0 + scale) + shift


def _in_proj_kernel(*refs, has_emb, has_halo, n_tiles):
    it = iter(refs)
    x_ref = next(it)
    xp_ref = next(it) if has_halo else None
    xn_ref = next(it) if has_halo else None
    emb_ref = next(it) if has_emb else None
    embp_ref = next(it) if (has_emb and has_halo) else None
    embn_ref = next(it) if (has_emb and has_halo) else None
    mod_ref, g_ref, w_ref, mu_ref = next(it), next(it), next(it), next(it)
    rk_ref, v_ref, lwla_ref, grec_ref, fin_ref, gf_ref = (next(it) for _ in range(6))

    i = pl.program_id(1)
    g = g_ref[...]
    shift = mod_ref[0, 0:1, :]
    scale = mod_ref[0, 1:2, :]
    nb, tt, _ = x_ref.shape
    tm = nb * tt
    x = x_ref[...].reshape(tm, D_MODEL)
    if has_emb:
        x = x + emb_ref[...]
    part = tm // ROW_SPLIT
    p = jnp.concatenate(
        [jnp.dot(_modulated_norm(x[r0:r0 + part], g, scale, shift).astype(BF16), w_ref[...],
                 preferred_element_type=F32) for r0 in range(0, tm, part)], axis=0)
    ps = p[:, :D_SHIFT]
    if has_halo:
        xh = jnp.concatenate([xp_ref[0], xn_ref[0]], axis=0)
        if has_emb:
            xh = xh + jnp.concatenate([embp_ref[...], embn_ref[...]], axis=0)
        hh = _modulated_norm(xh, g, scale, shift)
        ph = jnp.dot(hh.astype(BF16), w_ref[:, :D_SHIFT], preferred_element_type=F32)
        prev_row = jnp.where(i > 0, ph[HALO - 1:HALO, :], 0.0)
        next_row = jnp.where(i < n_tiles - 1, ph[HALO:HALO + 1, :], 0.0)
    else:
        prev_row = jnp.zeros((1, D_SHIFT), F32)
        next_row = jnp.zeros((1, D_SHIFT), F32)
    pos = lax.rem(lax.broadcasted_iota(jnp.int32, (tm, D_SHIFT), 0), tt)
    prev = jnp.where(pos == 0, prev_row, pltpu.roll(ps, 1, 0))
    nxt = jnp.where(pos == tt - 1, next_row, pltpu.roll(ps, tm - 1, 0))
    p_rec = ps + mu_ref[...] * (0.5 * (prev + nxt) - ps)

    def put(ref, val):
        ref[...] = val.astype(ref.dtype).reshape(ref.shape)

    put(rk_ref, p_rec[:, :2 * D_RWKV])
    put(v_ref, p_rec[:, 2 * D_RWKV:3 * D_RWKV])
    put(lwla_ref, p_rec[:, 3 * D_RWKV:])
    put(grec_ref, p[:, D_SHIFT:D_SHIFT + D_RWKV])
    put(fin_ref, p[:, D_SHIFT + D_RWKV:D_SHIFT + D_RWKV + D_FNET])
    put(gf_ref, p[:, D_SHIFT + D_RWKV + D_FNET:])


def _in_proj_call(x, emb, mod, norm_g, w_in_bf16, mu):
    B, T, _ = x.shape
    tt = min(T, ROW_TILE)
    nb = max(1, ROW_TILE // T)
    n_tiles = T // tt
    has_halo = n_tiles > 1
    has_emb = emb is not None
    per_batch_mod = mod.shape[0] > 1
    assert B % nb == 0 and T % tt == 0 and (nb == 1 or not (has_emb or per_batch_mod))
    tm = tt
    blocks_per_tile = tm // HALO
    last_halo_block = T // HALO - 1

    in_specs = [pl.BlockSpec((nb, tt, D_MODEL), lambda b, i: (b, i, 0))]
    args = [x]
    if has_halo:
        in_specs += [
            pl.BlockSpec((1, HALO, D_MODEL), lambda b, i: (b, jnp.maximum(i * blocks_per_tile - 1, 0), 0)),
            pl.BlockSpec((1, HALO, D_MODEL),
                         lambda b, i: (b, jnp.minimum((i + 1) * blocks_per_tile, last_halo_block), 0))]
        args += [x, x]
    if has_emb:
        in_specs.append(pl.BlockSpec((tm, D_MODEL), lambda b, i: (i, 0)))
        args.append(emb)
        if has_halo:
            in_specs += [
                pl.BlockSpec((HALO, D_MODEL), lambda b, i: (jnp.maximum(i * blocks_per_tile - 1, 0), 0)),
                pl.BlockSpec((HALO, D_MODEL),
                             lambda b, i: (jnp.minimum((i + 1) * blocks_per_tile, last_halo_block), 0))]
            args += [emb, emb]
    mod_map = (lambda b, i: (b, 0, 0)) if per_batch_mod else (lambda b, i: (0, 0, 0))
    in_specs += [pl.BlockSpec((1, 3, D_MODEL), mod_map),
                 pl.BlockSpec((1, D_MODEL), lambda b, i: (0, 0)),
                 pl.BlockSpec((D_MODEL, D_IN), lambda b, i: (0, 0), pipeline_mode=pl.Buffered(1)),
                 pl.BlockSpec((1, D_SHIFT), lambda b, i: (0, 0))]
    args += [mod, norm_g, w_in_bf16, mu]
    outs = ((2 * D_RWKV, BF16), (D_RWKV, BF16), (N_DIR * 2 * LORA, F32), (D_RWKV, BF16), (D_FNET, BF16),
            (D_FNET, BF16))
    out_specs = [pl.BlockSpec((nb, tt, w), lambda b, i: (b, i, 0)) for w, _ in outs]
    out_shape = [jax.ShapeDtypeStruct((B, T, w), dt) for w, dt in outs]
    kern = functools.partial(_in_proj_kernel, has_emb=has_emb, has_halo=has_halo, n_tiles=n_tiles)
    return pl.pallas_call(
        kern, grid=(B // nb, n_tiles), in_specs=in_specs, out_specs=out_specs, out_shape=out_shape,
        compiler_params=pltpu.CompilerParams(dimension_semantics=("arbitrary", "arbitrary"),
                                             vmem_limit_bytes=VMEM_LIMIT),
        name="in_proj",
    )(*args)


def _same_block(i, j, size):
    shift = size.bit_length() - 1
    return jnp.right_shift(i, shift) == jnp.right_shift(j, shift)


def _block_diag_rows(y, head0_lanes):
    return jnp.concatenate([jnp.where(head0_lanes, y, 0.0), jnp.where(head0_lanes, 0.0, y)], axis=0)


def _scan_kernel(*refs, zero_init, write_state, n_steps, n_seq, n_sub):
    it = iter(refs)
    rk_refs, v_refs, lwla_refs = [None, None], [None, None], [None, None]
    for d in range(N_DIR):
        rk_refs[d], v_refs[d], lwla_refs[d] = next(it), next(it), next(it)
    w0_ref, w2_ref, a0_ref, a2_ref = (next(it) for _ in range(4))
    kk_ref, ka_ref, tri_ref, ones_ref = (next(it) for _ in range(4))
    s0_ref = None if zero_init else next(it)
    y_refs = [next(it), next(it)]
    sfin_refs = [next(it), next(it)] if write_state else None
    h_scr = next(it)

    j = pl.program_id(1)

    @pl.when(j == 0)
    def _():
        if zero_init:
            h_scr[...] = jnp.zeros_like(h_scr)
        else:
            h_scr[...] = s0_ref[...]

    C = CHUNK
    row = lax.broadcasted_iota(jnp.int32, (C, LANES), 0)
    lane = lax.broadcasted_iota(jnp.int32, (C, LANES), 1)
    s_idx = jnp.bitwise_and(lane, HEAD_DIM - 1)
    head0 = lane < HEAD_DIM
    row_c = lax.broadcasted_iota(jnp.int32, (SUB, LANES), 0)
    lane_c = lax.broadcasted_iota(jnp.int32, (SUB, LANES), 1)
    col_c = jnp.bitwise_and(lane_c, SUB - 1)
    blk_c = jnp.right_shift(jnp.bitwise_and(lane_c, HEAD_DIM - 1), SUB_SHIFT)
    lane_blk_c = jnp.right_shift(lane_c, SUB_SHIFT)
    eye_c = (row_c == col_c).astype(F32)

    def bd_c(y):
        return jnp.concatenate([jnp.where(lane_blk_c == g, y, 0.0) for g in range(LANES // SUB)], axis=0)

    def inverse_size4(l_c, upper):
        r4 = jnp.bitwise_and(row_c, 3)
        c4 = jnp.bitwise_and(col_c, 3)
        down, up = (lambda x, k: pltpu.roll(x, k, 0)), (lambda x, k: pltpu.roll(x, SUB - k, 0))
        right, left = (lambda x, k: pltpu.roll(x, k, 1)), (lambda x, k: pltpu.roll(x, LANES - k, 1))
        s2 = jnp.where(_same_block(row_c, col_c, 2), l_c, 0.0)
        e4 = jnp.where(_same_block(row_c, col_c, 4), l_c, 0.0) - s2
        if upper:
            s_col = jnp.where(r4 == 1, up(s2, 1), up(s2, 2))
            s_row = jnp.where(c4 == 2, right(s2, 1), right(s2, 2))
            e_s = jnp.where(c4 == 3, right(e4, 1) * s_col, 0.0)
            s_e = jnp.where(r4 == 0, s_row * up(e4, 1), 0.0)
            s_e_s = jnp.where((r4 == 0) & (c4 == 3), s_row * up(e_s, 1), 0.0)
        else:
            s_col = jnp.where(r4 == 2, down(s2, 1), down(s2, 2))
            s_row = jnp.where(c4 == 1, left(s2, 1), left(s2, 2))
            e_s = jnp.where(c4 == 0, left(e4, 1) * s_col, 0.0)
            s_e = jnp.where(r4 == 3, s_row * down(e4, 1), 0.0)
            s_e_s = jnp.where((r4 == 3) & (c4 == 0), s_row * down(e_s, 1), 0.0)
        return eye_c - s2 - e4 + e_s + s_e - s_e_s

    row2 = lax.broadcasted_iota(jnp.int32, (LANES, LANES), 0)
    lane2 = lax.broadcasted_iota(jnp.int32, (LANES, LANES), 1)
    same_head = (row2 < HEAD_DIM) == (lane2 < HEAD_DIM)
    decay_scale = math.exp(-0.5)

    row0 = lambda o, d: (o if d == 0 else n_sub - 1 - o) * C
    keys = [(o, n, d) for o in range(n_sub) for n in range(n_seq) for d in range(N_DIR)]
    rk_all = {(o, n, d): rk_refs[d][n, row0(o, d):row0(o, d) + C, :].astype(F32) for o, n, d in keys}
    kkr_all = {key: rk_all[key][:, D_RWKV:] * kk_ref[...] for key in keys}
    ssq_rows = _bdot(jnp.concatenate([kkr_all[key] * kkr_all[key] for key in keys], axis=0), ones_ref[...])
    dir_keys = [[key for key in keys if key[2] == d] for d in range(N_DIR)]
    z_w_rows, z_a_rows = [], []
    for d in range(N_DIR):
        ll_rows = jnp.concatenate([lwla_refs[d][n, row0(o, d):row0(o, d) + C, :] for o, n, _ in dir_keys[d]], axis=0)
        z_w_rows.append(w0_ref[d:d + 1, :] + _bdot(jnp.tanh(ll_rows), w2_ref[d]))
        z_a_rows.append(a0_ref[d:d + 1, :] + _bdot(ll_rows, a2_ref[d]))

    sets = [[] for _ in range(n_sub)]
    for i, (o, n, d) in enumerate(keys):
        rk = rk_all[(o, n, d)]
        r = rk[:, :D_RWKV]
        k = rk[:, D_RWKV:]
        v = v_refs[d][n, row0(o, d):row0(o, d) + C, :]
        i_d = dir_keys[d].index((o, n, d))
        logw = -decay_scale * jax.nn.sigmoid(z_w_rows[d][i_d * C:(i_d + 1) * C])
        a = jax.nn.sigmoid(z_a_rows[d][i_d * C:(i_d + 1) * C])
        kd = k * (1.0 + (a - 1.0) * ka_ref[...])
        kkr = kkr_all[(o, n, d)]
        ssq = ssq_rows[i * C:(i + 1) * C]
        kk = kkr * lax.rsqrt(jnp.maximum(ssq, KK_EPS * KK_EPS))
        bvec = kk * a
        lw_hi, lw_lo = _split2(logw)
        tri = tri_ref[d]
        cum = (jnp.dot(tri, lw_hi, preferred_element_type=F32)
               + jnp.dot(tri, lw_lo, preferred_element_type=F32))
        cum_prev = cum - logw
        tot = cum[C - 1:C, :] if d == 0 else cum[0:1, :]
        kap_t = kk * jnp.exp(cum_prev)
        r_t = r * jnp.exp(cum)
        e_neg = jnp.exp(-cum)
        gam = jnp.exp(tot)
        b_t = bvec * e_neg
        k_t = kd * e_neg
        e_rem = gam * e_neg
        b_h = bvec * e_rem
        k_h = kd * e_rem
        if d == 0:
            strict, incl = row > s_idx, row >= s_idx
        else:
            strict, incl = row < s_idx, row <= s_idx

        for p in range(N_PAIRS):
            sl = slice(p * LANES, (p + 1) * LANES)
            dup_t = lambda x: jnp.where(same_head, jnp.concatenate([x[:, sl], x[:, sl]], axis=0).T, 0.0)
            sets[o].append(dict(
                n=n, d=d, p=p, sl=sl, rows=slice(row0(o, d), row0(o, d) + C), strict=strict, incl=incl,
                prev=None if o == 0 else sets[o - 1][len(sets[o])],
                lhs=jnp.concatenate([kap_t[:, sl], r_t[:, sl]], axis=0).astype(BF16),
                v=v[:, sl], w_lm=jnp.concatenate([dup_t(b_t), dup_t(k_t)], axis=1).astype(BF16),
                t2=jnp.concatenate([b_h[:, sl], k_h[:, sl]], axis=0).T.astype(BF16),
                gam_col=jnp.broadcast_to(gam[:, sl], (LANES, LANES)).T))

    bd = lambda y: _block_diag_rows(y, head0)

    def st_scores(probs):
        for q in probs:
            lm = jnp.dot(q["lhs"], q["w_lm"], preferred_element_type=F32)
            q["l_b"] = jnp.where(q["strict"], lm[:C, :LANES], 0.0)
            q["m_b"] = jnp.where(q["incl"], lm[C:, :LANES], 0.0)
            q["lm_k"] = jnp.concatenate([jnp.where(q["strict"], lm[:C, LANES:], 0.0),
                                         jnp.where(q["incl"], lm[C:, LANES:], 0.0)], axis=0)

    def st_values(probs):
        for q in probs:
            lmv = _bdot(q["lm_k"], bd(q["v"]))
            q["l_kv"], q["m_kv"] = lmv[:C], lmv[C:]
            l_b = q["l_b"]
            l_c = l_b[0:SUB]
            for jb in range(1, C // SUB):
                l_c = jnp.where(blk_c == jb, l_b[jb * SUB:(jb + 1) * SUB], l_c)
            q["l_c"] = l_c
            q["t_c"] = inverse_size4(l_c, q["d"] == 1)

    def st_compact_a(s):
        def run(probs):
            off_mask = _same_block(row_c, col_c, 2 * s) & ~_same_block(row_c, col_c, s)
            for q in probs:
                q["et"] = _bdot(jnp.where(off_mask, q["l_c"], 0.0), bd_c(q["t_c"]))
        return run

    def st_compact_b(last):
        def run(probs):
            for q in probs:
                q["t_c"] = q["t_c"] - _bdot(q["t_c"], bd_c(q["et"]))
                if last:
                    q["tinv"] = jnp.concatenate([jnp.where(blk_c == jb, q["t_c"], 0.0) for jb in range(C // SUB)],
                                                axis=0)
        return run

    def st_full_a(s):
        def run(probs):
            off_mask = _same_block(row, s_idx, 2 * s) & ~_same_block(row, s_idx, s)
            for q in probs:
                q["et"] = _bdot(jnp.where(off_mask, q["l_b"], 0.0), bd(q["tinv"]))
        return run

    def st_full_b(probs):
        for q in probs:
            q["tinv"] = q["tinv"] - _bdot(q["tinv"], bd(q["et"]))

    def st_state_read(probs):
        for q in probs:
            q["h"] = h_scr[q["n"], q["d"], q["p"]] if q["prev"] is None else q["prev"]["h_new"]
            q["kr_h"] = _bdot(q["lhs"], q["h"])

    def st_solve(probs):
        for q in probs:
            q["u_n"] = _bdot(q["tinv"], bd(q["kr_h"][:C] + q["l_kv"]))

    def st_output(probs):
        for q in probs:
            y = q["kr_h"][C:] + q["m_kv"] - _bdot(q["m_b"], bd(q["u_n"]))
            y_refs[q["d"]][q["n"], q["rows"], q["sl"]] = y.astype(BF16)

    def st_state_write(probs):
        for q in probs:
            upd = _bdot(q["t2"], jnp.concatenate([(-q["u_n"]).astype(BF16), q["v"]], axis=0))
            q["h_new"] = q["gam_col"] * q["h"] + jnp.where(same_head, upd, 0.0)

    stages = [st_scores, st_values] + ([st_state_read] if n_sub == 1 else [])
    s = 4
    while s < SUB:
        stages += [st_compact_a(s), st_compact_b(2 * s == SUB)]
        s *= 2
    while s < C:
        stages += [st_full_a(s), st_full_b]
        s *= 2
    tail = [st_state_read, st_solve, st_output, st_state_write]
    stages += tail[1:] if n_sub == 1 else tail
    skew = len(tail)
    for t in range(len(stages) + skew * (n_sub - 1)):
        for o in range(n_sub):
            if 0 <= t - skew * o < len(stages):
                stages[t - skew * o](sets[o])
    for q in sets[-1]:
        h_scr[q["n"], q["d"], q["p"]] = q["h_new"]

    if write_state:
        @pl.when(j == n_steps - 1)
        def _():
            for n, d, p in [(n, d, p) for n in range(n_seq) for d in range(N_DIR) for p in range(N_PAIRS)]:
                ht = h_scr[n, d, p].T
                sfin_refs[d][n, 0, PAIR * p] = ht[:HEAD_DIM, :HEAD_DIM]
                sfin_refs[d][n, 0, PAIR * p + 1] = ht[HEAD_DIM:, HEAD_DIM:]


def _scan_call(rk, v, lwla, wts, s0_bd, write_state):
    B, T, _ = rk.shape
    n_sub = min(SCAN_SUB, T // CHUNK)
    ns = min(max(1, SCAN_WORK // n_sub), B)
    rows = n_sub * CHUNK
    nc = T // rows
    assert B % ns == 0 and T % rows == 0
    zero_init = s0_bd is None
    fwd = lambda b, j: (b, j, 0)
    bwd = lambda b, j: (b, nc - 1 - j, 0)
    full = lambda *shape: pl.BlockSpec(shape, lambda b, j: (0,) * len(shape))
    tok = lambda w, m: pl.BlockSpec((ns, rows, w), m)
    in_specs = [tok(2 * D_RWKV, fwd), tok(D_RWKV, fwd), tok(LANES, fwd),
                tok(2 * D_RWKV, bwd), tok(D_RWKV, bwd), tok(LANES, bwd),
                full(N_DIR, D_RWKV), full(N_DIR, LANES, D_RWKV),
                full(N_DIR, D_RWKV), full(N_DIR, LANES, D_RWKV),
                full(1, D_RWKV), full(1, D_RWKV), full(N_DIR, CHUNK, CHUNK), full(D_RWKV, D_RWKV)]
    args = [rk, v, lwla, rk, v, lwla, wts["w0"], wts["w2"], wts["a0"], wts["a2"],
            wts["k_k"], wts["k_a"], wts["tri"], wts["ones_bd"]]
    state_block = (ns, N_DIR, N_PAIRS, LANES, LANES)
    if not zero_init:
        in_specs.append(pl.BlockSpec(state_block, lambda b, j: (b, 0, 0, 0, 0)))
        args.append(s0_bd)
    out_specs = [tok(D_RWKV, fwd), tok(D_RWKV, bwd)]
    out_shape = [jax.ShapeDtypeStruct((B, T, D_RWKV), BF16), jax.ShapeDtypeStruct((B, T, D_RWKV), BF16)]
    if write_state:
        final_block = (ns, 1, N_HEADS, HEAD_DIM, HEAD_DIM)
        out_specs += [pl.BlockSpec(final_block, lambda b, j: (b, 0, 0, 0, 0))] * N_DIR
        out_shape += [jax.ShapeDtypeStruct((B,) + final_block[1:], F32)] * N_DIR
    kern = functools.partial(_scan_kernel, zero_init=zero_init, write_state=write_state, n_steps=nc,
                             n_seq=ns, n_sub=n_sub)
    return pl.pallas_call(
        kern, grid=(B // ns, nc), in_specs=in_specs, out_specs=out_specs, out_shape=out_shape,
        scratch_shapes=[pltpu.VMEM(state_block, F32)],
        compiler_params=pltpu.CompilerParams(dimension_semantics=("arbitrary", "arbitrary"),
                                             vmem_limit_bytes=VMEM_LIMIT),
        name="scan",
    )(*args)


def _fnet_kernel(fin_ref, gf_ref, dft_ref, cs_ref, wbd_ref, b_ref, o_ref, g_scr):
    u = pl.program_id(1)
    nb, seq_len, _ = fin_ref.shape
    tu = o_ref.shape[1]

    @pl.when(u == 0)
    def _():
        step = min(seq_len, ROW_TILE)
        for n in range(nb):
            for t0 in range(0, seq_len, step):
                fc = _bdot(fin_ref[n, t0:t0 + step, :], cs_ref[...])
                g_scr[n, t0:t0 + step, :] = fc[:, :D_FNET].astype(BF16)
                g_scr[n, seq_len + t0:seq_len + t0 + step, :] = fc[:, D_FNET:].astype(BF16)

    f_re = jnp.concatenate([jnp.dot(dft_ref[...], g_scr[n], preferred_element_type=F32) for n in range(nb)],
                           axis=0)
    f_out = _bdot(f_re, wbd_ref[...]) + b_ref[...]
    gate = _silu(gf_ref[...].astype(F32)).reshape(nb * tu, D_FNET)
    o_ref[...] = (f_out * gate).astype(BF16).reshape(nb, tu, D_FNET)


def _fnet_call(fin, gf, dft_bf16, cs_bf16, wbd_bf16, b_fnet):
    B, T, _ = fin.shape
    tu = min(T, FNET_ROWS)
    nb = max(1, FNET_ROWS // T)
    assert B % nb == 0 and T % tu == 0
    return pl.pallas_call(
        _fnet_kernel,
        grid=(B // nb, T // tu),
        in_specs=[pl.BlockSpec((nb, T, D_FNET), lambda b, u: (b, 0, 0)),
                  pl.BlockSpec((nb, tu, D_FNET), lambda b, u: (b, u, 0)),
                  pl.BlockSpec((tu, 2 * T), lambda b, u: (u, 0)),
                  pl.BlockSpec((D_FNET, 2 * D_FNET), lambda b, u: (0, 0)),
                  pl.BlockSpec((D_FNET, D_FNET), lambda b, u: (0, 0)),
                  pl.BlockSpec((1, D_FNET), lambda b, u: (0, 0))],
        out_specs=pl.BlockSpec((nb, tu, D_FNET), lambda b, u: (b, u, 0)),
        out_shape=jax.ShapeDtypeStruct((B, T, D_FNET), BF16),
        scratch_shapes=[pltpu.VMEM((nb, 2 * T, D_FNET), BF16)],
        compiler_params=pltpu.CompilerParams(dimension_semantics=("arbitrary", "arbitrary"),
                                             vmem_limit_bytes=VMEM_LIMIT),
        name="fnet",
    )(fin, gf, dft_bf16, cs_bf16, wbd_bf16, b_fnet)


def _out_kernel(*refs, has_emb, final_norm):
    it = iter(refs)
    x_ref = next(it)
    emb_ref = next(it) if has_emb else None
    (mod_ref, yf_ref, yb_ref, rk_ref, v_ref, lwla_ref, grec_ref, fo_ref, a0_ref, a2_ref, ka_ref,
     rkw_ref, gng_ref, gnb_ref, avg_ref, ones_ref, wout_ref, fng_ref, o_ref) = (next(it) for _ in range(19))

    nb, tt, _ = x_ref.shape
    tm = nb * tt
    rows = lambda ref: ref[...].reshape(tm, ref.shape[-1])
    y = rows(yf_ref).astype(F32) + rows(yb_ref).astype(F32)
    mu = _bdot(y, avg_ref[...])
    dlt = y - mu
    var = _bdot(dlt * dlt, avg_ref[...])
    y_n = dlt * lax.rsqrt(var + GN_EPS) * gng_ref[...] + gnb_ref[...]
    rk = rows(rk_ref).astype(F32)
    r = rk[:, :D_RWKV]
    k = rk[:, D_RWKV:]
    ll = rows(lwla_ref)
    a_sum = (jax.nn.sigmoid(a0_ref[0:1, :] + _bdot(ll, a2_ref[0]))
             + jax.nn.sigmoid(a0_ref[1:2, :] + _bdot(ll, a2_ref[1])))
    k_sum = k * (2.0 + (a_sum - 2.0) * ka_ref[...])
    bonus = _bdot(r * k_sum * rkw_ref[...], ones_ref[...]) * rows(v_ref).astype(F32)
    rec_out = (y_n + bonus) * _silu(rows(grec_ref).astype(F32))
    mixed = jnp.concatenate([rec_out.astype(BF16), rows(fo_ref)], axis=-1)
    out = jnp.dot(mixed, wout_ref[...], preferred_element_type=F32)
    x = rows(x_ref)
    if has_emb:
        x = x + emb_ref[...]
    z = x + mod_ref[0, 2:3, :] * out
    if final_norm:
        ms = jnp.mean(z * z, axis=-1, keepdims=True)
        z = z * lax.rsqrt(ms + NORM_EPS) * fng_ref[...]
    o_ref[...] = z.reshape(nb, tt, D_MODEL)


def _out_call(x, emb, mod, y_f, y_b, rk, v, lwla, grec, fo, wts, w_out_bf16, final_norm_g, final_norm):
    B, T, _ = x.shape
    tt = min(T, ROW_TILE)
    nb = max(1, ROW_TILE // T)
    has_emb = emb is not None
    per_batch_mod = mod.shape[0] > 1
    assert B % nb == 0 and T % tt == 0 and (nb == 1 or not (has_emb or per_batch_mod))
    tok = lambda w: pl.BlockSpec((nb, tt, w), lambda b, i: (b, i, 0))
    full = lambda *shape: pl.BlockSpec(shape, lambda b, i: (0,) * len(shape))
    in_specs = [tok(D_MODEL)]
    args = [x]
    if has_emb:
        in_specs.append(pl.BlockSpec((tt, D_MODEL), lambda b, i: (i, 0)))
        args.append(emb)
    mod_map = (lambda b, i: (b, 0, 0)) if per_batch_mod else (lambda b, i: (0, 0, 0))
    in_specs += [pl.BlockSpec((1, 3, D_MODEL), mod_map), tok(D_RWKV), tok(D_RWKV), tok(2 * D_RWKV),
                 tok(D_RWKV), tok(LANES), tok(D_RWKV), tok(D_FNET),
                 full(N_DIR, D_RWKV), full(N_DIR, LANES, D_RWKV),
                 full(1, D_RWKV), full(1, D_RWKV), full(1, D_RWKV), full(1, D_RWKV),
                 full(D_RWKV, D_RWKV), full(D_RWKV, D_RWKV), full(D_MODEL, D_MODEL), full(1, D_MODEL)]
    args += [mod, y_f, y_b, rk, v, lwla, grec, fo, wts["a0"], wts["a2"], wts["k_a"],
             wts["r_k"], wts["gn_g"], wts["gn_b"], wts["avg_bd"], wts["ones_bd"], w_out_bf16, final_norm_g]
    return pl.pallas_call(
        functools.partial(_out_kernel, has_emb=has_emb, final_norm=final_norm),
        grid=(B // nb, T // tt), in_specs=in_specs,
        out_specs=tok(D_MODEL),
        out_shape=jax.ShapeDtypeStruct((B, T, D_MODEL), F32),
        compiler_params=pltpu.CompilerParams(dimension_semantics=("arbitrary", "arbitrary"),
                                             vmem_limit_bytes=VMEM_LIMIT),
        name="out_proj",
    )(*args)


def _dft_table(seq_len):
    idx = np.arange(seq_len, dtype=np.int64)
    ang = 2.0 * np.pi * ((idx[:, None] * idx[None, :]) % seq_len).astype(np.float64) / seq_len
    scale = 1.0 / math.sqrt(seq_len)
    return np.concatenate([np.cos(ang) * scale, -np.sin(ang) * scale], axis=1).astype(np.float32)


def _channel_dft_table():
    n = FNET_GROUP
    idx = np.arange(n, dtype=np.int64)
    ang = 2.0 * np.pi * ((idx[:, None] * idx[None, :]) % n).astype(np.float64) / n
    c = np.cos(ang) / math.sqrt(n)
    s = np.sin(ang) / math.sqrt(n)
    eye = np.eye(D_FNET // n)
    return np.concatenate([np.kron(eye, c), np.kron(eye, s)], axis=1).astype(np.float32)


def _sincos_2d(n_tokens):
    rows = n_tokens // GRID_W
    pos = np.arange(rows * GRID_W)
    row = (pos // GRID_W).astype(np.float32)
    col = (pos % GRID_W).astype(np.float32)
    quarter = D_MODEL // 4
    freq = np.exp(np.float32(-math.log(POS_BASE)) * np.arange(quarter, dtype=np.float32) / np.float32(quarter))
    ang_r = row[:, None] * freq
    ang_c = col[:, None] * freq
    return np.concatenate([np.sin(ang_r), np.cos(ang_r), np.sin(ang_c), np.cos(ang_c)], axis=-1).astype(np.float32)


def _head_block_matrix(value):
    blk = np.kron(np.eye(N_HEADS), np.ones((HEAD_DIM, HEAD_DIM))) * value
    return jnp.asarray(blk, dtype=BF16)


def _pad_lora(w, row_offset):
    rows = [jnp.pad(w[d], ((row_offset + d * LORA, LANES - row_offset - (d + 1) * LORA), (0, 0)))
            for d in range(N_DIR)]
    return jnp.stack(rows).astype(BF16)


def _layer_weights(l, w0, w2, a0, a2, k_k, k_a, r_k, gn_g, gn_b):
    w2_p = _pad_lora(w2[l], 0)
    a2_p = _pad_lora(a2[l], N_DIR * LORA)
    tri_f = np.tril(np.ones((CHUNK, CHUNK)))
    tri = jnp.asarray(np.stack([tri_f, tri_f.T]), dtype=BF16)
    return dict(w0=w0[l], w2=w2_p, a0=a0[l], a2=a2_p,
                k_k=k_k[l][None], k_a=k_a[l][None], r_k=r_k[l].reshape(1, D_RWKV),
                gn_g=gn_g[l].reshape(1, D_RWKV), gn_b=gn_b[l].reshape(1, D_RWKV), tri=tri,
                ones_bd=_head_block_matrix(1.0), avg_bd=_head_block_matrix(1.0 / HEAD_DIM))


def _state_to_block_diag(s_f, s_b):
    def one(s):
        h = jnp.swapaxes(s.astype(F32), -1, -2)
        b = h.shape[0]
        h = h.reshape(b, N_PAIRS, PAIR, HEAD_DIM, HEAD_DIM)
        z = jnp.zeros_like(h[:, :, 0])
        top = jnp.concatenate([h[:, :, 0], z], axis=-1)
        bot = jnp.concatenate([z, h[:, :, 1]], axis=-1)
        return jnp.concatenate([top, bot], axis=-2)
    return jnp.stack([one(s_f), one(s_b)], axis=1)


def kernel(x_prompt, x_sample, state_rwkv_fwd, state_rwkv_bwd, c, c_ctx, w_ada, b_ada, norm_g, w_in,
           mu_shift, w0, w2, a0, a2, k_k, k_a, r_k, gn_g, gn_b, w_fnet, b_fnet, w_out, final_norm_g):
    depth = w_in.shape[0]
    n_dec = c.shape[0]
    assert n_dec + 1 <= SUBLANES
    bp, tp, _ = x_prompt.shape
    bs, ts, _ = x_sample.shape
    cvec = jnp.concatenate([c_ctx[None], c, jnp.zeros((SUBLANES - 1 - n_dec, D_MODEL), F32)], axis=0)
    emb = jnp.asarray(_sincos_2d(ts)).astype(x_sample.dtype)
    cs_tab = jnp.asarray(_channel_dft_table()).astype(BF16)
    dft_p = jnp.asarray(_dft_table(tp)).astype(BF16)
    dft_s = jnp.asarray(_dft_table(ts)).astype(BF16)
    fng = final_norm_g[None]

    xp, xs = x_prompt, x_sample
    new_f, new_b = [], []
    for l in range(depth):
        mod = _mod_call(cvec, w_ada[l], b_ada[l][None]).reshape(SUBLANES, 3, D_MODEL)
        mod_ctx, mod_lat = mod[0:1], mod[1:1 + n_dec]
        wts = _layer_weights(l, w0, w2, a0, a2, k_k, k_a, r_k, gn_g, gn_b)
        w_in_b = w_in[l].astype(BF16)
        w_out_b = w_out[l].astype(BF16)
        n_grp = w_fnet.shape[1]
        wbd = (w_fnet[l][:, :, None, :] * jnp.eye(n_grp, dtype=F32)[:, None, :, None]).reshape(
            D_FNET, D_FNET).astype(BF16)
        ng, mu, bf = norm_g[l][None], mu_shift[l][None], b_fnet[l][None]
        emb_l = emb if l == 0 else None

        rk, v, lwla, grec, fin, gf = _in_proj_call(xp, None, mod_ctx, ng, w_in_b, mu)
        y_f, y_b, s_f, s_b = _scan_call(rk, v, lwla, wts, None, True)
        fo = _fnet_call(fin, gf, dft_p, cs_tab, wbd, bf)
        last = l == depth - 1
        xp = _out_call(xp, None, mod_ctx, y_f, y_b, rk, v, lwla, grec, fo, wts, w_out_b, fng, last)
        new_f.append(s_f)
        new_b.append(s_b)

        rk, v, lwla, grec, fin, gf = _in_proj_call(xs, emb_l, mod_lat, ng, w_in_b, mu)
        s0 = _state_to_block_diag(state_rwkv_fwd[:, l], state_rwkv_bwd[:, l])
        y_f, y_b = _scan_call(rk, v, lwla, wts, s0, False)
        fo = _fnet_call(fin, gf, dft_s, cs_tab, wbd, bf)
        xs = _out_call(xs, emb_l, mod_lat, y_f, y_b, rk, v, lwla, grec, fo, wts, w_out_b, fng, last)
    return (xp, xs, jnp.concatenate(new_f, axis=1), jnp.concatenate(new_b, axis=1))
```

```python
import functools
import math

import numpy as np
import jax
import jax.numpy as jnp
from jax import lax
from jax.experimental import pallas as pl
from jax.experimental.pallas import tpu as pltpu

F32 = jnp.float32
BF16 = jnp.bfloat16

D_MODEL = 1024
GRID_W = 64
D_RWKV = 512
D_FNET = D_MODEL - D_RWKV
HEAD_DIM = 64
N_HEADS = D_RWKV // HEAD_DIM
FNET_GROUP = 64
LORA = 32
N_DIR = 2
D_SHIFT = 3 * D_RWKV + N_DIR * 2 * LORA
D_IN = D_SHIFT + D_RWKV + 2 * D_FNET
NORM_EPS = 1e-6
KK_EPS = 1e-12
GN_EPS = 64e-5
POS_BASE = 10000.0

LANES = 128
SUBLANES = 8
PAIR = LANES // HEAD_DIM
N_PAIRS = N_HEADS // PAIR
CHUNK = 64
SUB = 16
SUB_SHIFT = SUB.bit_length() - 1
SCAN_WORK = 8
SCAN_SUB = 4
ROW_TILE = 512
ROW_SPLIT = 4
FNET_ROWS = 1024
HALO = 8
VMEM_LIMIT = 56 * 1024 * 1024


def _silu(x):
    return x * jax.nn.sigmoid(x)


def _bdot(a, b):
    return jnp.dot(a.astype(BF16), b.astype(BF16), preferred_element_type=F32)


def _split2(x):
    hi = x.astype(BF16)
    lo = (x - hi.astype(F32)).astype(BF16)
    return hi, lo


def _mod_kernel(c_ref, w_ref, b_ref, o_ref):
    @pl.when(pl.program_id(0) == 0)
    def _():
        o_ref[...] = jnp.broadcast_to(b_ref[...], o_ref.shape)

    s_hi, s_lo = _split2(_silu(c_ref[...]))
    w = w_ref[...].astype(BF16)
    o_ref[...] += jnp.dot(s_hi, w, preferred_element_type=F32) + jnp.dot(s_lo, w, preferred_element_type=F32)


def _mod_call(cvec, w_ada, b_ada):
    rows = D_MODEL // 4
    return pl.pallas_call(
        _mod_kernel,
        grid=(D_MODEL // rows,),
        in_specs=[pl.BlockSpec((SUBLANES, rows), lambda i: (0, i)),
                  pl.BlockSpec((rows, 3 * D_MODEL), lambda i: (i, 0)),
                  pl.BlockSpec((1, 3 * D_MODEL), lambda i: (0, 0))],
        out_specs=pl.BlockSpec((SUBLANES, 3 * D_MODEL), lambda i: (0, 0)),
        out_shape=jax.ShapeDtypeStruct((SUBLANES, 3 * D_MODEL), F32),
        compiler_params=pltpu.CompilerParams(dimension_semantics=("arbitrary",),
                                             vmem_limit_bytes=VMEM_LIMIT),
        name="mod",
    )(cvec, w_ada, b_ada)


def _modulated_norm(x, g, scale, shift):
    ms = jnp.mean(x * x, axis=-1, keepdims=True)
    y = x * lax.rsqrt(ms + NORM_EPS) * g
    return y * (1.0 + scale) + shift


def _in_proj_kernel(*refs, has_emb, has_halo, n_tiles):
    it = iter(refs)
    x_ref = next(it)
    xp_ref = next(it) if has_halo else None
    xn_ref = next(it) if has_halo else None
    emb_ref = next(it) if has_emb else None
    embp_ref = next(it) if (has_emb and has_halo) else None
    embn_ref = next(it) if (has_emb and has_halo) else None
    mod_ref, g_ref, w_ref, mu_ref = next(it), next(it), next(it), next(it)
    rk_ref, v_ref, lwla_ref, grec_ref, fin_ref, gf_ref = (next(it) for _ in range(6))

    i = pl.program_id(1)
    g = g_ref[...]
    shift = mod_ref[0, 0:1, :]
    scale = mod_ref[0, 1:2, :]
    nb, tt, _ = x_ref.shape
    tm = nb * tt
    x = x_ref[...].reshape(tm, D_MODEL)
    if has_emb:
        x = x + emb_ref[...]
    part = tm // ROW_SPLIT
    p = jnp.concatenate(
        [jnp.dot(_modulated_norm(x[r0:r0 + part], g, scale, shift).astype(BF16), w_ref[...],
                 preferred_element_type=F32) for r0 in range(0, tm, part)], axis=0)
    ps = p[:, :D_SHIFT]
    if has_halo:
        xh = jnp.concatenate([xp_ref[0], xn_ref[0]], axis=0)
        if has_emb:
            xh = xh + jnp.concatenate([embp_ref[...], embn_ref[...]], axis=0)
        hh = _modulated_norm(xh, g, scale, shift)
        ph = jnp.dot(hh.astype(BF16), w_ref[:, :D_SHIFT], preferred_element_type=F32)
        prev_row = jnp.where(i > 0, ph[HALO - 1:HALO, :], 0.0)
        next_row = jnp.where(i < n_tiles - 1, ph[HALO:HALO + 1, :], 0.0)
    else:
        prev_row = jnp.zeros((1, D_SHIFT), F32)
        next_row = jnp.zeros((1, D_SHIFT), F32)
    pos = lax.rem(lax.broadcasted_iota(jnp.int32, (tm, D_SHIFT), 0), tt)
    prev = jnp.where(pos == 0, prev_row, pltpu.roll(ps, 1, 0))
    nxt = jnp.where(pos == tt - 1, next_row, pltpu.roll(ps, tm - 1, 0))
    p_rec = ps + mu_ref[...] * (0.5 * (prev + nxt) - ps)

    def put(ref, val):
        ref[...] = val.astype(ref.dtype).reshape(ref.shape)

    put(rk_ref, p_rec[:, :2 * D_RWKV])
    put(v_ref, p_rec[:, 2 * D_RWKV:3 * D_RWKV])
    put(lwla_ref, p_rec[:, 3 * D_RWKV:])
    put(grec_ref, p[:, D_SHIFT:D_SHIFT + D_RWKV])
    put(fin_ref, p[:, D_SHIFT + D_RWKV:D_SHIFT + D_RWKV + D_FNET])
    put(gf_ref, p[:, D_SHIFT + D_RWKV + D_FNET:])


def _in_proj_call(x, emb, mod, norm_g, w_in_bf16, mu):
    B, T, _ = x.shape
    tt = min(T, ROW_TILE)
    nb = max(1, ROW_TILE // T)
    n_tiles = T // tt
    has_halo = n_tiles > 1
    has_emb = emb is not None
    per_batch_mod = mod.shape[0] > 1
    assert B % nb == 0 and T % tt == 0 and (nb == 1 or not (has_emb or per_batch_mod))
    tm = tt
    blocks_per_tile = tm // HALO
    last_halo_block = T // HALO - 1

    in_specs = [pl.BlockSpec((nb, tt, D_MODEL), lambda b, i: (b, i, 0))]
    args = [x]
    if has_halo:
        in_specs += [
            pl.BlockSpec((1, HALO, D_MODEL), lambda b, i: (b, jnp.maximum(i * blocks_per_tile - 1, 0), 0)),
            pl.BlockSpec((1, HALO, D_MODEL),
                         lambda b, i: (b, jnp.minimum((i + 1) * blocks_per_tile, last_halo_block), 0))]
        args += [x, x]
    if has_emb:
        in_specs.append(pl.BlockSpec((tm, D_MODEL), lambda b, i: (i, 0)))
        args.append(emb)
        if has_halo:
            in_specs += [
                pl.BlockSpec((HALO, D_MODEL), lambda b, i: (jnp.maximum(i * blocks_per_tile - 1, 0), 0)),
                pl.BlockSpec((HALO, D_MODEL),
                             lambda b, i: (jnp.minimum((i + 1) * blocks_per_tile, last_halo_block), 0))]
            args += [emb, emb]
    mod_map = (lambda b, i: (b, 0, 0)) if per_batch_mod else (lambda b, i: (0, 0, 0))
    in_specs += [pl.BlockSpec((1, 3, D_MODEL), mod_map),
                 pl.BlockSpec((1, D_MODEL), lambda b, i: (0, 0)),
                 pl.BlockSpec((D_MODEL, D_IN), lambda b, i: (0, 0), pipeline_mode=pl.Buffered(1)),
                 pl.BlockSpec((1, D_SHIFT), lambda b, i: (0, 0))]
    args += [mod, norm_g, w_in_bf16, mu]
    outs = ((2 * D_RWKV, BF16), (D_RWKV, BF16), (N_DIR * 2 * LORA, F32), (D_RWKV, BF16), (D_FNET, BF16),
            (D_FNET, BF16))
    out_specs = [pl.BlockSpec((nb, tt, w), lambda b, i: (b, i, 0)) for w, _ in outs]
    out_shape = [jax.ShapeDtypeStruct((B, T, w), dt) for w, dt in outs]
    kern = functools.partial(_in_proj_kernel, has_emb=has_emb, has_halo=has_halo, n_tiles=n_tiles)
    return pl.pallas_call(
        kern, grid=(B // nb, n_tiles), in_specs=in_specs, out_specs=out_specs, out_shape=out_shape,
        compiler_params=pltpu.CompilerParams(dimension_semantics=("arbitrary", "arbitrary"),
                                             vmem_limit_bytes=VMEM_LIMIT),
        name="in_proj",
    )(*args)


def _same_block(i, j, size):
    shift = size.bit_length() - 1
    return jnp.right_shift(i, shift) == jnp.right_shift(j, shift)


def _block_diag_rows(y, head0_lanes):
    return jnp.concatenate([jnp.where(head0_lanes, y, 0.0), jnp.where(head0_lanes, 0.0, y)], axis=0)


def _scan_kernel(*refs, zero_init, write_state, n_steps, n_seq, n_sub):
    it = iter(refs)
    rk_refs, v_refs, lwla_refs = [None, None], [None, None], [None, None]
    for d in range(N_DIR):
        rk_refs[d], v_refs[d], lwla_refs[d] = next(it), next(it), next(it)
    w0_ref, w2_ref, a0_ref, a2_ref = (next(it) for _ in range(4))
    kk_ref, ka_ref, tri_ref, ones_ref = (next(it) for _ in range(4))
    s0_ref = None if zero_init else next(it)
    y_refs = [next(it), next(it)]
    sfin_refs = [next(it), next(it)] if write_state else None
    h_scr = next(it)

    j = pl.program_id(1)

    @pl.when(j == 0)
    def _():
        if zero_init:
            h_scr[...] = jnp.zeros_like(h_scr)
        else:
            h_scr[...] = s0_ref[...]

    C = CHUNK
    row = lax.broadcasted_iota(jnp.int32, (C, LANES), 0)
    lane = lax.broadcasted_iota(jnp.int32, (C, LANES), 1)
    s_idx = jnp.bitwise_and(lane, HEAD_DIM - 1)
    head0 = lane < HEAD_DIM
    row_c = lax.broadcasted_iota(jnp.int32, (SUB, LANES), 0)
    lane_c = lax.broadcasted_iota(jnp.int32, (SUB, LANES), 1)
    col_c = jnp.bitwise_and(lane_c, SUB - 1)
    blk_c = jnp.right_shift(jnp.bitwise_and(lane_c, HEAD_DIM - 1), SUB_SHIFT)
    lane_blk_c = jnp.right_shift(lane_c, SUB_SHIFT)
    eye_c = (row_c == col_c).astype(F32)

    def bd_c(y):
        return jnp.concatenate([jnp.where(lane_blk_c == g, y, 0.0) for g in range(LANES // SUB)], axis=0)

    def inverse_size4(l_c, upper):
        r4 = jnp.bitwise_and(row_c, 3)
        c4 = jnp.bitwise_and(col_c, 3)
        down, up = (lambda x, k: pltpu.roll(x, k, 0)), (lambda x, k: pltpu.roll(x, SUB - k, 0))
        right, left = (lambda x, k: pltpu.roll(x, k, 1)), (lambda x, k: pltpu.roll(x, LANES - k, 1))
        s2 = jnp.where(_same_block(row_c, col_c, 2), l_c, 0.0)
        e4 = jnp.where(_same_block(row_c, col_c, 4), l_c, 0.0) - s2
        if upper:
            s_col = jnp.where(r4 == 1, up(s2, 1), up(s2, 2))
            s_row = jnp.where(c4 == 2, right(s2, 1), right(s2, 2))
            e_s = jnp.where(c4 == 3, right(e4, 1) * s_col, 0.0)
            s_e = jnp.where(r4 == 0, s_row * up(e4, 1), 0.0)
            s_e_s = jnp.where((r4 == 0) & (c4 == 3), s_row * up(e_s, 1), 0.0)
        else:
            s_col = jnp.where(r4 == 2, down(s2, 1), down(s2, 2))
            s_row = jnp.where(c4 == 1, left(s2, 1), left(s2, 2))
            e_s = jnp.where(c4 == 0, left(e4, 1) * s_col, 0.0)
            s_e = jnp.where(r4 == 3, s_row * down(e4, 1), 0.0)
            s_e_s = jnp.where((r4 == 3) & (c4 == 0), s_row * down(e_s, 1), 0.0)
        return eye_c - s2 - e4 + e_s + s_e - s_e_s

    row2 = lax.broadcasted_iota(jnp.int32, (LANES, LANES), 0)
    lane2 = lax.broadcasted_iota(jnp.int32, (LANES, LANES), 1)
    same_head = (row2 < HEAD_DIM) == (lane2 < HEAD_DIM)
    decay_scale = math.exp(-0.5)

    row0 = lambda o, d: (o if d == 0 else n_sub - 1 - o) * C
    keys = [(o, n, d) for o in range(n_sub) for n in range(n_seq) for d in range(N_DIR)]
    rk_all = {(o, n, d): rk_refs[d][n, row0(o, d):row0(o, d) + C, :].astype(F32) for o, n, d in keys}
    kkr_all = {key: rk_all[key][:, D_RWKV:] * kk_ref[...] for key in keys}
    ssq_rows = _bdot(jnp.concatenate([kkr_all[key] * kkr_all[key] for key in keys], axis=0), ones_ref[...])
    dir_keys = [[key for key in keys if key[2] == d] for d in range(N_DIR)]
    z_w_rows, z_a_rows = [], []
    for d in range(N_DIR):
        ll_rows = jnp.concatenate([lwla_refs[d][n, row0(o, d):row0(o, d) + C, :] for o, n, _ in dir_keys[d]], axis=0)
        z_w_rows.append(w0_ref[d:d + 1, :] + _bdot(jnp.tanh(ll_rows), w2_ref[d]))
        z_a_rows.append(a0_ref[d:d + 1, :] + _bdot(ll_rows, a2_ref[d]))

    sets = [[] for _ in range(n_sub)]
    for i, (o, n, d) in enumerate(keys):
        rk = rk_all[(o, n, d)]
        r = rk[:, :D_RWKV]
        k = rk[:, D_RWKV:]
        v = v_refs[d][n, row0(o, d):row0(o, d) + C, :]
        i_d = dir_keys[d].index((o, n, d))
        logw = -decay_scale * jax.nn.sigmoid(z_w_rows[d][i_d * C:(i_d + 1) * C])
        a = jax.nn.sigmoid(z_a_rows[d][i_d * C:(i_d + 1) * C])
        kd = k * (1.0 + (a - 1.0) * ka_ref[...])
        kkr = kkr_all[(o, n, d)]
        ssq = ssq_rows[i * C:(i + 1) * C]
        kk = kkr * lax.rsqrt(jnp.maximum(ssq, KK_EPS * KK_EPS))
        bvec = kk * a
        lw_hi, lw_lo = _split2(logw)
        tri = tri_ref[d]
        cum = (jnp.dot(tri, lw_hi, preferred_element_type=F32)
               + jnp.dot(tri, lw_lo, preferred_element_type=F32))
        cum_prev = cum - logw
        tot = cum[C - 1:C, :] if d == 0 else cum[0:1, :]
        kap_t = kk * jnp.exp(cum_prev)
        r_t = r * jnp.exp(cum)
        e_neg = jnp.exp(-cum)
        gam = jnp.exp(tot)
        b_t = bvec * e_neg
        k_t = kd * e_neg
        e_rem = gam * e_neg
        b_h = bvec * e_rem
        k_h = kd * e_rem
        if d == 0:
            strict, incl = row > s_idx, row >= s_idx
        else:
            strict, incl = row < s_idx, row <= s_idx

        for p in range(N_PAIRS):
            sl = slice(p * LANES, (p + 1) * LANES)
            dup_t = lambda x: jnp.where(same_head, jnp.concatenate([x[:, sl], x[:, sl]], axis=0).T, 0.0)
            sets[o].append(dict(
                n=n, d=d, p=p, sl=sl, rows=slice(row0(o, d), row0(o, d) + C), strict=strict, incl=incl,
                prev=None if o == 0 else sets[o - 1][len(sets[o])],
                lhs=jnp.concatenate([kap_t[:, sl], r_t[:, sl]], axis=0).astype(BF16),
                v=v[:, sl], w_lm=jnp.concatenate([dup_t(b_t), dup_t(k_t)], axis=1).astype(BF16),
                t2=jnp.concatenate([b_h[:, sl], k_h[:, sl]], axis=0).T.astype(BF16),
                gam_col=jnp.broadcast_to(gam[:, sl], (LANES, LANES)).T))

    bd = lambda y: _block_diag_rows(y, head0)

    def st_scores(probs):
        for q in probs:
            lm = jnp.dot(q["lhs"], q["w_lm"], preferred_element_type=F32)
            q["l_b"] = jnp.where(q["strict"], lm[:C, :LANES], 0.0)
            q["m_b"] = jnp.where(q["incl"], lm[C:, :LANES], 0.0)
            q["lm_k"] = jnp.concatenate([jnp.where(q["strict"], lm[:C, LANES:], 0.0),
                                         jnp.where(q["incl"], lm[C:, LANES:], 0.0)], axis=0)

    def st_values(probs):
        for q in probs:
            lmv = _bdot(q["lm_k"], bd(q["v"]))
            q["l_kv"], q["m_kv"] = lmv[:C], lmv[C:]
            l_b = q["l_b"]
            l_c = l_b[0:SUB]
            for jb in range(1, C // SUB):
                l_c = jnp.where(blk_c == jb, l_b[jb * SUB:(jb + 1) * SUB], l_c)
            q["l_c"] = l_c
            q["t_c"] = inverse_size4(l_c, q["d"] == 1)

    def st_compact_a(s):
        def run(probs):
            off_mask = _same_block(row_c, col_c, 2 * s) & ~_same_block(row_c, col_c, s)
            for q in probs:
                q["et"] = _bdot(jnp.where(off_mask, q["l_c"], 0.0), bd_c(q["t_c"]))
        return run

    def st_compact_b(last):
        def run(probs):
            for q in probs:
                q["t_c"] = q["t_c"] - _bdot(q["t_c"], bd_c(q["et"]))
                if last:
                    q["tinv"] = jnp.concatenate([jnp.where(blk_c == jb, q["t_c"], 0.0) for jb in range(C // SUB)],
                                                axis=0)
        return run

    def st_full_a(s):
        def run(probs):
            off_mask = _same_block(row, s_idx, 2 * s) & ~_same_block(row, s_idx, s)
            for q in probs:
                q["et"] = _bdot(jnp.where(off_mask, q["l_b"], 0.0), bd(q["tinv"]))
        return run

    def st_full_b(probs):
        for q in probs:
            q["tinv"] = q["tinv"] - _bdot(q["tinv"], bd(q["et"]))

    def st_state_read(probs):
        for q in probs:
            q["h"] = h_scr[q["n"], q["d"], q["p"]] if q["prev"] is None else q["prev"]["h_new"]
            q["kr_h"] = _bdot(q["lhs"], q["h"])

    def st_solve(probs):
        for q in probs:
            q["u_n"] = _bdot(q["tinv"], bd(q["kr_h"][:C] + q["l_kv"]))

    def st_output(probs):
        for q in probs:
            y = q["kr_h"][C:] + q["m_kv"] - _bdot(q["m_b"], bd(q["u_n"]))
            y_refs[q["d"]][q["n"], q["rows"], q["sl"]] = y.astype(BF16)

    def st_state_write(probs):
        for q in probs:
            upd = _bdot(q["t2"], jnp.concatenate([(-q["u_n"]).astype(BF16), q["v"]], axis=0))
            q["h_new"] = q["gam_col"] * q["h"] + jnp.where(same_head, upd, 0.0)

    stages = [st_scores, st_values] + ([st_state_read] if n_sub == 1 else [])
    s = 4
    while s < SUB:
        stages += [st_compact_a(s), st_compact_b(2 * s == SUB)]
        s *= 2
    while s < C:
        stages += [st_full_a(s), st_full_b]
        s *= 2
    tail = [st_state_read, st_solve, st_output, st_state_write]
    stages += tail[1:] if n_sub == 1 else tail
    skew = len(tail)
    for t in range(len(stages) + skew * (n_sub - 1)):
        for o in range(n_sub):
            if 0 <= t - skew * o < len(stages):
                stages[t - skew * o](sets[o])
    for q in sets[-1]:
        h_scr[q["n"], q["d"], q["p"]] = q["h_new"]

    if write_state:
        @pl.when(j == n_steps - 1)
        def _():
            for n, d, p in [(n, d, p) for n in range(n_seq) for d in range(N_DIR) for p in range(N_PAIRS)]:
                ht = h_scr[n, d, p].T
                sfin_refs[d][n, 0, PAIR * p] = ht[:HEAD_DIM, :HEAD_DIM]
                sfin_refs[d][n, 0, PAIR * p + 1] = ht[HEAD_DIM:, HEAD_DIM:]


def _scan_call(rk, v, lwla, wts, s0_bd, write_state):
    B, T, _ = rk.shape
    n_sub = min(SCAN_SUB, T // CHUNK)
    ns = min(max(1, SCAN_WORK // n_sub), B)
    rows = n_sub * CHUNK
    nc = T // rows
    assert B % ns == 0 and T % rows == 0
    zero_init = s0_bd is None
    fwd = lambda b, j: (b, j, 0)
    bwd = lambda b, j: (b, nc - 1 - j, 0)
    full = lambda *shape: pl.BlockSpec(shape, lambda b, j: (0,) * len(shape))
    tok = lambda w, m: pl.BlockSpec((ns, rows, w), m)
    in_specs = [tok(2 * D_RWKV, fwd), tok(D_RWKV, fwd), tok(LANES, fwd),
                tok(2 * D_RWKV, bwd), tok(D_RWKV, bwd), tok(LANES, bwd),
                full(N_DIR, D_RWKV), full(N_DIR, LANES, D_RWKV),
                full(N_DIR, D_RWKV), full(N_DIR, LANES, D_RWKV),
                full(1, D_RWKV), full(1, D_RWKV), full(N_DIR, CHUNK, CHUNK), full(D_RWKV, D_RWKV)]
    args = [rk, v, lwla, rk, v, lwla, wts["w0"], wts["w2"], wts["a0"], wts["a2"],
            wts["k_k"], wts["k_a"], wts["tri"], wts["ones_bd"]]
    state_block = (ns, N_DIR, N_PAIRS, LANES, LANES)
    if not zero_init:
        in_specs.append(pl.BlockSpec(state_block, lambda b, j: (b, 0, 0, 0, 0)))
        args.append(s0_bd)
    out_specs = [tok(D_RWKV, fwd), tok(D_RWKV, bwd)]
    out_shape = [jax.ShapeDtypeStruct((B, T, D_RWKV), BF16), jax.ShapeDtypeStruct((B, T, D_RWKV), BF16)]
    if write_state:
        final_block = (ns, 1, N_HEADS, HEAD_DIM, HEAD_DIM)
        out_specs += [pl.BlockSpec(final_block, lambda b, j: (b, 0, 0, 0, 0))] * N_DIR
        out_shape += [jax.ShapeDtypeStruct((B,) + final_block[1:], F32)] * N_DIR
    kern = functools.partial(_scan_kernel, zero_init=zero_init, write_state=write_state, n_steps=nc,
                             n_seq=ns, n_sub=n_sub)
    return pl.pallas_call(
        kern, grid=(B // ns, nc), in_specs=in_specs, out_specs=out_specs, out_shape=out_shape,
        scratch_shapes=[pltpu.VMEM(state_block, F32)],
        compiler_params=pltpu.CompilerParams(dimension_semantics=("arbitrary", "arbitrary"),
                                             vmem_limit_bytes=VMEM_LIMIT),
        name="scan",
    )(*args)


def _fnet_kernel(fin_ref, gf_ref, dft_ref, cs_ref, wbd_ref, b_ref, o_ref, g_scr, csw_scr):
    u = pl.program_id(1)
    nb, seq_len, _ = fin_ref.shape
    tu = o_ref.shape[1]

    @pl.when((pl.program_id(0) == 0) & (u == 0))
    def _():
        for half in range(2):
            cols = slice(half * D_FNET, (half + 1) * D_FNET)
            csw_scr[:, cols] = _bdot(cs_ref[:, cols], wbd_ref[...]).astype(BF16)

    @pl.when(u == 0)
    def _():
        step = min(seq_len, ROW_TILE)
        for n in range(nb):
            for t0 in range(0, seq_len, step):
                fc = _bdot(fin_ref[n, t0:t0 + step, :], csw_scr[...])
                g_scr[n, t0:t0 + step, :] = fc[:, :D_FNET].astype(BF16)
                g_scr[n, seq_len + t0:seq_len + t0 + step, :] = fc[:, D_FNET:].astype(BF16)

    f_out = jnp.concatenate([jnp.dot(dft_ref[...], g_scr[n], preferred_element_type=F32) for n in range(nb)],
                            axis=0) + b_ref[...]
    gate = _silu(gf_ref[...].astype(F32)).reshape(nb * tu, D_FNET)
    o_ref[...] = (f_out * gate).astype(BF16).reshape(nb, tu, D_FNET)


def _fnet_call(fin, gf, dft_bf16, cs_bf16, wbd_bf16, b_fnet):
    B, T, _ = fin.shape
    tu = min(T, FNET_ROWS)
    nb = max(1, FNET_ROWS // T)
    assert B % nb == 0 and T % tu == 0
    return pl.pallas_call(
        _fnet_kernel,
        grid=(B // nb, T // tu),
        in_specs=[pl.BlockSpec((nb, T, D_FNET), lambda b, u: (b, 0, 0)),
                  pl.BlockSpec((nb, tu, D_FNET), lambda b, u: (b, u, 0)),
                  pl.BlockSpec((tu, 2 * T), lambda b, u: (u, 0)),
                  pl.BlockSpec((D_FNET, 2 * D_FNET), lambda b, u: (0, 0)),
                  pl.BlockSpec((D_FNET, D_FNET), lambda b, u: (0, 0)),
                  pl.BlockSpec((1, D_FNET), lambda b, u: (0, 0))],
        out_specs=pl.BlockSpec((nb, tu, D_FNET), lambda b, u: (b, u, 0)),
        out_shape=jax.ShapeDtypeStruct((B, T, D_FNET), BF16),
        scratch_shapes=[pltpu.VMEM((nb, 2 * T, D_FNET), BF16), pltpu.VMEM((D_FNET, 2 * D_FNET), BF16)],
        compiler_params=pltpu.CompilerParams(dimension_semantics=("arbitrary", "arbitrary"),
                                             vmem_limit_bytes=VMEM_LIMIT),
        name="fnet",
    )(fin, gf, dft_bf16, cs_bf16, wbd_bf16, b_fnet)


def _out_kernel(*refs, has_emb, final_norm):
    it = iter(refs)
    x_ref = next(it)
    emb_ref = next(it) if has_emb else None
    (mod_ref, yf_ref, yb_ref, rk_ref, v_ref, lwla_ref, grec_ref, fo_ref, a0_ref, a2_ref, ka_ref,
     rkw_ref, gng_ref, gnb_ref, avg_ref, ones_ref, wout_ref, fng_ref, o_ref) = (next(it) for _ in range(19))

    nb, tt, _ = x_ref.shape
    tm = nb * tt
    rows = lambda ref: ref[...].reshape(tm, ref.shape[-1])
    y = rows(yf_ref).astype(F32) + rows(yb_ref).astype(F32)
    mu = _bdot(y, avg_ref[...])
    dlt = y - mu
    var = _bdot(dlt * dlt, avg_ref[...])
    y_n = dlt * lax.rsqrt(var + GN_EPS) * gng_ref[...] + gnb_ref[...]
    rk = rows(rk_ref).astype(F32)
    r = rk[:, :D_RWKV]
    k = rk[:, D_RWKV:]
    ll = rows(lwla_ref)
    a_sum = (jax.nn.sigmoid(a0_ref[0:1, :] + _bdot(ll, a2_ref[0]))
             + jax.nn.sigmoid(a0_ref[1:2, :] + _bdot(ll, a2_ref[1])))
    k_sum = k * (2.0 + (a_sum - 2.0) * ka_ref[...])
    bonus = _bdot(r * k_sum * rkw_ref[...], ones_ref[...]) * rows(v_ref).astype(F32)
    rec_out = (y_n + bonus) * _silu(rows(grec_ref).astype(F32))
    mixed = jnp.concatenate([rec_out.astype(BF16), rows(fo_ref)], axis=-1)
    out = jnp.dot(mixed, wout_ref[...], preferred_element_type=F32)
    x = rows(x_ref)
    if has_emb:
        x = x + emb_ref[...]
    z = x + mod_ref[0, 2:3, :] * out
    if final_norm:
        ms = jnp.mean(z * z, axis=-1, keepdims=True)
        z = z * lax.rsqrt(ms + NORM_EPS) * fng_ref[...]
    o_ref[...] = z.reshape(nb, tt, D_MODEL)


def _out_call(x, emb, mod, y_f, y_b, rk, v, lwla, grec, fo, wts, w_out_bf16, final_norm_g, final_norm):
    B, T, _ = x.shape
    tt = min(T, ROW_TILE)
    nb = max(1, ROW_TILE // T)
    has_emb = emb is not None
    per_batch_mod = mod.shape[0] > 1
    assert B % nb == 0 and T % tt == 0 and (nb == 1 or not (has_emb or per_batch_mod))
    tok = lambda w: pl.BlockSpec((nb, tt, w), lambda b, i: (b, i, 0))
    full = lambda *shape: pl.BlockSpec(shape, lambda b, i: (0,) * len(shape))
    in_specs = [tok(D_MODEL)]
    args = [x]
    if has_emb:
        in_specs.append(pl.BlockSpec((tt, D_MODEL), lambda b, i: (i, 0)))
        args.append(emb)
    mod_map = (lambda b, i: (b, 0, 0)) if per_batch_mod else (lambda b, i: (0, 0, 0))
    in_specs += [pl.BlockSpec((1, 3, D_MODEL), mod_map), tok(D_RWKV), tok(D_RWKV), tok(2 * D_RWKV),
                 tok(D_RWKV), tok(LANES), tok(D_RWKV), tok(D_FNET),
                 full(N_DIR, D_RWKV), full(N_DIR, LANES, D_RWKV),
                 full(1, D_RWKV), full(1, D_RWKV), full(1, D_RWKV), full(1, D_RWKV),
                 full(D_RWKV, D_RWKV), full(D_RWKV, D_RWKV), full(D_MODEL, D_MODEL), full(1, D_MODEL)]
    args += [mod, y_f, y_b, rk, v, lwla, grec, fo, wts["a0"], wts["a2"], wts["k_a"],
             wts["r_k"], wts["gn_g"], wts["gn_b"], wts["avg_bd"], wts["ones_bd"], w_out_bf16, final_norm_g]
    return pl.pallas_call(
        functools.partial(_out_kernel, has_emb=has_emb, final_norm=final_norm),
        grid=(B // nb, T // tt), in_specs=in_specs,
        out_specs=tok(D_MODEL),
        out_shape=jax.ShapeDtypeStruct((B, T, D_MODEL), F32),
        compiler_params=pltpu.CompilerParams(dimension_semantics=("arbitrary", "arbitrary"),
                                             vmem_limit_bytes=VMEM_LIMIT),
        name="out_proj",
    )(*args)


def _dft_table(seq_len):
    idx = np.arange(seq_len, dtype=np.int64)
    ang = 2.0 * np.pi * ((idx[:, None] * idx[None, :]) % seq_len).astype(np.float64) / seq_len
    scale = 1.0 / math.sqrt(seq_len)
    return np.concatenate([np.cos(ang) * scale, -np.sin(ang) * scale], axis=1).astype(np.float32)


def _channel_dft_table():
    n = FNET_GROUP
    idx = np.arange(n, dtype=np.int64)
    ang = 2.0 * np.pi * ((idx[:, None] * idx[None, :]) % n).astype(np.float64) / n
    c = np.cos(ang) / math.sqrt(n)
    s = np.sin(ang) / math.sqrt(n)
    eye = np.eye(D_FNET // n)
    return np.concatenate([np.kron(eye, c), np.kron(eye, s)], axis=1).astype(np.float32)


def _sincos_2d(n_tokens):
    rows = n_tokens // GRID_W
    pos = np.arange(rows * GRID_W)
    row = (pos // GRID_W).astype(np.float32)
    col = (pos % GRID_W).astype(np.float32)
    quarter = D_MODEL // 4
    freq = np.exp(np.float32(-math.log(POS_BASE)) * np.arange(quarter, dtype=np.float32) / np.float32(quarter))
    ang_r = row[:, None] * freq
    ang_c = col[:, None] * freq
    return np.concatenate([np.sin(ang_r), np.cos(ang_r), np.sin(ang_c), np.cos(ang_c)], axis=-1).astype(np.float32)


def _head_block_matrix(value):
    blk = np.kron(np.eye(N_HEADS), np.ones((HEAD_DIM, HEAD_DIM))) * value
    return jnp.asarray(blk, dtype=BF16)


def _pad_lora(w, row_offset):
    rows = [jnp.pad(w[d], ((row_offset + d * LORA, LANES - row_offset - (d + 1) * LORA), (0, 0)))
            for d in range(N_DIR)]
    return jnp.stack(rows).astype(BF16)


def _layer_weights(l, w0, w2, a0, a2, k_k, k_a, r_k, gn_g, gn_b):
    w2_p = _pad_lora(w2[l], 0)
    a2_p = _pad_lora(a2[l], N_DIR * LORA)
    tri_f = np.tril(np.ones((CHUNK, CHUNK)))
    tri = jnp.asarray(np.stack([tri_f, tri_f.T]), dtype=BF16)
    return dict(w0=w0[l], w2=w2_p, a0=a0[l], a2=a2_p,
                k_k=k_k[l][None], k_a=k_a[l][None], r_k=r_k[l].reshape(1, D_RWKV),
                gn_g=gn_g[l].reshape(1, D_RWKV), gn_b=gn_b[l].reshape(1, D_RWKV), tri=tri,
                ones_bd=_head_block_matrix(1.0), avg_bd=_head_block_matrix(1.0 / HEAD_DIM))


def _state_to_block_diag(s_f, s_b):
    def one(s):
        h = jnp.swapaxes(s.astype(F32), -1, -2)
        b = h.shape[0]
        h = h.reshape(b, N_PAIRS, PAIR, HEAD_DIM, HEAD_DIM)
        z = jnp.zeros_like(h[:, :, 0])
        top = jnp.concatenate([h[:, :, 0], z], axis=-1)
        bot = jnp.concatenate([z, h[:, :, 1]], axis=-1)
        return jnp.concatenate([top, bot], axis=-2)
    return jnp.stack([one(s_f), one(s_b)], axis=1)


def kernel(x_prompt, x_sample, state_rwkv_fwd, state_rwkv_bwd, c, c_ctx, w_ada, b_ada, norm_g, w_in,
           mu_shift, w0, w2, a0, a2, k_k, k_a, r_k, gn_g, gn_b, w_fnet, b_fnet, w_out, final_norm_g):
    depth = w_in.shape[0]
    n_dec = c.shape[0]
    assert n_dec + 1 <= SUBLANES
    bp, tp, _ = x_prompt.shape
    bs, ts, _ = x_sample.shape
    cvec = jnp.concatenate([c_ctx[None], c, jnp.zeros((SUBLANES - 1 - n_dec, D_MODEL), F32)], axis=0)
    emb = jnp.asarray(_sincos_2d(ts)).astype(x_sample.dtype)
    cs_tab = jnp.asarray(_channel_dft_table()).astype(BF16)
    dft_p = jnp.asarray(_dft_table(tp)).astype(BF16)
    dft_s = jnp.asarray(_dft_table(ts)).astype(BF16)
    fng = final_norm_g[None]

    xp, xs = x_prompt, x_sample
    new_f, new_b = [], []
    for l in range(depth):
        mod = _mod_call(cvec, w_ada[l], b_ada[l][None]).reshape(SUBLANES, 3, D_MODEL)
        mod_ctx, mod_lat = mod[0:1], mod[1:1 + n_dec]
        wts = _layer_weights(l, w0, w2, a0, a2, k_k, k_a, r_k, gn_g, gn_b)
        w_in_b = w_in[l].astype(BF16)
        w_out_b = w_out[l].astype(BF16)
        n_grp = w_fnet.shape[1]
        wbd = (w_fnet[l][:, :, None, :] * jnp.eye(n_grp, dtype=F32)[:, None, :, None]).reshape(
            D_FNET, D_FNET).astype(BF16)
        ng, mu, bf = norm_g[l][None], mu_shift[l][None], b_fnet[l][None]
        emb_l = emb if l == 0 else None

        rk, v, lwla, grec, fin, gf = _in_proj_call(xp, None, mod_ctx, ng, w_in_b, mu)
        y_f, y_b, s_f, s_b = _scan_call(rk, v, lwla, wts, None, True)
        fo = _fnet_call(fin, gf, dft_p, cs_tab, wbd, bf)
        last = l == depth - 1
        xp = _out_call(xp, None, mod_ctx, y_f, y_b, rk, v, lwla, grec, fo, wts, w_out_b, fng, last)
        new_f.append(s_f)
        new_b.append(s_b)

        rk, v, lwla, grec, fin, gf = _in_proj_call(xs, emb_l, mod_lat, ng, w_in_b, mu)
        s0 = _state_to_block_diag(state_rwkv_fwd[:, l], state_rwkv_bwd[:, l])
        y_f, y_b = _scan_call(rk, v, lwla, wts, s0, False)
        fo = _fnet_call(fin, gf, dft_s, cs_tab, wbd, bf)
        xs = _out_call(xs, emb_l, mod_lat, y_f, y_b, rk, v, lwla, grec, fo, wts, w_out_b, fng, last)
    return (xp, xs, jnp.concatenate(new_f, axis=1), jnp.concatenate(new_b, axis=1))
```

```python
import functools
import math

import numpy as np
import jax
import jax.numpy as jnp
from jax import lax
from jax.experimental import pallas as pl
from jax.experimental.pallas import tpu as pltpu

F32 = jnp.float32
BF16 = jnp.bfloat16

D_MODEL = 1024
GRID_W = 64
D_RWKV = 512
D_FNET = D_MODEL - D_RWKV
HEAD_DIM = 64
N_HEADS = D_RWKV // HEAD_DIM
FNET_GROUP = 64
LORA = 32
N_DIR = 2
D_SHIFT = 3 * D_RWKV + N_DIR * 2 * LORA
D_IN = D_SHIFT + D_RWKV + 2 * D_FNET
NORM_EPS = 1e-6
KK_EPS = 1e-12
GN_EPS = 64e-5
POS_BASE = 10000.0

LANES = 128
SUBLANES = 8
PAIR = LANES // HEAD_DIM
N_PAIRS = N_HEADS // PAIR
CHUNK = 64
SUB = 16
SUB_SHIFT = SUB.bit_length() - 1
SCAN_WORK = 8
SCAN_SUB = 4
ROW_TILE = 512
ROW_SPLIT = 4
FNET_ROWS = 1024
HALO = 8
VMEM_LIMIT = 56 * 1024 * 1024


def _silu(x):
    return x * jax.nn.sigmoid(x)


def _bdot(a, b):
    return jnp.dot(a.astype(BF16), b.astype(BF16), preferred_element_type=F32)


def _split2(x):
    hi = x.astype(BF16)
    lo = (x - hi.astype(F32)).astype(BF16)
    return hi, lo


def _mod_kernel(c_ref, w_ref, b_ref, o_ref):
    @pl.when(pl.program_id(0) == 0)
    def _():
        o_ref[...] = jnp.broadcast_to(b_ref[...], o_ref.shape)

    s_hi, s_lo = _split2(_silu(c_ref[...]))
    w = w_ref[...].astype(BF16)
    o_ref[...] += jnp.dot(s_hi, w, preferred_element_type=F32) + jnp.dot(s_lo, w, preferred_element_type=F32)


def _mod_call(cvec, w_ada, b_ada):
    rows = D_MODEL // 4
    return pl.pallas_call(
        _mod_kernel,
        grid=(D_MODEL // rows,),
        in_specs=[pl.BlockSpec((SUBLANES, rows), lambda i: (0, i)),
                  pl.BlockSpec((rows, 3 * D_MODEL), lambda i: (i, 0)),
                  pl.BlockSpec((1, 3 * D_MODEL), lambda i: (0, 0))],
        out_specs=pl.BlockSpec((SUBLANES, 3 * D_MODEL), lambda i: (0, 0)),
        out_shape=jax.ShapeDtypeStruct((SUBLANES, 3 * D_MODEL), F32),
        compiler_params=pltpu.CompilerParams(dimension_semantics=("arbitrary",),
                                             vmem_limit_bytes=VMEM_LIMIT),
        name="mod",
    )(cvec, w_ada, b_ada)


def _modulated_norm(x, g, scale, shift):
    ms = jnp.mean(x * x, axis=-1, keepdims=True)
    y = x * lax.rsqrt(ms + NORM_EPS) * g
    return y * (1.0 + scale) + shift


def _in_proj_kernel(*refs, has_emb, has_halo, n_tiles):
    it = iter(refs)
    x_ref = next(it)
    xp_ref = next(it) if has_halo else None
    xn_ref = next(it) if has_halo else None
    emb_ref = next(it) if has_emb else None
    embp_ref = next(it) if (has_emb and has_halo) else None
    embn_ref = next(it) if (has_emb and has_halo) else None
    mod_ref, g_ref, w_ref, mu_ref = next(it), next(it), next(it), next(it)
    rk_ref, v_ref, lwla_ref, grec_ref, fin_ref, gf_ref = (next(it) for _ in range(6))

    i = pl.program_id(1)
    g = g_ref[...]
    shift = mod_ref[0, 0:1, :]
    scale = mod_ref[0, 1:2, :]
    nb, tt, _ = x_ref.shape
    tm = nb * tt
    x = x_ref[...].reshape(tm, D_MODEL)
    if has_emb:
        x = x + emb_ref[...]
    part = tm // ROW_SPLIT
    p = jnp.concatenate(
        [jnp.dot(_modulated_norm(x[r0:r0 + part], g, scale, shift).astype(BF16), w_ref[...],
                 preferred_element_type=F32) for r0 in range(0, tm, part)], axis=0)
    ps = p[:, :D_SHIFT]
    if has_halo:
        xh = jnp.concatenate([xp_ref[0], xn_ref[0]], axis=0)
        if has_emb:
            xh = xh + jnp.concatenate([embp_ref[...], embn_ref[...]], axis=0)
        hh = _modulated_norm(xh, g, scale, shift)
        ph = jnp.dot(hh.astype(BF16), w_ref[:, :D_SHIFT], preferred_element_type=F32)
        prev_row = jnp.where(i > 0, ph[HALO - 1:HALO, :], 0.0)
        next_row = jnp.where(i < n_tiles - 1, ph[HALO:HALO + 1, :], 0.0)
    else:
        prev_row = jnp.zeros((1, D_SHIFT), F32)
        next_row = jnp.zeros((1, D_SHIFT), F32)
    pos = lax.rem(lax.broadcasted_iota(jnp.int32, (tm, D_SHIFT), 0), tt)
    prev = jnp.where(pos == 0, prev_row, pltpu.roll(ps, 1, 0))
    nxt = jnp.where(pos == tt - 1, next_row, pltpu.roll(ps, tm - 1, 0))
    p_rec = ps + mu_ref[...] * (0.5 * (prev + nxt) - ps)

    def put(ref, val):
        ref[...] = val.astype(ref.dtype).reshape(ref.shape)

    put(rk_ref, p_rec[:, :2 * D_RWKV])
    put(v_ref, p_rec[:, 2 * D_RWKV:3 * D_RWKV])
    put(lwla_ref, p_rec[:, 3 * D_RWKV:])
    put(grec_ref, p[:, D_SHIFT:D_SHIFT + D_RWKV])
    put(fin_ref, p[:, D_SHIFT + D_RWKV:D_SHIFT + D_RWKV + D_FNET])
    put(gf_ref, p[:, D_SHIFT + D_RWKV + D_FNET:])


def _in_proj_call(x, emb, mod, norm_g, w_in_bf16, mu):
    B, T, _ = x.shape
    tt = min(T, ROW_TILE)
    nb = max(1, ROW_TILE // T)
    n_tiles = T // tt
    has_halo = n_tiles > 1
    has_emb = emb is not None
    per_batch_mod = mod.shape[0] > 1
    assert B % nb == 0 and T % tt == 0 and (nb == 1 or not (has_emb or per_batch_mod))
    tm = tt
    blocks_per_tile = tm // HALO
    last_halo_block = T // HALO - 1

    in_specs = [pl.BlockSpec((nb, tt, D_MODEL), lambda b, i: (b, i, 0))]
    args = [x]
    if has_halo:
        in_specs += [
            pl.BlockSpec((1, HALO, D_MODEL), lambda b, i: (b, jnp.maximum(i * blocks_per_tile - 1, 0), 0)),
            pl.BlockSpec((1, HALO, D_MODEL),
                         lambda b, i: (b, jnp.minimum((i + 1) * blocks_per_tile, last_halo_block), 0))]
        args += [x, x]
    if has_emb:
        in_specs.append(pl.BlockSpec((tm, D_MODEL), lambda b, i: (i, 0)))
        args.append(emb)
        if has_halo:
            in_specs += [
                pl.BlockSpec((HALO, D_MODEL), lambda b, i: (jnp.maximum(i * blocks_per_tile - 1, 0), 0)),
                pl.BlockSpec((HALO, D_MODEL),
                             lambda b, i: (jnp.minimum((i + 1) * blocks_per_tile, last_halo_block), 0))]
            args += [emb, emb]
    mod_map = (lambda b, i: (b, 0, 0)) if per_batch_mod else (lambda b, i: (0, 0, 0))
    in_specs += [pl.BlockSpec((1, 3, D_MODEL), mod_map),
                 pl.BlockSpec((1, D_MODEL), lambda b, i: (0, 0)),
                 pl.BlockSpec((D_MODEL, D_IN), lambda b, i: (0, 0), pipeline_mode=pl.Buffered(1)),
                 pl.BlockSpec((1, D_SHIFT), lambda b, i: (0, 0))]
    args += [mod, norm_g, w_in_bf16, mu]
    outs = ((2 * D_RWKV, BF16), (D_RWKV, BF16), (N_DIR * 2 * LORA, F32), (D_RWKV, BF16), (D_FNET, BF16),
            (D_FNET, BF16))
    out_specs = [pl.BlockSpec((nb, tt, w), lambda b, i: (b, i, 0)) for w, _ in outs]
    out_shape = [jax.ShapeDtypeStruct((B, T, w), dt) for w, dt in outs]
    kern = functools.partial(_in_proj_kernel, has_emb=has_emb, has_halo=has_halo, n_tiles=n_tiles)
    return pl.pallas_call(
        kern, grid=(B // nb, n_tiles), in_specs=in_specs, out_specs=out_specs, out_shape=out_shape,
        compiler_params=pltpu.CompilerParams(dimension_semantics=("arbitrary", "arbitrary"),
                                             vmem_limit_bytes=VMEM_LIMIT),
        name="in_proj",
    )(*args)


def _same_block(i, j, size):
    shift = size.bit_length() - 1
    return jnp.right_shift(i, shift) == jnp.right_shift(j, shift)


def _block_diag_rows(y, head0_lanes):
    return jnp.concatenate([jnp.where(head0_lanes, y, 0.0), jnp.where(head0_lanes, 0.0, y)], axis=0)


def _scan_kernel(*refs, zero_init, write_state, n_steps, n_seq, n_sub):
    it = iter(refs)
    rk_refs, v_refs, lwla_refs = [None, None], [None, None], [None, None]
    for d in range(N_DIR):
        rk_refs[d], v_refs[d], lwla_refs[d] = next(it), next(it), next(it)
    w0_ref, w2_ref, a0_ref, a2_ref = (next(it) for _ in range(4))
    kk_ref, ka_ref, tri_ref, ones_ref = (next(it) for _ in range(4))
    s0_ref = None if zero_init else next(it)
    y_refs = [next(it), next(it)]
    sfin_refs = [next(it), next(it)] if write_state else None
    h_scr = next(it)

    j = pl.program_id(1)

    @pl.when(j == 0)
    def _():
        if zero_init:
            h_scr[...] = jnp.zeros_like(h_scr)
        else:
            h_scr[...] = s0_ref[...]

    C = CHUNK
    row = lax.broadcasted_iota(jnp.int32, (C, LANES), 0)
    lane = lax.broadcasted_iota(jnp.int32, (C, LANES), 1)
    s_idx = jnp.bitwise_and(lane, HEAD_DIM - 1)
    head0 = lane < HEAD_DIM
    row_c = lax.broadcasted_iota(jnp.int32, (SUB, LANES), 0)
    lane_c = lax.broadcasted_iota(jnp.int32, (SUB, LANES), 1)
    col_c = jnp.bitwise_and(lane_c, SUB - 1)
    blk_c = jnp.right_shift(jnp.bitwise_and(lane_c, HEAD_DIM - 1), SUB_SHIFT)
    lane_blk_c = jnp.right_shift(lane_c, SUB_SHIFT)
    eye_c = (row_c == col_c).astype(F32)

    def bd_c(y):
        return jnp.concatenate([jnp.where(lane_blk_c == g, y, 0.0) for g in range(LANES // SUB)], axis=0)

    def inverse_size4(l_c, upper):
        r4 = jnp.bitwise_and(row_c, 3)
        c4 = jnp.bitwise_and(col_c, 3)
        down, up = (lambda x, k: pltpu.roll(x, k, 0)), (lambda x, k: pltpu.roll(x, SUB - k, 0))
        right, left = (lambda x, k: pltpu.roll(x, k, 1)), (lambda x, k: pltpu.roll(x, LANES - k, 1))
        s2 = jnp.where(_same_block(row_c, col_c, 2), l_c, 0.0)
        e4 = jnp.where(_same_block(row_c, col_c, 4), l_c, 0.0) - s2
        if upper:
            s_col = jnp.where(r4 == 1, up(s2, 1), up(s2, 2))
            s_row = jnp.where(c4 == 2, right(s2, 1), right(s2, 2))
            e_s = jnp.where(c4 == 3, right(e4, 1) * s_col, 0.0)
            s_e = jnp.where(r4 == 0, s_row * up(e4, 1), 0.0)
            s_e_s = jnp.where((r4 == 0) & (c4 == 3), s_row * up(e_s, 1), 0.0)
        else:
            s_col = jnp.where(r4 == 2, down(s2, 1), down(s2, 2))
            s_row = jnp.where(c4 == 1, left(s2, 1), left(s2, 2))
            e_s = jnp.where(c4 == 0, left(e4, 1) * s_col, 0.0)
            s_e = jnp.where(r4 == 3, s_row * down(e4, 1), 0.0)
            s_e_s = jnp.where((r4 == 3) & (c4 == 0), s_row * down(e_s, 1), 0.0)
        return eye_c - s2 - e4 + e_s + s_e - s_e_s

    row2 = lax.broadcasted_iota(jnp.int32, (LANES, LANES), 0)
    lane2 = lax.broadcasted_iota(jnp.int32, (LANES, LANES), 1)
    same_head = (row2 < HEAD_DIM) == (lane2 < HEAD_DIM)
    decay_scale = math.exp(-0.5)

    row0 = lambda o, d: (o if d == 0 else n_sub - 1 - o) * C
    keys = [(o, n, d) for o in range(n_sub) for n in range(n_seq) for d in range(N_DIR)]
    rk_all = {(o, n, d): rk_refs[d][n, row0(o, d):row0(o, d) + C, :].astype(F32) for o, n, d in keys}
    kkr_all = {key: rk_all[key][:, D_RWKV:] * kk_ref[...] for key in keys}
    ssq_rows = _bdot(jnp.concatenate([kkr_all[key] * kkr_all[key] for key in keys], axis=0), ones_ref[...])
    dir_keys = [[key for key in keys if key[2] == d] for d in range(N_DIR)]
    z_w_rows, z_a_rows = [], []
    for d in range(N_DIR):
        ll_rows = jnp.concatenate([lwla_refs[d][n, row0(o, d):row0(o, d) + C, :] for o, n, _ in dir_keys[d]], axis=0)
        z_w_rows.append(w0_ref[d:d + 1, :] + _bdot(jnp.tanh(ll_rows), w2_ref[d]))
        z_a_rows.append(a0_ref[d:d + 1, :] + _bdot(ll_rows, a2_ref[d]))

    sets = [[] for _ in range(n_sub)]
    for i, (o, n, d) in enumerate(keys):
        rk = rk_all[(o, n, d)]
        r = rk[:, :D_RWKV]
        k = rk[:, D_RWKV:]
        v = v_refs[d][n, row0(o, d):row0(o, d) + C, :]
        i_d = dir_keys[d].index((o, n, d))
        logw = -decay_scale * jax.nn.sigmoid(z_w_rows[d][i_d * C:(i_d + 1) * C])
        a = jax.nn.sigmoid(z_a_rows[d][i_d * C:(i_d + 1) * C])
        kd = k * (1.0 + (a - 1.0) * ka_ref[...])
        kkr = kkr_all[(o, n, d)]
        ssq = ssq_rows[i * C:(i + 1) * C]
        kk = kkr * lax.rsqrt(jnp.maximum(ssq, KK_EPS * KK_EPS))
        bvec = kk * a
        lw_hi, lw_lo = _split2(logw)
        tri = tri_ref[d]
        cum = (jnp.dot(tri, lw_hi, preferred_element_type=F32)
               + jnp.dot(tri, lw_lo, preferred_element_type=F32))
        cum_prev = cum - logw
        tot = cum[C - 1:C, :] if d == 0 else cum[0:1, :]
        kap_t = kk * jnp.exp(cum_prev)
        r_t = r * jnp.exp(cum)
        e_neg = jnp.exp(-cum)
        gam = jnp.exp(tot)
        b_t = bvec * e_neg
        k_t = kd * e_neg
        e_rem = gam * e_neg
        b_h = bvec * e_rem
        k_h = kd * e_rem
        if d == 0:
            strict, incl = row > s_idx, row >= s_idx
        else:
            strict, incl = row < s_idx, row <= s_idx

        for p in range(N_PAIRS):
            sl = slice(p * LANES, (p + 1) * LANES)
            dup_t = lambda x: jnp.where(same_head, jnp.concatenate([x[:, sl], x[:, sl]], axis=0).T, 0.0)
            sets[o].append(dict(
                n=n, d=d, p=p, sl=sl, rows=slice(row0(o, d), row0(o, d) + C), strict=strict, incl=incl,
                prev=None if o == 0 else sets[o - 1][len(sets[o])],
                lhs=jnp.concatenate([kap_t[:, sl], r_t[:, sl]], axis=0).astype(BF16),
                v=v[:, sl], w_lm=jnp.concatenate([dup_t(b_t), dup_t(k_t)], axis=1).astype(BF16),
                t2=jnp.concatenate([b_h[:, sl], k_h[:, sl]], axis=0).T.astype(BF16),
                gam_col=jnp.broadcast_to(gam[:, sl], (LANES, LANES)).T))

    bd = lambda y: _block_diag_rows(y, head0)

    def st_scores(probs):
        for q in probs:
            lm = jnp.dot(q["lhs"], q["w_lm"], preferred_element_type=F32)
            q["l_b"] = jnp.where(q["strict"], lm[:C, :LANES], 0.0)
            q["m_b"] = jnp.where(q["incl"], lm[C:, :LANES], 0.0)
            q["lm_k"] = jnp.concatenate([jnp.where(q["strict"], lm[:C, LANES:], 0.0),
                                         jnp.where(q["incl"], lm[C:, LANES:], 0.0)], axis=0)

    def st_values(probs):
        for q in probs:
            lmv = _bdot(q["lm_k"], bd(q["v"]))
            q["l_kv"], q["m_kv"] = lmv[:C], lmv[C:]
            l_b = q["l_b"]
            l_c = l_b[0:SUB]
            for jb in range(1, C // SUB):
                l_c = jnp.where(blk_c == jb, l_b[jb * SUB:(jb + 1) * SUB], l_c)
            q["l_c"] = l_c
            q["t_c"] = inverse_size4(l_c, q["d"] == 1)

    def st_compact_a(s):
        def run(probs):
            off_mask = _same_block(row_c, col_c, 2 * s) & ~_same_block(row_c, col_c, s)
            for q in probs:
                q["et"] = _bdot(jnp.where(off_mask, q["l_c"], 0.0), bd_c(q["t_c"]))
        return run

    def st_compact_b(last):
        def run(probs):
            for q in probs:
                q["t_c"] = q["t_c"] - _bdot(q["t_c"], bd_c(q["et"]))
                if last:
                    q["tinv"] = jnp.concatenate([jnp.where(blk_c == jb, q["t_c"], 0.0) for jb in range(C // SUB)],
                                                axis=0)
        return run

    def st_full_a(s):
        def run(probs):
            off_mask = _same_block(row, s_idx, 2 * s) & ~_same_block(row, s_idx, s)
            for q in probs:
                q["et"] = _bdot(jnp.where(off_mask, q["l_b"], 0.0), bd(q["tinv"]))
        return run

    def st_full_b(probs):
        for q in probs:
            q["tinv"] = q["tinv"] - _bdot(q["tinv"], bd(q["et"]))

    def st_state_read(probs):
        for q in probs:
            q["h"] = h_scr[q["n"], q["d"], q["p"]] if q["prev"] is None else q["prev"]["h_new"]
            q["kr_h"] = _bdot(q["lhs"], q["h"])

    def st_solve(probs):
        for q in probs:
            q["u_n"] = _bdot(q["tinv"], bd(q["kr_h"][:C] + q["l_kv"]))

    def st_output(probs):
        for q in probs:
            y = q["kr_h"][C:] + q["m_kv"] - _bdot(q["m_b"], bd(q["u_n"]))
            y_refs[q["d"]][q["n"], q["rows"], q["sl"]] = y.astype(BF16)

    def st_state_write(probs):
        for q in probs:
            upd = _bdot(q["t2"], jnp.concatenate([(-q["u_n"]).astype(BF16), q["v"]], axis=0))
            q["h_new"] = q["gam_col"] * q["h"] + jnp.where(same_head, upd, 0.0)

    stages = [st_scores, st_values] + ([st_state_read] if n_sub == 1 else [])
    s = 4
    while s < SUB:
        stages += [st_compact_a(s), st_compact_b(2 * s == SUB)]
        s *= 2
    while s < C:
        stages += [st_full_a(s), st_full_b]
        s *= 2
    tail = [st_state_read, st_solve, st_output, st_state_write]
    stages += tail[1:] if n_sub == 1 else tail
    skew = len(tail)
    for t in range(len(stages) + skew * (n_sub - 1)):
        for o in range(n_sub):
            if 0 <= t - skew * o < len(stages):
                stages[t - skew * o](sets[o])
    for q in sets[-1]:
        h_scr[q["n"], q["d"], q["p"]] = q["h_new"]

    if write_state:
        @pl.when(j == n_steps - 1)
        def _():
            for n, d, p in [(n, d, p) for n in range(n_seq) for d in range(N_DIR) for p in range(N_PAIRS)]:
                ht = h_scr[n, d, p].T
                sfin_refs[d][n, 0, PAIR * p] = ht[:HEAD_DIM, :HEAD_DIM]
                sfin_refs[d][n, 0, PAIR * p + 1] = ht[HEAD_DIM:, HEAD_DIM:]


def _scan_call(rk, v, lwla, wts, s0_bd, write_state):
    B, T, _ = rk.shape
    n_sub = min(SCAN_SUB, T // CHUNK)
    ns = min(max(1, SCAN_WORK // n_sub), B)
    rows = n_sub * CHUNK
    nc = T // rows
    assert B % ns == 0 and T % rows == 0
    zero_init = s0_bd is None
    fwd = lambda b, j: (b, j, 0)
    bwd = lambda b, j: (b, nc - 1 - j, 0)
    full = lambda *shape: pl.BlockSpec(shape, lambda b, j: (0,) * len(shape))
    tok = lambda w, m: pl.BlockSpec((ns, rows, w), m)
    in_specs = [tok(2 * D_RWKV, fwd), tok(D_RWKV, fwd), tok(LANES, fwd),
                tok(2 * D_RWKV, bwd), tok(D_RWKV, bwd), tok(LANES, bwd),
                full(N_DIR, D_RWKV), full(N_DIR, LANES, D_RWKV),
                full(N_DIR, D_RWKV), full(N_DIR, LANES, D_RWKV),
                full(1, D_RWKV), full(1, D_RWKV), full(N_DIR, CHUNK, CHUNK), full(D_RWKV, D_RWKV)]
    args = [rk, v, lwla, rk, v, lwla, wts["w0"], wts["w2"], wts["a0"], wts["a2"],
            wts["k_k"], wts["k_a"], wts["tri"], wts["ones_bd"]]
    state_block = (ns, N_DIR, N_PAIRS, LANES, LANES)
    if not zero_init:
        in_specs.append(pl.BlockSpec(state_block, lambda b, j: (b, 0, 0, 0, 0)))
        args.append(s0_bd)
    out_specs = [tok(D_RWKV, fwd), tok(D_RWKV, bwd)]
    out_shape = [jax.ShapeDtypeStruct((B, T, D_RWKV), BF16), jax.ShapeDtypeStruct((B, T, D_RWKV), BF16)]
    if write_state:
        final_block = (ns, 1, N_HEADS, HEAD_DIM, HEAD_DIM)
        out_specs += [pl.BlockSpec(final_block, lambda b, j: (b, 0, 0, 0, 0))] * N_DIR
        out_shape += [jax.ShapeDtypeStruct((B,) + final_block[1:], F32)] * N_DIR
    kern = functools.partial(_scan_kernel, zero_init=zero_init, write_state=write_state, n_steps=nc,
                             n_seq=ns, n_sub=n_sub)
    return pl.pallas_call(
        kern, grid=(B // ns, nc), in_specs=in_specs, out_specs=out_specs, out_shape=out_shape,
        scratch_shapes=[pltpu.VMEM(state_block, F32)],
        compiler_params=pltpu.CompilerParams(dimension_semantics=("arbitrary", "arbitrary"),
                                             vmem_limit_bytes=VMEM_LIMIT),
        name="scan",
    )(*args)


def _fnet_kernel(fin_ref, gf_ref, dft_ref, mid_ref, rev_ref, cs_ref, wbd_ref, b_ref, o_ref, g_scr, csw_scr, d_scr,
                 *, mirrored):
    u = pl.program_id(1)
    nb, seq_len, _ = fin_ref.shape
    tu = o_ref.shape[1]

    @pl.when((pl.program_id(0) == 0) & (u == 0))
    def _():
        for half in range(2):
            cols = slice(half * D_FNET, (half + 1) * D_FNET)
            csw_scr[:, cols] = _bdot(cs_ref[:, cols], wbd_ref[...]).astype(BF16)

    @pl.when(u == 0)
    def _():
        step = min(seq_len, ROW_TILE)
        for n in range(nb):
            for t0 in range(0, seq_len, step):
                fc = _bdot(fin_ref[n, t0:t0 + step, :], csw_scr[...])
                g_scr[n, t0:t0 + step, :] = fc[:, :D_FNET].astype(BF16)
                g_scr[n, seq_len + t0:seq_len + t0 + step, :] = fc[:, D_FNET:].astype(BF16)

    def finish(f_re):
        gate = _silu(gf_ref[...].astype(F32)).reshape(nb * tu, D_FNET)
        o_ref[...] = ((f_re + b_ref[...]) * gate).astype(BF16).reshape(nb, tu, D_FNET)

    if not mirrored:
        finish(jnp.concatenate([jnp.dot(dft_ref[...], g_scr[n], preferred_element_type=F32) for n in range(nb)],
                               axis=0))
        return

    @pl.when(u == 0)
    def _():
        parts = []
        for n in range(nb):
            c_part = jnp.dot(dft_ref[:, :seq_len], g_scr[n, :seq_len], preferred_element_type=F32)
            s_part = jnp.dot(dft_ref[:, seq_len:], g_scr[n, seq_len:], preferred_element_type=F32)
            d_scr[n] = (c_part - s_part).astype(BF16)
            parts.append(c_part + s_part)
        finish(jnp.concatenate(parts, axis=0))

    @pl.when(u == 1)
    def _():
        parts = []
        for n in range(nb):
            mid = jnp.dot(mid_ref[...], g_scr[n], preferred_element_type=F32)[0:1, :]
            rev = jnp.dot(rev_ref[...], d_scr[n], preferred_element_type=F32)
            row = lax.broadcasted_iota(jnp.int32, rev.shape, 0)
            parts.append(jnp.where(row == 0, mid, rev))
        finish(jnp.concatenate(parts, axis=0))


def _fnet_call(fin, gf, dft_bf16, cs_bf16, wbd_bf16, b_fnet):
    B, T, _ = fin.shape
    tu = min(T, FNET_ROWS)
    nb = max(1, FNET_ROWS // T)
    n_tiles = T // tu
    assert B % nb == 0 and T % tu == 0 and n_tiles in (1, 2)
    mirrored = n_tiles == 2
    mid_rows = 2 * SUBLANES
    if mirrored:
        rev_np = np.zeros((tu, tu), np.float32)
        rev_np[np.arange(1, tu), tu - np.arange(1, tu)] = 1.0
        rev = jnp.asarray(rev_np).astype(BF16)
        d_shape = (nb, tu, D_FNET)
    else:
        rev = jnp.zeros((mid_rows, LANES), BF16)
        d_shape = (1, mid_rows, LANES)
    return pl.pallas_call(
        functools.partial(_fnet_kernel, mirrored=mirrored),
        grid=(B // nb, n_tiles),
        in_specs=[pl.BlockSpec((nb, T, D_FNET), lambda b, u: (b, 0, 0)),
                  pl.BlockSpec((nb, tu, D_FNET), lambda b, u: (b, u, 0)),
                  pl.BlockSpec((tu, 2 * T), lambda b, u: (0, 0)),
                  pl.BlockSpec((mid_rows, 2 * T), lambda b, u: (tu // mid_rows if mirrored else 0, 0)),
                  pl.BlockSpec(rev.shape, lambda b, u: (0, 0)),
                  pl.BlockSpec((D_FNET, 2 * D_FNET), lambda b, u: (0, 0)),
                  pl.BlockSpec((D_FNET, D_FNET), lambda b, u: (0, 0)),
                  pl.BlockSpec((1, D_FNET), lambda b, u: (0, 0))],
        out_specs=pl.BlockSpec((nb, tu, D_FNET), lambda b, u: (b, u, 0)),
        out_shape=jax.ShapeDtypeStruct((B, T, D_FNET), BF16),
        scratch_shapes=[pltpu.VMEM((nb, 2 * T, D_FNET), BF16), pltpu.VMEM((D_FNET, 2 * D_FNET), BF16),
                        pltpu.VMEM(d_shape, BF16)],
        compiler_params=pltpu.CompilerParams(dimension_semantics=("arbitrary", "arbitrary"),
                                             vmem_limit_bytes=VMEM_LIMIT),
        name="fnet",
    )(fin, gf, dft_bf16, dft_bf16, rev, cs_bf16, wbd_bf16, b_fnet)


def _out_kernel(*refs, has_emb, final_norm):
    it = iter(refs)
    x_ref = next(it)
    emb_ref = next(it) if has_emb else None
    (mod_ref, yf_ref, yb_ref, rk_ref, v_ref, lwla_ref, grec_ref, fo_ref, a0_ref, a2_ref, ka_ref,
     rkw_ref, gng_ref, gnb_ref, avg_ref, ones_ref, wout_ref, fng_ref, o_ref) = (next(it) for _ in range(19))

    nb, tt, _ = x_ref.shape
    tm = nb * tt
    rows = lambda ref: ref[...].reshape(tm, ref.shape[-1])
    y = rows(yf_ref).astype(F32) + rows(yb_ref).astype(F32)
    mu = _bdot(y, avg_ref[...])
    dlt = y - mu
    var = _bdot(dlt * dlt, avg_ref[...])
    y_n = dlt * lax.rsqrt(var + GN_EPS) * gng_ref[...] + gnb_ref[...]
    rk = rows(rk_ref).astype(F32)
    r = rk[:, :D_RWKV]
    k = rk[:, D_RWKV:]
    ll = rows(lwla_ref)
    a_sum = (jax.nn.sigmoid(a0_ref[0:1, :] + _bdot(ll, a2_ref[0]))
             + jax.nn.sigmoid(a0_ref[1:2, :] + _bdot(ll, a2_ref[1])))
    k_sum = k * (2.0 + (a_sum - 2.0) * ka_ref[...])
    bonus = _bdot(r * k_sum * rkw_ref[...], ones_ref[...]) * rows(v_ref).astype(F32)
    rec_out = (y_n + bonus) * _silu(rows(grec_ref).astype(F32))
    mixed = jnp.concatenate([rec_out.astype(BF16), rows(fo_ref)], axis=-1)
    out = jnp.dot(mixed, wout_ref[...], preferred_element_type=F32)
    x = rows(x_ref)
    if has_emb:
        x = x + emb_ref[...]
    z = x + mod_ref[0, 2:3, :] * out
    if final_norm:
        ms = jnp.mean(z * z, axis=-1, keepdims=True)
        z = z * lax.rsqrt(ms + NORM_EPS) * fng_ref[...]
    o_ref[...] = z.reshape(nb, tt, D_MODEL)


def _out_call(x, emb, mod, y_f, y_b, rk, v, lwla, grec, fo, wts, w_out_bf16, final_norm_g, final_norm):
    B, T, _ = x.shape
    tt = min(T, ROW_TILE)
    nb = max(1, ROW_TILE // T)
    has_emb = emb is not None
    per_batch_mod = mod.shape[0] > 1
    assert B % nb == 0 and T % tt == 0 and (nb == 1 or not (has_emb or per_batch_mod))
    tok = lambda w: pl.BlockSpec((nb, tt, w), lambda b, i: (b, i, 0))
    full = lambda *shape: pl.BlockSpec(shape, lambda b, i: (0,) * len(shape))
    in_specs = [tok(D_MODEL)]
    args = [x]
    if has_emb:
        in_specs.append(pl.BlockSpec((tt, D_MODEL), lambda b, i: (i, 0)))
        args.append(emb)
    mod_map = (lambda b, i: (b, 0, 0)) if per_batch_mod else (lambda b, i: (0, 0, 0))
    in_specs += [pl.BlockSpec((1, 3, D_MODEL), mod_map), tok(D_RWKV), tok(D_RWKV), tok(2 * D_RWKV),
                 tok(D_RWKV), tok(LANES), tok(D_RWKV), tok(D_FNET),
                 full(N_DIR, D_RWKV), full(N_DIR, LANES, D_RWKV),
                 full(1, D_RWKV), full(1, D_RWKV), full(1, D_RWKV), full(1, D_RWKV),
                 full(D_RWKV, D_RWKV), full(D_RWKV, D_RWKV), full(D_MODEL, D_MODEL), full(1, D_MODEL)]
    args += [mod, y_f, y_b, rk, v, lwla, grec, fo, wts["a0"], wts["a2"], wts["k_a"],
             wts["r_k"], wts["gn_g"], wts["gn_b"], wts["avg_bd"], wts["ones_bd"], w_out_bf16, final_norm_g]
    return pl.pallas_call(
        functools.partial(_out_kernel, has_emb=has_emb, final_norm=final_norm),
        grid=(B // nb, T // tt), in_specs=in_specs,
        out_specs=tok(D_MODEL),
        out_shape=jax.ShapeDtypeStruct((B, T, D_MODEL), F32),
        compiler_params=pltpu.CompilerParams(dimension_semantics=("arbitrary", "arbitrary"),
                                             vmem_limit_bytes=VMEM_LIMIT),
        name="out_proj",
    )(*args)


def _dft_table(seq_len):
    idx = np.arange(seq_len, dtype=np.int64)
    ang = 2.0 * np.pi * ((idx[:, None] * idx[None, :]) % seq_len).astype(np.float64) / seq_len
    scale = 1.0 / math.sqrt(seq_len)
    return np.concatenate([np.cos(ang) * scale, -np.sin(ang) * scale], axis=1).astype(np.float32)


def _channel_dft_table():
    n = FNET_GROUP
    idx = np.arange(n, dtype=np.int64)
    ang = 2.0 * np.pi * ((idx[:, None] * idx[None, :]) % n).astype(np.float64) / n
    c = np.cos(ang) / math.sqrt(n)
    s = np.sin(ang) / math.sqrt(n)
    eye = np.eye(D_FNET // n)
    return np.concatenate([np.kron(eye, c), np.kron(eye, s)], axis=1).astype(np.float32)


def _sincos_2d(n_tokens):
    rows = n_tokens // GRID_W
    pos = np.arange(rows * GRID_W)
    row = (pos // GRID_W).astype(np.float32)
    col = (pos % GRID_W).astype(np.float32)
    quarter = D_MODEL // 4
    freq = np.exp(np.float32(-math.log(POS_BASE)) * np.arange(quarter, dtype=np.float32) / np.float32(quarter))
    ang_r = row[:, None] * freq
    ang_c = col[:, None] * freq
    return np.concatenate([np.sin(ang_r), np.cos(ang_r), np.sin(ang_c), np.cos(ang_c)], axis=-1).astype(np.float32)


def _head_block_matrix(value):
    blk = np.kron(np.eye(N_HEADS), np.ones((HEAD_DIM, HEAD_DIM))) * value
    return jnp.asarray(blk, dtype=BF16)


def _pad_lora(w, row_offset):
    rows = [jnp.pad(w[d], ((row_offset + d * LORA, LANES - row_offset - (d + 1) * LORA), (0, 0)))
            for d in range(N_DIR)]
    return jnp.stack(rows).astype(BF16)


def _layer_weights(l, w0, w2, a0, a2, k_k, k_a, r_k, gn_g, gn_b):
    w2_p = _pad_lora(w2[l], 0)
    a2_p = _pad_lora(a2[l], N_DIR * LORA)
    tri_f = np.tril(np.ones((CHUNK, CHUNK)))
    tri = jnp.asarray(np.stack([tri_f, tri_f.T]), dtype=BF16)
    return dict(w0=w0[l], w2=w2_p, a0=a0[l], a2=a2_p,
                k_k=k_k[l][None], k_a=k_a[l][None], r_k=r_k[l].reshape(1, D_RWKV),
                gn_g=gn_g[l].reshape(1, D_RWKV), gn_b=gn_b[l].reshape(1, D_RWKV), tri=tri,
                ones_bd=_head_block_matrix(1.0), avg_bd=_head_block_matrix(1.0 / HEAD_DIM))


def _state_to_block_diag(s_f, s_b):
    def one(s):
        h = jnp.swapaxes(s.astype(F32), -1, -2)
        b = h.shape[0]
        h = h.reshape(b, N_PAIRS, PAIR, HEAD_DIM, HEAD_DIM)
        z = jnp.zeros_like(h[:, :, 0])
        top = jnp.concatenate([h[:, :, 0], z], axis=-1)
        bot = jnp.concatenate([z, h[:, :, 1]], axis=-1)
        return jnp.concatenate([top, bot], axis=-2)
    return jnp.stack([one(s_f), one(s_b)], axis=1)


def kernel(x_prompt, x_sample, state_rwkv_fwd, state_rwkv_bwd, c, c_ctx, w_ada, b_ada, norm_g, w_in,
           mu_shift, w0, w2, a0, a2, k_k, k_a, r_k, gn_g, gn_b, w_fnet, b_fnet, w_out, final_norm_g):
    depth = w_in.shape[0]
    n_dec = c.shape[0]
    assert n_dec + 1 <= SUBLANES
    bp, tp, _ = x_prompt.shape
    bs, ts, _ = x_sample.shape
    cvec = jnp.concatenate([c_ctx[None], c, jnp.zeros((SUBLANES - 1 - n_dec, D_MODEL), F32)], axis=0)
    emb = jnp.asarray(_sincos_2d(ts)).astype(x_sample.dtype)
    cs_tab = jnp.asarray(_channel_dft_table()).astype(BF16)
    dft_p = jnp.asarray(_dft_table(tp)).astype(BF16)
    dft_s = jnp.asarray(_dft_table(ts)).astype(BF16)
    fng = final_norm_g[None]

    xp, xs = x_prompt, x_sample
    new_f, new_b = [], []
    for l in range(depth):
        mod = _mod_call(cvec, w_ada[l], b_ada[l][None]).reshape(SUBLANES, 3, D_MODEL)
        mod_ctx, mod_lat = mod[0:1], mod[1:1 + n_dec]
        wts = _layer_weights(l, w0, w2, a0, a2, k_k, k_a, r_k, gn_g, gn_b)
        w_in_b = w_in[l].astype(BF16)
        w_out_b = w_out[l].astype(BF16)
        n_grp = w_fnet.shape[1]
        wbd = (w_fnet[l][:, :, None, :] * jnp.eye(n_grp, dtype=F32)[:, None, :, None]).reshape(
            D_FNET, D_FNET).astype(BF16)
        ng, mu, bf = norm_g[l][None], mu_shift[l][None], b_fnet[l][None]
        emb_l = emb if l == 0 else None

        rk, v, lwla, grec, fin, gf = _in_proj_call(xp, None, mod_ctx, ng, w_in_b, mu)
        y_f, y_b, s_f, s_b = _scan_call(rk, v, lwla, wts, None, True)
        fo = _fnet_call(fin, gf, dft_p, cs_tab, wbd, bf)
        last = l == depth - 1
        xp = _out_call(xp, None, mod_ctx, y_f, y_b, rk, v, lwla, grec, fo, wts, w_out_b, fng, last)
        new_f.append(s_f)
        new_b.append(s_b)

        rk, v, lwla, grec, fin, gf = _in_proj_call(xs, emb_l, mod_lat, ng, w_in_b, mu)
        s0 = _state_to_block_diag(state_rwkv_fwd[:, l], state_rwkv_bwd[:, l])
        y_f, y_b = _scan_call(rk, v, lwla, wts, s0, False)
        fo = _fnet_call(fin, gf, dft_s, cs_tab, wbd, bf)
        xs = _out_call(xs, emb_l, mod_lat, y_f, y_b, rk, v, lwla, grec, fo, wts, w_out_b, fng, last)
    return (xp, xs, jnp.concatenate(new_f, axis=1), jnp.concatenate(new_b, axis=1))
```

```python
import functools
import math

import numpy as np
import jax
import jax.numpy as jnp
from jax import lax
from jax.experimental import pallas as pl
from jax.experimental.pallas import tpu as pltpu

F32 = jnp.float32
BF16 = jnp.bfloat16

D_MODEL = 1024
GRID_W = 64
D_RWKV = 512
D_FNET = D_MODEL - D_RWKV
HEAD_DIM = 64
N_HEADS = D_RWKV // HEAD_DIM
FNET_GROUP = 64
LORA = 32
N_DIR = 2
D_SHIFT = 3 * D_RWKV + N_DIR * 2 * LORA
D_IN = D_SHIFT + D_RWKV + 2 * D_FNET
NORM_EPS = 1e-6
KK_EPS = 1e-12
GN_EPS = 64e-5
POS_BASE = 10000.0

LANES = 128
SUBLANES = 8
PAIR = LANES // HEAD_DIM
N_PAIRS = N_HEADS // PAIR
CHUNK = 64
SUB = 16
SUB_SHIFT = SUB.bit_length() - 1
SCAN_WORK = 8
SCAN_SUB = 4
ROW_TILE = 512
ROW_SPLIT = 4
FNET_ROWS = 1024
HALO = 8
X_RING = 3
VMEM_LIMIT = 56 * 1024 * 1024


def _silu(x):
    return x * jax.nn.sigmoid(x)


def _bdot(a, b):
    return jnp.dot(a.astype(BF16), b.astype(BF16), preferred_element_type=F32)


def _split2(x):
    hi = x.astype(BF16)
    lo = (x - hi.astype(F32)).astype(BF16)
    return hi, lo


def _mod_kernel(c_ref, w_ref, b_ref, o_ref):
    @pl.when(pl.program_id(0) == 0)
    def _():
        o_ref[...] = jnp.broadcast_to(b_ref[...], o_ref.shape)

    s_hi, s_lo = _split2(_silu(c_ref[...]))
    w = w_ref[...].astype(BF16)
    o_ref[...] += jnp.dot(s_hi, w, preferred_element_type=F32) + jnp.dot(s_lo, w, preferred_element_type=F32)


def _mod_call(cvec, w_ada, b_ada):
    rows = D_MODEL // 4
    return pl.pallas_call(
        _mod_kernel,
        grid=(D_MODEL // rows,),
        in_specs=[pl.BlockSpec((SUBLANES, rows), lambda i: (0, i)),
                  pl.BlockSpec((rows, 3 * D_MODEL), lambda i: (i, 0)),
                  pl.BlockSpec((1, 3 * D_MODEL), lambda i: (0, 0))],
        out_specs=pl.BlockSpec((SUBLANES, 3 * D_MODEL), lambda i: (0, 0)),
        out_shape=jax.ShapeDtypeStruct((SUBLANES, 3 * D_MODEL), F32),
        compiler_params=pltpu.CompilerParams(dimension_semantics=("arbitrary",),
                                             vmem_limit_bytes=VMEM_LIMIT),
        name="mod",
    )(cvec, w_ada, b_ada)


def _modulated_norm(x, g, scale, shift):
    ms = jnp.mean(x * x, axis=-1, keepdims=True)
    y = x * lax.rsqrt(ms + NORM_EPS) * g
    return y * (1.0 + scale) + shift


def _in_proj_kernel(*refs, has_emb, has_halo, n_tiles):
    it = iter(refs)
    x_ref = next(it)
    xp_ref = next(it) if has_halo else None
    xn_ref = next(it) if has_halo else None
    emb_ref = next(it) if has_emb else None
    embp_ref = next(it) if (has_emb and has_halo) else None
    embn_ref = next(it) if (has_emb and has_halo) else None
    mod_ref, g_ref, w_ref, mu_ref = next(it), next(it), next(it), next(it)
    rk_ref, v_ref, lwla_ref, grec_ref, fin_ref, gf_ref = (next(it) for _ in range(6))

    i = pl.program_id(1)
    g = g_ref[...]
    shift = mod_ref[0, 0:1, :]
    scale = mod_ref[0, 1:2, :]
    nb, tt, _ = x_ref.shape
    tm = nb * tt
    x = x_ref[...].reshape(tm, D_MODEL)
    if has_emb:
        x = x + emb_ref[...]
    part = tm // ROW_SPLIT
    p = jnp.concatenate(
        [jnp.dot(_modulated_norm(x[r0:r0 + part], g, scale, shift).astype(BF16), w_ref[...],
                 preferred_element_type=F32) for r0 in range(0, tm, part)], axis=0)
    ps = p[:, :D_SHIFT]
    if has_halo:
        xh = jnp.concatenate([xp_ref[0], xn_ref[0]], axis=0)
        if has_emb:
            xh = xh + jnp.concatenate([embp_ref[...], embn_ref[...]], axis=0)
        hh = _modulated_norm(xh, g, scale, shift)
        ph = jnp.dot(hh.astype(BF16), w_ref[:, :D_SHIFT], preferred_element_type=F32)
        prev_row = jnp.where(i > 0, ph[HALO - 1:HALO, :], 0.0)
        next_row = jnp.where(i < n_tiles - 1, ph[HALO:HALO + 1, :], 0.0)
    else:
        prev_row = jnp.zeros((1, D_SHIFT), F32)
        next_row = jnp.zeros((1, D_SHIFT), F32)
    pos = lax.rem(lax.broadcasted_iota(jnp.int32, (tm, D_SHIFT), 0), tt)
    prev = jnp.where(pos == 0, prev_row, pltpu.roll(ps, 1, 0))
    nxt = jnp.where(pos == tt - 1, next_row, pltpu.roll(ps, tm - 1, 0))
    p_rec = ps + mu_ref[...] * (0.5 * (prev + nxt) - ps)

    def put(ref, val):
        ref[...] = val.astype(ref.dtype).reshape(ref.shape)

    put(rk_ref, p_rec[:, :2 * D_RWKV])
    put(v_ref, p_rec[:, 2 * D_RWKV:3 * D_RWKV])
    put(lwla_ref, p_rec[:, 3 * D_RWKV:])
    put(grec_ref, p[:, D_SHIFT:D_SHIFT + D_RWKV])
    put(fin_ref, p[:, D_SHIFT + D_RWKV:D_SHIFT + D_RWKV + D_FNET])
    put(gf_ref, p[:, D_SHIFT + D_RWKV + D_FNET:])


def _in_proj_call(x, emb, mod, norm_g, w_in_bf16, mu):
    B, T, _ = x.shape
    tt = min(T, ROW_TILE)
    nb = max(1, ROW_TILE // T)
    n_tiles = T // tt
    has_halo = n_tiles > 1
    has_emb = emb is not None
    per_batch_mod = mod.shape[0] > 1
    assert B % nb == 0 and T % tt == 0 and (nb == 1 or not (has_emb or per_batch_mod))
    tm = tt
    blocks_per_tile = tm // HALO
    last_halo_block = T // HALO - 1

    in_specs = [pl.BlockSpec((nb, tt, D_MODEL), lambda b, i: (b, i, 0))]
    args = [x]
    if has_halo:
        in_specs += [
            pl.BlockSpec((1, HALO, D_MODEL), lambda b, i: (b, jnp.maximum(i * blocks_per_tile - 1, 0), 0)),
            pl.BlockSpec((1, HALO, D_MODEL),
                         lambda b, i: (b, jnp.minimum((i + 1) * blocks_per_tile, last_halo_block), 0))]
        args += [x, x]
    if has_emb:
        in_specs.append(pl.BlockSpec((tm, D_MODEL), lambda b, i: (i, 0)))
        args.append(emb)
        if has_halo:
            in_specs += [
                pl.BlockSpec((HALO, D_MODEL), lambda b, i: (jnp.maximum(i * blocks_per_tile - 1, 0), 0)),
                pl.BlockSpec((HALO, D_MODEL),
                             lambda b, i: (jnp.minimum((i + 1) * blocks_per_tile, last_halo_block), 0))]
            args += [emb, emb]
    mod_map = (lambda b, i: (b, 0, 0)) if per_batch_mod else (lambda b, i: (0, 0, 0))
    in_specs += [pl.BlockSpec((1, 3, D_MODEL), mod_map),
                 pl.BlockSpec((1, D_MODEL), lambda b, i: (0, 0)),
                 pl.BlockSpec((D_MODEL, D_IN), lambda b, i: (0, 0), pipeline_mode=pl.Buffered(1)),
                 pl.BlockSpec((1, D_SHIFT), lambda b, i: (0, 0))]
    args += [mod, norm_g, w_in_bf16, mu]
    outs = ((2 * D_RWKV, BF16), (D_RWKV, BF16), (N_DIR * 2 * LORA, F32), (D_RWKV, BF16), (D_FNET, BF16),
            (D_FNET, BF16))
    out_specs = [pl.BlockSpec((nb, tt, w), lambda b, i: (b, i, 0)) for w, _ in outs]
    out_shape = [jax.ShapeDtypeStruct((B, T, w), dt) for w, dt in outs]
    kern = functools.partial(_in_proj_kernel, has_emb=has_emb, has_halo=has_halo, n_tiles=n_tiles)
    return pl.pallas_call(
        kern, grid=(B // nb, n_tiles), in_specs=in_specs, out_specs=out_specs, out_shape=out_shape,
        compiler_params=pltpu.CompilerParams(dimension_semantics=("arbitrary", "arbitrary"),
                                             vmem_limit_bytes=VMEM_LIMIT),
        name="in_proj",
    )(*args)


def _same_block(i, j, size):
    shift = size.bit_length() - 1
    return jnp.right_shift(i, shift) == jnp.right_shift(j, shift)


def _block_diag_rows(y, head0_lanes):
    return jnp.concatenate([jnp.where(head0_lanes, y, 0.0), jnp.where(head0_lanes, 0.0, y)], axis=0)


def _scan_kernel(*refs, zero_init, write_state, n_steps, n_seq, n_sub):
    it = iter(refs)
    rk_refs, v_refs, lwla_refs = [None, None], [None, None], [None, None]
    for d in range(N_DIR):
        rk_refs[d], v_refs[d], lwla_refs[d] = next(it), next(it), next(it)
    w0_ref, w2_ref, a0_ref, a2_ref = (next(it) for _ in range(4))
    kk_ref, ka_ref, tri_ref, ones_ref = (next(it) for _ in range(4))
    s0_ref = None if zero_init else next(it)
    y_refs = [next(it), next(it)]
    sfin_refs = [next(it), next(it)] if write_state else None
    h_scr = next(it)

    j = pl.program_id(1)

    @pl.when(j == 0)
    def _():
        if zero_init:
            h_scr[...] = jnp.zeros_like(h_scr)
        else:
            h_scr[...] = s0_ref[...]

    C = CHUNK
    row = lax.broadcasted_iota(jnp.int32, (C, LANES), 0)
    lane = lax.broadcasted_iota(jnp.int32, (C, LANES), 1)
    s_idx = jnp.bitwise_and(lane, HEAD_DIM - 1)
    head0 = lane < HEAD_DIM
    row_c = lax.broadcasted_iota(jnp.int32, (SUB, LANES), 0)
    lane_c = lax.broadcasted_iota(jnp.int32, (SUB, LANES), 1)
    col_c = jnp.bitwise_and(lane_c, SUB - 1)
    blk_c = jnp.right_shift(jnp.bitwise_and(lane_c, HEAD_DIM - 1), SUB_SHIFT)
    lane_blk_c = jnp.right_shift(lane_c, SUB_SHIFT)
    eye_c = (row_c == col_c).astype(F32)

    def bd_c(y):
        return jnp.concatenate([jnp.where(lane_blk_c == g, y, 0.0) for g in range(LANES // SUB)], axis=0)

    def inverse_size4(l_c, upper):
        r4 = jnp.bitwise_and(row_c, 3)
        c4 = jnp.bitwise_and(col_c, 3)
        down, up = (lambda x, k: pltpu.roll(x, k, 0)), (lambda x, k: pltpu.roll(x, SUB - k, 0))
        right, left = (lambda x, k: pltpu.roll(x, k, 1)), (lambda x, k: pltpu.roll(x, LANES - k, 1))
        s2 = jnp.where(_same_block(row_c, col_c, 2), l_c, 0.0)
        e4 = jnp.where(_same_block(row_c, col_c, 4), l_c, 0.0) - s2
        if upper:
            s_col = jnp.where(r4 == 1, up(s2, 1), up(s2, 2))
            s_row = jnp.where(c4 == 2, right(s2, 1), right(s2, 2))
            e_s = jnp.where(c4 == 3, right(e4, 1) * s_col, 0.0)
            s_e = jnp.where(r4 == 0, s_row * up(e4, 1), 0.0)
            s_e_s = jnp.where((r4 == 0) & (c4 == 3), s_row * up(e_s, 1), 0.0)
        else:
            s_col = jnp.where(r4 == 2, down(s2, 1), down(s2, 2))
            s_row = jnp.where(c4 == 1, left(s2, 1), left(s2, 2))
            e_s = jnp.where(c4 == 0, left(e4, 1) * s_col, 0.0)
            s_e = jnp.where(r4 == 3, s_row * down(e4, 1), 0.0)
            s_e_s = jnp.where((r4 == 3) & (c4 == 0), s_row * down(e_s, 1), 0.0)
        return eye_c - s2 - e4 + e_s + s_e - s_e_s

    row2 = lax.broadcasted_iota(jnp.int32, (LANES, LANES), 0)
    lane2 = lax.broadcasted_iota(jnp.int32, (LANES, LANES), 1)
    same_head = (row2 < HEAD_DIM) == (lane2 < HEAD_DIM)
    decay_scale = math.exp(-0.5)

    row0 = lambda o, d: (o if d == 0 else n_sub - 1 - o) * C
    keys = [(o, n, d) for o in range(n_sub) for n in range(n_seq) for d in range(N_DIR)]
    rk_all = {(o, n, d): rk_refs[d][n, row0(o, d):row0(o, d) + C, :].astype(F32) for o, n, d in keys}
    kkr_all = {key: rk_all[key][:, D_RWKV:] * kk_ref[...] for key in keys}
    ssq_rows = _bdot(jnp.concatenate([kkr_all[key] * kkr_all[key] for key in keys], axis=0), ones_ref[...])
    dir_keys = [[key for key in keys if key[2] == d] for d in range(N_DIR)]
    z_w_rows, z_a_rows = [], []
    for d in range(N_DIR):
        ll_rows = jnp.concatenate([lwla_refs[d][n, row0(o, d):row0(o, d) + C, :] for o, n, _ in dir_keys[d]], axis=0)
        z_w_rows.append(w0_ref[d:d + 1, :] + _bdot(jnp.tanh(ll_rows), w2_ref[d]))
        z_a_rows.append(a0_ref[d:d + 1, :] + _bdot(ll_rows, a2_ref[d]))

    sets = [[] for _ in range(n_sub)]
    for i, (o, n, d) in enumerate(keys):
        rk = rk_all[(o, n, d)]
        r = rk[:, :D_RWKV]
        k = rk[:, D_RWKV:]
        v = v_refs[d][n, row0(o, d):row0(o, d) + C, :]
        i_d = dir_keys[d].index((o, n, d))
        logw = -decay_scale * jax.nn.sigmoid(z_w_rows[d][i_d * C:(i_d + 1) * C])
        a = jax.nn.sigmoid(z_a_rows[d][i_d * C:(i_d + 1) * C])
        kd = k * (1.0 + (a - 1.0) * ka_ref[...])
        kkr = kkr_all[(o, n, d)]
        ssq = ssq_rows[i * C:(i + 1) * C]
        kk = kkr * lax.rsqrt(jnp.maximum(ssq, KK_EPS * KK_EPS))
        bvec = kk * a
        lw_hi, lw_lo = _split2(logw)
        tri = tri_ref[d]
        cum = (jnp.dot(tri, lw_hi, preferred_element_type=F32)
               + jnp.dot(tri, lw_lo, preferred_element_type=F32))
        cum_prev = cum - logw
        tot = cum[C - 1:C, :] if d == 0 else cum[0:1, :]
        kap_t = kk * jnp.exp(cum_prev)
        r_t = r * jnp.exp(cum)
        e_neg = jnp.exp(-cum)
        gam = jnp.exp(tot)
        b_t = bvec * e_neg
        k_t = kd * e_neg
        e_rem = gam * e_neg
        b_h = bvec * e_rem
        k_h = kd * e_rem
        if d == 0:
            strict, incl = row > s_idx, row >= s_idx
        else:
            strict, incl = row < s_idx, row <= s_idx

        for p in range(N_PAIRS):
            sl = slice(p * LANES, (p + 1) * LANES)
            dup_t = lambda x: jnp.where(same_head, jnp.concatenate([x[:, sl], x[:, sl]], axis=0).T, 0.0)
            sets[o].append(dict(
                n=n, d=d, p=p, sl=sl, rows=slice(row0(o, d), row0(o, d) + C), strict=strict, incl=incl,
                prev=None if o == 0 else sets[o - 1][len(sets[o])],
                lhs=jnp.concatenate([kap_t[:, sl], r_t[:, sl]], axis=0).astype(BF16),
                v=v[:, sl], w_lm=jnp.concatenate([dup_t(b_t), dup_t(k_t)], axis=1).astype(BF16),
                t2=jnp.concatenate([b_h[:, sl], k_h[:, sl]], axis=0).T.astype(BF16),
                gam_col=jnp.broadcast_to(gam[:, sl], (LANES, LANES)).T))

    bd = lambda y: _block_diag_rows(y, head0)

    def st_scores(probs):
        for q in probs:
            lm = jnp.dot(q["lhs"], q["w_lm"], preferred_element_type=F32)
            q["l_b"] = jnp.where(q["strict"], lm[:C, :LANES], 0.0)
            q["m_b"] = jnp.where(q["incl"], lm[C:, :LANES], 0.0)
            q["lm_k"] = jnp.concatenate([jnp.where(q["strict"], lm[:C, LANES:], 0.0),
                                         jnp.where(q["incl"], lm[C:, LANES:], 0.0)], axis=0)

    def st_values(probs):
        for q in probs:
            lmv = _bdot(q["lm_k"], bd(q["v"]))
            q["l_kv"], q["m_kv"] = lmv[:C], lmv[C:]
            l_b = q["l_b"]
            l_c = l_b[0:SUB]
            for jb in range(1, C // SUB):
                l_c = jnp.where(blk_c == jb, l_b[jb * SUB:(jb + 1) * SUB], l_c)
            q["l_c"] = l_c
            q["t_c"] = inverse_size4(l_c, q["d"] == 1)

    def st_compact_a(s):
        def run(probs):
            off_mask = _same_block(row_c, col_c, 2 * s) & ~_same_block(row_c, col_c, s)
            for q in probs:
                q["et"] = _bdot(jnp.where(off_mask, q["l_c"], 0.0), bd_c(q["t_c"]))
        return run

    def st_compact_b(last):
        def run(probs):
            for q in probs:
                q["t_c"] = q["t_c"] - _bdot(q["t_c"], bd_c(q["et"]))
                if last:
                    q["tinv"] = jnp.concatenate([jnp.where(blk_c == jb, q["t_c"], 0.0) for jb in range(C // SUB)],
                                                axis=0)
        return run

    def st_full_a(s):
        def run(probs):
            off_mask = _same_block(row, s_idx, 2 * s) & ~_same_block(row, s_idx, s)
            for q in probs:
                q["et"] = _bdot(jnp.where(off_mask, q["l_b"], 0.0), bd(q["tinv"]))
        return run

    def st_full_b(probs):
        for q in probs:
            q["tinv"] = q["tinv"] - _bdot(q["tinv"], bd(q["et"]))

    def st_state_read(probs):
        for q in probs:
            q["h"] = h_scr[q["n"], q["d"], q["p"]] if q["prev"] is None else q["prev"]["h_new"]
            q["kr_h"] = _bdot(q["lhs"], q["h"])

    def st_solve(probs):
        for q in probs:
            q["u_n"] = _bdot(q["tinv"], bd(q["kr_h"][:C] + q["l_kv"]))

    def st_output(probs):
        for q in probs:
            y = q["kr_h"][C:] + q["m_kv"] - _bdot(q["m_b"], bd(q["u_n"]))
            y_refs[q["d"]][q["n"], q["rows"], q["sl"]] = y.astype(BF16)

    def st_state_write(probs):
        for q in probs:
            upd = _bdot(q["t2"], jnp.concatenate([(-q["u_n"]).astype(BF16), q["v"]], axis=0))
            q["h_new"] = q["gam_col"] * q["h"] + jnp.where(same_head, upd, 0.0)

    stages = [st_scores, st_values] + ([st_state_read] if n_sub == 1 else [])
    s = 4
    while s < SUB:
        stages += [st_compact_a(s), st_compact_b(2 * s == SUB)]
        s *= 2
    while s < C:
        stages += [st_full_a(s), st_full_b]
        s *= 2
    tail = [st_state_read, st_solve, st_output, st_state_write]
    stages += tail[1:] if n_sub == 1 else tail
    skew = len(tail)
    for t in range(len(stages) + skew * (n_sub - 1)):
        for o in range(n_sub):
            if 0 <= t - skew * o < len(stages):
                stages[t - skew * o](sets[o])
    for q in sets[-1]:
        h_scr[q["n"], q["d"], q["p"]] = q["h_new"]

    if write_state:
        @pl.when(j == n_steps - 1)
        def _():
            for n, d, p in [(n, d, p) for n in range(n_seq) for d in range(N_DIR) for p in range(N_PAIRS)]:
                ht = h_scr[n, d, p].T
                sfin_refs[d][n, 0, PAIR * p] = ht[:HEAD_DIM, :HEAD_DIM]
                sfin_refs[d][n, 0, PAIR * p + 1] = ht[HEAD_DIM:, HEAD_DIM:]


def _scan_call(rk, v, lwla, wts, s0_bd, write_state):
    B, T, _ = rk.shape
    n_sub = min(SCAN_SUB, T // CHUNK)
    ns = min(max(1, SCAN_WORK // n_sub), B)
    rows = n_sub * CHUNK
    nc = T // rows
    assert B % ns == 0 and T % rows == 0
    zero_init = s0_bd is None
    fwd = lambda b, j: (b, j, 0)
    bwd = lambda b, j: (b, nc - 1 - j, 0)
    full = lambda *shape: pl.BlockSpec(shape, lambda b, j: (0,) * len(shape))
    tok = lambda w, m: pl.BlockSpec((ns, rows, w), m)
    in_specs = [tok(2 * D_RWKV, fwd), tok(D_RWKV, fwd), tok(LANES, fwd),
                tok(2 * D_RWKV, bwd), tok(D_RWKV, bwd), tok(LANES, bwd),
                full(N_DIR, D_RWKV), full(N_DIR, LANES, D_RWKV),
                full(N_DIR, D_RWKV), full(N_DIR, LANES, D_RWKV),
                full(1, D_RWKV), full(1, D_RWKV), full(N_DIR, CHUNK, CHUNK), full(D_RWKV, D_RWKV)]
    args = [rk, v, lwla, rk, v, lwla, wts["w0"], wts["w2"], wts["a0"], wts["a2"],
            wts["k_k"], wts["k_a"], wts["tri"], wts["ones_bd"]]
    state_block = (ns, N_DIR, N_PAIRS, LANES, LANES)
    if not zero_init:
        in_specs.append(pl.BlockSpec(state_block, lambda b, j: (b, 0, 0, 0, 0)))
        args.append(s0_bd)
    out_specs = [tok(D_RWKV, fwd), tok(D_RWKV, bwd)]
    out_shape = [jax.ShapeDtypeStruct((B, T, D_RWKV), BF16), jax.ShapeDtypeStruct((B, T, D_RWKV), BF16)]
    if write_state:
        final_block = (ns, 1, N_HEADS, HEAD_DIM, HEAD_DIM)
        out_specs += [pl.BlockSpec(final_block, lambda b, j: (b, 0, 0, 0, 0))] * N_DIR
        out_shape += [jax.ShapeDtypeStruct((B,) + final_block[1:], F32)] * N_DIR
    kern = functools.partial(_scan_kernel, zero_init=zero_init, write_state=write_state, n_steps=nc,
                             n_seq=ns, n_sub=n_sub)
    return pl.pallas_call(
        kern, grid=(B // ns, nc), in_specs=in_specs, out_specs=out_specs, out_shape=out_shape,
        scratch_shapes=[pltpu.VMEM(state_block, F32)],
        compiler_params=pltpu.CompilerParams(dimension_semantics=("arbitrary", "arbitrary"),
                                             vmem_limit_bytes=VMEM_LIMIT),
        name="scan",
    )(*args)


def _fnet_kernel(fin_ref, gf_ref, dft_ref, mid_ref, rev_ref, cs_ref, wbd_ref, b_ref, o_ref, g_scr, csw_scr, d_scr,
                 *, mirrored):
    u = pl.program_id(1)
    nb, seq_len, _ = fin_ref.shape
    tu = o_ref.shape[1]

    @pl.when((pl.program_id(0) == 0) & (u == 0))
    def _():
        for half in range(2):
            cols = slice(half * D_FNET, (half + 1) * D_FNET)
            csw_scr[:, cols] = _bdot(cs_ref[:, cols], wbd_ref[...]).astype(BF16)

    @pl.when(u == 0)
    def _():
        step = min(seq_len, ROW_TILE)
        for n in range(nb):
            for t0 in range(0, seq_len, step):
                fc = _bdot(fin_ref[n, t0:t0 + step, :], csw_scr[...])
                g_scr[n, t0:t0 + step, :] = fc[:, :D_FNET].astype(BF16)
                g_scr[n, seq_len + t0:seq_len + t0 + step, :] = fc[:, D_FNET:].astype(BF16)

    def finish(f_re):
        gate = _silu(gf_ref[...].astype(F32)).reshape(nb * tu, D_FNET)
        o_ref[...] = ((f_re + b_ref[...]) * gate).astype(BF16).reshape(nb, tu, D_FNET)

    if not mirrored:
        finish(jnp.concatenate([jnp.dot(dft_ref[...], g_scr[n], preferred_element_type=F32) for n in range(nb)],
                               axis=0))
        return

    @pl.when(u == 0)
    def _():
        parts = []
        for n in range(nb):
            c_part = jnp.dot(dft_ref[:, :seq_len], g_scr[n, :seq_len], preferred_element_type=F32)
            s_part = jnp.dot(dft_ref[:, seq_len:], g_scr[n, seq_len:], preferred_element_type=F32)
            d_scr[n] = (c_part - s_part).astype(BF16)
            parts.append(c_part + s_part)
        finish(jnp.concatenate(parts, axis=0))

    @pl.when(u == 1)
    def _():
        parts = []
        for n in range(nb):
            mid = jnp.dot(mid_ref[...], g_scr[n], preferred_element_type=F32)[0:1, :]
            rev = jnp.dot(rev_ref[...], d_scr[n], preferred_element_type=F32)
            row = lax.broadcasted_iota(jnp.int32, rev.shape, 0)
            parts.append(jnp.where(row == 0, mid, rev))
        finish(jnp.concatenate(parts, axis=0))


def _fnet_call(fin, gf, dft_bf16, cs_bf16, wbd_bf16, b_fnet):
    B, T, _ = fin.shape
    tu = min(T, FNET_ROWS)
    nb = max(1, FNET_ROWS // T)
    n_tiles = T // tu
    assert B % nb == 0 and T % tu == 0 and n_tiles in (1, 2)
    mirrored = n_tiles == 2
    mid_rows = 2 * SUBLANES
    if mirrored:
        rev_np = np.zeros((tu, tu), np.float32)
        rev_np[np.arange(1, tu), tu - np.arange(1, tu)] = 1.0
        rev = jnp.asarray(rev_np).astype(BF16)
        d_shape = (nb, tu, D_FNET)
    else:
        rev = jnp.zeros((mid_rows, LANES), BF16)
        d_shape = (1, mid_rows, LANES)
    return pl.pallas_call(
        functools.partial(_fnet_kernel, mirrored=mirrored),
        grid=(B // nb, n_tiles),
        in_specs=[pl.BlockSpec((nb, T, D_FNET), lambda b, u: (b, 0, 0)),
                  pl.BlockSpec((nb, tu, D_FNET), lambda b, u: (b, u, 0)),
                  pl.BlockSpec((tu, 2 * T), lambda b, u: (0, 0)),
                  pl.BlockSpec((mid_rows, 2 * T), lambda b, u: (tu // mid_rows if mirrored else 0, 0)),
                  pl.BlockSpec(rev.shape, lambda b, u: (0, 0)),
                  pl.BlockSpec((D_FNET, 2 * D_FNET), lambda b, u: (0, 0)),
                  pl.BlockSpec((D_FNET, D_FNET), lambda b, u: (0, 0)),
                  pl.BlockSpec((1, D_FNET), lambda b, u: (0, 0))],
        out_specs=pl.BlockSpec((nb, tu, D_FNET), lambda b, u: (b, u, 0)),
        out_shape=jax.ShapeDtypeStruct((B, T, D_FNET), BF16),
        scratch_shapes=[pltpu.VMEM((nb, 2 * T, D_FNET), BF16), pltpu.VMEM((D_FNET, 2 * D_FNET), BF16),
                        pltpu.VMEM(d_shape, BF16)],
        compiler_params=pltpu.CompilerParams(dimension_semantics=("arbitrary", "arbitrary"),
                                             vmem_limit_bytes=VMEM_LIMIT),
        name="fnet",
    )(fin, gf, dft_bf16, dft_bf16, rev, cs_bf16, wbd_bf16, b_fnet)


def _out_kernel(*refs, has_emb, final_norm):
    it = iter(refs)
    x_ref = next(it)
    emb_ref = next(it) if has_emb else None
    (mod_ref, yf_ref, yb_ref, rk_ref, v_ref, lwla_ref, grec_ref, fo_ref, a0_ref, a2_ref, ka_ref,
     rkw_ref, gng_ref, gnb_ref, avg_ref, ones_ref, wout_ref, fng_ref, o_ref, x_buf, x_sem) = (
        next(it) for _ in range(21))

    nb, tt, _ = o_ref.shape
    tm = nb * tt
    n_inner = pl.num_programs(1)
    n_steps = pl.num_programs(0) * n_inner
    step = pl.program_id(0) * n_inner + pl.program_id(1)

    def x_copy(s):
        slot = s % X_RING
        return pltpu.make_async_copy(x_ref.at[pl.ds((s // n_inner) * nb, nb), pl.ds((s % n_inner) * tt, tt), :],
                                     x_buf.at[slot], x_sem.at[slot])

    @pl.when(step == 0)
    def _():
        x_copy(step).start()

    @pl.when((step == 0) & (n_steps > 1))
    def _():
        x_copy(step + 1).start()

    @pl.when(step + (X_RING - 1) < n_steps)
    def _():
        x_copy(step + (X_RING - 1)).start()

    x_copy(step).wait()
    rows = lambda ref: ref[...].reshape(tm, ref.shape[-1])
    y = rows(yf_ref).astype(F32) + rows(yb_ref).astype(F32)
    mu = _bdot(y, avg_ref[...])
    dlt = y - mu
    var = _bdot(dlt * dlt, avg_ref[...])
    y_n = dlt * lax.rsqrt(var + GN_EPS) * gng_ref[...] + gnb_ref[...]
    rk = rows(rk_ref).astype(F32)
    r = rk[:, :D_RWKV]
    k = rk[:, D_RWKV:]
    ll = rows(lwla_ref)
    a_sum = (jax.nn.sigmoid(a0_ref[0:1, :] + _bdot(ll, a2_ref[0]))
             + jax.nn.sigmoid(a0_ref[1:2, :] + _bdot(ll, a2_ref[1])))
    k_sum = k * (2.0 + (a_sum - 2.0) * ka_ref[...])
    bonus = _bdot(r * k_sum * rkw_ref[...], ones_ref[...]) * rows(v_ref).astype(F32)
    rec_out = (y_n + bonus) * _silu(rows(grec_ref).astype(F32))
    mixed = jnp.concatenate([rec_out.astype(BF16), rows(fo_ref)], axis=-1)
    out = jnp.dot(mixed, wout_ref[...], preferred_element_type=F32)
    x = x_buf[step % X_RING].reshape(tm, D_MODEL)
    if has_emb:
        x = x + emb_ref[...]
    z = x + mod_ref[0, 2:3, :] * out
    if final_norm:
        ms = jnp.mean(z * z, axis=-1, keepdims=True)
        z = z * lax.rsqrt(ms + NORM_EPS) * fng_ref[...]
    o_ref[...] = z.reshape(nb, tt, D_MODEL)


def _out_call(x, emb, mod, y_f, y_b, rk, v, lwla, grec, fo, wts, w_out_bf16, final_norm_g, final_norm):
    B, T, _ = x.shape
    tt = min(T, ROW_TILE)
    nb = max(1, ROW_TILE // T)
    has_emb = emb is not None
    per_batch_mod = mod.shape[0] > 1
    assert B % nb == 0 and T % tt == 0 and (nb == 1 or not (has_emb or per_batch_mod))
    tok = lambda w: pl.BlockSpec((nb, tt, w), lambda b, i: (b, i, 0))
    full = lambda *shape: pl.BlockSpec(shape, lambda b, i: (0,) * len(shape))
    in_specs = [pl.BlockSpec(memory_space=pl.ANY)]
    args = [x]
    if has_emb:
        in_specs.append(pl.BlockSpec((tt, D_MODEL), lambda b, i: (i, 0)))
        args.append(emb)
    mod_map = (lambda b, i: (b, 0, 0)) if per_batch_mod else (lambda b, i: (0, 0, 0))
    in_specs += [pl.BlockSpec((1, 3, D_MODEL), mod_map), tok(D_RWKV), tok(D_RWKV), tok(2 * D_RWKV),
                 tok(D_RWKV), tok(LANES), tok(D_RWKV), tok(D_FNET),
                 full(N_DIR, D_RWKV), full(N_DIR, LANES, D_RWKV),
                 full(1, D_RWKV), full(1, D_RWKV), full(1, D_RWKV), full(1, D_RWKV),
                 full(D_RWKV, D_RWKV), full(D_RWKV, D_RWKV), full(D_MODEL, D_MODEL), full(1, D_MODEL)]
    args += [mod, y_f, y_b, rk, v, lwla, grec, fo, wts["a0"], wts["a2"], wts["k_a"],
             wts["r_k"], wts["gn_g"], wts["gn_b"], wts["avg_bd"], wts["ones_bd"], w_out_bf16, final_norm_g]
    return pl.pallas_call(
        functools.partial(_out_kernel, has_emb=has_emb, final_norm=final_norm),
        grid=(B // nb, T // tt), in_specs=in_specs,
        out_specs=tok(D_MODEL),
        out_shape=jax.ShapeDtypeStruct((B, T, D_MODEL), F32),
        scratch_shapes=[pltpu.VMEM((X_RING, nb, tt, D_MODEL), F32), pltpu.SemaphoreType.DMA((X_RING,))],
        compiler_params=pltpu.CompilerParams(dimension_semantics=("arbitrary", "arbitrary"),
                                             vmem_limit_bytes=VMEM_LIMIT),
        name="out_proj",
    )(*args)


def _dft_table(seq_len):
    idx = np.arange(seq_len, dtype=np.int64)
    ang = 2.0 * np.pi * ((idx[:, None] * idx[None, :]) % seq_len).astype(np.float64) / seq_len
    scale = 1.0 / math.sqrt(seq_len)
    return np.concatenate([np.cos(ang) * scale, -np.sin(ang) * scale], axis=1).astype(np.float32)


def _channel_dft_table():
    n = FNET_GROUP
    idx = np.arange(n, dtype=np.int64)
    ang = 2.0 * np.pi * ((idx[:, None] * idx[None, :]) % n).astype(np.float64) / n
    c = np.cos(ang) / math.sqrt(n)
    s = np.sin(ang) / math.sqrt(n)
    eye = np.eye(D_FNET // n)
    return np.concatenate([np.kron(eye, c), np.kron(eye, s)], axis=1).astype(np.float32)


def _sincos_2d(n_tokens):
    rows = n_tokens // GRID_W
    pos = np.arange(rows * GRID_W)
    row = (pos // GRID_W).astype(np.float32)
    col = (pos % GRID_W).astype(np.float32)
    quarter = D_MODEL // 4
    freq = np.exp(np.float32(-math.log(POS_BASE)) * np.arange(quarter, dtype=np.float32) / np.float32(quarter))
    ang_r = row[:, None] * freq
    ang_c = col[:, None] * freq
    return np.concatenate([np.sin(ang_r), np.cos(ang_r), np.sin(ang_c), np.cos(ang_c)], axis=-1).astype(np.float32)


def _head_block_matrix(value):
    blk = np.kron(np.eye(N_HEADS), np.ones((HEAD_DIM, HEAD_DIM))) * value
    return jnp.asarray(blk, dtype=BF16)


def _pad_lora(w, row_offset):
    rows = [jnp.pad(w[d], ((row_offset + d * LORA, LANES - row_offset - (d + 1) * LORA), (0, 0)))
            for d in range(N_DIR)]
    return jnp.stack(rows).astype(BF16)


def _layer_weights(l, w0, w2, a0, a2, k_k, k_a, r_k, gn_g, gn_b):
    w2_p = _pad_lora(w2[l], 0)
    a2_p = _pad_lora(a2[l], N_DIR * LORA)
    tri_f = np.tril(np.ones((CHUNK, CHUNK)))
    tri = jnp.asarray(np.stack([tri_f, tri_f.T]), dtype=BF16)
    return dict(w0=w0[l], w2=w2_p, a0=a0[l], a2=a2_p,
                k_k=k_k[l][None], k_a=k_a[l][None], r_k=r_k[l].reshape(1, D_RWKV),
                gn_g=gn_g[l].reshape(1, D_RWKV), gn_b=gn_b[l].reshape(1, D_RWKV), tri=tri,
                ones_bd=_head_block_matrix(1.0), avg_bd=_head_block_matrix(1.0 / HEAD_DIM))


def _state_to_block_diag(s_f, s_b):
    def one(s):
        h = jnp.swapaxes(s.astype(F32), -1, -2)
        b = h.shape[0]
        h = h.reshape(b, N_PAIRS, PAIR, HEAD_DIM, HEAD_DIM)
        z = jnp.zeros_like(h[:, :, 0])
        top = jnp.concatenate([h[:, :, 0], z], axis=-1)
        bot = jnp.concatenate([z, h[:, :, 1]], axis=-1)
        return jnp.concatenate([top, bot], axis=-2)
    return jnp.stack([one(s_f), one(s_b)], axis=1)


def kernel(x_prompt, x_sample, state_rwkv_fwd, state_rwkv_bwd, c, c_ctx, w_ada, b_ada, norm_g, w_in,
           mu_shift, w0, w2, a0, a2, k_k, k_a, r_k, gn_g, gn_b, w_fnet, b_fnet, w_out, final_norm_g):
    depth = w_in.shape[0]
    n_dec = c.shape[0]
    assert n_dec + 1 <= SUBLANES
    bp, tp, _ = x_prompt.shape
    bs, ts, _ = x_sample.shape
    cvec = jnp.concatenate([c_ctx[None], c, jnp.zeros((SUBLANES - 1 - n_dec, D_MODEL), F32)], axis=0)
    emb = jnp.asarray(_sincos_2d(ts)).astype(x_sample.dtype)
    cs_tab = jnp.asarray(_channel_dft_table()).astype(BF16)
    dft_p = jnp.asarray(_dft_table(tp)).astype(BF16)
    dft_s = jnp.asarray(_dft_table(ts)).astype(BF16)
    fng = final_norm_g[None]

    xp, xs = x_prompt, x_sample
    new_f, new_b = [], []
    for l in range(depth):
        mod = _mod_call(cvec, w_ada[l], b_ada[l][None]).reshape(SUBLANES, 3, D_MODEL)
        mod_ctx, mod_lat = mod[0:1], mod[1:1 + n_dec]
        wts = _layer_weights(l, w0, w2, a0, a2, k_k, k_a, r_k, gn_g, gn_b)
        w_in_b = w_in[l].astype(BF16)
        w_out_b = w_out[l].astype(BF16)
        n_grp = w_fnet.shape[1]
        wbd = (w_fnet[l][:, :, None, :] * jnp.eye(n_grp, dtype=F32)[:, None, :, None]).reshape(
            D_FNET, D_FNET).astype(BF16)
        ng, mu, bf = norm_g[l][None], mu_shift[l][None], b_fnet[l][None]
        emb_l = emb if l == 0 else None

        rk, v, lwla, grec, fin, gf = _in_proj_call(xp, None, mod_ctx, ng, w_in_b, mu)
        y_f, y_b, s_f, s_b = _scan_call(rk, v, lwla, wts, None, True)
        fo = _fnet_call(fin, gf, dft_p, cs_tab, wbd, bf)
        last = l == depth - 1
        xp = _out_call(xp, None, mod_ctx, y_f, y_b, rk, v, lwla, grec, fo, wts, w_out_b, fng, last)
        new_f.append(s_f)
        new_b.append(s_b)

        rk, v, lwla, grec, fin, gf = _in_proj_call(xs, emb_l, mod_lat, ng, w_in_b, mu)
        s0 = _state_to_block_diag(state_rwkv_fwd[:, l], state_rwkv_bwd[:, l])
        y_f, y_b = _scan_call(rk, v, lwla, wts, s0, False)
        fo = _fnet_call(fin, gf, dft_s, cs_tab, wbd, bf)
        xs = _out_call(xs, emb_l, mod_lat, y_f, y_b, rk, v, lwla, grec, fo, wts, w_out_b, fng, last)
    return (xp, xs, jnp.concatenate(new_f, axis=1), jnp.concatenate(new_b, axis=1))
```

```python
import functools
import math

import numpy as np
import jax
import jax.numpy as jnp
from jax import lax
from jax.experimental import pallas as pl
from jax.experimental.pallas import tpu as pltpu

F32 = jnp.float32
BF16 = jnp.bfloat16

D_MODEL = 1024
GRID_W = 64
D_RWKV = 512
D_FNET = D_MODEL - D_RWKV
HEAD_DIM = 64
N_HEADS = D_RWKV // HEAD_DIM
FNET_GROUP = 64
LORA = 32
N_DIR = 2
D_SHIFT = 3 * D_RWKV + N_DIR * 2 * LORA
D_IN = D_SHIFT + D_RWKV + 2 * D_FNET
NORM_EPS = 1e-6
KK_EPS = 1e-12
GN_EPS = 64e-5
POS_BASE = 10000.0

LANES = 128
SUBLANES = 8
PAIR = LANES // HEAD_DIM
N_PAIRS = N_HEADS // PAIR
CHUNK = 64
SUB = 16
SUB_SHIFT = SUB.bit_length() - 1
SCAN_WORK = 8
SCAN_SUB = 4
ROW_TILE = 512
ROW_SPLIT = 4
FNET_ROWS = 1024
HALO = 8
X_RING = 3
VMEM_LIMIT = 56 * 1024 * 1024


def _silu(x):
    return x * jax.nn.sigmoid(x)


def _bdot(a, b):
    return jnp.dot(a.astype(BF16), b.astype(BF16), preferred_element_type=F32)


def _split2(x):
    hi = x.astype(BF16)
    lo = (x - hi.astype(F32)).astype(BF16)
    return hi, lo


def _mod_kernel(c_ref, w_ref, b_ref, o_ref):
    @pl.when(pl.program_id(0) == 0)
    def _():
        o_ref[...] = jnp.broadcast_to(b_ref[...], o_ref.shape)

    s_hi, s_lo = _split2(_silu(c_ref[...]))
    w = w_ref[...].astype(BF16)
    o_ref[...] += jnp.dot(s_hi, w, preferred_element_type=F32) + jnp.dot(s_lo, w, preferred_element_type=F32)


def _mod_call(cvec, w_ada, b_ada):
    rows = D_MODEL // 4
    return pl.pallas_call(
        _mod_kernel,
        grid=(D_MODEL // rows,),
        in_specs=[pl.BlockSpec((SUBLANES, rows), lambda i: (0, i)),
                  pl.BlockSpec((rows, 3 * D_MODEL), lambda i: (i, 0)),
                  pl.BlockSpec((1, 3 * D_MODEL), lambda i: (0, 0))],
        out_specs=pl.BlockSpec((SUBLANES, 3 * D_MODEL), lambda i: (0, 0)),
        out_shape=jax.ShapeDtypeStruct((SUBLANES, 3 * D_MODEL), F32),
        compiler_params=pltpu.CompilerParams(dimension_semantics=("arbitrary",),
                                             vmem_limit_bytes=VMEM_LIMIT),
        name="mod",
    )(cvec, w_ada, b_ada)


def _modulated_norm(x, g, scale, shift):
    ms = jnp.mean(x * x, axis=-1, keepdims=True)
    y = x * lax.rsqrt(ms + NORM_EPS) * g
    return y * (1.0 + scale) + shift


def _ring_fetch(x_hbm, x_buf, x_sem, nb, tt):
    n_inner = pl.num_programs(1)
    n_steps = pl.num_programs(0) * n_inner
    step = pl.program_id(0) * n_inner + pl.program_id(1)

    def x_copy(s):
        slot = s % X_RING
        return pltpu.make_async_copy(x_hbm.at[pl.ds((s // n_inner) * nb, nb), pl.ds((s % n_inner) * tt, tt), :],
                                     x_buf.at[slot], x_sem.at[slot])

    for ahead in range(X_RING - 1):
        @pl.when((step == 0) & (ahead < n_steps))
        def _():
            x_copy(step + ahead).start()

    @pl.when(step + (X_RING - 1) < n_steps)
    def _():
        x_copy(step + (X_RING - 1)).start()

    x_copy(step).wait()
    return x_buf.at[step % X_RING]


def _in_proj_kernel(*refs, has_emb, has_halo, n_tiles):
    it = iter(refs)
    x_ref = next(it)
    xp_ref = next(it) if has_halo else None
    xn_ref = next(it) if has_halo else None
    emb_ref = next(it) if has_emb else None
    embp_ref = next(it) if (has_emb and has_halo) else None
    embn_ref = next(it) if (has_emb and has_halo) else None
    mod_ref, g_ref, w_ref, mu_ref = next(it), next(it), next(it), next(it)
    rk_ref, v_ref, lwla_ref, grec_ref, fin_ref, gf_ref = (next(it) for _ in range(6))
    x_buf, x_sem = next(it), next(it)

    i = pl.program_id(1)
    g = g_ref[...]
    shift = mod_ref[0, 0:1, :]
    scale = mod_ref[0, 1:2, :]
    nb, tt, _ = v_ref.shape
    tm = nb * tt
    x = _ring_fetch(x_ref, x_buf, x_sem, nb, tt)[...].reshape(tm, D_MODEL)
    if has_emb:
        x = x + emb_ref[...]
    part = tm // ROW_SPLIT
    p = jnp.concatenate(
        [jnp.dot(_modulated_norm(x[r0:r0 + part], g, scale, shift).astype(BF16), w_ref[...],
                 preferred_element_type=F32) for r0 in range(0, tm, part)], axis=0)
    ps = p[:, :D_SHIFT]
    if has_halo:
        xh = jnp.concatenate([xp_ref[0], xn_ref[0]], axis=0)
        if has_emb:
            xh = xh + jnp.concatenate([embp_ref[...], embn_ref[...]], axis=0)
        hh = _modulated_norm(xh, g, scale, shift)
        ph = jnp.dot(hh.astype(BF16), w_ref[:, :D_SHIFT], preferred_element_type=F32)
        prev_row = jnp.where(i > 0, ph[HALO - 1:HALO, :], 0.0)
        next_row = jnp.where(i < n_tiles - 1, ph[HALO:HALO + 1, :], 0.0)
    else:
        prev_row = jnp.zeros((1, D_SHIFT), F32)
        next_row = jnp.zeros((1, D_SHIFT), F32)
    pos = lax.rem(lax.broadcasted_iota(jnp.int32, (tm, D_SHIFT), 0), tt)
    prev = jnp.where(pos == 0, prev_row, pltpu.roll(ps, 1, 0))
    nxt = jnp.where(pos == tt - 1, next_row, pltpu.roll(ps, tm - 1, 0))
    p_rec = ps + mu_ref[...] * (0.5 * (prev + nxt) - ps)

    def put(ref, val):
        ref[...] = val.astype(ref.dtype).reshape(ref.shape)

    put(rk_ref, p_rec[:, :2 * D_RWKV])
    put(v_ref, p_rec[:, 2 * D_RWKV:3 * D_RWKV])
    put(lwla_ref, p_rec[:, 3 * D_RWKV:])
    put(grec_ref, p[:, D_SHIFT:D_SHIFT + D_RWKV])
    put(fin_ref, p[:, D_SHIFT + D_RWKV:D_SHIFT + D_RWKV + D_FNET])
    put(gf_ref, p[:, D_SHIFT + D_RWKV + D_FNET:])


def _in_proj_call(x, emb, mod, norm_g, w_in_bf16, mu):
    B, T, _ = x.shape
    tt = min(T, ROW_TILE)
    nb = max(1, ROW_TILE // T)
    n_tiles = T // tt
    has_halo = n_tiles > 1
    has_emb = emb is not None
    per_batch_mod = mod.shape[0] > 1
    assert B % nb == 0 and T % tt == 0 and (nb == 1 or not (has_emb or per_batch_mod))
    tm = tt
    blocks_per_tile = tm // HALO
    last_halo_block = T // HALO - 1

    in_specs = [pl.BlockSpec(memory_space=pl.ANY)]
    args = [x]
    if has_halo:
        in_specs += [
            pl.BlockSpec((1, HALO, D_MODEL), lambda b, i: (b, jnp.maximum(i * blocks_per_tile - 1, 0), 0)),
            pl.BlockSpec((1, HALO, D_MODEL),
                         lambda b, i: (b, jnp.minimum((i + 1) * blocks_per_tile, last_halo_block), 0))]
        args += [x, x]
    if has_emb:
        in_specs.append(pl.BlockSpec((tm, D_MODEL), lambda b, i: (i, 0)))
        args.append(emb)
        if has_halo:
            in_specs += [
                pl.BlockSpec((HALO, D_MODEL), lambda b, i: (jnp.maximum(i * blocks_per_tile - 1, 0), 0)),
                pl.BlockSpec((HALO, D_MODEL),
                             lambda b, i: (jnp.minimum((i + 1) * blocks_per_tile, last_halo_block), 0))]
            args += [emb, emb]
    mod_map = (lambda b, i: (b, 0, 0)) if per_batch_mod else (lambda b, i: (0, 0, 0))
    in_specs += [pl.BlockSpec((1, 3, D_MODEL), mod_map),
                 pl.BlockSpec((1, D_MODEL), lambda b, i: (0, 0)),
                 pl.BlockSpec((D_MODEL, D_IN), lambda b, i: (0, 0), pipeline_mode=pl.Buffered(1)),
                 pl.BlockSpec((1, D_SHIFT), lambda b, i: (0, 0))]
    args += [mod, norm_g, w_in_bf16, mu]
    outs = ((2 * D_RWKV, BF16), (D_RWKV, BF16), (N_DIR * 2 * LORA, F32), (D_RWKV, BF16), (D_FNET, BF16),
            (D_FNET, BF16))
    out_specs = [pl.BlockSpec((nb, tt, w), lambda b, i: (b, i, 0)) for w, _ in outs]
    out_shape = [jax.ShapeDtypeStruct((B, T, w), dt) for w, dt in outs]
    kern = functools.partial(_in_proj_kernel, has_emb=has_emb, has_halo=has_halo, n_tiles=n_tiles)
    return pl.pallas_call(
        kern, grid=(B // nb, n_tiles), in_specs=in_specs, out_specs=out_specs, out_shape=out_shape,
        scratch_shapes=[pltpu.VMEM((X_RING, nb, tt, D_MODEL), F32), pltpu.SemaphoreType.DMA((X_RING,))],
        compiler_params=pltpu.CompilerParams(dimension_semantics=("arbitrary", "arbitrary"),
                                             vmem_limit_bytes=VMEM_LIMIT),
        name="in_proj",
    )(*args)


def _same_block(i, j, size):
    shift = size.bit_length() - 1
    return jnp.right_shift(i, shift) == jnp.right_shift(j, shift)


def _block_diag_rows(y, head0_lanes):
    return jnp.concatenate([jnp.where(head0_lanes, y, 0.0), jnp.where(head0_lanes, 0.0, y)], axis=0)


def _scan_kernel(*refs, zero_init, write_state, n_steps, n_seq, n_sub):
    it = iter(refs)
    rk_refs, v_refs, lwla_refs = [None, None], [None, None], [None, None]
    for d in range(N_DIR):
        rk_refs[d], v_refs[d], lwla_refs[d] = next(it), next(it), next(it)
    w0_ref, w2_ref, a0_ref, a2_ref = (next(it) for _ in range(4))
    kk_ref, ka_ref, tri_ref, ones_ref = (next(it) for _ in range(4))
    s0_ref = None if zero_init else next(it)
    y_refs = [next(it), next(it)]
    sfin_refs = [next(it), next(it)] if write_state else None
    h_scr = next(it)

    j = pl.program_id(1)

    @pl.when(j == 0)
    def _():
        if zero_init:
            h_scr[...] = jnp.zeros_like(h_scr)
        else:
            h_scr[...] = s0_ref[...]

    C = CHUNK
    row = lax.broadcasted_iota(jnp.int32, (C, LANES), 0)
    lane = lax.broadcasted_iota(jnp.int32, (C, LANES), 1)
    s_idx = jnp.bitwise_and(lane, HEAD_DIM - 1)
    head0 = lane < HEAD_DIM
    row_c = lax.broadcasted_iota(jnp.int32, (SUB, LANES), 0)
    lane_c = lax.broadcasted_iota(jnp.int32, (SUB, LANES), 1)
    col_c = jnp.bitwise_and(lane_c, SUB - 1)
    blk_c = jnp.right_shift(jnp.bitwise_and(lane_c, HEAD_DIM - 1), SUB_SHIFT)
    lane_blk_c = jnp.right_shift(lane_c, SUB_SHIFT)
    eye_c = (row_c == col_c).astype(F32)

    def bd_c(y):
        return jnp.concatenate([jnp.where(lane_blk_c == g, y, 0.0) for g in range(LANES // SUB)], axis=0)

    def inverse_size4(l_c, upper):
        r4 = jnp.bitwise_and(row_c, 3)
        c4 = jnp.bitwise_and(col_c, 3)
        down, up = (lambda x, k: pltpu.roll(x, k, 0)), (lambda x, k: pltpu.roll(x, SUB - k, 0))
        right, left = (lambda x, k: pltpu.roll(x, k, 1)), (lambda x, k: pltpu.roll(x, LANES - k, 1))
        s2 = jnp.where(_same_block(row_c, col_c, 2), l_c, 0.0)
        e4 = jnp.where(_same_block(row_c, col_c, 4), l_c, 0.0) - s2
        if upper:
            s_col = jnp.where(r4 == 1, up(s2, 1), up(s2, 2))
            s_row = jnp.where(c4 == 2, right(s2, 1), right(s2, 2))
            e_s = jnp.where(c4 == 3, right(e4, 1) * s_col, 0.0)
            s_e = jnp.where(r4 == 0, s_row * up(e4, 1), 0.0)
            s_e_s = jnp.where((r4 == 0) & (c4 == 3), s_row * up(e_s, 1), 0.0)
        else:
            s_col = jnp.where(r4 == 2, down(s2, 1), down(s2, 2))
            s_row = jnp.where(c4 == 1, left(s2, 1), left(s2, 2))
            e_s = jnp.where(c4 == 0, left(e4, 1) * s_col, 0.0)
            s_e = jnp.where(r4 == 3, s_row * down(e4, 1), 0.0)
            s_e_s = jnp.where((r4 == 3) & (c4 == 0), s_row * down(e_s, 1), 0.0)
        return eye_c - s2 - e4 + e_s + s_e - s_e_s

    row2 = lax.broadcasted_iota(jnp.int32, (LANES, LANES), 0)
    lane2 = lax.broadcasted_iota(jnp.int32, (LANES, LANES), 1)
    same_head = (row2 < HEAD_DIM) == (lane2 < HEAD_DIM)
    decay_scale = math.exp(-0.5)

    row0 = lambda o, d: (o if d == 0 else n_sub - 1 - o) * C
    keys = [(o, n, d) for o in range(n_sub) for n in range(n_seq) for d in range(N_DIR)]
    rk_all = {(o, n, d): rk_refs[d][n, row0(o, d):row0(o, d) + C, :].astype(F32) for o, n, d in keys}
    kkr_all = {key: rk_all[key][:, D_RWKV:] * kk_ref[...] for key in keys}
    ssq_rows = _bdot(jnp.concatenate([kkr_all[key] * kkr_all[key] for key in keys], axis=0), ones_ref[...])
    dir_keys = [[key for key in keys if key[2] == d] for d in range(N_DIR)]
    z_w_rows, z_a_rows = [], []
    for d in range(N_DIR):
        ll_rows = jnp.concatenate([lwla_refs[d][n, row0(o, d):row0(o, d) + C, :] for o, n, _ in dir_keys[d]], axis=0)
        z_w_rows.append(w0_ref[d:d + 1, :] + _bdot(jnp.tanh(ll_rows), w2_ref[d]))
        z_a_rows.append(a0_ref[d:d + 1, :] + _bdot(ll_rows, a2_ref[d]))

    sets = [[] for _ in range(n_sub)]
    for i, (o, n, d) in enumerate(keys):
        rk = rk_all[(o, n, d)]
        r = rk[:, :D_RWKV]
        k = rk[:, D_RWKV:]
        v = v_refs[d][n, row0(o, d):row0(o, d) + C, :]
        i_d = dir_keys[d].index((o, n, d))
        logw = -decay_scale * jax.nn.sigmoid(z_w_rows[d][i_d * C:(i_d + 1) * C])
        a = jax.nn.sigmoid(z_a_rows[d][i_d * C:(i_d + 1) * C])
        kd = k * (1.0 + (a - 1.0) * ka_ref[...])
        kkr = kkr_all[(o, n, d)]
        ssq = ssq_rows[i * C:(i + 1) * C]
        kk = kkr * lax.rsqrt(jnp.maximum(ssq, KK_EPS * KK_EPS))
        bvec = kk * a
        lw_hi, lw_lo = _split2(logw)
        tri = tri_ref[d]
        cum = (jnp.dot(tri, lw_hi, preferred_element_type=F32)
               + jnp.dot(tri, lw_lo, preferred_element_type=F32))
        cum_prev = cum - logw
        tot = cum[C - 1:C, :] if d == 0 else cum[0:1, :]
        kap_t = kk * jnp.exp(cum_prev)
        r_t = r * jnp.exp(cum)
        e_neg = jnp.exp(-cum)
        gam = jnp.exp(tot)
        b_t = bvec * e_neg
        k_t = kd * e_neg
        e_rem = gam * e_neg
        b_h = bvec * e_rem
        k_h = kd * e_rem
        if d == 0:
            strict, incl = row > s_idx, row >= s_idx
        else:
            strict, incl = row < s_idx, row <= s_idx

        for p in range(N_PAIRS):
            sl = slice(p * LANES, (p + 1) * LANES)
            dup_t = lambda x: jnp.where(same_head, jnp.concatenate([x[:, sl], x[:, sl]], axis=0).T, 0.0)
            sets[o].append(dict(
                n=n, d=d, p=p, sl=sl, rows=slice(row0(o, d), row0(o, d) + C), strict=strict, incl=incl,
                prev=None if o == 0 else sets[o - 1][len(sets[o])],
                lhs=jnp.concatenate([kap_t[:, sl], r_t[:, sl]], axis=0).astype(BF16),
                v=v[:, sl], w_lm=jnp.concatenate([dup_t(b_t), dup_t(k_t)], axis=1).astype(BF16),
                t2=jnp.concatenate([b_h[:, sl], k_h[:, sl]], axis=0).T.astype(BF16),
                gam_col=jnp.broadcast_to(gam[:, sl], (LANES, LANES)).T))

    bd = lambda y: _block_diag_rows(y, head0)

    def st_scores(probs):
        for q in probs:
            lm = jnp.dot(q["lhs"], q["w_lm"], preferred_element_type=F32)
            q["l_b"] = jnp.where(q["strict"], lm[:C, :LANES], 0.0)
            q["m_b"] = jnp.where(q["incl"], lm[C:, :LANES], 0.0)
            q["lm_k"] = jnp.concatenate([jnp.where(q["strict"], lm[:C, LANES:], 0.0),
                                         jnp.where(q["incl"], lm[C:, LANES:], 0.0)], axis=0)

    def st_values(probs):
        for q in probs:
            lmv = _bdot(q["lm_k"], bd(q["v"]))
            q["l_kv"], q["m_kv"] = lmv[:C], lmv[C:]
            l_b = q["l_b"]
            l_c = l_b[0:SUB]
            for jb in range(1, C // SUB):
                l_c = jnp.where(blk_c == jb, l_b[jb * SUB:(jb + 1) * SUB], l_c)
            q["l_c"] = l_c
            q["t_c"] = inverse_size4(l_c, q["d"] == 1)

    def st_compact_a(s):
        def run(probs):
            off_mask = _same_block(row_c, col_c, 2 * s) & ~_same_block(row_c, col_c, s)
            for q in probs:
                q["et"] = _bdot(jnp.where(off_mask, q["l_c"], 0.0), bd_c(q["t_c"]))
        return run

    def st_compact_b(last):
        def run(probs):
            for q in probs:
                q["t_c"] = q["t_c"] - _bdot(q["t_c"], bd_c(q["et"]))
                if last:
                    q["tinv"] = jnp.concatenate([jnp.where(blk_c == jb, q["t_c"], 0.0) for jb in range(C // SUB)],
                                                axis=0)
        return run

    def st_full_a(s):
        def run(probs):
            off_mask = _same_block(row, s_idx, 2 * s) & ~_same_block(row, s_idx, s)
            for q in probs:
                q["et"] = _bdot(jnp.where(off_mask, q["l_b"], 0.0), bd(q["tinv"]))
        return run

    def st_full_b(probs):
        for q in probs:
            q["tinv"] = q["tinv"] - _bdot(q["tinv"], bd(q["et"]))

    def st_state_read(probs):
        for q in probs:
            q["h"] = h_scr[q["n"], q["d"], q["p"]] if q["prev"] is None else q["prev"]["h_new"]
            q["kr_h"] = _bdot(q["lhs"], q["h"])

    def st_solve(probs):
        for q in probs:
            q["u_n"] = _bdot(q["tinv"], bd(q["kr_h"][:C] + q["l_kv"]))

    def st_output(probs):
        for q in probs:
            y = q["kr_h"][C:] + q["m_kv"] - _bdot(q["m_b"], bd(q["u_n"]))
            y_refs[q["d"]][q["n"], q["rows"], q["sl"]] = y.astype(BF16)

    def st_state_write(probs):
        for q in probs:
            upd = _bdot(q["t2"], jnp.concatenate([(-q["u_n"]).astype(BF16), q["v"]], axis=0))
            q["h_new"] = q["gam_col"] * q["h"] + jnp.where(same_head, upd, 0.0)

    stages = [st_scores, st_values] + ([st_state_read] if n_sub == 1 else [])
    s = 4
    while s < SUB:
        stages += [st_compact_a(s), st_compact_b(2 * s == SUB)]
        s *= 2
    while s < C:
        stages += [st_full_a(s), st_full_b]
        s *= 2
    tail = [st_state_read, st_solve, st_output, st_state_write]
    stages += tail[1:] if n_sub == 1 else tail
    skew = len(tail)
    for t in range(len(stages) + skew * (n_sub - 1)):
        for o in range(n_sub):
            if 0 <= t - skew * o < len(stages):
                stages[t - skew * o](sets[o])
    for q in sets[-1]:
        h_scr[q["n"], q["d"], q["p"]] = q["h_new"]

    if write_state:
        @pl.when(j == n_steps - 1)
        def _():
            for n, d, p in [(n, d, p) for n in range(n_seq) for d in range(N_DIR) for p in range(N_PAIRS)]:
                ht = h_scr[n, d, p].T
                sfin_refs[d][n, 0, PAIR * p] = ht[:HEAD_DIM, :HEAD_DIM]
                sfin_refs[d][n, 0, PAIR * p + 1] = ht[HEAD_DIM:, HEAD_DIM:]


def _scan_call(rk, v, lwla, wts, s0_bd, write_state):
    B, T, _ = rk.shape
    n_sub = min(SCAN_SUB, T // CHUNK)
    ns = min(max(1, SCAN_WORK // n_sub), B)
    rows = n_sub * CHUNK
    nc = T // rows
    assert B % ns == 0 and T % rows == 0
    zero_init = s0_bd is None
    fwd = lambda b, j: (b, j, 0)
    bwd = lambda b, j: (b, nc - 1 - j, 0)
    full = lambda *shape: pl.BlockSpec(shape, lambda b, j: (0,) * len(shape))
    tok = lambda w, m: pl.BlockSpec((ns, rows, w), m)
    in_specs = [tok(2 * D_RWKV, fwd), tok(D_RWKV, fwd), tok(LANES, fwd),
                tok(2 * D_RWKV, bwd), tok(D_RWKV, bwd), tok(LANES, bwd),
                full(N_DIR, D_RWKV), full(N_DIR, LANES, D_RWKV),
                full(N_DIR, D_RWKV), full(N_DIR, LANES, D_RWKV),
                full(1, D_RWKV), full(1, D_RWKV), full(N_DIR, CHUNK, CHUNK), full(D_RWKV, D_RWKV)]
    args = [rk, v, lwla, rk, v, lwla, wts["w0"], wts["w2"], wts["a0"], wts["a2"],
            wts["k_k"], wts["k_a"], wts["tri"], wts["ones_bd"]]
    state_block = (ns, N_DIR, N_PAIRS, LANES, LANES)
    if not zero_init:
        in_specs.append(pl.BlockSpec(state_block, lambda b, j: (b, 0, 0, 0, 0)))
        args.append(s0_bd)
    out_specs = [tok(D_RWKV, fwd), tok(D_RWKV, bwd)]
    out_shape = [jax.ShapeDtypeStruct((B, T, D_RWKV), BF16), jax.ShapeDtypeStruct((B, T, D_RWKV), BF16)]
    if write_state:
        final_block = (ns, 1, N_HEADS, HEAD_DIM, HEAD_DIM)
        out_specs += [pl.BlockSpec(final_block, lambda b, j: (b, 0, 0, 0, 0))] * N_DIR
        out_shape += [jax.ShapeDtypeStruct((B,) + final_block[1:], F32)] * N_DIR
    kern = functools.partial(_scan_kernel, zero_init=zero_init, write_state=write_state, n_steps=nc,
                             n_seq=ns, n_sub=n_sub)
    return pl.pallas_call(
        kern, grid=(B // ns, nc), in_specs=in_specs, out_specs=out_specs, out_shape=out_shape,
        scratch_shapes=[pltpu.VMEM(state_block, F32)],
        compiler_params=pltpu.CompilerParams(dimension_semantics=("arbitrary", "arbitrary"),
                                             vmem_limit_bytes=VMEM_LIMIT),
        name="scan",
    )(*args)


def _fnet_kernel(fin_ref, gf_ref, dft_ref, mid_ref, rev_ref, cs_ref, wbd_ref, b_ref, o_ref, g_scr, csw_scr, d_scr,
                 *, mirrored):
    u = pl.program_id(1)
    nb, seq_len, _ = fin_ref.shape
    tu = o_ref.shape[1]

    @pl.when((pl.program_id(0) == 0) & (u == 0))
    def _():
        for half in range(2):
            cols = slice(half * D_FNET, (half + 1) * D_FNET)
            csw_scr[:, cols] = _bdot(cs_ref[:, cols], wbd_ref[...]).astype(BF16)

    @pl.when(u == 0)
    def _():
        step = min(seq_len, ROW_TILE)
        for n in range(nb):
            for t0 in range(0, seq_len, step):
                fc = _bdot(fin_ref[n, t0:t0 + step, :], csw_scr[...])
                g_scr[n, t0:t0 + step, :] = fc[:, :D_FNET].astype(BF16)
                g_scr[n, seq_len + t0:seq_len + t0 + step, :] = fc[:, D_FNET:].astype(BF16)

    def finish(f_re):
        gate = _silu(gf_ref[...].astype(F32)).reshape(nb * tu, D_FNET)
        o_ref[...] = ((f_re + b_ref[...]) * gate).astype(BF16).reshape(nb, tu, D_FNET)

    if not mirrored:
        finish(jnp.concatenate([jnp.dot(dft_ref[...], g_scr[n], preferred_element_type=F32) for n in range(nb)],
                               axis=0))
        return

    @pl.when(u == 0)
    def _():
        parts = []
        for n in range(nb):
            c_part = jnp.dot(dft_ref[:, :seq_len], g_scr[n, :seq_len], preferred_element_type=F32)
            s_part = jnp.dot(dft_ref[:, seq_len:], g_scr[n, seq_len:], preferred_element_type=F32)
            d_scr[n] = (c_part - s_part).astype(BF16)
            parts.append(c_part + s_part)
        finish(jnp.concatenate(parts, axis=0))

    @pl.when(u == 1)
    def _():
        parts = []
        for n in range(nb):
            mid = jnp.dot(mid_ref[...], g_scr[n], preferred_element_type=F32)[0:1, :]
            rev = jnp.dot(rev_ref[...], d_scr[n], preferred_element_type=F32)
            row = lax.broadcasted_iota(jnp.int32, rev.shape, 0)
            parts.append(jnp.where(row == 0, mid, rev))
        finish(jnp.concatenate(parts, axis=0))


def _fnet_call(fin, gf, dft_bf16, cs_bf16, wbd_bf16, b_fnet):
    B, T, _ = fin.shape
    tu = min(T, FNET_ROWS)
    nb = max(1, FNET_ROWS // T)
    n_tiles = T // tu
    assert B % nb == 0 and T % tu == 0 and n_tiles in (1, 2)
    mirrored = n_tiles == 2
    mid_rows = 2 * SUBLANES
    if mirrored:
        rev_np = np.zeros((tu, tu), np.float32)
        rev_np[np.arange(1, tu), tu - np.arange(1, tu)] = 1.0
        rev = jnp.asarray(rev_np).astype(BF16)
        d_shape = (nb, tu, D_FNET)
    else:
        rev = jnp.zeros((mid_rows, LANES), BF16)
        d_shape = (1, mid_rows, LANES)
    return pl.pallas_call(
        functools.partial(_fnet_kernel, mirrored=mirrored),
        grid=(B // nb, n_tiles),
        in_specs=[pl.BlockSpec((nb, T, D_FNET), lambda b, u: (b, 0, 0)),
                  pl.BlockSpec((nb, tu, D_FNET), lambda b, u: (b, u, 0)),
                  pl.BlockSpec((tu, 2 * T), lambda b, u: (0, 0)),
                  pl.BlockSpec((mid_rows, 2 * T), lambda b, u: (tu // mid_rows if mirrored else 0, 0)),
                  pl.BlockSpec(rev.shape, lambda b, u: (0, 0)),
                  pl.BlockSpec((D_FNET, 2 * D_FNET), lambda b, u: (0, 0)),
                  pl.BlockSpec((D_FNET, D_FNET), lambda b, u: (0, 0)),
                  pl.BlockSpec((1, D_FNET), lambda b, u: (0, 0))],
        out_specs=pl.BlockSpec((nb, tu, D_FNET), lambda b, u: (b, u, 0)),
        out_shape=jax.ShapeDtypeStruct((B, T, D_FNET), BF16),
        scratch_shapes=[pltpu.VMEM((nb, 2 * T, D_FNET), BF16), pltpu.VMEM((D_FNET, 2 * D_FNET), BF16),
                        pltpu.VMEM(d_shape, BF16)],
        compiler_params=pltpu.CompilerParams(dimension_semantics=("arbitrary", "arbitrary"),
                                             vmem_limit_bytes=VMEM_LIMIT),
        name="fnet",
    )(fin, gf, dft_bf16, dft_bf16, rev, cs_bf16, wbd_bf16, b_fnet)


def _out_kernel(*refs, has_emb, final_norm):
    it = iter(refs)
    x_ref = next(it)
    emb_ref = next(it) if has_emb else None
    (mod_ref, yf_ref, yb_ref, rk_ref, v_ref, lwla_ref, grec_ref, fo_ref, a0_ref, a2_ref, ka_ref,
     rkw_ref, gng_ref, gnb_ref, avg_ref, ones_ref, wout_ref, fng_ref, o_ref, x_buf, x_sem) = (
        next(it) for _ in range(21))

    nb, tt, _ = o_ref.shape
    tm = nb * tt
    x_blk = _ring_fetch(x_ref, x_buf, x_sem, nb, tt)
    rows = lambda ref: ref[...].reshape(tm, ref.shape[-1])
    y = rows(yf_ref).astype(F32) + rows(yb_ref).astype(F32)
    mu = _bdot(y, avg_ref[...])
    dlt = y - mu
    var = _bdot(dlt * dlt, avg_ref[...])
    y_n = dlt * lax.rsqrt(var + GN_EPS) * gng_ref[...] + gnb_ref[...]
    rk = rows(rk_ref).astype(F32)
    r = rk[:, :D_RWKV]
    k = rk[:, D_RWKV:]
    ll = rows(lwla_ref)
    a_sum = (jax.nn.sigmoid(a0_ref[0:1, :] + _bdot(ll, a2_ref[0]))
             + jax.nn.sigmoid(a0_ref[1:2, :] + _bdot(ll, a2_ref[1])))
    k_sum = k * (2.0 + (a_sum - 2.0) * ka_ref[...])
    bonus = _bdot(r * k_sum * rkw_ref[...], ones_ref[...]) * rows(v_ref).astype(F32)
    rec_out = (y_n + bonus) * _silu(rows(grec_ref).astype(F32))
    mixed = jnp.concatenate([rec_out.astype(BF16), rows(fo_ref)], axis=-1)
    out = jnp.dot(mixed, wout_ref[...], preferred_element_type=F32)
    x = x_blk[...].reshape(tm, D_MODEL)
    if has_emb:
        x = x + emb_ref[...]
    z = x + mod_ref[0, 2:3, :] * out
    if final_norm:
        ms = jnp.mean(z * z, axis=-1, keepdims=True)
        z = z * lax.rsqrt(ms + NORM_EPS) * fng_ref[...]
    o_ref[...] = z.reshape(nb, tt, D_MODEL)


def _out_call(x, emb, mod, y_f, y_b, rk, v, lwla, grec, fo, wts, w_out_bf16, final_norm_g, final_norm):
    B, T, _ = x.shape
    tt = min(T, ROW_TILE)
    nb = max(1, ROW_TILE // T)
    has_emb = emb is not None
    per_batch_mod = mod.shape[0] > 1
    assert B % nb == 0 and T % tt == 0 and (nb == 1 or not (has_emb or per_batch_mod))
    tok = lambda w: pl.BlockSpec((nb, tt, w), lambda b, i: (b, i, 0))
    full = lambda *shape: pl.BlockSpec(shape, lambda b, i: (0,) * len(shape))
    in_specs = [pl.BlockSpec(memory_space=pl.ANY)]
    args = [x]
    if has_emb:
        in_specs.append(pl.BlockSpec((tt, D_MODEL), lambda b, i: (i, 0)))
        args.append(emb)
    mod_map = (lambda b, i: (b, 0, 0)) if per_batch_mod else (lambda b, i: (0, 0, 0))
    in_specs += [pl.BlockSpec((1, 3, D_MODEL), mod_map), tok(D_RWKV), tok(D_RWKV), tok(2 * D_RWKV),
                 tok(D_RWKV), tok(LANES), tok(D_RWKV), tok(D_FNET),
                 full(N_DIR, D_RWKV), full(N_DIR, LANES, D_RWKV),
                 full(1, D_RWKV), full(1, D_RWKV), full(1, D_RWKV), full(1, D_RWKV),
                 full(D_RWKV, D_RWKV), full(D_RWKV, D_RWKV), full(D_MODEL, D_MODEL), full(1, D_MODEL)]
    args += [mod, y_f, y_b, rk, v, lwla, grec, fo, wts["a0"], wts["a2"], wts["k_a"],
             wts["r_k"], wts["gn_g"], wts["gn_b"], wts["avg_bd"], wts["ones_bd"], w_out_bf16, final_norm_g]
    return pl.pallas_call(
        functools.partial(_out_kernel, has_emb=has_emb, final_norm=final_norm),
        grid=(B // nb, T // tt), in_specs=in_specs,
        out_specs=tok(D_MODEL),
        out_shape=jax.ShapeDtypeStruct((B, T, D_MODEL), F32),
        scratch_shapes=[pltpu.VMEM((X_RING, nb, tt, D_MODEL), F32), pltpu.SemaphoreType.DMA((X_RING,))],
        compiler_params=pltpu.CompilerParams(dimension_semantics=("arbitrary", "arbitrary"),
                                             vmem_limit_bytes=VMEM_LIMIT),
        name="out_proj",
    )(*args)


def _dft_table(seq_len):
    idx = np.arange(seq_len, dtype=np.int64)
    ang = 2.0 * np.pi * ((idx[:, None] * idx[None, :]) % seq_len).astype(np.float64) / seq_len
    scale = 1.0 / math.sqrt(seq_len)
    return np.concatenate([np.cos(ang) * scale, -np.sin(ang) * scale], axis=1).astype(np.float32)


def _channel_dft_table():
    n = FNET_GROUP
    idx = np.arange(n, dtype=np.int64)
    ang = 2.0 * np.pi * ((idx[:, None] * idx[None, :]) % n).astype(np.float64) / n
    c = np.cos(ang) / math.sqrt(n)
    s = np.sin(ang) / math.sqrt(n)
    eye = np.eye(D_FNET // n)
    return np.concatenate([np.kron(eye, c), np.kron(eye, s)], axis=1).astype(np.float32)


def _sincos_2d(n_tokens):
    rows = n_tokens // GRID_W
    pos = np.arange(rows * GRID_W)
    row = (pos // GRID_W).astype(np.float32)
    col = (pos % GRID_W).astype(np.float32)
    quarter = D_MODEL // 4
    freq = np.exp(np.float32(-math.log(POS_BASE)) * np.arange(quarter, dtype=np.float32) / np.float32(quarter))
    ang_r = row[:, None] * freq
    ang_c = col[:, None] * freq
    return np.concatenate([np.sin(ang_r), np.cos(ang_r), np.sin(ang_c), np.cos(ang_c)], axis=-1).astype(np.float32)


def _head_block_matrix(value):
    blk = np.kron(np.eye(N_HEADS), np.ones((HEAD_DIM, HEAD_DIM))) * value
    return jnp.asarray(blk, dtype=BF16)


def _pad_lora(w, row_offset):
    rows = [jnp.pad(w[d], ((row_offset + d * LORA, LANES - row_offset - (d + 1) * LORA), (0, 0)))
            for d in range(N_DIR)]
    return jnp.stack(rows).astype(BF16)


def _layer_weights(l, w0, w2, a0, a2, k_k, k_a, r_k, gn_g, gn_b):
    w2_p = _pad_lora(w2[l], 0)
    a2_p = _pad_lora(a2[l], N_DIR * LORA)
    tri_f = np.tril(np.ones((CHUNK, CHUNK)))
    tri = jnp.asarray(np.stack([tri_f, tri_f.T]), dtype=BF16)
    return dict(w0=w0[l], w2=w2_p, a0=a0[l], a2=a2_p,
                k_k=k_k[l][None], k_a=k_a[l][None], r_k=r_k[l].reshape(1, D_RWKV),
                gn_g=gn_g[l].reshape(1, D_RWKV), gn_b=gn_b[l].reshape(1, D_RWKV), tri=tri,
                ones_bd=_head_block_matrix(1.0), avg_bd=_head_block_matrix(1.0 / HEAD_DIM))


def _state_to_block_diag(s_f, s_b):
    def one(s):
        h = jnp.swapaxes(s.astype(F32), -1, -2)
        b = h.shape[0]
        h = h.reshape(b, N_PAIRS, PAIR, HEAD_DIM, HEAD_DIM)
        z = jnp.zeros_like(h[:, :, 0])
        top = jnp.concatenate([h[:, :, 0], z], axis=-1)
        bot = jnp.concatenate([z, h[:, :, 1]], axis=-1)
        return jnp.concatenate([top, bot], axis=-2)
    return jnp.stack([one(s_f), one(s_b)], axis=1)


def kernel(x_prompt, x_sample, state_rwkv_fwd, state_rwkv_bwd, c, c_ctx, w_ada, b_ada, norm_g, w_in,
           mu_shift, w0, w2, a0, a2, k_k, k_a, r_k, gn_g, gn_b, w_fnet, b_fnet, w_out, final_norm_g):
    depth = w_in.shape[0]
    n_dec = c.shape[0]
    assert n_dec + 1 <= SUBLANES
    bp, tp, _ = x_prompt.shape
    bs, ts, _ = x_sample.shape
    cvec = jnp.concatenate([c_ctx[None], c, jnp.zeros((SUBLANES - 1 - n_dec, D_MODEL), F32)], axis=0)
    emb = jnp.asarray(_sincos_2d(ts)).astype(x_sample.dtype)
    cs_tab = jnp.asarray(_channel_dft_table()).astype(BF16)
    dft_p = jnp.asarray(_dft_table(tp)).astype(BF16)
    dft_s = jnp.asarray(_dft_table(ts)).astype(BF16)
    fng = final_norm_g[None]

    xp, xs = x_prompt, x_sample
    new_f, new_b = [], []
    for l in range(depth):
        mod = _mod_call(cvec, w_ada[l], b_ada[l][None]).reshape(SUBLANES, 3, D_MODEL)
        mod_ctx, mod_lat = mod[0:1], mod[1:1 + n_dec]
        wts = _layer_weights(l, w0, w2, a0, a2, k_k, k_a, r_k, gn_g, gn_b)
        w_in_b = w_in[l].astype(BF16)
        w_out_b = w_out[l].astype(BF16)
        n_grp = w_fnet.shape[1]
        wbd = (w_fnet[l][:, :, None, :] * jnp.eye(n_grp, dtype=F32)[:, None, :, None]).reshape(
            D_FNET, D_FNET).astype(BF16)
        ng, mu, bf = norm_g[l][None], mu_shift[l][None], b_fnet[l][None]
        emb_l = emb if l == 0 else None

        rk, v, lwla, grec, fin, gf = _in_proj_call(xp, None, mod_ctx, ng, w_in_b, mu)
        y_f, y_b, s_f, s_b = _scan_call(rk, v, lwla, wts, None, True)
        fo = _fnet_call(fin, gf, dft_p, cs_tab, wbd, bf)
        last = l == depth - 1
        xp = _out_call(xp, None, mod_ctx, y_f, y_b, rk, v, lwla, grec, fo, wts, w_out_b, fng, last)
        new_f.append(s_f)
        new_b.append(s_b)

        rk, v, lwla, grec, fin, gf = _in_proj_call(xs, emb_l, mod_lat, ng, w_in_b, mu)
        s0 = _state_to_block_diag(state_rwkv_fwd[:, l], state_rwkv_bwd[:, l])
        y_f, y_b = _scan_call(rk, v, lwla, wts, s0, False)
        fo = _fnet_call(fin, gf, dft_s, cs_tab, wbd, bf)
        xs = _out_call(xs, emb_l, mod_lat, y_f, y_b, rk, v, lwla, grec, fo, wts, w_out_b, fng, last)
    return (xp, xs, jnp.concatenate(new_f, axis=1), jnp.concatenate(new_b, axis=1))
```
